```python
import math
import jax, jax.numpy as jnp
from jax import lax
import numpy as np

D_MODEL = 1024
BATCH = 16
SEQ = 2048
DEPTH = 4

GRID_W = 64
CTX_LEN = 256
HEAD_DIM = 64
A_HEADS = 4
A_KV_HEADS = 2
A_WINDOW = 128
A_BLOCK = 128
B_HEADS = 4
NA_ROWS = 8
NA_COLS = 16
C_WIDTH = D_MODEL // 2
C_ORDER = 2
C_DIRS = 2
C_FILTER_WIDTH = 64
C_BANDS = 16
C_EMB = 2 * C_BANDS + 1
HYENA_TARGET = 1e-2
HYENA_FAST_DECAY = 0.3
HYENA_SLOW_DECAY = 1.5
HYENA_MIN_DECAY = math.log(HYENA_TARGET) / HYENA_SLOW_DECAY
HYENA_MAX_DECAY = math.log(HYENA_TARGET) / HYENA_FAST_DECAY
MIX_COLS = (A_HEADS * HEAD_DIM, A_KV_HEADS * HEAD_DIM, A_KV_HEADS * HEAD_DIM,
            B_HEADS * HEAD_DIM, B_HEADS * HEAD_DIM, B_HEADS * HEAD_DIM, 3 * C_WIDTH)
IN_WIDTH = sum(MIX_COLS)
MIX_WIDTH = A_HEADS * HEAD_DIM + B_HEADS * HEAD_DIM + C_WIDTH
D_FF = -(-8 * D_MODEL // (3 * 256)) * 256
ROPE_BASE = 10000.0
EPS = 1e-6
NEG_INF = -1e30

kernel_name = 'hybrid_parallel_heads_diffusion_trunk'


def _rmsnorm(x, g):
    xf = x.astype(jnp.float32)
    y = xf * lax.rsqrt(jnp.mean(xf * xf, axis=-1, keepdims=True) + EPS)
    return (y * g.astype(jnp.float32)).astype(x.dtype)


def _modulate(h, shift, scale):
    return h * (1 + scale) + shift


def _split_cols(p):
    bn, l, _ = p.shape
    offs = [int(o) for o in np.cumsum(MIX_COLS)[:-1]]
    parts = jnp.split(p, offs, axis=-1)
    qa, ka, va, qb, kb, vb = [t.reshape(bn, l, -1, HEAD_DIM) for t in parts[:6]]
    return qa, ka, va, qb, kb, vb, parts[6]


def _axial_rope(t, rows, cols):
    half = HEAD_DIM // 2
    nfreq = half // 2
    inv = ROPE_BASE ** (-jnp.arange(nfreq, dtype=jnp.float32) / nfreq)

    def rot(u, pos):
        ang = pos.astype(jnp.float32)[:, None] * inv[None, :]
        cos = jnp.cos(ang)[None, :, None, :]
        sin = jnp.sin(ang)[None, :, None, :]
        u = u.astype(jnp.float32)
        u1, u2 = u[..., :nfreq], u[..., nfreq:]
        return jnp.concatenate([u1 * cos - u2 * sin, u2 * cos + u1 * sin], axis=-1)

    out = jnp.concatenate([rot(t[..., :half], rows), rot(t[..., half:], cols)], axis=-1)
    return out.astype(t.dtype)


def _dense_attn(q, k, v, sink):
    bn, lq, hq, dh = q.shape
    hkv = k.shape[2]
    g = hq // hkv
    lk = k.shape[1]
    qg = q.reshape(bn, lq, hkv, g, dh)
    s = jnp.einsum('bqhgd,bkhd->bhgqk', qg, k).astype(jnp.float32) * (dh ** -0.5)
    if sink is not None:
        s_sink = jnp.broadcast_to(sink.astype(jnp.float32).reshape(hkv, g, 1, 1), (bn, hkv, g, lq, 1))
        s = jnp.concatenate([s, s_sink], axis=-1)
    p = jax.nn.softmax(s, axis=-1)[..., :lk].astype(v.dtype)
    return jnp.einsum('bhgqk,bkhd->bqhgd', p, v).reshape(bn, lq, hq * dh)


def _window_attn(q, k, v, kc, vc, sink):
    bn, l, hq, dh = q.shape
    hkv = k.shape[2]
    g = hq // hkv
    nb = l // A_BLOCK
    qb = q.reshape(bn, nb, A_BLOCK, hkv, g, dh)

    def band(t):
        tp = jnp.pad(t, ((0, 0), (A_BLOCK, A_BLOCK), (0, 0), (0, 0)))
        tp = tp.reshape(bn, nb + 2, A_BLOCK, hkv, dh)
        return jnp.concatenate([tp[:, :-2], tp[:, 1:-1], tp[:, 2:]], axis=2)

    kb, vb = band(k), band(v)
    start = jnp.arange(nb)[:, None] * A_BLOCK
    qpos = start + jnp.arange(A_BLOCK)[None, :]
    kpos = start - A_BLOCK + jnp.arange(3 * A_BLOCK)[None, :]
    ok = ((kpos >= 0) & (kpos < l))[:, None, :] & (jnp.abs(qpos[:, :, None] - kpos[:, None, :]) <= A_WINDOW)
    scale = dh ** -0.5
    s_loc = jnp.einsum('bnqhgd,bnkhd->bnhgqk', qb, kb).astype(jnp.float32) * scale
    s_loc = jnp.where(ok[None, :, None, None], s_loc, NEG_INF)
    s_ctx = jnp.einsum('bnqhgd,bchd->bnhgqc', qb, kc).astype(jnp.float32) * scale
    s_sink = jnp.broadcast_to(sink.astype(jnp.float32).reshape(hkv, g, 1, 1), (bn, nb, hkv, g, A_BLOCK, 1))
    p = jax.nn.softmax(jnp.concatenate([s_loc, s_ctx, s_sink], axis=-1), axis=-1).astype(v.dtype)
    nloc = 3 * A_BLOCK
    nctx = kc.shape[1]
    out = (jnp.einsum('bnhgqk,bnkhd->bnqhgd', p[..., :nloc], vb)
           + jnp.einsum('bnhgqc,bchd->bnqhgd', p[..., nloc:nloc + nctx], vc))
    return out.reshape(bn, l, hq * dh)


def _neighbourhood_attn(q, k, v, kc, vc, rpb):
    bn, l, h, dh = q.shape
    rows = l // GRID_W
    kh = min(NA_ROWS, rows)
    qg = q.reshape(bn, rows, GRID_W, h, dh)
    kg = k.reshape(bn, rows, GRID_W, h, dh)
    vg = v.reshape(bn, rows, GRID_W, h, dh)
    r = jnp.arange(rows)
    row_start = jnp.clip(r - kh // 2, 0, rows - kh)
    row_idx = row_start[:, None] + jnp.arange(kh)[None, :]
    k_slab = jnp.take(kg, row_idx, axis=1)
    v_slab = jnp.take(vg, row_idx, axis=1)
    col = jnp.arange(GRID_W)
    col_start = jnp.clip(col - NA_COLS // 2, 0, GRID_W - NA_COLS)
    col_ok = (col[None, :] >= col_start[:, None]) & (col[None, :] < col_start[:, None] + NA_COLS)
    dr = row_idx - r[:, None] + NA_ROWS - 1
    dc = jnp.clip(col[None, :] - col[:, None], 1 - NA_COLS, NA_COLS - 1) + NA_COLS - 1
    bias = rpb[:, dr[:, None, :, None], dc[None, :, None, :]]
    bias = jnp.transpose(bias, (1, 0, 2, 3, 4)).astype(jnp.float32)
    scale = dh ** -0.5
    s_loc = jnp.einsum('brqhd,brkwhd->brhqkw', qg, k_slab).astype(jnp.float32) * scale + bias[None]
    s_loc = jnp.where(col_ok[:, None, :], s_loc, NEG_INF)
    nloc = kh * GRID_W
    s_loc = s_loc.reshape(bn, rows, h, GRID_W, nloc)
    s_ctx = jnp.einsum('brqhd,bchd->brhqc', qg, kc).astype(jnp.float32) * scale
    p = jax.nn.softmax(jnp.concatenate([s_loc, s_ctx], axis=-1), axis=-1).astype(v.dtype)
    p_loc = p[..., :nloc].reshape(bn, rows, h, GRID_W, kh, GRID_W)
    out = (jnp.einsum('brhqkw,brkwhd->brqhd', p_loc, v_slab)
           + jnp.einsum('brhqc,bchd->brqhd', p[..., nloc:], vc))
    return out.reshape(bn, l, h * dh)


def _hyena_kernels(l, w1, b1, freq, w2, b2, w3):
    t = jnp.linspace(0.0, 1.0, l, dtype=jnp.float32)[:, None]
    w = 2.0 * math.pi * jnp.arange(l, dtype=jnp.float32)[:, None] / l
    f = jnp.linspace(1e-4, C_BANDS - 1, C_BANDS, dtype=jnp.float32)[None, :]
    z = jnp.concatenate([t, jnp.cos(w * f), -jnp.sin(w * f)], axis=-1)
    freq = freq.astype(jnp.float32)
    hid = jnp.sin(freq * (z @ w1.astype(jnp.float32) + b1.astype(jnp.float32)))
    hid = jnp.sin(freq * (hid @ w2.astype(jnp.float32) + b2.astype(jnp.float32)))
    hf = (hid @ w3.astype(jnp.float32)).reshape(l, C_ORDER, C_DIRS, C_WIDTH)
    deltas = jnp.abs(jnp.linspace(HYENA_MIN_DECAY, HYENA_MAX_DECAY, C_WIDTH, dtype=jnp.float32))
    hf = hf * jnp.exp(-t[:, :, None, None] * deltas)
    kern = jnp.concatenate([hf[:, :, 0], jnp.zeros((1, C_ORDER, C_WIDTH), jnp.float32), hf[:0:-1, :, 1]], axis=0)
    kern = kern / jnp.sum(jnp.abs(kern), axis=0, keepdims=True)
    return jnp.fft.rfft(kern, axis=0)


def _long_conv(u, kf, d):
    l = u.shape[1]
    uf = u.astype(jnp.float32)
    y = jnp.fft.irfft(jnp.fft.rfft(uf, n=2 * l, axis=1) * kf[None], n=2 * l, axis=1)[:, :l]
    return (y + uf * d.astype(jnp.float32)).astype(u.dtype)


def _short_conv(u, w, b):
    up = jnp.pad(u, ((0, 0), (1, 1), (0, 0)))
    return up[:, :-2] * w[0] + up[:, 1:-1] * w[1] + up[:, 2:] * w[2] + b


def _hyena(u, conv_w, conv_b, kf, d):
    u = _short_conv(u, conv_w, conv_b)
    v, x1, x2 = u[..., :C_WIDTH], u[..., C_WIDTH:2 * C_WIDTH], u[..., 2 * C_WIDTH:]
    z = x1 * _long_conv(v, kf[:, 0], d[0])
    return x2 * _long_conv(z, kf[:, 1], d[1])


def _swiglu(h, wg, wu, wd):
    return (jax.nn.silu(h @ wg) * (h @ wu)) @ wd


def setup_inputs(seed: int = 0) -> dict:
    key = jax.random.key(seed)
    ks = jax.random.split(key, 28)

    def nrm(k, shape, scale):
        return scale * jax.random.normal(k, shape, jnp.float32)

    def gain(k, shape):
        return 1.0 + nrm(k, shape, 0.02)

    return {
        'x': nrm(ks[0], (BATCH, SEQ, D_MODEL), 1.0),
        'c': nrm(ks[1], (BATCH, D_MODEL), 1.0),
        'ctx': nrm(ks[2], (BATCH, CTX_LEN, D_MODEL), 1.0),
        'c_ctx': nrm(ks[3], (D_MODEL,), 1.0),
        'ada_w': nrm(ks[4], (DEPTH, D_MODEL, 6 * D_MODEL), 0.5 * D_MODEL ** -0.5),
        'ada_b': nrm(ks[5], (DEPTH, 6 * D_MODEL), 0.01),
        'norm1_g': gain(ks[6], (DEPTH, D_MODEL)),
        'norm2_g': gain(ks[7], (DEPTH, D_MODEL)),
        'w_in': nrm(ks[8], (DEPTH, D_MODEL, IN_WIDTH), D_MODEL ** -0.5),
        'qnorm_a': gain(ks[9], (DEPTH, HEAD_DIM)),
        'knorm_a': gain(ks[10], (DEPTH, HEAD_DIM)),
        'sink_a': nrm(ks[11], (DEPTH, A_HEADS), 0.5),
        'qnorm_b': gain(ks[12], (DEPTH, HEAD_DIM)),
        'knorm_b': gain(ks[13], (DEPTH, HEAD_DIM)),
        'rpb_b': nrm(ks[14], (DEPTH, B_HEADS, 2 * NA_ROWS - 1, 2 * NA_COLS - 1), 0.1),
        'conv_w': nrm(ks[15], (DEPTH, 3, 3 * C_WIDTH), 3 ** -0.5),
        'conv_b': nrm(ks[16], (DEPTH, 3 * C_WIDTH), 0.01),
        'filt_w1': nrm(ks[17], (DEPTH, C_EMB, C_FILTER_WIDTH), C_EMB ** -0.5),
        'filt_b1': nrm(ks[18], (DEPTH, C_FILTER_WIDTH), 0.1),
        'filt_freq': 1.0 + nrm(ks[19], (DEPTH, C_FILTER_WIDTH), 0.1),
        'filt_w2': nrm(ks[20], (DEPTH, C_FILTER_WIDTH, C_FILTER_WIDTH), C_FILTER_WIDTH ** -0.5),
        'filt_b2': nrm(ks[21], (DEPTH, C_FILTER_WIDTH), 0.1),
        'filt_w3': nrm(ks[22], (DEPTH, C_FILTER_WIDTH, C_ORDER * C_DIRS * C_WIDTH), C_FILTER_WIDTH ** -0.5),
        'hyena_bias': nrm(ks[23], (DEPTH, C_ORDER, C_WIDTH), 1.0),
        'w_out': nrm(ks[24], (DEPTH, MIX_WIDTH, D_MODEL), MIX_WIDTH ** -0.5),
        'ffn_w_gate': nrm(ks[25], (DEPTH, D_MODEL, D_FF), D_MODEL ** -0.5),
        'ffn_w_up': nrm(ks[26], (DEPTH, D_MODEL, D_FF), D_MODEL ** -0.5),
        'ffn_w_down': nrm(ks[27], (DEPTH, D_FF, D_MODEL), D_FF ** -0.5),
    }


def reference(x, c, ctx, c_ctx, ada_w, ada_b, norm1_g, norm2_g, w_in, qnorm_a, knorm_a, sink_a,
              qnorm_b, knorm_b, rpb_b, conv_w, conv_b, filt_w1, filt_b1, filt_freq, filt_w2, filt_b2,
              filt_w3, hyena_bias, w_out, ffn_w_gate, ffn_w_up, ffn_w_down):
    l = x.shape[1]
    lc = ctx.shape[1]
    pos = jnp.arange(l)
    rows, cols = pos // GRID_W, pos % GRID_W
    sc = jax.nn.silu(c)
    sc_ctx = jax.nn.silu(c_ctx)
    xc = ctx
    for i in range(DEPTH):
        last = i == DEPTH - 1
        mod = (sc @ ada_w[i] + ada_b[i])[:, None, :]
        mod_c = sc_ctx @ ada_w[i] + ada_b[i]
        sh1, s1, g1, sh2, s2, g2 = jnp.split(mod, 6, axis=-1)
        ch1, cs1, cg1, ch2, cs2, cg2 = jnp.split(mod_c, 6, axis=-1)

        h = _modulate(_rmsnorm(x, norm1_g[i]), sh1, s1) @ w_in[i]
        hc = _modulate(_rmsnorm(xc, norm1_g[i]), ch1, cs1) @ w_in[i]
        qa, ka, va, qb, kb, vb, hy = _split_cols(h)
        qa_c, ka_c, va_c, qb_c, kb_c, vb_c, hy_c = _split_cols(hc)
        qa = _axial_rope(_rmsnorm(qa, qnorm_a[i]), rows, cols)
        ka = _axial_rope(_rmsnorm(ka, knorm_a[i]), rows, cols)
        qb = _rmsnorm(qb, qnorm_b[i])
        kb = _rmsnorm(kb, knorm_b[i])
        ka_c = _rmsnorm(ka_c, knorm_a[i])
        kb_c = _rmsnorm(kb_c, knorm_b[i])
        kf = _hyena_kernels(l, filt_w1[i], filt_b1[i], filt_freq[i], filt_w2[i], filt_b2[i], filt_w3[i])
        out_a = _window_attn(qa, ka, va, ka_c, va_c, sink_a[i])
        out_b = _neighbourhood_attn(qb, kb, vb, kb_c, vb_c, rpb_b[i])
        out_c = _hyena(hy, conv_w[i], conv_b[i], kf, hyena_bias[i])
        x = x + g1 * (jnp.concatenate([out_a, out_b, out_c], axis=-1) @ w_out[i])
        x = x + g2 * _swiglu(_modulate(_rmsnorm(x, norm2_g[i]), sh2, s2), ffn_w_gate[i], ffn_w_up[i], ffn_w_down[i])

        if not last:
            qa_c = _rmsnorm(qa_c, qnorm_a[i])
            qb_c = _rmsnorm(qb_c, qnorm_b[i])
            kf_c = _hyena_kernels(lc, filt_w1[i], filt_b1[i], filt_freq[i], filt_w2[i], filt_b2[i], filt_w3[i])
            oa_c = _dense_attn(qa_c, ka_c, va_c, sink_a[i])
            ob_c = _dense_attn(qb_c, kb_c, vb_c, None)
            oc_c = _hyena(hy_c, conv_w[i], conv_b[i], kf_c, hyena_bias[i])
            xc = xc + cg1 * (jnp.concatenate([oa_c, ob_c, oc_c], axis=-1) @ w_out[i])
            xc = xc + cg2 * _swiglu(_modulate(_rmsnorm(xc, norm2_g[i]), ch2, cs2), ffn_w_gate[i], ffn_w_up[i], ffn_w_down[i])
    return x
```

```python
import functools
import math

import jax
import jax.numpy as jnp
import numpy as np
from jax import lax
from jax.experimental import pallas as pl
from jax.experimental.pallas import tpu as pltpu

F32 = jnp.float32
BF16 = jnp.bfloat16

GRID_W = 64
HEAD_DIM = 64
A_HEADS = 4
A_KV_HEADS = 2
A_BLOCK = 128
B_HEADS = 4
NA_ROWS = 8
NA_COLS = 16
C_ORDER = 2
C_DIRS = 2
C_FILTER_WIDTH = 64
C_BANDS = 16
ROPE_BASE = 10000.0
EPS = 1e-6
NEG_INF = -1e30
HYENA_MIN_DECAY = math.log(1e-2) / 1.5
HYENA_MAX_DECAY = math.log(1e-2) / 0.3

V7X_VMEM_BYTES = 64 * 1024 * 1024
VMEM_LIMIT = V7X_VMEM_BYTES - 8 * 1024 * 1024
LANES = 128


def _params(*sem):
    return pltpu.CompilerParams(dimension_semantics=sem, vmem_limit_bytes=VMEM_LIMIT)


def _dot(a, b):
    return jnp.dot(a, b, preferred_element_type=F32)


def _dot_nt(a, b):
    return lax.dot_general(a, b, (((1,), (1,)), ((), ())), preferred_element_type=F32)


def _dot_hp(a, b):
    return jnp.dot(a, b, preferred_element_type=F32, precision=lax.Precision.HIGHEST)


def _split_bf16(v):
    hi = v.astype(BF16)
    lo = (v - hi.astype(F32)).astype(BF16)
    return hi, lo


def _resident(shape):
    nd = len(shape)
    return pl.BlockSpec(shape, lambda *_: (0,) * nd, pipeline_mode=pl.Buffered(1))


def _mod_kernel(c_ref, w_ref, b_ref, o_ref):
    cv = c_ref[...]
    sc = (cv * jax.nn.sigmoid(cv)).astype(BF16)
    o_ref[...] = _dot(sc, w_ref[...].astype(BF16)) + b_ref[...]


def _modulation(c_all, ada_w, ada_b):
    depth, d, n = ada_w.shape
    rows = c_all.shape[0]
    tn = 512
    return pl.pallas_call(
        _mod_kernel,
        out_shape=jax.ShapeDtypeStruct((depth, rows, n), F32),
        grid=(depth, n // tn),
        in_specs=[
            pl.BlockSpec((rows, d), lambda i, j: (0, 0)),
            pl.BlockSpec((None, d, tn), lambda i, j: (i, 0, j)),
            pl.BlockSpec((None, 1, tn), lambda i, j: (i, 0, j)),
        ],
        out_specs=pl.BlockSpec((None, rows, tn), lambda i, j: (i, 0, j)),
        compiler_params=_params("arbitrary", "arbitrary"),
        name="modulation",
    )(c_all, ada_w, ada_b.reshape(depth, 1, n))


def _rms_mod(x, g, shift, scale):
    y = x * lax.rsqrt(jnp.mean(x * x, axis=-1, keepdims=True) + EPS)
    return (y * g) * (1 + scale) + shift


def _head_norm(h, g, e):
    hi, lo = _split_bf16(h * h)
    ss = _dot(hi, e) + _dot(lo, e)
    return (h * lax.rsqrt(ss * (1.0 / HEAD_DIM) + EPS)) * g


def _rope(t, cos, sin_a, sin_b):
    outs = []
    for c in range(t.shape[1] // LANES):
        tc = t[:, c * LANES:(c + 1) * LANES]
        outs.append(tc * cos + pltpu.roll(tc, LANES - 16, 1) * sin_a + pltpu.roll(tc, 16, 1) * sin_b)
    return outs[0] if len(outs) == 1 else jnp.concatenate(outs, axis=1)


def _proj_in_kernel(x_ref, shift_ref, scale_ref, g_ref, w_ref, gqa_ref, gka_ref, gqb_ref, gkb_ref, e_ref,
                    cos_ref, sa_ref, sb_ref,
                    qa_ref, ka_ref, va_ref, qb_ref, kb_ref, vb_ref, hy_ref, *, rope):
    xb = _rms_mod(x_ref[...], g_ref[...], shift_ref[...], scale_ref[...]).astype(BF16)
    qscale = HEAD_DIM ** -0.5

    def proj(lo, hi):
        return _dot(xb, w_ref[:, lo:hi])

    def maybe_rope(t):
        return _rope(t, cos_ref[...], sa_ref[...], sb_ref[...]) if rope else t

    qa = maybe_rope(_head_norm(proj(0, 256), gqa_ref[...], e_ref[...]))
    qa_ref[...] = (qa * qscale).astype(BF16)
    ka = maybe_rope(_head_norm(proj(256, 384), gka_ref[:, :128], e_ref[:128, :128]))
    ka_ref[...] = ka.astype(BF16)
    va_ref[...] = proj(384, 512).astype(BF16)
    qb = _head_norm(proj(512, 768), gqb_ref[...], e_ref[...])
    qb_ref[...] = (qb * qscale).astype(BF16)
    kb_ref[...] = _head_norm(proj(768, 1024), gkb_ref[...], e_ref[...]).astype(BF16)
    vb_ref[...] = proj(1024, 1280).astype(BF16)
    hy_ref[...] = proj(1280, 2816)


def _proj_in(x, shift, scale, g, w, gqa, gka, gqb, gkb, e, rope_tabs, *, rope):
    bn, l, d = x.shape
    n = w.shape[1]
    tm = 256
    row = lambda width: pl.BlockSpec((None, 1, width), lambda b, t: (b, 0, 0))
    const = lambda shape: pl.BlockSpec(shape, lambda b, t: (0,) * len(shape))
    tok = lambda width: pl.BlockSpec((None, tm, width), lambda b, t: (b, t, 0))
    tab = pl.BlockSpec((tm, LANES), lambda b, t: (t, 0))
    widths = (256, 128, 128, 256, 256, 256, n - 1280)
    dts = (BF16,) * 6 + (F32,)
    return pl.pallas_call(
        functools.partial(_proj_in_kernel, rope=rope),
        out_shape=[jax.ShapeDtypeStruct((bn, l, wd), dt) for wd, dt in zip(widths, dts)],
        grid=(bn, l // tm),
        in_specs=[tok(d), row(d), row(d), const((1, d)), _resident((d, n)),
                  const((1, 256)), const((1, 256)), const((1, 256)), const((1, 256)), const((256, 256)),
                  tab, tab, tab],
        out_specs=[tok(wd) for wd in widths],
        compiler_params=_params("parallel", "parallel"),
        name="proj_in_rope" if rope else "proj_in",
    )(x, shift, scale, g, w, gqa, gka, gqb, gkb, e, *rope_tabs)


def _softmax_pv(s_list, v_list, sink):
    m = s_list[0].max(axis=-1, keepdims=True)
    for s in s_list[1:]:
        m = jnp.maximum(m, s.max(axis=-1, keepdims=True))
    if sink is not None:
        m = jnp.maximum(m, sink)
    den = None
    out = None
    for s, v in zip(s_list, v_list):
        p = jnp.exp(s - m)
        ps = p.sum(axis=-1, keepdims=True)
        den = ps if den is None else den + ps
        o = _dot(p.astype(BF16), v)
        out = o if out is None else out + o
    if sink is not None:
        den = den + jnp.exp(sink - m)
    return out * (1.0 / den)


def _attn_a_kernel(sink_ref, q_ref, kp_ref, kc_ref, kn_ref, vp_ref, vc_ref, vn_ref, kx_ref, vx_ref, o_ref):
    n = pl.program_id(1)
    nb = pl.num_programs(1)
    ii = lax.broadcasted_iota(jnp.int32, (A_BLOCK, A_BLOCK), 0)
    jj = lax.broadcasted_iota(jnp.int32, (A_BLOCK, A_BLOCK), 1)
    far = 4 * A_BLOCK
    ok_prev = (jj - ii) >= jnp.where(n > 0, 0, far)
    ok_next = (ii - jj) >= jnp.where(n < nb - 1, 0, far)
    g = A_HEADS // A_KV_HEADS
    for h in range(A_HEADS):
        hs = slice(h * HEAD_DIM, (h + 1) * HEAD_DIM)
        ks = slice((h // g) * HEAD_DIM, (h // g + 1) * HEAD_DIM)
        q = q_ref[:, hs]
        s_p = jnp.where(ok_prev, _dot_nt(q, kp_ref[:, ks]), NEG_INF)
        s_c = _dot_nt(q, kc_ref[:, ks])
        s_n = jnp.where(ok_next, _dot_nt(q, kn_ref[:, ks]), NEG_INF)
        s_x = _dot_nt(q, kx_ref[:, ks])
        o = _softmax_pv([s_p, s_c, s_n, s_x], [vp_ref[:, ks], vc_ref[:, ks], vn_ref[:, ks], vx_ref[:, ks]],
                        sink_ref[h])
        o_ref[:, hs] = o.astype(o_ref.dtype)


def _attn_a(q, k, v, kx, vx, sink):
    bn, l, _ = q.shape
    nb = l // A_BLOCK
    lc = kx.shape[1]
    kvw = A_KV_HEADS * HEAD_DIM
    qw = A_HEADS * HEAD_DIM
    blk = lambda off: pl.BlockSpec((None, A_BLOCK, kvw), lambda b, n: (b, jnp.clip(n + off, 0, nb - 1), 0))
    ctx = pl.BlockSpec((None, lc, kvw), lambda b, n: (b, 0, 0))
    return pl.pallas_call(
        _attn_a_kernel,
        out_shape=jax.ShapeDtypeStruct((bn, l, qw), BF16),
        grid=(bn, nb),
        in_specs=[pl.BlockSpec(memory_space=pltpu.SMEM),
                  pl.BlockSpec((None, A_BLOCK, qw), lambda b, n: (b, n, 0)),
                  blk(-1), blk(0), blk(1), blk(-1), blk(0), blk(1), ctx, ctx],
        out_specs=pl.BlockSpec((None, A_BLOCK, qw), lambda b, n: (b, n, 0)),
        compiler_params=_params("parallel", "parallel"),
        name="window_attn",
    )(sink, q, k, k, k, v, v, v, kx, vx)


def _dense_attn_kernel(sink_ref, q_ref, k_ref, v_ref, o_ref, *, heads, kv_heads, use_sink):
    g = heads // kv_heads
    for h in range(heads):
        hs = slice(h * HEAD_DIM, (h + 1) * HEAD_DIM)
        ks = slice((h // g) * HEAD_DIM, (h // g + 1) * HEAD_DIM)
        s = _dot_nt(q_ref[:, hs], k_ref[:, ks])
        o = _softmax_pv([s], [v_ref[:, ks]], sink_ref[h] if use_sink else None)
        o_ref[:, hs] = o.astype(o_ref.dtype)


def _dense_attn(q, k, v, sink, *, use_sink):
    bn, l, qw = q.shape
    kvw = k.shape[2]
    heads, kv_heads = qw // HEAD_DIM, kvw // HEAD_DIM
    full = lambda wd: pl.BlockSpec((None, l, wd), lambda b: (b, 0, 0))
    return pl.pallas_call(
        functools.partial(_dense_attn_kernel, heads=heads, kv_heads=kv_heads, use_sink=use_sink),
        out_shape=jax.ShapeDtypeStruct((bn, l, qw), BF16),
        grid=(bn,),
        in_specs=[pl.BlockSpec(memory_space=pltpu.SMEM), full(qw), full(kvw), full(kvw)],
        out_specs=full(qw),
        compiler_params=_params("parallel"),
        name="ctx_attn_sink" if use_sink else "ctx_attn",
    )(sink, q, k, v)


def _rpb_kernel(r_ref, oh_ref, o_ref):
    r = r_ref[...]
    b1 = r.astype(BF16)
    r2 = r - b1.astype(F32)
    b2 = r2.astype(BF16)
    b3 = (r2 - b2.astype(F32)).astype(BF16)
    oh = oh_ref[...]
    o_ref[...] = (_dot(b1, oh) + _dot(b2, oh)) + _dot(b3, oh)


def _rpb_tables(rpb):
    h, nr, nc = rpb.shape
    col = np.arange(GRID_W)
    dc = np.clip(col[None, :] - col[:, None], 1 - NA_COLS, NA_COLS - 1) + NA_COLS - 1
    onehot = (dc.reshape(1, -1) == np.arange(nc)[:, None]).astype(np.float32)
    onehot = np.concatenate([onehot, np.zeros((32 - nc, GRID_W * GRID_W), np.float32)], axis=0)
    rows = 64
    r2 = jnp.zeros((rows, 32), F32).at[:h * nr, :nc].set(rpb.reshape(h * nr, nc))
    full = pl.pallas_call(
        _rpb_kernel,
        out_shape=jax.ShapeDtypeStruct((rows, GRID_W * GRID_W), F32),
        name="rpb_table",
    )(r2, jnp.asarray(onehot, BF16))
    full = full[:h * nr].reshape(h, nr, GRID_W, GRID_W)
    return jnp.concatenate([full[:, :-1], full[:, 1:]], axis=-1)


def _attn_b_kernel(q_ref, k_ref, v_ref, kx_ref, vx_ref, tbl_ref, ok_ref, o_ref):
    r = pl.program_id(1)
    rows = pl.num_programs(1)
    kh = NA_ROWS
    row_start = jnp.clip(r - kh // 2, 0, rows - kh)
    dr0 = row_start - r + NA_ROWS - 1
    start = pl.multiple_of(row_start * GRID_W, GRID_W)
    nloc = kh * GRID_W
    ok = jnp.concatenate([ok_ref[...]] * (kh // 2), axis=1) > 0.5
    for h in range(B_HEADS):
        hs = slice(h * HEAD_DIM, (h + 1) * HEAD_DIM)
        q = q_ref[:, hs]
        k = k_ref[pl.ds(start, nloc), hs]
        v = v_ref[pl.ds(start, nloc), hs]
        bias = jnp.concatenate([tbl_ref[h, dr0 + 2 * i] for i in range(kh // 2)], axis=1)
        s_loc = jnp.where(ok, _dot_nt(q, k) + bias, NEG_INF)
        s_x = _dot_nt(q, kx_ref[:, hs])
        o = _softmax_pv([s_loc, s_x], [v, vx_ref[:, hs]], None)
        o_ref[:, hs] = o.astype(o_ref.dtype)


def _attn_b(q, k, v, kx, vx, tbl, ok):
    bn, l, w = q.shape
    rows = l // GRID_W
    lc = kx.shape[1]
    seq = pl.BlockSpec((None, l, w), lambda b, r: (b, 0, 0))
    ctx = pl.BlockSpec((None, lc, w), lambda b, r: (b, 0, 0))
    qrow = pl.BlockSpec((None, GRID_W, w), lambda b, r: (b, r, 0))
    return pl.pallas_call(
        _attn_b_kernel,
        out_shape=jax.ShapeDtypeStruct((bn, l, w), BF16),
        grid=(bn, rows),
        in_specs=[qrow, seq, seq, ctx, ctx, _resident(tbl.shape), _resident(ok.shape)],
        out_specs=qrow,
        compiler_params=_params("parallel", "arbitrary"),
        name="nbr_attn",
    )(q, k, v, kx, vx, tbl, ok)


def _dft_matrices(l, fh):
    k = jnp.arange(l, dtype=jnp.int32)[:, None]
    n = jnp.arange(l, dtype=jnp.int32)[None, :]
    ang = (((2 * k + 1) * n) % (4 * l)).astype(F32) * (math.pi / (2 * l))
    fre = jnp.cos(ang).reshape(l // fh, fh, l)
    fim = (-jnp.sin(ang)).reshape(l // fh, fh, l)
    f = jnp.concatenate([fre, fim], axis=1).reshape(2 * l, l)
    g = f.T * (1.0 / l)
    return f.astype(BF16), g.astype(BF16)


def _filter_kernel(z_ref, t_ref, w1_ref, b1_ref, fr_ref, w2_ref, b2_ref, w3_ref, dl_ref,
                   ah_ref, al_ref, sh_ref, sl_ref):
    fr = fr_ref[...]
    hid = jnp.sin(fr * (_dot_hp(z_ref[...], w1_ref[...]) + b1_ref[...]))
    hid = jnp.sin(fr * (_dot_hp(hid, w2_ref[...]) + b2_ref[...]))
    cw = dl_ref.shape[1]
    decay = jnp.exp(-t_ref[...] * dl_ref[...])
    kf = _dot_hp(hid, w3_ref[:, :cw]) * decay
    kb = _dot_hp(hid, w3_ref[:, cw:]) * decay
    first = lax.broadcasted_iota(jnp.int32, kb.shape, 0) == 0
    kb = jnp.where(first, 0.0, kb)
    zn = jnp.sum(jnp.abs(kf), axis=0, keepdims=True) + jnp.sum(jnp.abs(kb), axis=0, keepdims=True)
    ah, al = _split_bf16((kf + kb) / zn)
    sh, sl = _split_bf16((kf - kb) / zn)
    ah_ref[...] = ah
    al_ref[...] = al
    sh_ref[...] = sh
    sl_ref[...] = sl


def _filters(z, t, w1, b1, freq, w2, b2, w3, deltas):
    l = z.shape[0]
    cw = deltas.shape[1]
    fw = w2.shape[0]
    const = lambda shape: pl.BlockSpec(shape, lambda o: (0,) * len(shape))
    out = jax.ShapeDtypeStruct((l, C_ORDER * cw), BF16)
    oblk = pl.BlockSpec((l, cw), lambda o: (0, o))
    return pl.pallas_call(
        _filter_kernel,
        out_shape=[out] * 4,
        grid=(C_ORDER,),
        in_specs=[const(z.shape), const((l, 1)), const(w1.shape), const((1, fw)), const((1, fw)),
                  const((fw, fw)), const((1, fw)), pl.BlockSpec((fw, C_DIRS * cw), lambda o: (0, o)),
                  const((1, cw))],
        out_specs=[oblk] * 4,
        compiler_params=_params("arbitrary"),
        name="hyena_filter",
    )(z, t, w1, b1, freq, w2, b2, w3, deltas)


def _spectrum_kernel(f_ref, ah_ref, al_ref, sh_ref, sl_ref, kre_ref, kim_ref):
    fh = kre_ref.shape[0]
    fre = f_ref[:fh, :]
    fim = f_ref[fh:, :]
    kre_ref[...] = _dot(fre, ah_ref[...]) + _dot(fre, al_ref[...])
    kim_ref[...] = _dot(fim, sh_ref[...]) + _dot(fim, sl_ref[...])


def _spectrum(f, ah, al, sh, sl, fh):
    l, n = ah.shape
    const = pl.BlockSpec((l, n), lambda j: (0, 0), pipeline_mode=pl.Buffered(1))
    out = jax.ShapeDtypeStruct((l, n), F32)
    oblk = pl.BlockSpec((fh, n), lambda j: (j, 0))
    return pl.pallas_call(
        _spectrum_kernel,
        out_shape=[out, out],
        grid=(l // fh,),
        in_specs=[pl.BlockSpec((2 * fh, l), lambda j: (j, 0)), const, const, const, const],
        out_specs=[oblk, oblk],
        compiler_params=_params("arbitrary"),
        name="hyena_spectrum",
    )(f, ah, al, sh, sl)


def _short_conv(u, w_ref, b_ref):
    n = u.shape[0]
    row = lax.broadcasted_iota(jnp.int32, u.shape, 0)
    prev = jnp.where(row == 0, 0.0, pltpu.roll(u, 1, 0))
    nxt = jnp.where(row == n - 1, 0.0, pltpu.roll(u, n - 1, 0))
    return prev * w_ref[0:1, :] + u * w_ref[1:2, :] + nxt * w_ref[2:3, :] + b_ref[...]


def _longconv_kernel(u_ref, g_ref, cwu_ref, cbu_ref, cwg_ref, cbg_ref, f_ref, gm_ref, kre_ref, kim_ref, d_ref,
                     o_ref, ub_ref, acc_ref, *, conv_u):
    j = pl.program_id(1)
    fh = kre_ref.shape[0]

    def load_u():
        return _short_conv(u_ref[...], cwu_ref, cbu_ref) if conv_u else u_ref[...]

    @pl.when(j == 0)
    def _():
        ub_ref[...] = load_u().astype(BF16)
        acc_ref[...] = jnp.zeros_like(acc_ref)

    spec = _dot(f_ref[...], ub_ref[...])
    vre, vim = spec[:fh], spec[fh:]
    kre, kim = kre_ref[...], kim_ref[...]
    y = jnp.concatenate([vre * kre - vim * kim, vre * kim + vim * kre], axis=0).astype(BF16)
    acc_ref[...] += _dot(gm_ref[...], y)

    @pl.when(j == pl.num_programs(1) - 1)
    def _():
        gate = _short_conv(g_ref[...], cwg_ref, cbg_ref)
        o_ref[...] = (gate * (acc_ref[...] + load_u() * d_ref[...])).astype(o_ref.dtype)


def _longconv(u_arr, u_blk, g_arr, g_blk, conv_w, conv_b, f, gm, kre, kim, k_blk, d, *, conv_u, out_dtype, fh):
    bn, l, _ = u_arr.shape
    cw = d.shape[1]
    ub = u_blk if conv_u else 0
    tok = lambda blk: pl.BlockSpec((None, l, cw), lambda b, j: (b, 0, blk))
    cpar = lambda rows, blk: pl.BlockSpec((rows, cw), lambda b, j: (0, blk))
    return pl.pallas_call(
        functools.partial(_longconv_kernel, conv_u=conv_u),
        out_shape=jax.ShapeDtypeStruct((bn, l, cw), out_dtype),
        grid=(bn, l // fh),
        in_specs=[tok(u_blk), tok(g_blk), cpar(3, ub), cpar(1, ub), cpar(3, g_blk), cpar(1, g_blk),
                  pl.BlockSpec((2 * fh, l), lambda b, j: (j, 0)),
                  pl.BlockSpec((l, 2 * fh), lambda b, j: (0, j)),
                  pl.BlockSpec((fh, cw), lambda b, j: (j, k_blk)),
                  pl.BlockSpec((fh, cw), lambda b, j: (j, k_blk)),
                  pl.BlockSpec((1, cw), lambda b, j: (0, 0))],
        out_specs=tok(0),
        scratch_shapes=[pltpu.VMEM((l, cw), BF16), pltpu.VMEM((l, cw), F32)],
        compiler_params=_params("parallel", "arbitrary"),
        name="hyena_longconv",
    )(u_arr, g_arr, conv_w, conv_b, conv_w, conv_b, f, gm, kre, kim, d)


def _hyena_consts(l, cw):
    t = np.linspace(0.0, 1.0, l, dtype=np.float32)[:, None]
    w = (2.0 * math.pi * np.arange(l, dtype=np.float32)[:, None] / l).astype(np.float32)
    fq = np.linspace(1e-4, C_BANDS - 1, C_BANDS, dtype=np.float32)[None, :]
    wf = jnp.asarray(w) * jnp.asarray(fq)
    z = jnp.concatenate([jnp.asarray(t), jnp.cos(wf), -jnp.sin(wf)], axis=-1)
    z = jnp.pad(z, ((0, 0), (0, C_FILTER_WIDTH - z.shape[1])))
    deltas = np.abs(np.linspace(HYENA_MIN_DECAY, HYENA_MAX_DECAY, cw, dtype=np.float32))[None, :]
    return z, jnp.asarray(t), jnp.asarray(deltas)


def _hyena(hy, hconst, f, gm, fh, conv_w, conv_b, w1p, b1, freq, w2, b2, w3, dbias):
    z, t, deltas = hconst
    cw = deltas.shape[1]
    ah, al, sh, sl = _filters(z, t, w1p, b1, freq, w2, b2, w3, deltas)
    kre, kim = _spectrum(f, ah, al, sh, sl, fh)
    zz = _longconv(hy, 0, hy, 1, conv_w, conv_b, f, gm, kre, kim, 0, dbias[0:1],
                   conv_u=True, out_dtype=F32, fh=fh)
    return _longconv(zz, 0, hy, 2, conv_w, conv_b, f, gm, kre, kim, 1, dbias[1:2],
                     conv_u=False, out_dtype=BF16, fh=fh)


def _mix_ffn_kernel(x_ref, a_ref, b_ref, c_ref, wo_ref, g1_ref, gn_ref, sh_ref, sc_ref, g2_ref,
                    wg_ref, wu_ref, wd_ref, o_ref, *, fc):
    wa, wb = a_ref.shape[1], b_ref.shape[1]
    mix = (_dot(a_ref[...], wo_ref[:wa, :]) + _dot(b_ref[...], wo_ref[wa:wa + wb, :])
           + _dot(c_ref[...], wo_ref[wa + wb:, :]))
    x1 = x_ref[...] + g1_ref[...] * mix
    xb = _rms_mod(x1, gn_ref[...], sh_ref[...], sc_ref[...]).astype(BF16)
    acc = None
    for c0 in range(0, wg_ref.shape[1], fc):
        hg = _dot(xb, wg_ref[:, c0:c0 + fc])
        hu = _dot(xb, wu_ref[:, c0:c0 + fc])
        act = ((hg * jax.nn.sigmoid(hg)) * hu).astype(BF16)
        part = _dot(act, wd_ref[c0:c0 + fc, :])
        acc = part if acc is None else acc + part
    o_ref[...] = x1 + g2_ref[...] * acc


def _mix_ffn(x, a, b, c, wo, g1, gn, sh2, sc2, g2, wg, wu, wd):
    bn, l, d = x.shape
    tm = 256
    row = pl.BlockSpec((None, 1, d), lambda bb, t: (bb, 0, 0))
    tok = lambda width: pl.BlockSpec((None, tm, width), lambda bb, t: (bb, t, 0))
    return pl.pallas_call(
        functools.partial(_mix_ffn_kernel, fc=256),
        out_shape=jax.ShapeDtypeStruct((bn, l, d), F32),
        grid=(bn, l // tm),
        in_specs=[tok(d), tok(a.shape[2]), tok(b.shape[2]), tok(c.shape[2]), _resident(wo.shape),
                  row, pl.BlockSpec((1, d), lambda bb, t: (0, 0)), row, row, row,
                  _resident(wg.shape), _resident(wu.shape), _resident(wd.shape)],
        out_specs=tok(d),
        compiler_params=_params("parallel", "parallel"),
        name="mix_ffn",
    )(x, a, b, c, wo, g1, gn, sh2, sc2, g2, wg, wu, wd)


def _rope_tables(l):
    half = HEAD_DIM // 2
    nfreq = half // 2
    inv = ROPE_BASE ** (-jnp.arange(nfreq, dtype=F32) / nfreq)
    pos = jnp.arange(l)
    rows, cols = pos // GRID_W, pos % GRID_W
    ang = jnp.concatenate([rows.astype(F32)[:, None] * inv[None, :]] * 2
                          + [cols.astype(F32)[:, None] * inv[None, :]] * 2, axis=-1)
    first = (np.arange(HEAD_DIM) % half) < nfreq
    cos, sin = jnp.cos(ang), jnp.sin(ang)
    sin_a = jnp.where(first[None, :], -sin, 0.0)
    sin_b = jnp.where(first[None, :], 0.0, sin)
    reps = LANES // HEAD_DIM
    return tuple(jnp.tile(tb, (1, reps)) for tb in (cos, sin_a, sin_b))


def kernel(x, c, ctx, c_ctx, ada_w, ada_b, norm1_g, norm2_g, w_in, qnorm_a, knorm_a, sink_a, qnorm_b, knorm_b,
           rpb_b, conv_w, conv_b, filt_w1, filt_b1, filt_freq, filt_w2, filt_b2, filt_w3, hyena_bias, w_out,
           ffn_w_gate, ffn_w_up, ffn_w_down):
    bn, l, d = x.shape
    lc = ctx.shape[1]
    depth = ada_w.shape[0]
    cw = hyena_bias.shape[2]
    assert l // GRID_W >= NA_ROWS and l % 256 == 0 and lc % 256 == 0

    mod_rows = 8 * (-(-(bn + 1) // 8))
    c_all = jnp.zeros((mod_rows, d), F32).at[:bn].set(c).at[bn].set(c_ctx)
    mod = _modulation(c_all, ada_w, ada_b)

    w_in_b = w_in.astype(BF16)
    w_out_b = w_out.astype(BF16)
    wg_b, wu_b, wd_b = ffn_w_gate.astype(BF16), ffn_w_up.astype(BF16), ffn_w_down.astype(BF16)

    lane = np.arange(256)
    e_heads = jnp.asarray((lane[:, None] // HEAD_DIM) == (lane[None, :] // HEAD_DIM), BF16)
    tile4 = lambda g: jnp.tile(g.reshape(1, HEAD_DIM), (1, 256 // HEAD_DIM))
    rope_lat = _rope_tables(l)
    rope_ctx = tuple(tb[:lc] for tb in rope_lat)

    col = np.arange(GRID_W)
    col_start = np.clip(col - NA_COLS // 2, 0, GRID_W - NA_COLS)
    col_ok = (col[None, :] >= col_start[:, None]) & (col[None, :] < col_start[:, None] + NA_COLS)
    ok2 = jnp.asarray(np.concatenate([col_ok, col_ok], axis=1), F32)

    fh_lat, fh_ctx = 256, min(256, lc)
    f_lat, g_lat = _dft_matrices(l, fh_lat)
    f_ctx, g_ctx = _dft_matrices(lc, fh_ctx)
    hc_lat = _hyena_consts(l, cw)
    hc_ctx = _hyena_consts(lc, cw)

    xc = ctx
    for i in range(depth):
        last = i == depth - 1
        m = mod[i]
        part = lambda rows, k: rows[:, None, k * d:(k + 1) * d]
        lat = [part(m[:bn], k) for k in range(6)]
        cx = [jnp.broadcast_to(part(m[bn:bn + 1], k), (bn, 1, d)) for k in range(6)]
        g1n = norm1_g[i].reshape(1, d)
        g2n = norm2_g[i].reshape(1, d)
        gqa, gka, gqb, gkb = tile4(qnorm_a[i]), tile4(knorm_a[i]), tile4(qnorm_b[i]), tile4(knorm_b[i])

        qa, ka, va, qb, kb, vb, hy = _proj_in(x, lat[0], lat[1], g1n, w_in_b[i], gqa, gka, gqb, gkb, e_heads,
                                              rope_lat, rope=True)
        qa_c, ka_c, va_c, qb_c, kb_c, vb_c, hy_c = _proj_in(xc, cx[0], cx[1], g1n, w_in_b[i], gqa, gka, gqb, gkb,
                                                            e_heads, rope_ctx, rope=False)
        w1p = jnp.pad(filt_w1[i], ((0, C_FILTER_WIDTH - filt_w1.shape[1]), (0, 0)))
        hy_args = (conv_w[i], conv_b[i].reshape(1, -1), w1p, filt_b1[i].reshape(1, -1),
                   filt_freq[i].reshape(1, -1), filt_w2[i], filt_b2[i].reshape(1, -1), filt_w3[i], hyena_bias[i])

        out_a = _attn_a(qa, ka, va, ka_c, va_c, sink_a[i])
        out_b = _attn_b(qb, kb, vb, kb_c, vb_c, _rpb_tables(rpb_b[i]), ok2)
        out_c = _hyena(hy, hc_lat, f_lat, g_lat, fh_lat, *hy_args)
        x = _mix_ffn(x, out_a, out_b, out_c, w_out_b[i], lat[2], g2n, lat[3], lat[4], lat[5],
                     wg_b[i], wu_b[i], wd_b[i])

        if not last:
            oa_c = _dense_attn(qa_c, ka_c, va_c, sink_a[i], use_sink=True)
            ob_c = _dense_attn(qb_c, kb_c, vb_c, sink_a[i], use_sink=False)
            oc_c = _hyena(hy_c, hc_ctx, f_ctx, g_ctx, fh_ctx, *hy_args)
            xc = _mix_ffn(xc, oa_c, ob_c, oc_c, w_out_b[i], cx[2], g2n, cx[3], cx[4], cx[5],
                          wg_b[i], wu_b[i], wd_b[i])
    return x
```

```python
import functools
import math

import jax
import jax.numpy as jnp
import numpy as np
from jax import lax
from jax.experimental import pallas as pl
from jax.experimental.pallas import tpu as pltpu

F32 = jnp.float32
BF16 = jnp.bfloat16

GRID_W = 64
HEAD_DIM = 64
A_HEADS = 4
A_KV_HEADS = 2
A_BLOCK = 128
B_HEADS = 4
NA_ROWS = 8
NA_COLS = 16
C_ORDER = 2
C_DIRS = 2
C_FILTER_WIDTH = 64
C_BANDS = 16
ROPE_BASE = 10000.0
EPS = 1e-6
NEG_INF = -1e30
HYENA_MIN_DECAY = math.log(1e-2) / 1.5
HYENA_MAX_DECAY = math.log(1e-2) / 0.3

V7X_VMEM_BYTES = 64 * 1024 * 1024
VMEM_LIMIT = V7X_VMEM_BYTES - 8 * 1024 * 1024
LANES = 128


def _params(*sem):
    return pltpu.CompilerParams(dimension_semantics=sem, vmem_limit_bytes=VMEM_LIMIT)


def _dot(a, b):
    return jnp.dot(a, b, preferred_element_type=F32)


def _dot_nt(a, b):
    return lax.dot_general(a, b, (((1,), (1,)), ((), ())), preferred_element_type=F32)


def _dot_hp(a, b):
    return jnp.dot(a, b, preferred_element_type=F32, precision=lax.Precision.HIGHEST)


def _split_bf16(v):
    hi = v.astype(BF16)
    lo = (v - hi.astype(F32)).astype(BF16)
    return hi, lo


def _resident(shape):
    nd = len(shape)
    return pl.BlockSpec(shape, lambda *_: (0,) * nd, pipeline_mode=pl.Buffered(1))


def _mod_kernel(c_ref, w_ref, b_ref, o_ref):
    cv = c_ref[...]
    sc = (cv * jax.nn.sigmoid(cv)).astype(BF16)
    o_ref[...] = _dot(sc, w_ref[...].astype(BF16)) + b_ref[...]


def _modulation(c_all, ada_w, ada_b):
    depth, d, n = ada_w.shape
    rows = c_all.shape[0]
    tn = 512
    return pl.pallas_call(
        _mod_kernel,
        out_shape=jax.ShapeDtypeStruct((depth, rows, n), F32),
        grid=(depth, n // tn),
        in_specs=[
            pl.BlockSpec((rows, d), lambda i, j: (0, 0)),
            pl.BlockSpec((None, d, tn), lambda i, j: (i, 0, j)),
            pl.BlockSpec((None, 1, tn), lambda i, j: (i, 0, j)),
        ],
        out_specs=pl.BlockSpec((None, rows, tn), lambda i, j: (i, 0, j)),
        compiler_params=_params("arbitrary", "arbitrary"),
        name="modulation",
    )(c_all, ada_w, ada_b.reshape(depth, 1, n))


def _rms_mod(x, g, shift, scale):
    y = x * lax.rsqrt(jnp.mean(x * x, axis=-1, keepdims=True) + EPS)
    return (y * g) * (1 + scale) + shift


def _head_norm(h, g, e):
    hi, lo = _split_bf16(h * h)
    ss = _dot(hi, e) + _dot(lo, e)
    return (h * lax.rsqrt(ss * (1.0 / HEAD_DIM) + EPS)) * g


def _rope(t, cos, sin_a, sin_b):
    outs = []
    for c in range(t.shape[1] // LANES):
        tc = t[:, c * LANES:(c + 1) * LANES]
        outs.append(tc * cos + pltpu.roll(tc, LANES - 16, 1) * sin_a + pltpu.roll(tc, 16, 1) * sin_b)
    return outs[0] if len(outs) == 1 else jnp.concatenate(outs, axis=1)


def _proj_in_kernel(x_ref, shift_ref, scale_ref, g_ref, w_ref, gqa_ref, gka_ref, gqb_ref, gkb_ref, e_ref,
                    cos_ref, sa_ref, sb_ref,
                    qa_ref, ka_ref, va_ref, qb_ref, kb_ref, vb_ref, hy_ref, *, rope):
    xb = _rms_mod(x_ref[...], g_ref[...], shift_ref[...], scale_ref[...]).astype(BF16)
    qscale = HEAD_DIM ** -0.5

    def proj(lo, hi):
        return _dot(xb, w_ref[:, lo:hi])

    def maybe_rope(t):
        return _rope(t, cos_ref[...], sa_ref[...], sb_ref[...]) if rope else t

    qa = maybe_rope(_head_norm(proj(0, 256), gqa_ref[...], e_ref[...]))
    qa_ref[...] = (qa * qscale).astype(BF16)
    ka = maybe_rope(_head_norm(proj(256, 384), gka_ref[:, :128], e_ref[:128, :128]))
    ka_ref[...] = ka.astype(BF16)
    va_ref[...] = proj(384, 512).astype(BF16)
    qb = _head_norm(proj(512, 768), gqb_ref[...], e_ref[...])
    qb_ref[...] = (qb * qscale).astype(BF16)
    kb_ref[...] = _head_norm(proj(768, 1024), gkb_ref[...], e_ref[...]).astype(BF16)
    vb_ref[...] = proj(1024, 1280).astype(BF16)
    hy_ref[...] = proj(1280, 2816)


def _proj_in(x, shift, scale, g, w, gqa, gka, gqb, gkb, e, rope_tabs, *, rope):
    bn, l, d = x.shape
    n = w.shape[1]
    tm = 256
    row = lambda width: pl.BlockSpec((None, 1, width), lambda b, t: (b, 0, 0))
    const = lambda shape: pl.BlockSpec(shape, lambda b, t: (0,) * len(shape))
    tok = lambda width: pl.BlockSpec((None, tm, width), lambda b, t: (b, t, 0))
    tab = pl.BlockSpec((tm, LANES), lambda b, t: (t, 0))
    widths = (256, 128, 128, 256, 256, 256, n - 1280)
    dts = (BF16,) * 6 + (F32,)
    return pl.pallas_call(
        functools.partial(_proj_in_kernel, rope=rope),
        out_shape=[jax.ShapeDtypeStruct((bn, l, wd), dt) for wd, dt in zip(widths, dts)],
        grid=(bn, l // tm),
        in_specs=[tok(d), row(d), row(d), const((1, d)), _resident((d, n)),
                  const((1, 256)), const((1, 256)), const((1, 256)), const((1, 256)), const((256, 256)),
                  tab, tab, tab],
        out_specs=[tok(wd) for wd in widths],
        compiler_params=_params("parallel", "parallel"),
        name="proj_in_rope" if rope else "proj_in",
    )(x, shift, scale, g, w, gqa, gka, gqb, gkb, e, *rope_tabs)


def _softmax_pv(s_list, v_list, sink):
    m = s_list[0].max(axis=-1, keepdims=True)
    for s in s_list[1:]:
        m = jnp.maximum(m, s.max(axis=-1, keepdims=True))
    if sink is not None:
        m = jnp.maximum(m, sink)
    den = None
    out = None
    for s, v in zip(s_list, v_list):
        p = jnp.exp(s - m)
        ps = p.sum(axis=-1, keepdims=True)
        den = ps if den is None else den + ps
        o = _dot(p.astype(BF16), v)
        out = o if out is None else out + o
    if sink is not None:
        den = den + jnp.exp(sink - m)
    return out * (1.0 / den)


def _stack_heads(q):
    lo = lax.broadcasted_iota(jnp.int32, q.shape, 1) < HEAD_DIM
    zero = jnp.zeros_like(q)
    return jnp.concatenate([jnp.where(lo, q, zero), jnp.where(lo, zero, q)], axis=0)


def _unstack_heads(o):
    m = o.shape[0] // 2
    lo = lax.broadcasted_iota(jnp.int32, (m, LANES), 1) < HEAD_DIM
    return jnp.where(lo, o[:m], o[m:])


def _softmax_pv2(s_loc, s_ctx, v_loc, v_ctx, sink):
    m = jnp.maximum(s_loc.max(axis=-1, keepdims=True), s_ctx.max(axis=-1, keepdims=True))
    if sink is not None:
        m = jnp.maximum(m, sink)
    p_loc = jnp.exp(s_loc - m)
    p_ctx = jnp.exp(s_ctx - m)
    den = p_loc.sum(axis=-1, keepdims=True) + p_ctx.sum(axis=-1, keepdims=True)
    if sink is not None:
        den = den + jnp.exp(sink - m)
    out = _dot(p_loc.astype(BF16), v_loc) + _dot(p_ctx.astype(BF16), v_ctx)
    return out * (1.0 / den)


A_STEP_BLOCKS = 4
A_SPAN = 3 * A_BLOCK


def _attn_a_kernel(q_ref, k_ref, v_ref, kx_ref, vx_ref, mask_ref, sink_ref, o_ref):
    l = k_ref.shape[0]
    nb = l // A_BLOCK
    kx, vx = kx_ref[...], vx_ref[...]
    sink = sink_ref[...]
    for u in range(A_STEP_BLOCKS):
        n = pl.program_id(1) * A_STEP_BLOCKS + u
        start = pl.multiple_of(jnp.clip((n - 1) * A_BLOCK, 0, l - A_SPAN), A_BLOCK)
        pat = jnp.where(n == 0, 0, jnp.where(n == nb - 1, 2, 1))
        q = q_ref[u * A_BLOCK:(u + 1) * A_BLOCK, :]
        qs = jnp.concatenate([_stack_heads(q[:, :LANES]), _stack_heads(q[:, LANES:])], axis=0)
        k = k_ref[pl.ds(start, A_SPAN), :]
        v = v_ref[pl.ds(start, A_SPAN), :]
        s_loc = _dot_nt(qs, k).reshape(A_HEADS, A_BLOCK, A_SPAN) + mask_ref[pat][None]
        s_loc = s_loc.reshape(A_HEADS * A_BLOCK, A_SPAN)
        o = _softmax_pv2(s_loc, _dot_nt(qs, kx), v, vx, sink)
        half = 2 * A_BLOCK
        o_ref[u * A_BLOCK:(u + 1) * A_BLOCK, :] = jnp.concatenate(
            [_unstack_heads(o[:half]), _unstack_heads(o[half:])], axis=1).astype(o_ref.dtype)


def _attn_a_mask():
    i = np.arange(A_BLOCK)[:, None]
    j = np.arange(A_SPAN)[None, :]
    offs = (0, A_BLOCK, 2 * A_BLOCK)
    return jnp.asarray(np.stack([np.where(np.abs(j - i - o) <= A_BLOCK, 0.0, -np.inf) for o in offs]), F32)


def _attn_a(q, k, v, kx, vx, sink_rows):
    bn, l, qw = q.shape
    lc = kx.shape[1]
    kvw = k.shape[2]
    qs = A_STEP_BLOCKS * A_BLOCK
    mask = _attn_a_mask()
    seq = pl.BlockSpec((None, l, kvw), lambda b, s: (b, 0, 0))
    ctx = pl.BlockSpec((None, lc, kvw), lambda b, s: (b, 0, 0))
    qblk = pl.BlockSpec((None, qs, qw), lambda b, s: (b, s, 0))
    return pl.pallas_call(
        _attn_a_kernel,
        out_shape=jax.ShapeDtypeStruct((bn, l, qw), BF16),
        grid=(bn, l // qs),
        in_specs=[qblk, seq, seq, ctx, ctx, _resident(mask.shape), _resident(sink_rows.shape)],
        out_specs=qblk,
        compiler_params=_params("parallel", "arbitrary"),
        name="window_attn",
    )(q, k, v, kx, vx, mask, sink_rows)


def _dense_attn_kernel(sink_ref, q_ref, k_ref, v_ref, o_ref, *, kv_of, use_sink):
    for h, kv in enumerate(kv_of):
        hs = slice(h * HEAD_DIM, (h + 1) * HEAD_DIM)
        ks = slice(kv * HEAD_DIM, (kv + 1) * HEAD_DIM)
        s = _dot_nt(q_ref[:, hs], k_ref[:, ks])
        o = _softmax_pv([s], [v_ref[:, ks]], sink_ref[h] if use_sink else None)
        o_ref[:, hs] = o.astype(o_ref.dtype)


def _dense_attn(q, k, v, sink, *, kv_of, use_sink):
    bn, l, qw = q.shape
    kvw = k.shape[2]
    full = lambda wd: pl.BlockSpec((None, l, wd), lambda b: (b, 0, 0))
    return pl.pallas_call(
        functools.partial(_dense_attn_kernel, kv_of=kv_of, use_sink=use_sink),
        out_shape=jax.ShapeDtypeStruct((bn, l, qw), BF16),
        grid=(bn,),
        in_specs=[pl.BlockSpec(memory_space=pltpu.SMEM), full(qw), full(kvw), full(kvw)],
        out_specs=full(qw),
        compiler_params=_params("parallel"),
        name="ctx_attn_sink" if use_sink else "ctx_attn",
    )(sink, q, k, v)


def _rpb_kernel(r_ref, oh_ref, o_ref):
    r = r_ref[...]
    b1 = r.astype(BF16)
    r2 = r - b1.astype(F32)
    b2 = r2.astype(BF16)
    b3 = (r2 - b2.astype(F32)).astype(BF16)
    oh = oh_ref[...]
    o_ref[...] = (_dot(b1, oh) + _dot(b2, oh)) + _dot(b3, oh)


def _rpb_tables(rpb):
    h, nr, nc = rpb.shape
    col = np.arange(GRID_W)
    dc = np.clip(col[None, :] - col[:, None], 1 - NA_COLS, NA_COLS - 1) + NA_COLS - 1
    onehot = (dc.reshape(1, -1) == np.arange(nc)[:, None]).astype(np.float32)
    onehot = np.concatenate([onehot, np.zeros((32 - nc, GRID_W * GRID_W), np.float32)], axis=0)
    rows = 64
    r2 = jnp.zeros((rows, 32), F32).at[:h * nr, :nc].set(rpb.reshape(h * nr, nc))
    full = pl.pallas_call(
        _rpb_kernel,
        out_shape=jax.ShapeDtypeStruct((rows, GRID_W * GRID_W), F32),
        name="rpb_table",
    )(r2, jnp.asarray(onehot, BF16))
    full = full[:h * nr].reshape(h, nr, GRID_W, GRID_W)
    a = np.arange(B_GROUP)[:, None]
    kr = np.arange(B_SLAB)[None, :]
    dr = np.stack([kr - a + NA_ROWS - 1, kr - a + NA_ROWS // 2 - 1, kr - a + 0 * kr])
    lo = np.stack([0 * a + 0 * kr, a + 0 * kr, 0 * a + (B_SLAB - NA_ROWS) + 0 * kr])
    valid = (kr[None] >= lo) & (kr[None] < lo + NA_ROWS)
    col_start = np.clip(col - NA_COLS // 2, 0, GRID_W - NA_COLS)
    col_ok = (col[None, :] >= col_start[:, None]) & (col[None, :] < col_start[:, None] + NA_COLS)
    ok = valid[:, :, None, :, None] & col_ok[None, None, :, None, :]
    tbl = full[:, np.clip(dr, 0, nr - 1)]
    tbl = jnp.transpose(tbl, (1, 0, 2, 4, 3, 5))
    tbl = jnp.where(ok[:, None], tbl, -jnp.inf)
    return tbl.reshape(3, h // 2, 2 * B_GROUP * GRID_W, B_SLAB * GRID_W)


B_GROUP = 4
B_SLAB = B_GROUP + NA_ROWS - 1


def _attn_b_kernel(q_ref, k_ref, v_ref, kx_ref, vx_ref, tbl_ref, o_ref):
    g = pl.program_id(1)
    ng = pl.num_programs(1)
    rows = k_ref.shape[0] // GRID_W
    base = jnp.clip(g * B_GROUP - NA_ROWS // 2, 0, rows - B_SLAB)
    start = pl.multiple_of(base * GRID_W, GRID_W)
    pat = jnp.where(g == 0, 0, jnp.where(g == ng - 1, 2, 1))
    outs = []
    for t in range(B_HEADS // 2):
        ts = slice(t * LANES, (t + 1) * LANES)
        qs = _stack_heads(q_ref[:, ts])
        k = k_ref[pl.ds(start, B_SLAB * GRID_W), ts]
        v = v_ref[pl.ds(start, B_SLAB * GRID_W), ts]
        s_loc = _dot_nt(qs, k) + tbl_ref[pat, t]
        o = _softmax_pv2(s_loc, _dot_nt(qs, kx_ref[:, ts]), v, vx_ref[:, ts], None)
        outs.append(_unstack_heads(o))
    o_ref[...] = jnp.concatenate(outs, axis=1).astype(o_ref.dtype)


def _attn_b(q, k, v, kx, vx, tbl):
    bn, l, w = q.shape
    lc = kx.shape[1]
    gq = B_GROUP * GRID_W
    seq = pl.BlockSpec((None, l, w), lambda b, g: (b, 0, 0))
    ctx = pl.BlockSpec((None, lc, w), lambda b, g: (b, 0, 0))
    qblk = pl.BlockSpec((None, gq, w), lambda b, g: (b, g, 0))
    return pl.pallas_call(
        _attn_b_kernel,
        out_shape=jax.ShapeDtypeStruct((bn, l, w), BF16),
        grid=(bn, l // gq),
        in_specs=[qblk, seq, seq, ctx, ctx, _resident(tbl.shape)],
        out_specs=qblk,
        compiler_params=_params("parallel", "arbitrary"),
        name="nbr_attn",
    )(q, k, v, kx, vx, tbl)


def _dft_matrices(l, fh):
    k = jnp.arange(l, dtype=jnp.int32)[:, None]
    n = jnp.arange(l, dtype=jnp.int32)[None, :]
    ang = (((2 * k + 1) * n) % (4 * l)).astype(F32) * (math.pi / (2 * l))
    fre = jnp.cos(ang).reshape(l // fh, fh, l)
    fim = (-jnp.sin(ang)).reshape(l // fh, fh, l)
    f = jnp.concatenate([fre, fim], axis=1).reshape(2 * l, l)
    g = f.T * (1.0 / l)
    return f.astype(BF16), g.astype(BF16)


def _filter_kernel(z_ref, t_ref, w1_ref, b1_ref, fr_ref, w2_ref, b2_ref, w3_ref, dl_ref,
                   ah_ref, al_ref, sh_ref, sl_ref):
    fr = fr_ref[...]
    hid = jnp.sin(fr * (_dot_hp(z_ref[...], w1_ref[...]) + b1_ref[...]))
    hid = jnp.sin(fr * (_dot_hp(hid, w2_ref[...]) + b2_ref[...]))
    cw = dl_ref.shape[1]
    decay = jnp.exp(-t_ref[...] * dl_ref[...])
    kf = _dot_hp(hid, w3_ref[:, :cw]) * decay
    kb = _dot_hp(hid, w3_ref[:, cw:]) * decay
    first = lax.broadcasted_iota(jnp.int32, kb.shape, 0) == 0
    kb = jnp.where(first, 0.0, kb)
    zn = jnp.sum(jnp.abs(kf), axis=0, keepdims=True) + jnp.sum(jnp.abs(kb), axis=0, keepdims=True)
    ah, al = _split_bf16((kf + kb) / zn)
    sh, sl = _split_bf16((kf - kb) / zn)
    ah_ref[...] = ah
    al_ref[...] = al
    sh_ref[...] = sh
    sl_ref[...] = sl


def _filters(z, t, w1, b1, freq, w2, b2, w3, deltas):
    l = z.shape[0]
    cw = deltas.shape[1]
    fw = w2.shape[0]
    const = lambda shape: pl.BlockSpec(shape, lambda o: (0,) * len(shape))
    out = jax.ShapeDtypeStruct((l, C_ORDER * cw), BF16)
    oblk = pl.BlockSpec((l, cw), lambda o: (0, o))
    return pl.pallas_call(
        _filter_kernel,
        out_shape=[out] * 4,
        grid=(C_ORDER,),
        in_specs=[const(z.shape), const((l, 1)), const(w1.shape), const((1, fw)), const((1, fw)),
                  const((fw, fw)), const((1, fw)), pl.BlockSpec((fw, C_DIRS * cw), lambda o: (0, o)),
                  const((1, cw))],
        out_specs=[oblk] * 4,
        compiler_params=_params("arbitrary"),
        name="hyena_filter",
    )(z, t, w1, b1, freq, w2, b2, w3, deltas)


def _spectrum_kernel(f_ref, ah_ref, al_ref, sh_ref, sl_ref, kre_ref, kim_ref):
    fh = kre_ref.shape[0]
    fre = f_ref[:fh, :]
    fim = f_ref[fh:, :]
    kre_ref[...] = _dot(fre, ah_ref[...]) + _dot(fre, al_ref[...])
    kim_ref[...] = _dot(fim, sh_ref[...]) + _dot(fim, sl_ref[...])


def _spectrum(f, ah, al, sh, sl, fh):
    l, n = ah.shape
    const = pl.BlockSpec((l, n), lambda j: (0, 0), pipeline_mode=pl.Buffered(1))
    out = jax.ShapeDtypeStruct((l, n), F32)
    oblk = pl.BlockSpec((fh, n), lambda j: (j, 0))
    return pl.pallas_call(
        _spectrum_kernel,
        out_shape=[out, out],
        grid=(l // fh,),
        in_specs=[pl.BlockSpec((2 * fh, l), lambda j: (j, 0)), const, const, const, const],
        out_specs=[oblk, oblk],
        compiler_params=_params("arbitrary"),
        name="hyena_spectrum",
    )(f, ah, al, sh, sl)


def _short_conv(u, w_ref, b_ref):
    n = u.shape[0]
    row = lax.broadcasted_iota(jnp.int32, u.shape, 0)
    prev = jnp.where(row == 0, 0.0, pltpu.roll(u, 1, 0))
    nxt = jnp.where(row == n - 1, 0.0, pltpu.roll(u, n - 1, 0))
    return prev * w_ref[0:1, :] + u * w_ref[1:2, :] + nxt * w_ref[2:3, :] + b_ref[...]


def _longconv_kernel(u_ref, g_ref, cwu_ref, cbu_ref, cwg_ref, cbg_ref, f_ref, gm_ref, kre_ref, kim_ref, d_ref,
                     o_ref, ub_ref, acc_ref, *, conv_u):
    j = pl.program_id(1)
    fh = kre_ref.shape[0]

    def load_u():
        return _short_conv(u_ref[...], cwu_ref, cbu_ref) if conv_u else u_ref[...]

    @pl.when(j == 0)
    def _():
        ub_ref[...] = load_u().astype(BF16)
        acc_ref[...] = jnp.zeros_like(acc_ref)

    spec = _dot(f_ref[...], ub_ref[...])
    vre, vim = spec[:fh], spec[fh:]
    kre, kim = kre_ref[...], kim_ref[...]
    y = jnp.concatenate([vre * kre - vim * kim, vre * kim + vim * kre], axis=0).astype(BF16)
    acc_ref[...] += _dot(gm_ref[...], y)

    @pl.when(j == pl.num_programs(1) - 1)
    def _():
        gate = _short_conv(g_ref[...], cwg_ref, cbg_ref)
        o_ref[...] = (gate * (acc_ref[...] + load_u() * d_ref[...])).astype(o_ref.dtype)


def _longconv(u_arr, u_blk, g_arr, g_blk, conv_w, conv_b, f, gm, kre, kim, k_blk, d, *, conv_u, out_dtype, fh):
    bn, l, _ = u_arr.shape
    cw = d.shape[1]
    ub = u_blk if conv_u else 0
    tok = lambda blk: pl.BlockSpec((None, l, cw), lambda b, j: (b, 0, blk))
    cpar = lambda rows, blk: pl.BlockSpec((rows, cw), lambda b, j: (0, blk))
    return pl.pallas_call(
        functools.partial(_longconv_kernel, conv_u=conv_u),
        out_shape=jax.ShapeDtypeStruct((bn, l, cw), out_dtype),
        grid=(bn, l // fh),
        in_specs=[tok(u_blk), tok(g_blk), cpar(3, ub), cpar(1, ub), cpar(3, g_blk), cpar(1, g_blk),
                  pl.BlockSpec((2 * fh, l), lambda b, j: (j, 0)),
                  pl.BlockSpec((l, 2 * fh), lambda b, j: (0, j)),
                  pl.BlockSpec((fh, cw), lambda b, j: (j, k_blk)),
                  pl.BlockSpec((fh, cw), lambda b, j: (j, k_blk)),
                  pl.BlockSpec((1, cw), lambda b, j: (0, 0))],
        out_specs=tok(0),
        scratch_shapes=[pltpu.VMEM((l, cw), BF16), pltpu.VMEM((l, cw), F32)],
        compiler_params=_params("parallel", "arbitrary"),
        name="hyena_longconv",
    )(u_arr, g_arr, conv_w, conv_b, conv_w, conv_b, f, gm, kre, kim, d)


def _hyena_consts(l, cw):
    t = np.linspace(0.0, 1.0, l, dtype=np.float32)[:, None]
    w = (2.0 * math.pi * np.arange(l, dtype=np.float32)[:, None] / l).astype(np.float32)
    fq = np.linspace(1e-4, C_BANDS - 1, C_BANDS, dtype=np.float32)[None, :]
    wf = jnp.asarray(w) * jnp.asarray(fq)
    z = jnp.concatenate([jnp.asarray(t), jnp.cos(wf), -jnp.sin(wf)], axis=-1)
    z = jnp.pad(z, ((0, 0), (0, C_FILTER_WIDTH - z.shape[1])))
    deltas = np.abs(np.linspace(HYENA_MIN_DECAY, HYENA_MAX_DECAY, cw, dtype=np.float32))[None, :]
    return z, jnp.asarray(t), jnp.asarray(deltas)


def _hyena(hy, hconst, f, gm, fh, conv_w, conv_b, w1p, b1, freq, w2, b2, w3, dbias):
    z, t, deltas = hconst
    cw = deltas.shape[1]
    ah, al, sh, sl = _filters(z, t, w1p, b1, freq, w2, b2, w3, deltas)
    kre, kim = _spectrum(f, ah, al, sh, sl, fh)
    zz = _longconv(hy, 0, hy, 1, conv_w, conv_b, f, gm, kre, kim, 0, dbias[0:1],
                   conv_u=True, out_dtype=F32, fh=fh)
    return _longconv(zz, 0, hy, 2, conv_w, conv_b, f, gm, kre, kim, 1, dbias[1:2],
                     conv_u=False, out_dtype=BF16, fh=fh)


def _mix_ffn_kernel(x_ref, a_ref, b_ref, c_ref, wo_ref, g1_ref, gn_ref, sh_ref, sc_ref, g2_ref,
                    wg_ref, wu_ref, wd_ref, o_ref, *, fc):
    wa, wb = a_ref.shape[1], b_ref.shape[1]
    mix = (_dot(a_ref[...], wo_ref[:wa, :]) + _dot(b_ref[...], wo_ref[wa:wa + wb, :])
           + _dot(c_ref[...], wo_ref[wa + wb:, :]))
    x1 = x_ref[...] + g1_ref[...] * mix
    xb = _rms_mod(x1, gn_ref[...], sh_ref[...], sc_ref[...]).astype(BF16)
    acc = None
    for c0 in range(0, wg_ref.shape[1], fc):
        hg = _dot(xb, wg_ref[:, c0:c0 + fc])
        hu = _dot(xb, wu_ref[:, c0:c0 + fc])
        act = ((hg * jax.nn.sigmoid(hg)) * hu).astype(BF16)
        part = _dot(act, wd_ref[c0:c0 + fc, :])
        acc = part if acc is None else acc + part
    o_ref[...] = x1 + g2_ref[...] * acc


def _mix_ffn(x, a, b, c, wo, g1, gn, sh2, sc2, g2, wg, wu, wd):
    bn, l, d = x.shape
    tm = 256
    row = pl.BlockSpec((None, 1, d), lambda bb, t: (bb, 0, 0))
    tok = lambda width: pl.BlockSpec((None, tm, width), lambda bb, t: (bb, t, 0))
    return pl.pallas_call(
        functools.partial(_mix_ffn_kernel, fc=256),
        out_shape=jax.ShapeDtypeStruct((bn, l, d), F32),
        grid=(bn, l // tm),
        in_specs=[tok(d), tok(a.shape[2]), tok(b.shape[2]), tok(c.shape[2]), _resident(wo.shape),
                  row, pl.BlockSpec((1, d), lambda bb, t: (0, 0)), row, row, row,
                  _resident(wg.shape), _resident(wu.shape), _resident(wd.shape)],
        out_specs=tok(d),
        compiler_params=_params("parallel", "parallel"),
        name="mix_ffn",
    )(x, a, b, c, wo, g1, gn, sh2, sc2, g2, wg, wu, wd)


def _rope_tables(l):
    half = HEAD_DIM // 2
    nfreq = half // 2
    inv = ROPE_BASE ** (-jnp.arange(nfreq, dtype=F32) / nfreq)
    pos = jnp.arange(l)
    rows, cols = pos // GRID_W, pos % GRID_W
    ang = jnp.concatenate([rows.astype(F32)[:, None] * inv[None, :]] * 2
                          + [cols.astype(F32)[:, None] * inv[None, :]] * 2, axis=-1)
    first = (np.arange(HEAD_DIM) % half) < nfreq
    cos, sin = jnp.cos(ang), jnp.sin(ang)
    sin_a = jnp.where(first[None, :], -sin, 0.0)
    sin_b = jnp.where(first[None, :], 0.0, sin)
    reps = LANES // HEAD_DIM
    return tuple(jnp.tile(tb, (1, reps)) for tb in (cos, sin_a, sin_b))


def kernel(x, c, ctx, c_ctx, ada_w, ada_b, norm1_g, norm2_g, w_in, qnorm_a, knorm_a, sink_a, qnorm_b, knorm_b,
           rpb_b, conv_w, conv_b, filt_w1, filt_b1, filt_freq, filt_w2, filt_b2, filt_w3, hyena_bias, w_out,
           ffn_w_gate, ffn_w_up, ffn_w_down):
    bn, l, d = x.shape
    lc = ctx.shape[1]
    depth = ada_w.shape[0]
    cw = hyena_bias.shape[2]
    assert l % (A_STEP_BLOCKS * A_BLOCK) == 0 and l >= 3 * A_SPAN and lc % 256 == 0
    assert l % (B_GROUP * GRID_W) == 0 and l // GRID_W >= 3 * B_GROUP

    mod_rows = 8 * (-(-(bn + 1) // 8))
    c_all = jnp.zeros((mod_rows, d), F32).at[:bn].set(c).at[bn].set(c_ctx)
    mod = _modulation(c_all, ada_w, ada_b)

    a_order = (0, 2, 1, 3)
    a_kv = tuple(hh // (A_HEADS // A_KV_HEADS) for hh in a_order)
    heads_a = lambda t, axis: [lax.slice_in_dim(t, hh * HEAD_DIM, (hh + 1) * HEAD_DIM, axis=axis) for hh in a_order]
    qa_w = A_HEADS * HEAD_DIM
    w_in_b = jnp.concatenate(heads_a(w_in, 2) + [w_in[:, :, qa_w:]], axis=2).astype(BF16)
    w_out_b = jnp.concatenate(heads_a(w_out, 1) + [w_out[:, qa_w:]], axis=1).astype(BF16)
    sink_p = jnp.stack([sink_a[:, hh] for hh in a_order], axis=1)
    wg_b, wu_b, wd_b = ffn_w_gate.astype(BF16), ffn_w_up.astype(BF16), ffn_w_down.astype(BF16)

    lane = np.arange(256)
    e_heads = jnp.asarray((lane[:, None] // HEAD_DIM) == (lane[None, :] // HEAD_DIM), BF16)
    tile4 = lambda g: jnp.tile(g.reshape(1, HEAD_DIM), (1, 256 // HEAD_DIM))
    rope_lat = _rope_tables(l)
    rope_ctx = tuple(tb[:lc] for tb in rope_lat)

    fh_lat, fh_ctx = 256, min(256, lc)
    f_lat, g_lat = _dft_matrices(l, fh_lat)
    f_ctx, g_ctx = _dft_matrices(lc, fh_ctx)
    hc_lat = _hyena_consts(l, cw)
    hc_ctx = _hyena_consts(lc, cw)

    xc = ctx
    for i in range(depth):
        last = i == depth - 1
        m = mod[i]
        part = lambda rows, k: rows[:, None, k * d:(k + 1) * d]
        lat = [part(m[:bn], k) for k in range(6)]
        cx = [jnp.broadcast_to(part(m[bn:bn + 1], k), (bn, 1, d)) for k in range(6)]
        g1n = norm1_g[i].reshape(1, d)
        g2n = norm2_g[i].reshape(1, d)
        gqa, gka, gqb, gkb = tile4(qnorm_a[i]), tile4(knorm_a[i]), tile4(qnorm_b[i]), tile4(knorm_b[i])

        qa, ka, va, qb, kb, vb, hy = _proj_in(x, lat[0], lat[1], g1n, w_in_b[i], gqa, gka, gqb, gkb, e_heads,
                                              rope_lat, rope=True)
        qa_c, ka_c, va_c, qb_c, kb_c, vb_c, hy_c = _proj_in(xc, cx[0], cx[1], g1n, w_in_b[i], gqa, gka, gqb, gkb,
                                                            e_heads, rope_ctx, rope=False)
        w1p = jnp.pad(filt_w1[i], ((0, C_FILTER_WIDTH - filt_w1.shape[1]), (0, 0)))
        hy_args = (conv_w[i], conv_b[i].reshape(1, -1), w1p, filt_b1[i].reshape(1, -1),
                   filt_freq[i].reshape(1, -1), filt_w2[i], filt_b2[i].reshape(1, -1), filt_w3[i], hyena_bias[i])

        sink_rows = jnp.repeat(sink_p[i], A_BLOCK).reshape(A_HEADS * A_BLOCK, 1)
        out_a = _attn_a(qa, ka, va, ka_c, va_c, sink_rows)
        out_b = _attn_b(qb, kb, vb, kb_c, vb_c, _rpb_tables(rpb_b[i]))
        out_c = _hyena(hy, hc_lat, f_lat, g_lat, fh_lat, *hy_args)
        x = _mix_ffn(x, out_a, out_b, out_c, w_out_b[i], lat[2], g2n, lat[3], lat[4], lat[5],
                     wg_b[i], wu_b[i], wd_b[i])

        if not last:
            oa_c = _dense_attn(qa_c, ka_c, va_c, sink_p[i], kv_of=a_kv, use_sink=True)
            ob_c = _dense_attn(qb_c, kb_c, vb_c, sink_p[i], kv_of=tuple(range(B_HEADS)), use_sink=False)
            oc_c = _hyena(hy_c, hc_ctx, f_ctx, g_ctx, fh_ctx, *hy_args)
            xc = _mix_ffn(xc, oa_c, ob_c, oc_c, w_out_b[i], cx[2], g2n, cx[3], cx[4], cx[5],
                          wg_b[i], wu_b[i], wd_b[i])
    return x
```

```python
import functools
import math

import jax
import jax.numpy as jnp
import numpy as np
from jax import lax
from jax.experimental import pallas as pl
from jax.experimental.pallas import tpu as pltpu

F32 = jnp.float32
BF16 = jnp.bfloat16

GRID_W = 64
HEAD_DIM = 64
A_HEADS = 4
A_KV_HEADS = 2
A_BLOCK = 128
B_HEADS = 4
NA_ROWS = 8
NA_COLS = 16
C_ORDER = 2
C_DIRS = 2
C_FILTER_WIDTH = 64
C_BANDS = 16
ROPE_BASE = 10000.0
EPS = 1e-6
NEG_INF = -1e30
HYENA_MIN_DECAY = math.log(1e-2) / 1.5
HYENA_MAX_DECAY = math.log(1e-2) / 0.3

V7X_VMEM_BYTES = 64 * 1024 * 1024
VMEM_LIMIT = V7X_VMEM_BYTES - 8 * 1024 * 1024
LANES = 128
TOKEN_TILE = 512


def _params(*sem):
    return pltpu.CompilerParams(dimension_semantics=sem, vmem_limit_bytes=VMEM_LIMIT)


def _dot(a, b):
    return jnp.dot(a, b, preferred_element_type=F32)


def _dot_nt(a, b):
    return lax.dot_general(a, b, (((1,), (1,)), ((), ())), preferred_element_type=F32)


def _dot_hp(a, b):
    return jnp.dot(a, b, preferred_element_type=F32, precision=lax.Precision.HIGHEST)


def _split_bf16(v):
    hi = v.astype(BF16)
    lo = (v - hi.astype(F32)).astype(BF16)
    return hi, lo


def _resident(shape):
    nd = len(shape)
    return pl.BlockSpec(shape, lambda *_: (0,) * nd, pipeline_mode=pl.Buffered(1))


def _mod_kernel(c_ref, w_ref, b_ref, o_ref):
    cv = c_ref[...]
    sc = (cv * jax.nn.sigmoid(cv)).astype(BF16)
    o_ref[...] = _dot(sc, w_ref[...].astype(BF16)) + b_ref[...]


def _modulation(c_all, ada_w, ada_b):
    depth, d, n = ada_w.shape
    rows = c_all.shape[0]
    tn = 512
    return pl.pallas_call(
        _mod_kernel,
        out_shape=jax.ShapeDtypeStruct((depth, rows, n), F32),
        grid=(depth, n // tn),
        in_specs=[
            pl.BlockSpec((rows, d), lambda i, j: (0, 0)),
            pl.BlockSpec((None, d, tn), lambda i, j: (i, 0, j)),
            pl.BlockSpec((None, 1, tn), lambda i, j: (i, 0, j)),
        ],
        out_specs=pl.BlockSpec((None, rows, tn), lambda i, j: (i, 0, j)),
        compiler_params=_params("arbitrary", "arbitrary"),
        name="modulation",
    )(c_all, ada_w, ada_b.reshape(depth, 1, n))


def _rms_mod(x, g, shift, scale):
    y = x * lax.rsqrt(jnp.mean(x * x, axis=-1, keepdims=True) + EPS)
    return (y * g) * (1 + scale) + shift


def _head_norm(h, g, e):
    hi, lo = _split_bf16(h * h)
    ss = _dot(hi, e) + _dot(lo, e)
    return (h * lax.rsqrt(ss * (1.0 / HEAD_DIM) + EPS)) * g


def _rope(t, cos, sin_a, sin_b):
    outs = []
    for c in range(t.shape[1] // LANES):
        tc = t[:, c * LANES:(c + 1) * LANES]
        outs.append(tc * cos + pltpu.roll(tc, LANES - 16, 1) * sin_a + pltpu.roll(tc, 16, 1) * sin_b)
    return outs[0] if len(outs) == 1 else jnp.concatenate(outs, axis=1)


def _proj_in_kernel(x_ref, shift_ref, scale_ref, g_ref, w_ref, gqa_ref, gka_ref, gqb_ref, gkb_ref, e_ref,
                    cos_ref, sa_ref, sb_ref,
                    qa_ref, ka_ref, va_ref, qb_ref, kb_ref, vb_ref, hy_ref, *, rope):
    xb = _rms_mod(x_ref[...], g_ref[...], shift_ref[...], scale_ref[...]).astype(BF16)
    qscale = HEAD_DIM ** -0.5

    def proj(lo, hi):
        return _dot(xb, w_ref[:, lo:hi])

    def maybe_rope(t):
        return _rope(t, cos_ref[...], sa_ref[...], sb_ref[...]) if rope else t

    qa = maybe_rope(_head_norm(proj(0, 256), gqa_ref[...], e_ref[...]))
    qa_ref[...] = (qa * qscale).astype(BF16)
    ka = maybe_rope(_head_norm(proj(256, 384), gka_ref[:, :128], e_ref[:128, :128]))
    ka_ref[...] = ka.astype(BF16)
    va_ref[...] = proj(384, 512).astype(BF16)
    qb = _head_norm(proj(512, 768), gqb_ref[...], e_ref[...])
    qb_ref[...] = (qb * qscale).astype(BF16)
    kb_ref[...] = _head_norm(proj(768, 1024), gkb_ref[...], e_ref[...]).astype(BF16)
    vb_ref[...] = proj(1024, 1280).astype(BF16)
    hy_ref[...] = proj(1280, 2816)


def _proj_in(x, shift, scale, g, w, gqa, gka, gqb, gkb, e, rope_tabs, *, rope):
    bn, l, d = x.shape
    n = w.shape[1]
    tm = min(TOKEN_TILE, l)
    row = lambda width: pl.BlockSpec((None, 1, width), lambda b, t: (b, 0, 0))
    const = lambda shape: pl.BlockSpec(shape, lambda b, t: (0,) * len(shape))
    tok = lambda width: pl.BlockSpec((None, tm, width), lambda b, t: (b, t, 0))
    tab = pl.BlockSpec((tm, LANES), lambda b, t: (t, 0))
    widths = (256, 128, 128, 256, 256, 256, n - 1280)
    dts = (BF16,) * 6 + (F32,)
    return pl.pallas_call(
        functools.partial(_proj_in_kernel, rope=rope),
        out_shape=[jax.ShapeDtypeStruct((bn, l, wd), dt) for wd, dt in zip(widths, dts)],
        grid=(bn, l // tm),
        in_specs=[tok(d), row(d), row(d), const((1, d)), _resident((d, n)),
                  const((1, 256)), const((1, 256)), const((1, 256)), const((1, 256)), const((256, 256)),
                  tab, tab, tab],
        out_specs=[tok(wd) for wd in widths],
        compiler_params=_params("parallel", "parallel"),
        name="proj_in_rope" if rope else "proj_in",
    )(x, shift, scale, g, w, gqa, gka, gqb, gkb, e, *rope_tabs)


def _softmax_pv(s_list, v_list, sink):
    m = s_list[0].max(axis=-1, keepdims=True)
    for s in s_list[1:]:
        m = jnp.maximum(m, s.max(axis=-1, keepdims=True))
    if sink is not None:
        m = jnp.maximum(m, sink)
    den = None
    out = None
    for s, v in zip(s_list, v_list):
        p = jnp.exp(s - m)
        ps = p.sum(axis=-1, keepdims=True)
        den = ps if den is None else den + ps
        o = _dot(p.astype(BF16), v)
        out = o if out is None else out + o
    if sink is not None:
        den = den + jnp.exp(sink - m)
    return out * (1.0 / den)


def _stack_heads(q):
    lo = lax.broadcasted_iota(jnp.int32, q.shape, 1) < HEAD_DIM
    zero = jnp.zeros_like(q)
    return jnp.concatenate([jnp.where(lo, q, zero), jnp.where(lo, zero, q)], axis=0)


def _unstack_heads(o):
    m = o.shape[0] // 2
    lo = lax.broadcasted_iota(jnp.int32, (m, LANES), 1) < HEAD_DIM
    return jnp.where(lo, o[:m], o[m:])


SOFTMAX_CHUNK = 32


def _softmax_keys(s_ref, p_ref, nloc, add_loc, sink):
    nk = s_ref.shape[0]
    ch = SOFTMAX_CHUNK

    def scores(r0):
        s = s_ref[r0:r0 + ch, :]
        return s + add_loc(r0) if r0 < nloc else s

    macc = scores(0)
    for r0 in range(ch, nk, ch):
        macc = jnp.maximum(macc, scores(r0))
    m = macc.max(axis=0, keepdims=True)
    if sink is not None:
        m = jnp.maximum(m, sink)
    sacc = None
    for r0 in range(0, nk, ch):
        p = jnp.exp(scores(r0) - m)
        sacc = p if sacc is None else sacc + p
        p_ref[r0:r0 + ch, :] = p.astype(BF16)
    den = sacc.sum(axis=0, keepdims=True)
    if sink is not None:
        den = den + jnp.exp(sink - m)
    return 1.0 / den


A_STEP_BLOCKS = 4
A_SPAN = 3 * A_BLOCK


def _attn_a_kernel(q_ref, k_ref, vt_ref, kx_ref, vxt_ref, mask_ref, sink_ref, o_ref, s_ref, p_ref):
    l = k_ref.shape[0]
    nb = l // A_BLOCK
    hd = HEAD_DIM
    for u in range(A_STEP_BLOCKS):
        n = pl.program_id(1) * A_STEP_BLOCKS + u
        tile0 = jnp.clip(n - 1, 0, nb - 3)
        start = pl.multiple_of(tile0 * A_BLOCK, A_BLOCK)
        pat = jnp.where(n == 0, 0, jnp.where(n == nb - 1, 2, 1))
        q = q_ref[u * A_BLOCK:(u + 1) * A_BLOCK, :]
        qs = jnp.concatenate([_stack_heads(q[:, :LANES]), _stack_heads(q[:, LANES:])], axis=0)
        sb, pb = s_ref.at[u % 2], p_ref.at[u % 2]
        sb[:A_SPAN, :] = _dot_nt(k_ref[pl.ds(start, A_SPAN), :], qs)
        sb[A_SPAN:, :] = _dot_nt(kx_ref[...], qs)
        r = _softmax_keys(sb, pb, A_SPAN, lambda r0: mask_ref[pat, r0:r0 + SOFTMAX_CHUNK, :], sink_ref[...])
        vt = jnp.concatenate([vt_ref[tile0 + i] for i in range(3)], axis=1)
        ot = (_dot(vt, pb[:A_SPAN, :]) + _dot(vxt_ref[...], pb[A_SPAN:, :])) * r
        ot = jnp.concatenate([ot[(i % 2) * hd:(i % 2 + 1) * hd, i * A_BLOCK:(i + 1) * A_BLOCK]
                              for i in range(A_HEADS)], axis=0)
        o_ref[u * A_BLOCK:(u + 1) * A_BLOCK, :] = ot.T.astype(o_ref.dtype)


def _attn_a_mask():
    i = np.arange(A_BLOCK)[None, :]
    j = np.arange(A_SPAN)[:, None]
    offs = (0, A_BLOCK, 2 * A_BLOCK)
    m = np.stack([np.where(np.abs(j - i - o) <= A_BLOCK, 0.0, -np.inf) for o in offs])
    return jnp.asarray(np.tile(m, (1, 1, A_HEADS)), F32)


def _attn_a(q, k, v, kx, vx, sink_row):
    bn, l, qw = q.shape
    lc = kx.shape[1]
    kvw = k.shape[2]
    nb = l // A_BLOCK
    qs = A_STEP_BLOCKS * A_BLOCK
    mask = _attn_a_mask()
    vt = jnp.swapaxes(v.reshape(bn, nb, A_BLOCK, kvw), 2, 3)
    vxt = jnp.swapaxes(vx, 1, 2)
    seq = pl.BlockSpec((None, l, kvw), lambda b, s: (b, 0, 0))
    qblk = pl.BlockSpec((None, qs, qw), lambda b, s: (b, s, 0))
    nq = A_HEADS * A_BLOCK
    return pl.pallas_call(
        _attn_a_kernel,
        out_shape=jax.ShapeDtypeStruct((bn, l, qw), BF16),
        grid=(bn, l // qs),
        in_specs=[qblk, seq, pl.BlockSpec((None, nb, kvw, A_BLOCK), lambda b, s: (b, 0, 0, 0)),
                  pl.BlockSpec((None, lc, kvw), lambda b, s: (b, 0, 0)),
                  pl.BlockSpec((None, kvw, lc), lambda b, s: (b, 0, 0)),
                  _resident(mask.shape), _resident(sink_row.shape)],
        out_specs=qblk,
        scratch_shapes=[pltpu.VMEM((2, A_SPAN + lc, nq), F32), pltpu.VMEM((2, A_SPAN + lc, nq), BF16)],
        compiler_params=_params("parallel", "arbitrary"),
        name="window_attn",
    )(q, k, vt, kx, vxt, mask, sink_row)


def _dense_attn_kernel(sink_ref, q_ref, k_ref, v_ref, o_ref, *, kv_of, use_sink):
    for h, kv in enumerate(kv_of):
        hs = slice(h * HEAD_DIM, (h + 1) * HEAD_DIM)
        ks = slice(kv * HEAD_DIM, (kv + 1) * HEAD_DIM)
        s = _dot_nt(q_ref[:, hs], k_ref[:, ks])
        o = _softmax_pv([s], [v_ref[:, ks]], sink_ref[h] if use_sink else None)
        o_ref[:, hs] = o.astype(o_ref.dtype)


def _dense_attn(q, k, v, sink, *, kv_of, use_sink):
    bn, l, qw = q.shape
    kvw = k.shape[2]
    full = lambda wd: pl.BlockSpec((None, l, wd), lambda b: (b, 0, 0))
    return pl.pallas_call(
        functools.partial(_dense_attn_kernel, kv_of=kv_of, use_sink=use_sink),
        out_shape=jax.ShapeDtypeStruct((bn, l, qw), BF16),
        grid=(bn,),
        in_specs=[pl.BlockSpec(memory_space=pltpu.SMEM), full(qw), full(kvw), full(kvw)],
        out_specs=full(qw),
        compiler_params=_params("parallel"),
        name="ctx_attn_sink" if use_sink else "ctx_attn",
    )(sink, q, k, v)


def _rpb_kernel(r_ref, oh_ref, o_ref):
    r = r_ref[...]
    b1 = r.astype(BF16)
    r2 = r - b1.astype(F32)
    b2 = r2.astype(BF16)
    b3 = (r2 - b2.astype(F32)).astype(BF16)
    oh = oh_ref[...]
    o_ref[...] = (_dot(b1, oh) + _dot(b2, oh)) + _dot(b3, oh)


def _rpb_tables(rpb):
    h, nr, nc = rpb.shape
    col = np.arange(GRID_W)
    dc = np.clip(col[None, :] - col[:, None], 1 - NA_COLS, NA_COLS - 1) + NA_COLS - 1
    onehot = (dc.reshape(1, -1) == np.arange(nc)[:, None]).astype(np.float32)
    onehot = np.concatenate([onehot, np.zeros((32 - nc, GRID_W * GRID_W), np.float32)], axis=0)
    rows = 64
    r2 = jnp.zeros((rows, 32), F32).at[:h * nr, :nc].set(rpb.reshape(h * nr, nc))
    full = pl.pallas_call(
        _rpb_kernel,
        out_shape=jax.ShapeDtypeStruct((rows, GRID_W * GRID_W), F32),
        name="rpb_table",
    )(r2, jnp.asarray(onehot, BF16))
    full = full[:h * nr].reshape(h, nr, GRID_W, GRID_W)
    a = np.arange(B_GROUP)[:, None]
    kr = np.arange(B_SLAB)[None, :]
    dr = np.stack([kr - a + NA_ROWS - 1, kr - a + NA_ROWS // 2 - 1, kr - a + (B_GROUP + NA_ROWS - 1 - B_SLAB)])
    lo = np.stack([0 * a + 0 * kr, a + 0 * kr, 0 * a + (B_SLAB - NA_ROWS) + 0 * kr])
    valid = (kr[None] >= lo) & (kr[None] < lo + NA_ROWS)
    col_start = np.clip(col - NA_COLS // 2, 0, GRID_W - NA_COLS)
    col_ok = (col[None, :] >= col_start[:, None]) & (col[None, :] < col_start[:, None] + NA_COLS)
    ok = valid[:, :, None, :, None] & col_ok[None, None, :, None, :]
    tbl = full[:, np.clip(dr, 0, nr - 1)]
    tbl = jnp.transpose(tbl, (1, 0, 2, 4, 3, 5))
    tbl = jnp.where(ok[:, None], tbl, -jnp.inf)
    tbl = tbl.reshape(3, h // 2, 2, B_GROUP, GRID_W, B_SLAB, GRID_W)
    tbl = jnp.transpose(tbl, (0, 1, 5, 6, 2, 3, 4))
    return tbl.reshape(3, h // 2, B_SLAB * GRID_W, 2 * B_GROUP * GRID_W)


B_GROUP = 4
B_SLAB = 12


def _attn_b_kernel(q_ref, k_ref, vt_ref, kx_ref, vxt_ref, tbl_ref, o_ref, s_ref, p_ref):
    g = pl.program_id(1)
    ng = pl.num_programs(1)
    rows = k_ref.shape[0] // GRID_W
    nloc = B_SLAB * GRID_W
    gq = B_GROUP * GRID_W
    base = jnp.clip(g * B_GROUP - NA_ROWS // 2, 0, rows - B_SLAB)
    start = pl.multiple_of(base * GRID_W, LANES)
    tile0 = base // (LANES // GRID_W)
    pat = jnp.where(g == 0, 0, jnp.where(g == ng - 1, 2, 1))
    for t in range(B_HEADS // 2):
        ts = slice(t * LANES, (t + 1) * LANES)
        qs = _stack_heads(q_ref[:, ts])
        sb, pb = s_ref.at[t], p_ref.at[t]
        sb[:nloc, :] = _dot_nt(k_ref[pl.ds(start, nloc), ts], qs)
        sb[nloc:, :] = _dot_nt(kx_ref[:, ts], qs)
        r = _softmax_keys(sb, pb, nloc, lambda r0: tbl_ref[pat, t, r0:r0 + SOFTMAX_CHUNK, :], None)
        vt = jnp.concatenate([vt_ref[tile0 + i, ts, :] for i in range(nloc // LANES)], axis=1)
        ot = (_dot(vt, pb[:nloc, :]) + _dot(vxt_ref[ts, :], pb[nloc:, :])) * r
        ot = jnp.concatenate([ot[:HEAD_DIM, :gq], ot[HEAD_DIM:, gq:]], axis=0)
        o_ref[:, ts] = ot.T.astype(o_ref.dtype)


def _attn_b(q, k, v, kx, vx, tbl):
    bn, l, w = q.shape
    lc = kx.shape[1]
    gq = B_GROUP * GRID_W
    nk = B_SLAB * GRID_W + lc
    vt = jnp.swapaxes(v.reshape(bn, l // LANES, LANES, w), 2, 3)
    vxt = jnp.swapaxes(vx, 1, 2)
    seq = pl.BlockSpec((None, l, w), lambda b, g: (b, 0, 0))
    qblk = pl.BlockSpec((None, gq, w), lambda b, g: (b, g, 0))
    return pl.pallas_call(
        _attn_b_kernel,
        out_shape=jax.ShapeDtypeStruct((bn, l, w), BF16),
        grid=(bn, l // gq),
        in_specs=[qblk, seq, pl.BlockSpec((None, l // LANES, w, LANES), lambda b, g: (b, 0, 0, 0)),
                  pl.BlockSpec((None, lc, w), lambda b, g: (b, 0, 0)),
                  pl.BlockSpec((None, w, lc), lambda b, g: (b, 0, 0)),
                  _resident(tbl.shape)],
        out_specs=qblk,
        scratch_shapes=[pltpu.VMEM((B_HEADS // 2, nk, 2 * gq), F32), pltpu.VMEM((B_HEADS // 2, nk, 2 * gq), BF16)],
        compiler_params=_params("parallel", "arbitrary"),
        name="nbr_attn",
    )(q, k, vt, kx, vxt, tbl)


def _dft_matrices(l, fh):
    k = jnp.arange(l, dtype=jnp.int32)[:, None]
    n = jnp.arange(l, dtype=jnp.int32)[None, :]
    ang = (((2 * k + 1) * n) % (4 * l)).astype(F32) * (math.pi / (2 * l))
    fre = jnp.cos(ang).reshape(l // fh, fh, l)
    fim = (-jnp.sin(ang)).reshape(l // fh, fh, l)
    f = jnp.concatenate([fre, fim], axis=1).reshape(2 * l, l)
    g = f.T * (1.0 / l)
    return f.astype(BF16), g.astype(BF16)


def _filter_kernel(z_ref, t_ref, w1_ref, b1_ref, fr_ref, w2_ref, b2_ref, w3_ref, dl_ref,
                   ah_ref, al_ref, sh_ref, sl_ref):
    fr = fr_ref[...]
    hid = jnp.sin(fr * (_dot_hp(z_ref[...], w1_ref[...]) + b1_ref[...]))
    hid = jnp.sin(fr * (_dot_hp(hid, w2_ref[...]) + b2_ref[...]))
    cw = dl_ref.shape[1]
    decay = jnp.exp(-t_ref[...] * dl_ref[...])
    kf = _dot_hp(hid, w3_ref[:, :cw]) * decay
    kb = _dot_hp(hid, w3_ref[:, cw:]) * decay
    first = lax.broadcasted_iota(jnp.int32, kb.shape, 0) == 0
    kb = jnp.where(first, 0.0, kb)
    zn = jnp.sum(jnp.abs(kf), axis=0, keepdims=True) + jnp.sum(jnp.abs(kb), axis=0, keepdims=True)
    ah, al = _split_bf16((kf + kb) / zn)
    sh, sl = _split_bf16((kf - kb) / zn)
    ah_ref[...] = ah
    al_ref[...] = al
    sh_ref[...] = sh
    sl_ref[...] = sl


def _filters(z, t, w1, b1, freq, w2, b2, w3, deltas):
    l = z.shape[0]
    cw = deltas.shape[1]
    fw = w2.shape[0]
    const = lambda shape: pl.BlockSpec(shape, lambda o: (0,) * len(shape))
    out = jax.ShapeDtypeStruct((l, C_ORDER * cw), BF16)
    oblk = pl.BlockSpec((l, cw), lambda o: (0, o))
    return pl.pallas_call(
        _filter_kernel,
        out_shape=[out] * 4,
        grid=(C_ORDER,),
        in_specs=[const(z.shape), const((l, 1)), const(w1.shape), const((1, fw)), const((1, fw)),
                  const((fw, fw)), const((1, fw)), pl.BlockSpec((fw, C_DIRS * cw), lambda o: (0, o)),
                  const((1, cw))],
        out_specs=[oblk] * 4,
        compiler_params=_params("arbitrary"),
        name="hyena_filter",
    )(z, t, w1, b1, freq, w2, b2, w3, deltas)


def _spectrum_kernel(f_ref, ah_ref, al_ref, sh_ref, sl_ref, kre_ref, kim_ref):
    fh = kre_ref.shape[0]
    fre = f_ref[:fh, :]
    fim = f_ref[fh:, :]
    kre_ref[...] = _dot(fre, ah_ref[...]) + _dot(fre, al_ref[...])
    kim_ref[...] = _dot(fim, sh_ref[...]) + _dot(fim, sl_ref[...])


def _spectrum(f, ah, al, sh, sl, fh):
    l, n = ah.shape
    const = pl.BlockSpec((l, n), lambda j: (0, 0), pipeline_mode=pl.Buffered(1))
    out = jax.ShapeDtypeStruct((l, n), F32)
    oblk = pl.BlockSpec((fh, n), lambda j: (j, 0))
    return pl.pallas_call(
        _spectrum_kernel,
        out_shape=[out, out],
        grid=(l // fh,),
        in_specs=[pl.BlockSpec((2 * fh, l), lambda j: (j, 0)), const, const, const, const],
        out_specs=[oblk, oblk],
        compiler_params=_params("arbitrary"),
        name="hyena_spectrum",
    )(f, ah, al, sh, sl)


def _short_conv(u, w_ref, b_ref):
    n = u.shape[0]
    row = lax.broadcasted_iota(jnp.int32, u.shape, 0)
    prev = jnp.where(row == 0, 0.0, pltpu.roll(u, 1, 0))
    nxt = jnp.where(row == n - 1, 0.0, pltpu.roll(u, n - 1, 0))
    return prev * w_ref[0:1, :] + u * w_ref[1:2, :] + nxt * w_ref[2:3, :] + b_ref[...]


def _longconv_kernel(u_ref, g_ref, cwu_ref, cbu_ref, cwg_ref, cbg_ref, f_ref, gm_ref, kre_ref, kim_ref, d_ref,
                     o_ref, ub_ref, acc_ref, *, conv_u):
    j = pl.program_id(1)
    fh = kre_ref.shape[0]

    def load_u():
        return _short_conv(u_ref[...], cwu_ref, cbu_ref) if conv_u else u_ref[...]

    @pl.when(j == 0)
    def _():
        ub_ref[...] = load_u().astype(BF16)
        acc_ref[...] = jnp.zeros_like(acc_ref)

    spec = _dot(f_ref[...], ub_ref[...])
    vre, vim = spec[:fh], spec[fh:]
    kre, kim = kre_ref[...], kim_ref[...]
    y = jnp.concatenate([vre * kre - vim * kim, vre * kim + vim * kre], axis=0).astype(BF16)
    acc_ref[...] += _dot(gm_ref[...], y)

    @pl.when(j == pl.num_programs(1) - 1)
    def _():
        gate = _short_conv(g_ref[...], cwg_ref, cbg_ref)
        o_ref[...] = (gate * (acc_ref[...] + load_u() * d_ref[...])).astype(o_ref.dtype)


def _longconv(u_arr, u_blk, g_arr, g_blk, conv_w, conv_b, f, gm, kre, kim, k_blk, d, *, conv_u, out_dtype, fh):
    bn, l, _ = u_arr.shape
    cw = d.shape[1]
    ub = u_blk if conv_u else 0
    tok = lambda blk: pl.BlockSpec((None, l, cw), lambda b, j: (b, 0, blk))
    cpar = lambda rows, blk: pl.BlockSpec((rows, cw), lambda b, j: (0, blk))
    return pl.pallas_call(
        functools.partial(_longconv_kernel, conv_u=conv_u),
        out_shape=jax.ShapeDtypeStruct((bn, l, cw), out_dtype),
        grid=(bn, l // fh),
        in_specs=[tok(u_blk), tok(g_blk), cpar(3, ub), cpar(1, ub), cpar(3, g_blk), cpar(1, g_blk),
                  pl.BlockSpec((2 * fh, l), lambda b, j: (j, 0)),
                  pl.BlockSpec((l, 2 * fh), lambda b, j: (0, j)),
                  pl.BlockSpec((fh, cw), lambda b, j: (j, k_blk)),
                  pl.BlockSpec((fh, cw), lambda b, j: (j, k_blk)),
                  pl.BlockSpec((1, cw), lambda b, j: (0, 0))],
        out_specs=tok(0),
        scratch_shapes=[pltpu.VMEM((l, cw), BF16), pltpu.VMEM((l, cw), F32)],
        compiler_params=_params("parallel", "arbitrary"),
        name="hyena_longconv",
    )(u_arr, g_arr, conv_w, conv_b, conv_w, conv_b, f, gm, kre, kim, d)


def _hyena_consts(l, cw):
    t = np.linspace(0.0, 1.0, l, dtype=np.float32)[:, None]
    w = (2.0 * math.pi * np.arange(l, dtype=np.float32)[:, None] / l).astype(np.float32)
    fq = np.linspace(1e-4, C_BANDS - 1, C_BANDS, dtype=np.float32)[None, :]
    wf = jnp.asarray(w) * jnp.asarray(fq)
    z = jnp.concatenate([jnp.asarray(t), jnp.cos(wf), -jnp.sin(wf)], axis=-1)
    z = jnp.pad(z, ((0, 0), (0, C_FILTER_WIDTH - z.shape[1])))
    deltas = np.abs(np.linspace(HYENA_MIN_DECAY, HYENA_MAX_DECAY, cw, dtype=np.float32))[None, :]
    return z, jnp.asarray(t), jnp.asarray(deltas)


def _hyena(hy, hconst, f, gm, fh, conv_w, conv_b, w1p, b1, freq, w2, b2, w3, dbias):
    z, t, deltas = hconst
    cw = deltas.shape[1]
    ah, al, sh, sl = _filters(z, t, w1p, b1, freq, w2, b2, w3, deltas)
    kre, kim = _spectrum(f, ah, al, sh, sl, fh)
    zz = _longconv(hy, 0, hy, 1, conv_w, conv_b, f, gm, kre, kim, 0, dbias[0:1],
                   conv_u=True, out_dtype=F32, fh=fh)
    return _longconv(zz, 0, hy, 2, conv_w, conv_b, f, gm, kre, kim, 1, dbias[1:2],
                     conv_u=False, out_dtype=BF16, fh=fh)


def _mix_ffn_kernel(x_ref, a_ref, b_ref, c_ref, wo_ref, g1_ref, gn_ref, sh_ref, sc_ref, g2_ref,
                    wg_ref, wu_ref, wd_ref, o_ref, *, fc):
    wa, wb = a_ref.shape[1], b_ref.shape[1]
    mix = (_dot(a_ref[...], wo_ref[:wa, :]) + _dot(b_ref[...], wo_ref[wa:wa + wb, :])
           + _dot(c_ref[...], wo_ref[wa + wb:, :]))
    x1 = x_ref[...] + g1_ref[...] * mix
    xb = _rms_mod(x1, gn_ref[...], sh_ref[...], sc_ref[...]).astype(BF16)
    acc = None
    for c0 in range(0, wg_ref.shape[1], fc):
        hg = _dot(xb, wg_ref[:, c0:c0 + fc])
        hu = _dot(xb, wu_ref[:, c0:c0 + fc])
        act = ((hg * jax.nn.sigmoid(hg)) * hu).astype(BF16)
        part = _dot(act, wd_ref[c0:c0 + fc, :])
        acc = part if acc is None else acc + part
    o_ref[...] = x1 + g2_ref[...] * acc


def _mix_ffn(x, a, b, c, wo, g1, gn, sh2, sc2, g2, wg, wu, wd):
    bn, l, d = x.shape
    tm = min(TOKEN_TILE, l)
    row = pl.BlockSpec((None, 1, d), lambda bb, t: (bb, 0, 0))
    tok = lambda width: pl.BlockSpec((None, tm, width), lambda bb, t: (bb, t, 0))
    return pl.pallas_call(
        functools.partial(_mix_ffn_kernel, fc=256),
        out_shape=jax.ShapeDtypeStruct((bn, l, d), F32),
        grid=(bn, l // tm),
        in_specs=[tok(d), tok(a.shape[2]), tok(b.shape[2]), tok(c.shape[2]), _resident(wo.shape),
                  row, pl.BlockSpec((1, d), lambda bb, t: (0, 0)), row, row, row,
                  _resident(wg.shape), _resident(wu.shape), _resident(wd.shape)],
        out_specs=tok(d),
        compiler_params=_params("parallel", "parallel"),
        name="mix_ffn",
    )(x, a, b, c, wo, g1, gn, sh2, sc2, g2, wg, wu, wd)


def _rope_tables(l):
    half = HEAD_DIM // 2
    nfreq = half // 2
    inv = ROPE_BASE ** (-jnp.arange(nfreq, dtype=F32) / nfreq)
    pos = jnp.arange(l)
    rows, cols = pos // GRID_W, pos % GRID_W
    ang = jnp.concatenate([rows.astype(F32)[:, None] * inv[None, :]] * 2
                          + [cols.astype(F32)[:, None] * inv[None, :]] * 2, axis=-1)
    first = (np.arange(HEAD_DIM) % half) < nfreq
    cos, sin = jnp.cos(ang), jnp.sin(ang)
    sin_a = jnp.where(first[None, :], -sin, 0.0)
    sin_b = jnp.where(first[None, :], 0.0, sin)
    reps = LANES // HEAD_DIM
    return tuple(jnp.tile(tb, (1, reps)) for tb in (cos, sin_a, sin_b))


def kernel(x, c, ctx, c_ctx, ada_w, ada_b, norm1_g, norm2_g, w_in, qnorm_a, knorm_a, sink_a, qnorm_b, knorm_b,
           rpb_b, conv_w, conv_b, filt_w1, filt_b1, filt_freq, filt_w2, filt_b2, filt_w3, hyena_bias, w_out,
           ffn_w_gate, ffn_w_up, ffn_w_down):
    bn, l, d = x.shape
    lc = ctx.shape[1]
    depth = ada_w.shape[0]
    cw = hyena_bias.shape[2]
    assert l % (A_STEP_BLOCKS * A_BLOCK) == 0 and l >= 3 * A_SPAN and lc % 256 == 0
    assert l % (B_GROUP * GRID_W) == 0 and l // GRID_W >= 3 * B_GROUP

    mod_rows = 8 * (-(-(bn + 1) // 8))
    c_all = jnp.zeros((mod_rows, d), F32).at[:bn].set(c).at[bn].set(c_ctx)
    mod = _modulation(c_all, ada_w, ada_b)

    a_order = (0, 2, 1, 3)
    a_kv = tuple(hh // (A_HEADS // A_KV_HEADS) for hh in a_order)
    heads_a = lambda t, axis: [lax.slice_in_dim(t, hh * HEAD_DIM, (hh + 1) * HEAD_DIM, axis=axis) for hh in a_order]
    qa_w = A_HEADS * HEAD_DIM
    w_in_b = jnp.concatenate(heads_a(w_in, 2) + [w_in[:, :, qa_w:]], axis=2).astype(BF16)
    w_out_b = jnp.concatenate(heads_a(w_out, 1) + [w_out[:, qa_w:]], axis=1).astype(BF16)
    sink_p = jnp.stack([sink_a[:, hh] for hh in a_order], axis=1)
    wg_b, wu_b, wd_b = ffn_w_gate.astype(BF16), ffn_w_up.astype(BF16), ffn_w_down.astype(BF16)

    lane = np.arange(256)
    e_heads = jnp.asarray((lane[:, None] // HEAD_DIM) == (lane[None, :] // HEAD_DIM), BF16)
    tile4 = lambda g: jnp.tile(g.reshape(1, HEAD_DIM), (1, 256 // HEAD_DIM))
    rope_lat = _rope_tables(l)
    rope_ctx = tuple(tb[:lc] for tb in rope_lat)

    fh_lat, fh_ctx = 256, min(256, lc)
    f_lat, g_lat = _dft_matrices(l, fh_lat)
    f_ctx, g_ctx = _dft_matrices(lc, fh_ctx)
    hc_lat = _hyena_consts(l, cw)
    hc_ctx = _hyena_consts(lc, cw)

    xc = ctx
    for i in range(depth):
        last = i == depth - 1
        m = mod[i]
        part = lambda rows, k: rows[:, None, k * d:(k + 1) * d]
        lat = [part(m[:bn], k) for k in range(6)]
        cx = [jnp.broadcast_to(part(m[bn:bn + 1], k), (bn, 1, d)) for k in range(6)]
        g1n = norm1_g[i].reshape(1, d)
        g2n = norm2_g[i].reshape(1, d)
        gqa, gka, gqb, gkb = tile4(qnorm_a[i]), tile4(knorm_a[i]), tile4(qnorm_b[i]), tile4(knorm_b[i])

        qa, ka, va, qb, kb, vb, hy = _proj_in(x, lat[0], lat[1], g1n, w_in_b[i], gqa, gka, gqb, gkb, e_heads,
                                              rope_lat, rope=True)
        qa_c, ka_c, va_c, qb_c, kb_c, vb_c, hy_c = _proj_in(xc, cx[0], cx[1], g1n, w_in_b[i], gqa, gka, gqb, gkb,
                                                            e_heads, rope_ctx, rope=False)
        w1p = jnp.pad(filt_w1[i], ((0, C_FILTER_WIDTH - filt_w1.shape[1]), (0, 0)))
        hy_args = (conv_w[i], conv_b[i].reshape(1, -1), w1p, filt_b1[i].reshape(1, -1),
                   filt_freq[i].reshape(1, -1), filt_w2[i], filt_b2[i].reshape(1, -1), filt_w3[i], hyena_bias[i])

        sink_row = jnp.repeat(sink_p[i], A_BLOCK).reshape(1, A_HEADS * A_BLOCK)
        out_a = _attn_a(qa, ka, va, ka_c, va_c, sink_row)
        out_b = _attn_b(qb, kb, vb, kb_c, vb_c, _rpb_tables(rpb_b[i]))
        out_c = _hyena(hy, hc_lat, f_lat, g_lat, fh_lat, *hy_args)
        x = _mix_ffn(x, out_a, out_b, out_c, w_out_b[i], lat[2], g2n, lat[3], lat[4], lat[5],
                     wg_b[i], wu_b[i], wd_b[i])

        if not last:
            oa_c = _dense_attn(qa_c, ka_c, va_c, sink_p[i], kv_of=a_kv, use_sink=True)
            ob_c = _dense_attn(qb_c, kb_c, vb_c, sink_p[i], kv_of=tuple(range(B_HEADS)), use_sink=False)
            oc_c = _hyena(hy_c, hc_ctx, f_ctx, g_ctx, fh_ctx, *hy_args)
            xc = _mix_ffn(xc, oa_c, ob_c, oc_c, w_out_b[i], cx[2], g2n, cx[3], cx[4], cx[5],
                          wg_b[i], wu_b[i], wd_b[i])
    return x
```

```python
import functools
import math

import jax
import jax.numpy as jnp
import numpy as np
from jax import lax
from jax.experimental import pallas as pl
from jax.experimental.pallas import tpu as pltpu

F32 = jnp.float32
BF16 = jnp.bfloat16

GRID_W = 64
HEAD_DIM = 64
A_HEADS = 4
A_KV_HEADS = 2
A_BLOCK = 128
B_HEADS = 4
NA_ROWS = 8
NA_COLS = 16
C_ORDER = 2
C_DIRS = 2
C_FILTER_WIDTH = 64
C_BANDS = 16
ROPE_BASE = 10000.0
EPS = 1e-6
NEG_INF = -1e30
HYENA_MIN_DECAY = math.log(1e-2) / 1.5
HYENA_MAX_DECAY = math.log(1e-2) / 0.3

V7X_VMEM_BYTES = 64 * 1024 * 1024
VMEM_LIMIT = V7X_VMEM_BYTES - 8 * 1024 * 1024
LANES = 128
TOKEN_TILE = 512
FREQ_TILE = 256


def _params(*sem):
    return pltpu.CompilerParams(dimension_semantics=sem, vmem_limit_bytes=VMEM_LIMIT)


def _dot(a, b):
    return jnp.dot(a, b, preferred_element_type=F32)


def _dot_nt(a, b):
    return lax.dot_general(a, b, (((1,), (1,)), ((), ())), preferred_element_type=F32)


def _dot_hp(a, b):
    return jnp.dot(a, b, preferred_element_type=F32, precision=lax.Precision.HIGHEST)


def _split_bf16(v):
    hi = v.astype(BF16)
    lo = (v - hi.astype(F32)).astype(BF16)
    return hi, lo


def _resident(shape):
    nd = len(shape)
    return pl.BlockSpec(shape, lambda *_: (0,) * nd, pipeline_mode=pl.Buffered(1))


def _mod_kernel(c_ref, w_ref, b_ref, o_ref):
    cv = c_ref[...]
    sc = (cv * jax.nn.sigmoid(cv)).astype(BF16)
    o_ref[...] = _dot(sc, w_ref[...].astype(BF16)) + b_ref[...]


def _modulation(c_all, ada_w, ada_b):
    depth, d, n = ada_w.shape
    rows = c_all.shape[0]
    tn = 512
    return pl.pallas_call(
        _mod_kernel,
        out_shape=jax.ShapeDtypeStruct((depth, rows, n), F32),
        grid=(depth, n // tn),
        in_specs=[
            pl.BlockSpec((rows, d), lambda i, j: (0, 0)),
            pl.BlockSpec((None, d, tn), lambda i, j: (i, 0, j)),
            pl.BlockSpec((None, 1, tn), lambda i, j: (i, 0, j)),
        ],
        out_specs=pl.BlockSpec((None, rows, tn), lambda i, j: (i, 0, j)),
        compiler_params=_params("arbitrary", "arbitrary"),
        name="modulation",
    )(c_all, ada_w, ada_b.reshape(depth, 1, n))


def _rms_mod(x, g, shift, scale):
    y = x * lax.rsqrt(jnp.mean(x * x, axis=-1, keepdims=True) + EPS)
    return (y * g) * (1 + scale) + shift


def _head_norm(h, g, e):
    hi, lo = _split_bf16(h * h)
    ss = _dot(hi, e) + _dot(lo, e)
    return (h * lax.rsqrt(ss * (1.0 / HEAD_DIM) + EPS)) * g


def _rope(t, cos, sin_a, sin_b):
    outs = []
    for c in range(t.shape[1] // LANES):
        tc = t[:, c * LANES:(c + 1) * LANES]
        outs.append(tc * cos + pltpu.roll(tc, LANES - 16, 1) * sin_a + pltpu.roll(tc, 16, 1) * sin_b)
    return outs[0] if len(outs) == 1 else jnp.concatenate(outs, axis=1)


def _proj_in_kernel(x_ref, shift_ref, scale_ref, g_ref, w_ref, gqa_ref, gka_ref, gqb_ref, gkb_ref, e_ref,
                    cos_ref, sa_ref, sb_ref,
                    qa_ref, ka_ref, va_ref, qb_ref, kb_ref, vb_ref, hy_ref, *, rope):
    xb = _rms_mod(x_ref[...], g_ref[...], shift_ref[...], scale_ref[...]).astype(BF16)
    qscale = HEAD_DIM ** -0.5

    def proj(lo, hi):
        return _dot(xb, w_ref[:, lo:hi])

    def maybe_rope(t):
        return _rope(t, cos_ref[...], sa_ref[...], sb_ref[...]) if rope else t

    qa = maybe_rope(_head_norm(proj(0, 256), gqa_ref[...], e_ref[...]))
    qa_ref[...] = (qa * qscale).astype(BF16)
    ka = maybe_rope(_head_norm(proj(256, 384), gka_ref[:, :128], e_ref[:128, :128]))
    ka_ref[...] = ka.astype(BF16)
    va_ref[...] = proj(384, 512).astype(BF16)
    qb = _head_norm(proj(512, 768), gqb_ref[...], e_ref[...])
    qb_ref[...] = (qb * qscale).astype(BF16)
    kb_ref[...] = _head_norm(proj(768, 1024), gkb_ref[...], e_ref[...]).astype(BF16)
    vb_ref[...] = proj(1024, 1280).astype(BF16)
    hy_ref[...] = proj(1280, 2816).astype(BF16)


def _proj_in(x, shift, scale, g, w, gqa, gka, gqb, gkb, e, rope_tabs, *, rope):
    bn, l, d = x.shape
    n = w.shape[1]
    tm = min(TOKEN_TILE, l)
    row = lambda width: pl.BlockSpec((None, 1, width), lambda b, t: (b, 0, 0))
    const = lambda shape: pl.BlockSpec(shape, lambda b, t: (0,) * len(shape))
    tok = lambda width: pl.BlockSpec((None, tm, width), lambda b, t: (b, t, 0))
    tab = pl.BlockSpec((tm, LANES), lambda b, t: (t, 0))
    widths = (256, 128, 128, 256, 256, 256, n - 1280)
    dts = (BF16,) * 7
    return pl.pallas_call(
        functools.partial(_proj_in_kernel, rope=rope),
        out_shape=[jax.ShapeDtypeStruct((bn, l, wd), dt) for wd, dt in zip(widths, dts)],
        grid=(bn, l // tm),
        in_specs=[tok(d), row(d), row(d), const((1, d)), _resident((d, n)),
                  const((1, 256)), const((1, 256)), const((1, 256)), const((1, 256)), const((256, 256)),
                  tab, tab, tab],
        out_specs=[tok(wd) for wd in widths],
        compiler_params=_params("parallel", "parallel"),
        name="proj_in_rope" if rope else "proj_in",
    )(x, shift, scale, g, w, gqa, gka, gqb, gkb, e, *rope_tabs)


def _softmax_pv(s_list, v_list, sink):
    m = s_list[0].max(axis=-1, keepdims=True)
    for s in s_list[1:]:
        m = jnp.maximum(m, s.max(axis=-1, keepdims=True))
    if sink is not None:
        m = jnp.maximum(m, sink)
    den = None
    out = None
    for s, v in zip(s_list, v_list):
        p = jnp.exp(s - m)
        ps = p.sum(axis=-1, keepdims=True)
        den = ps if den is None else den + ps
        o = _dot(p.astype(BF16), v)
        out = o if out is None else out + o
    if sink is not None:
        den = den + jnp.exp(sink - m)
    return out * (1.0 / den)


def _stack_heads(q):
    lo = lax.broadcasted_iota(jnp.int32, q.shape, 1) < HEAD_DIM
    zero = jnp.zeros_like(q)
    return jnp.concatenate([jnp.where(lo, q, zero), jnp.where(lo, zero, q)], axis=0)


def _unstack_heads(o):
    m = o.shape[0] // 2
    lo = lax.broadcasted_iota(jnp.int32, (m, LANES), 1) < HEAD_DIM
    return jnp.where(lo, o[:m], o[m:])


SOFTMAX_CHUNK = 32


def _softmax_keys(s_ref, p_ref, nloc, add_loc, sink):
    nk = s_ref.shape[0]
    ch = SOFTMAX_CHUNK

    def scores(r0):
        s = s_ref[r0:r0 + ch, :]
        return s + add_loc(r0) if r0 < nloc else s

    macc = scores(0)
    for r0 in range(ch, nk, ch):
        macc = jnp.maximum(macc, scores(r0))
    m = macc.max(axis=0, keepdims=True)
    if sink is not None:
        m = jnp.maximum(m, sink)
    sacc = None
    for r0 in range(0, nk, ch):
        p = jnp.exp(scores(r0) - m)
        sacc = p if sacc is None else sacc + p
        p_ref[r0:r0 + ch, :] = p.astype(BF16)
    den = sacc.sum(axis=0, keepdims=True)
    if sink is not None:
        den = den + jnp.exp(sink - m)
    return 1.0 / den


A_STEP_BLOCKS = 4
A_SPAN = 3 * A_BLOCK


def _attn_a_kernel(q_ref, k_ref, vt_ref, kx_ref, vxt_ref, mask_ref, sink_ref, o_ref, s_ref, p_ref):
    l = k_ref.shape[0]
    nb = l // A_BLOCK
    hd = HEAD_DIM
    for u in range(A_STEP_BLOCKS):
        n = pl.program_id(1) * A_STEP_BLOCKS + u
        tile0 = jnp.clip(n - 1, 0, nb - 3)
        start = pl.multiple_of(tile0 * A_BLOCK, A_BLOCK)
        pat = jnp.where(n == 0, 0, jnp.where(n == nb - 1, 2, 1))
        q = q_ref[u * A_BLOCK:(u + 1) * A_BLOCK, :]
        qs = jnp.concatenate([_stack_heads(q[:, :LANES]), _stack_heads(q[:, LANES:])], axis=0)
        sb, pb = s_ref.at[u % 2], p_ref.at[u % 2]
        sb[:A_SPAN, :] = _dot_nt(k_ref[pl.ds(start, A_SPAN), :], qs)
        sb[A_SPAN:, :] = _dot_nt(kx_ref[...], qs)
        r = _softmax_keys(sb, pb, A_SPAN, lambda r0: mask_ref[pat, r0:r0 + SOFTMAX_CHUNK, :], sink_ref[...])
        vt = jnp.concatenate([vt_ref[tile0 + i] for i in range(3)], axis=1)
        ot = (_dot(vt, pb[:A_SPAN, :]) + _dot(vxt_ref[...], pb[A_SPAN:, :])) * r
        ot = jnp.concatenate([ot[(i % 2) * hd:(i % 2 + 1) * hd, i * A_BLOCK:(i + 1) * A_BLOCK]
                              for i in range(A_HEADS)], axis=0)
        o_ref[u * A_BLOCK:(u + 1) * A_BLOCK, :] = ot.T.astype(o_ref.dtype)


def _attn_a_mask():
    i = np.arange(A_BLOCK)[None, :]
    j = np.arange(A_SPAN)[:, None]
    offs = (0, A_BLOCK, 2 * A_BLOCK)
    m = np.stack([np.where(np.abs(j - i - o) <= A_BLOCK, 0.0, -np.inf) for o in offs])
    return jnp.asarray(np.tile(m, (1, 1, A_HEADS)), F32)


def _attn_a(q, k, v, kx, vx, sink_row):
    bn, l, qw = q.shape
    lc = kx.shape[1]
    kvw = k.shape[2]
    nb = l // A_BLOCK
    qs = A_STEP_BLOCKS * A_BLOCK
    mask = _attn_a_mask()
    vt = jnp.swapaxes(v.reshape(bn, nb, A_BLOCK, kvw), 2, 3)
    vxt = jnp.swapaxes(vx, 1, 2)
    seq = pl.BlockSpec((None, l, kvw), lambda b, s: (b, 0, 0))
    qblk = pl.BlockSpec((None, qs, qw), lambda b, s: (b, s, 0))
    nq = A_HEADS * A_BLOCK
    return pl.pallas_call(
        _attn_a_kernel,
        out_shape=jax.ShapeDtypeStruct((bn, l, qw), BF16),
        grid=(bn, l // qs),
        in_specs=[qblk, seq, pl.BlockSpec((None, nb, kvw, A_BLOCK), lambda b, s: (b, 0, 0, 0)),
                  pl.BlockSpec((None, lc, kvw), lambda b, s: (b, 0, 0)),
                  pl.BlockSpec((None, kvw, lc), lambda b, s: (b, 0, 0)),
                  _resident(mask.shape), _resident(sink_row.shape)],
        out_specs=qblk,
        scratch_shapes=[pltpu.VMEM((2, A_SPAN + lc, nq), F32), pltpu.VMEM((2, A_SPAN + lc, nq), BF16)],
        compiler_params=_params("parallel", "arbitrary"),
        name="window_attn",
    )(q, k, vt, kx, vxt, mask, sink_row)


def _dense_attn_kernel(sink_ref, q_ref, k_ref, v_ref, o_ref, *, kv_of, use_sink):
    for h, kv in enumerate(kv_of):
        hs = slice(h * HEAD_DIM, (h + 1) * HEAD_DIM)
        ks = slice(kv * HEAD_DIM, (kv + 1) * HEAD_DIM)
        s = _dot_nt(q_ref[:, hs], k_ref[:, ks])
        o = _softmax_pv([s], [v_ref[:, ks]], sink_ref[h] if use_sink else None)
        o_ref[:, hs] = o.astype(o_ref.dtype)


def _dense_attn(q, k, v, sink, *, kv_of, use_sink):
    bn, l, qw = q.shape
    kvw = k.shape[2]
    full = lambda wd: pl.BlockSpec((None, l, wd), lambda b: (b, 0, 0))
    return pl.pallas_call(
        functools.partial(_dense_attn_kernel, kv_of=kv_of, use_sink=use_sink),
        out_shape=jax.ShapeDtypeStruct((bn, l, qw), BF16),
        grid=(bn,),
        in_specs=[pl.BlockSpec(memory_space=pltpu.SMEM), full(qw), full(kvw), full(kvw)],
        out_specs=full(qw),
        compiler_params=_params("parallel"),
        name="ctx_attn_sink" if use_sink else "ctx_attn",
    )(sink, q, k, v)


def _rpb_kernel(r_ref, oh_ref, o_ref):
    r = r_ref[...]
    b1 = r.astype(BF16)
    r2 = r - b1.astype(F32)
    b2 = r2.astype(BF16)
    b3 = (r2 - b2.astype(F32)).astype(BF16)
    oh = oh_ref[...]
    o_ref[...] = (_dot(b1, oh) + _dot(b2, oh)) + _dot(b3, oh)


def _rpb_tables(rpb):
    h, nr, nc = rpb.shape
    col = np.arange(GRID_W)
    dc = np.clip(col[None, :] - col[:, None], 1 - NA_COLS, NA_COLS - 1) + NA_COLS - 1
    onehot = (dc.reshape(1, -1) == np.arange(nc)[:, None]).astype(np.float32)
    onehot = np.concatenate([onehot, np.zeros((32 - nc, GRID_W * GRID_W), np.float32)], axis=0)
    rows = 64
    r2 = jnp.zeros((rows, 32), F32).at[:h * nr, :nc].set(rpb.reshape(h * nr, nc))
    full = pl.pallas_call(
        _rpb_kernel,
        out_shape=jax.ShapeDtypeStruct((rows, GRID_W * GRID_W), F32),
        name="rpb_table",
    )(r2, jnp.asarray(onehot, BF16))
    full = full[:h * nr].reshape(h, nr, GRID_W, GRID_W)
    a = np.arange(B_GROUP)[:, None]
    kr = np.arange(B_SLAB)[None, :]
    dr = np.stack([kr - a + NA_ROWS - 1, kr - a + NA_ROWS // 2 - 1, kr - a + (B_GROUP + NA_ROWS - 1 - B_SLAB)])
    lo = np.stack([0 * a + 0 * kr, a + 0 * kr, 0 * a + (B_SLAB - NA_ROWS) + 0 * kr])
    valid = (kr[None] >= lo) & (kr[None] < lo + NA_ROWS)
    col_start = np.clip(col - NA_COLS // 2, 0, GRID_W - NA_COLS)
    col_ok = (col[None, :] >= col_start[:, None]) & (col[None, :] < col_start[:, None] + NA_COLS)
    ok = valid[:, :, None, :, None] & col_ok[None, None, :, None, :]
    tbl = full[:, np.clip(dr, 0, nr - 1)]
    tbl = jnp.transpose(tbl, (1, 0, 2, 4, 3, 5))
    tbl = jnp.where(ok[:, None], tbl, -jnp.inf)
    tbl = tbl.reshape(3, h // 2, 2, B_GROUP, GRID_W, B_SLAB, GRID_W)
    tbl = jnp.transpose(tbl, (0, 1, 5, 6, 2, 3, 4))
    return tbl.reshape(3, h // 2, B_SLAB * GRID_W, 2 * B_GROUP * GRID_W)


B_GROUP = 4
B_SLAB = 12


def _attn_b_kernel(q_ref, k_ref, vt_ref, kx_ref, vxt_ref, tbl_ref, o_ref, s_ref, p_ref):
    g = pl.program_id(1)
    ng = pl.num_programs(1)
    rows = k_ref.shape[0] // GRID_W
    nloc = B_SLAB * GRID_W
    gq = B_GROUP * GRID_W
    base = jnp.clip(g * B_GROUP - NA_ROWS // 2, 0, rows - B_SLAB)
    start = pl.multiple_of(base * GRID_W, LANES)
    tile0 = base // (LANES // GRID_W)
    pat = jnp.where(g == 0, 0, jnp.where(g == ng - 1, 2, 1))
    for t in range(B_HEADS // 2):
        ts = slice(t * LANES, (t + 1) * LANES)
        qs = _stack_heads(q_ref[:, ts])
        sb, pb = s_ref.at[t], p_ref.at[t]
        sb[:nloc, :] = _dot_nt(k_ref[pl.ds(start, nloc), ts], qs)
        sb[nloc:, :] = _dot_nt(kx_ref[:, ts], qs)
        r = _softmax_keys(sb, pb, nloc, lambda r0: tbl_ref[pat, t, r0:r0 + SOFTMAX_CHUNK, :], None)
        vt = jnp.concatenate([vt_ref[tile0 + i, ts, :] for i in range(nloc // LANES)], axis=1)
        ot = (_dot(vt, pb[:nloc, :]) + _dot(vxt_ref[ts, :], pb[nloc:, :])) * r
        ot = jnp.concatenate([ot[:HEAD_DIM, :gq], ot[HEAD_DIM:, gq:]], axis=0)
        o_ref[:, ts] = ot.T.astype(o_ref.dtype)


def _attn_b(q, k, v, kx, vx, tbl):
    bn, l, w = q.shape
    lc = kx.shape[1]
    gq = B_GROUP * GRID_W
    nk = B_SLAB * GRID_W + lc
    vt = jnp.swapaxes(v.reshape(bn, l // LANES, LANES, w), 2, 3)
    vxt = jnp.swapaxes(vx, 1, 2)
    seq = pl.BlockSpec((None, l, w), lambda b, g: (b, 0, 0))
    qblk = pl.BlockSpec((None, gq, w), lambda b, g: (b, g, 0))
    return pl.pallas_call(
        _attn_b_kernel,
        out_shape=jax.ShapeDtypeStruct((bn, l, w), BF16),
        grid=(bn, l // gq),
        in_specs=[qblk, seq, pl.BlockSpec((None, l // LANES, w, LANES), lambda b, g: (b, 0, 0, 0)),
                  pl.BlockSpec((None, lc, w), lambda b, g: (b, 0, 0)),
                  pl.BlockSpec((None, w, lc), lambda b, g: (b, 0, 0)),
                  _resident(tbl.shape)],
        out_specs=qblk,
        scratch_shapes=[pltpu.VMEM((B_HEADS // 2, nk, 2 * gq), F32), pltpu.VMEM((B_HEADS // 2, nk, 2 * gq), BF16)],
        compiler_params=_params("parallel", "arbitrary"),
        name="nbr_attn",
    )(q, k, vt, kx, vxt, tbl)


def _dft_matrices(l, fh):
    k = jnp.arange(l, dtype=jnp.int32)[:, None]
    n = jnp.arange(l, dtype=jnp.int32)[None, :]
    ang = (((2 * k + 1) * n) % (4 * l)).astype(F32) * (math.pi / (2 * l))
    fre = jnp.cos(ang).reshape(l // fh, fh, l)
    fim = (-jnp.sin(ang)).reshape(l // fh, fh, l)
    f = jnp.concatenate([fre, fim], axis=1).reshape(2 * l, l)
    g = f.T * (1.0 / l)
    return f.astype(BF16), g.astype(BF16)


def _filter_kernel(z_ref, t_ref, w1_ref, b1_ref, fr_ref, w2_ref, b2_ref, w3_ref, dl_ref,
                   ah_ref, al_ref, sh_ref, sl_ref):
    fr = fr_ref[...]
    hid = jnp.sin(fr * (_dot_hp(z_ref[...], w1_ref[...]) + b1_ref[...]))
    hid = jnp.sin(fr * (_dot_hp(hid, w2_ref[...]) + b2_ref[...]))
    cw = dl_ref.shape[1]
    p = z_ref.shape[0] // 3
    decay = jnp.exp(-t_ref[...] * dl_ref[...])
    kf = _dot_hp(hid, w3_ref[:, :cw]) * decay
    kb = _dot_hp(hid, w3_ref[:, cw:]) * decay
    kf0, kf1, kfr = kf[:p], kf[p:2 * p], kf[2 * p:]
    kb0, kb1, kbr = kb[:p], kb[p:2 * p], kb[2 * p:]
    first = lax.broadcasted_iota(jnp.int32, (p, cw), 0) == 0
    drop0 = lambda v: jnp.where(first, 0.0, v)
    kb0 = drop0(kb0)
    colsum = lambda v: jnp.sum(jnp.abs(v), axis=0, keepdims=True)
    inv = 1.0 / (colsum(kf0) + colsum(kf1) + colsum(kb0) + colsum(kb1))
    pairs = ((kf0, kb0), (kf1, drop0(kfr)), (kbr, drop0(kb1)))
    for d, (cp, cm) in enumerate(pairs):
        cols = slice(d * cw, (d + 1) * cw)
        ah_ref[:, cols], al_ref[:, cols] = _split_bf16((cp + cm) * inv)
        sh_ref[:, cols], sl_ref[:, cols] = _split_bf16((cp - cm) * inv)


def _filters(z, t, w1, b1, freq, w2, b2, w3, deltas):
    p = z.shape[0] // 3
    cw = deltas.shape[1]
    fw = w2.shape[0]
    const = lambda shape: pl.BlockSpec(shape, lambda o: (0,) * len(shape))
    out = jax.ShapeDtypeStruct((p, C_ORDER * 3 * cw), BF16)
    oblk = pl.BlockSpec((p, 3 * cw), lambda o: (0, o))
    return pl.pallas_call(
        _filter_kernel,
        out_shape=[out] * 4,
        grid=(C_ORDER,),
        in_specs=[const(z.shape), const(t.shape), const(w1.shape), const((1, fw)), const((1, fw)),
                  const((fw, fw)), const((1, fw)), pl.BlockSpec((fw, C_DIRS * cw), lambda o: (0, o)),
                  const((1, cw))],
        out_specs=[oblk] * 4,
        compiler_params=_params("arbitrary"),
        name="hyena_filter",
    )(z, t, w1, b1, freq, w2, b2, w3, deltas)


def _spectrum_kernel(f_ref, ah_ref, al_ref, sh_ref, sl_ref, kre_ref, kim_ref):
    fh = kre_ref.shape[0]
    fre = f_ref[:fh, :]
    fim = f_ref[fh:, :]
    kre_ref[...] = _dot(fre, ah_ref[...]) + _dot(fre, al_ref[...])
    kim_ref[...] = _dot(fim, sh_ref[...]) + _dot(fim, sl_ref[...])


def _spectrum(f, ah, al, sh, sl, fh):
    p, n = ah.shape
    tn = n // C_ORDER
    taps = pl.BlockSpec((p, tn), lambda o, j: (0, o))
    out = jax.ShapeDtypeStruct((p, n), F32)
    oblk = pl.BlockSpec((fh, tn), lambda o, j: (j, o))
    return pl.pallas_call(
        _spectrum_kernel,
        out_shape=[out, out],
        grid=(C_ORDER, p // fh),
        in_specs=[pl.BlockSpec((2 * fh, p), lambda o, j: (j, 0)), taps, taps, taps, taps],
        out_specs=[oblk, oblk],
        compiler_params=_params("arbitrary", "arbitrary"),
        name="hyena_spectrum",
    )(f, ah, al, sh, sl)


def _short_conv(u, w_ref, b_ref):
    n = u.shape[0]
    row = lax.broadcasted_iota(jnp.int32, u.shape, 0)
    prev = jnp.where(row == 0, 0.0, pltpu.roll(u, 1, 0))
    nxt = jnp.where(row == n - 1, 0.0, pltpu.roll(u, n - 1, 0))
    return prev * w_ref[0:1, :] + u * w_ref[1:2, :] + nxt * w_ref[2:3, :] + b_ref[...]


def _longconv_kernel(u_ref, g_ref, cwu_ref, cbu_ref, cwg_ref, cbg_ref, f_ref, gm_ref, kre_ref, kim_ref, d_ref,
                     o_ref, ub_ref, acc_ref, *, conv_u):
    j = pl.program_id(1)
    fh = kre_ref.shape[0]
    p, cw = ub_ref.shape[0], d_ref.shape[1]

    def load_u():
        u = u_ref[...].astype(F32)
        return _short_conv(u, cwu_ref, cbu_ref) if conv_u else u

    @pl.when(j == 0)
    def _():
        u = load_u().astype(BF16)
        ub_ref[:, :cw] = u[:p]
        ub_ref[:, cw:] = u[p:]
        acc_ref[...] = jnp.zeros_like(acc_ref)

    spec = _dot(f_ref[...], ub_ref[...])
    u0r, u1r, u0i, u1i = spec[:fh, :cw], spec[:fh, cw:], spec[fh:, :cw], spec[fh:, cw:]
    tap = lambda ref, d: ref[:, d * cw:(d + 1) * cw]
    c0r, c1r, cmr = (tap(kre_ref, d) for d in range(3))
    c0i, c1i, cmi = (tap(kim_ref, d) for d in range(3))
    y0r = (c0r * u0r - c0i * u0i) + (cmr * u1r - cmi * u1i)
    y0i = (c0r * u0i + c0i * u0r) + (cmr * u1i + cmi * u1r)
    y1r = (c1r * u0r - c1i * u0i) + (c0r * u1r - c0i * u1i)
    y1i = (c1r * u0i + c1i * u0r) + (c0r * u1i + c0i * u1r)
    y = jnp.concatenate([jnp.concatenate([y0r, y1r], axis=1), jnp.concatenate([y0i, y1i], axis=1)], axis=0)
    acc_ref[...] += _dot(gm_ref[...], y.astype(BF16))

    @pl.when(j == pl.num_programs(1) - 1)
    def _():
        gate = _short_conv(g_ref[...].astype(F32), cwg_ref, cbg_ref)
        y = jnp.concatenate([acc_ref[:, :cw], acc_ref[:, cw:]], axis=0)
        o_ref[...] = (gate * (y + load_u() * d_ref[...])).astype(o_ref.dtype)


def _longconv(u_arr, u_blk, g_arr, g_blk, conv_w, conv_b, f, gm, kre, kim, order, d, *, conv_u, fh):
    bn, l, _ = u_arr.shape
    p = l // 2
    cw = d.shape[1]
    ub = u_blk if conv_u else 0
    tok = lambda blk: pl.BlockSpec((None, l, cw), lambda b, j: (b, 0, blk))
    cpar = lambda rows, blk: pl.BlockSpec((rows, cw), lambda b, j: (0, blk))
    ktab = pl.BlockSpec((fh, 3 * cw), lambda b, j: (j, order))
    return pl.pallas_call(
        functools.partial(_longconv_kernel, conv_u=conv_u),
        out_shape=jax.ShapeDtypeStruct((bn, l, cw), BF16),
        grid=(bn, p // fh),
        in_specs=[tok(u_blk), tok(g_blk), cpar(3, ub), cpar(1, ub), cpar(3, g_blk), cpar(1, g_blk),
                  pl.BlockSpec((2 * fh, p), lambda b, j: (j, 0)),
                  pl.BlockSpec((p, 2 * fh), lambda b, j: (0, j)),
                  ktab, ktab, pl.BlockSpec((1, cw), lambda b, j: (0, 0))],
        out_specs=tok(0),
        scratch_shapes=[pltpu.VMEM((p, 2 * cw), BF16), pltpu.VMEM((p, 2 * cw), F32)],
        compiler_params=_params("parallel", "arbitrary"),
        name="hyena_longconv",
    )(u_arr, g_arr, conv_w, conv_b, conv_w, conv_b, f, gm, kre, kim, d)


def _hyena_consts(l, cw):
    p = l // 2
    e = np.arange(p)
    pos = np.concatenate([e, p + e, p - e])
    t = np.linspace(0.0, 1.0, l, dtype=np.float32)[pos][:, None]
    w = (2.0 * math.pi * np.arange(l, dtype=np.float32) / l).astype(np.float32)[pos][:, None]
    fq = np.linspace(1e-4, C_BANDS - 1, C_BANDS, dtype=np.float32)[None, :]
    wf = jnp.asarray(w) * jnp.asarray(fq)
    z = jnp.concatenate([jnp.asarray(t), jnp.cos(wf), -jnp.sin(wf)], axis=-1)
    z = jnp.pad(z, ((0, 0), (0, C_FILTER_WIDTH - z.shape[1])))
    deltas = np.abs(np.linspace(HYENA_MIN_DECAY, HYENA_MAX_DECAY, cw, dtype=np.float32))[None, :]
    return z, jnp.asarray(t), jnp.asarray(deltas)


def _hyena(hy, hconst, f, gm, fh, conv_w, conv_b, w1p, b1, freq, w2, b2, w3, dbias):
    z, t, deltas = hconst
    ah, al, sh, sl = _filters(z, t, w1p, b1, freq, w2, b2, w3, deltas)
    kre, kim = _spectrum(f, ah, al, sh, sl, fh)
    zz = _longconv(hy, 0, hy, 1, conv_w, conv_b, f, gm, kre, kim, 0, dbias[0:1], conv_u=True, fh=fh)
    return _longconv(zz, 0, hy, 2, conv_w, conv_b, f, gm, kre, kim, 1, dbias[1:2], conv_u=False, fh=fh)


def _mix_ffn_kernel(x_ref, a_ref, b_ref, c_ref, wo_ref, g1_ref, gn_ref, sh_ref, sc_ref, g2_ref,
                    wg_ref, wu_ref, wd_ref, o_ref, *, fc):
    wa, wb = a_ref.shape[1], b_ref.shape[1]
    mix = (_dot(a_ref[...], wo_ref[:wa, :]) + _dot(b_ref[...], wo_ref[wa:wa + wb, :])
           + _dot(c_ref[...], wo_ref[wa + wb:, :]))
    x1 = x_ref[...] + g1_ref[...] * mix
    xb = _rms_mod(x1, gn_ref[...], sh_ref[...], sc_ref[...]).astype(BF16)
    acc = None
    for c0 in range(0, wg_ref.shape[1], fc):
        hg = _dot(xb, wg_ref[:, c0:c0 + fc])
        hu = _dot(xb, wu_ref[:, c0:c0 + fc])
        act = ((hg * jax.nn.sigmoid(hg)) * hu).astype(BF16)
        part = _dot(act, wd_ref[c0:c0 + fc, :])
        acc = part if acc is None else acc + part
    o_ref[...] = x1 + g2_ref[...] * acc


def _mix_ffn(x, a, b, c, wo, g1, gn, sh2, sc2, g2, wg, wu, wd):
    bn, l, d = x.shape
    tm = min(TOKEN_TILE, l)
    row = pl.BlockSpec((None, 1, d), lambda bb, t: (bb, 0, 0))
    tok = lambda width: pl.BlockSpec((None, tm, width), lambda bb, t: (bb, t, 0))
    return pl.pallas_call(
        functools.partial(_mix_ffn_kernel, fc=256),
        out_shape=jax.ShapeDtypeStruct((bn, l, d), F32),
        grid=(bn, l // tm),
        in_specs=[tok(d), tok(a.shape[2]), tok(b.shape[2]), tok(c.shape[2]), _resident(wo.shape),
                  row, pl.BlockSpec((1, d), lambda bb, t: (0, 0)), row, row, row,
                  _resident(wg.shape), _resident(wu.shape), _resident(wd.shape)],
        out_specs=tok(d),
        compiler_params=_params("parallel", "parallel"),
        name="mix_ffn",
    )(x, a, b, c, wo, g1, gn, sh2, sc2, g2, wg, wu, wd)


def _rope_tables(l):
    half = HEAD_DIM // 2
    nfreq = half // 2
    inv = ROPE_BASE ** (-jnp.arange(nfreq, dtype=F32) / nfreq)
    pos = jnp.arange(l)
    rows, cols = pos // GRID_W, pos % GRID_W
    ang = jnp.concatenate([rows.astype(F32)[:, None] * inv[None, :]] * 2
                          + [cols.astype(F32)[:, None] * inv[None, :]] * 2, axis=-1)
    first = (np.arange(HEAD_DIM) % half) < nfreq
    cos, sin = jnp.cos(ang), jnp.sin(ang)
    sin_a = jnp.where(first[None, :], -sin, 0.0)
    sin_b = jnp.where(first[None, :], 0.0, sin)
    reps = LANES // HEAD_DIM
    return tuple(jnp.tile(tb, (1, reps)) for tb in (cos, sin_a, sin_b))


def kernel(x, c, ctx, c_ctx, ada_w, ada_b, norm1_g, norm2_g, w_in, qnorm_a, knorm_a, sink_a, qnorm_b, knorm_b,
           rpb_b, conv_w, conv_b, filt_w1, filt_b1, filt_freq, filt_w2, filt_b2, filt_w3, hyena_bias, w_out,
           ffn_w_gate, ffn_w_up, ffn_w_down):
    bn, l, d = x.shape
    lc = ctx.shape[1]
    depth = ada_w.shape[0]
    cw = hyena_bias.shape[2]
    assert l % (A_STEP_BLOCKS * A_BLOCK) == 0 and l >= 3 * A_SPAN and lc % 256 == 0
    assert l % (B_GROUP * GRID_W) == 0 and l // GRID_W >= 3 * B_GROUP

    mod_rows = 8 * (-(-(bn + 1) // 8))
    c_all = jnp.zeros((mod_rows, d), F32).at[:bn].set(c).at[bn].set(c_ctx)
    mod = _modulation(c_all, ada_w, ada_b)

    a_order = (0, 2, 1, 3)
    a_kv = tuple(hh // (A_HEADS // A_KV_HEADS) for hh in a_order)
    heads_a = lambda t, axis: [lax.slice_in_dim(t, hh * HEAD_DIM, (hh + 1) * HEAD_DIM, axis=axis) for hh in a_order]
    qa_w = A_HEADS * HEAD_DIM
    w_in_b = jnp.concatenate(heads_a(w_in, 2) + [w_in[:, :, qa_w:]], axis=2).astype(BF16)
    w_out_b = jnp.concatenate(heads_a(w_out, 1) + [w_out[:, qa_w:]], axis=1).astype(BF16)
    sink_p = jnp.stack([sink_a[:, hh] for hh in a_order], axis=1)
    wg_b, wu_b, wd_b = ffn_w_gate.astype(BF16), ffn_w_up.astype(BF16), ffn_w_down.astype(BF16)

    lane = np.arange(256)
    e_heads = jnp.asarray((lane[:, None] // HEAD_DIM) == (lane[None, :] // HEAD_DIM), BF16)
    tile4 = lambda g: jnp.tile(g.reshape(1, HEAD_DIM), (1, 256 // HEAD_DIM))
    rope_lat = _rope_tables(l)
    rope_ctx = tuple(tb[:lc] for tb in rope_lat)

    fh_lat, fh_ctx = min(FREQ_TILE, l // 2), min(FREQ_TILE, lc // 2)
    f_lat, g_lat = _dft_matrices(l // 2, fh_lat)
    f_ctx, g_ctx = _dft_matrices(lc // 2, fh_ctx)
    hc_lat = _hyena_consts(l, cw)
    hc_ctx = _hyena_consts(lc, cw)

    xc = ctx
    for i in range(depth):
        last = i == depth - 1
        m = mod[i]
        part = lambda rows, k: rows[:, None, k * d:(k + 1) * d]
        lat = [part(m[:bn], k) for k in range(6)]
        cx = [jnp.broadcast_to(part(m[bn:bn + 1], k), (bn, 1, d)) for k in range(6)]
        g1n = norm1_g[i].reshape(1, d)
        g2n = norm2_g[i].reshape(1, d)
        gqa, gka, gqb, gkb = tile4(qnorm_a[i]), tile4(knorm_a[i]), tile4(qnorm_b[i]), tile4(knorm_b[i])

        qa, ka, va, qb, kb, vb, hy = _proj_in(x, lat[0], lat[1], g1n, w_in_b[i], gqa, gka, gqb, gkb, e_heads,
                                              rope_lat, rope=True)
        qa_c, ka_c, va_c, qb_c, kb_c, vb_c, hy_c = _proj_in(xc, cx[0], cx[1], g1n, w_in_b[i], gqa, gka, gqb, gkb,
                                                            e_heads, rope_ctx, rope=False)
        w1p = jnp.pad(filt_w1[i], ((0, C_FILTER_WIDTH - filt_w1.shape[1]), (0, 0)))
        hy_args = (conv_w[i], conv_b[i].reshape(1, -1), w1p, filt_b1[i].reshape(1, -1),
                   filt_freq[i].reshape(1, -1), filt_w2[i], filt_b2[i].reshape(1, -1), filt_w3[i], hyena_bias[i])

        sink_row = jnp.repeat(sink_p[i], A_BLOCK).reshape(1, A_HEADS * A_BLOCK)
        out_a = _attn_a(qa, ka, va, ka_c, va_c, sink_row)
        out_b = _attn_b(qb, kb, vb, kb_c, vb_c, _rpb_tables(rpb_b[i]))
        out_c = _hyena(hy, hc_lat, f_lat, g_lat, fh_lat, *hy_args)
        x = _mix_ffn(x, out_a, out_b, out_c, w_out_b[i], lat[2], g2n, lat[3], lat[4], lat[5],
                     wg_b[i], wu_b[i], wd_b[i])

        if not last:
            oa_c = _dense_attn(qa_c, ka_c, va_c, sink_p[i], kv_of=a_kv, use_sink=True)
            ob_c = _dense_attn(qb_c, kb_c, vb_c, sink_p[i], kv_of=tuple(range(B_HEADS)), use_sink=False)
            oc_c = _hyena(hy_c, hc_ctx, f_ctx, g_ctx, fh_ctx, *hy_args)
            xc = _mix_ffn(xc, oa_c, ob_c, oc_c, w_out_b[i], cx[2], g2n, cx[3], cx[4], cx[5],
                          wg_b[i], wu_b[i], wd_b[i])
    return x
```

```python
import functools
import math

import jax
import jax.numpy as jnp
import numpy as np
from jax import lax
from jax.experimental import pallas as pl
from jax.experimental.pallas import tpu as pltpu

F32 = jnp.float32
BF16 = jnp.bfloat16

GRID_W = 64
HEAD_DIM = 64
A_HEADS = 4
A_KV_HEADS = 2
A_BLOCK = 128
B_HEADS = 4
NA_ROWS = 8
NA_COLS = 16
C_ORDER = 2
C_DIRS = 2
C_FILTER_WIDTH = 64
C_BANDS = 16
ROPE_BASE = 10000.0
EPS = 1e-6
NEG_INF = -1e30
HYENA_MIN_DECAY = math.log(1e-2) / 1.5
HYENA_MAX_DECAY = math.log(1e-2) / 0.3

V7X_VMEM_BYTES = 64 * 1024 * 1024
VMEM_LIMIT = V7X_VMEM_BYTES - 8 * 1024 * 1024
LANES = 128
TOKEN_TILE = 512
FREQ_TILE = 256


def _params(*sem):
    return pltpu.CompilerParams(dimension_semantics=sem, vmem_limit_bytes=VMEM_LIMIT)


def _dot(a, b):
    return jnp.dot(a, b, preferred_element_type=F32)


def _dot_nt(a, b):
    return lax.dot_general(a, b, (((1,), (1,)), ((), ())), preferred_element_type=F32)


def _dot_hp(a, b):
    ah, al = _split_bf16(a)
    bh, bl = _split_bf16(b)
    return _dot(ah, bh) + (_dot(ah, bl) + _dot(al, bh))


def _split_bf16(v):
    hi = v.astype(BF16)
    lo = (v - hi.astype(F32)).astype(BF16)
    return hi, lo


def _resident(shape):
    nd = len(shape)
    return pl.BlockSpec(shape, lambda *_: (0,) * nd, pipeline_mode=pl.Buffered(1))


def _mod_kernel(c_ref, w_ref, b_ref, o_ref):
    cv = c_ref[...]
    sc = (cv * jax.nn.sigmoid(cv)).astype(BF16)
    o_ref[...] = _dot(sc, w_ref[...].astype(BF16)) + b_ref[...]


def _modulation(c_all, ada_w, ada_b):
    depth, d, n = ada_w.shape
    rows = c_all.shape[0]
    tn = 512
    return pl.pallas_call(
        _mod_kernel,
        out_shape=jax.ShapeDtypeStruct((depth, rows, n), F32),
        grid=(depth, n // tn),
        in_specs=[
            pl.BlockSpec((rows, d), lambda i, j: (0, 0)),
            pl.BlockSpec((None, d, tn), lambda i, j: (i, 0, j)),
            pl.BlockSpec((None, 1, tn), lambda i, j: (i, 0, j)),
        ],
        out_specs=pl.BlockSpec((None, rows, tn), lambda i, j: (i, 0, j)),
        compiler_params=_params("arbitrary", "arbitrary"),
        name="modulation",
    )(c_all, ada_w, ada_b.reshape(depth, 1, n))


def _rms_mod(x, g, shift, scale):
    y = x * lax.rsqrt(jnp.mean(x * x, axis=-1, keepdims=True) + EPS)
    return (y * g) * (1 + scale) + shift


def _head_norm(h, g, e):
    ss = _dot((h * h).astype(BF16), e)
    return (h * lax.rsqrt(ss * (1.0 / HEAD_DIM) + EPS)) * g


def _rope(t, cos, sin_a, sin_b):
    outs = []
    for c in range(t.shape[1] // LANES):
        tc = t[:, c * LANES:(c + 1) * LANES]
        outs.append(tc * cos + pltpu.roll(tc, LANES - 16, 1) * sin_a + pltpu.roll(tc, 16, 1) * sin_b)
    return outs[0] if len(outs) == 1 else jnp.concatenate(outs, axis=1)


def _proj_in_kernel(x_ref, shift_ref, scale_ref, g_ref, w_ref, gqa_ref, gka_ref, gqb_ref, gkb_ref, e_ref,
                    cos_ref, sa_ref, sb_ref,
                    qa_ref, ka_ref, va_ref, qb_ref, kb_ref, vb_ref, hy_ref, *, rope):
    xb = _rms_mod(x_ref[...], g_ref[...], shift_ref[...], scale_ref[...]).astype(BF16)
    qscale = HEAD_DIM ** -0.5

    def proj(lo, hi):
        return _dot(xb, w_ref[:, lo:hi])

    def maybe_rope(t):
        return _rope(t, cos_ref[...], sa_ref[...], sb_ref[...]) if rope else t

    qa = maybe_rope(_head_norm(proj(0, 256), gqa_ref[...], e_ref[...]))
    qa_ref[...] = (qa * qscale).astype(BF16)
    ka = maybe_rope(_head_norm(proj(256, 384), gka_ref[:, :128], e_ref[:128, :128]))
    ka_ref[...] = ka.astype(BF16)

    def store_v(ref, v):
        if rope:
            for i in range(v.shape[0] // LANES):
                ref[i] = v[i * LANES:(i + 1) * LANES, :].T.astype(BF16)
        else:
            ref[...] = v.astype(BF16)

    store_v(va_ref, proj(384, 512))
    qb = _head_norm(proj(512, 768), gqb_ref[...], e_ref[...])
    qb_ref[...] = (qb * qscale).astype(BF16)
    kb_ref[...] = _head_norm(proj(768, 1024), gkb_ref[...], e_ref[...]).astype(BF16)
    store_v(vb_ref, proj(1024, 1280))
    hy_ref[...] = proj(1280, 2816).astype(BF16)


def _proj_in(x, shift, scale, g, w, gqa, gka, gqb, gkb, e, rope_tabs, *, rope):
    bn, l, d = x.shape
    n = w.shape[1]
    tm = min(TOKEN_TILE, l)
    row = lambda width: pl.BlockSpec((None, 1, width), lambda b, t: (b, 0, 0))
    const = lambda shape: pl.BlockSpec(shape, lambda b, t: (0,) * len(shape))
    tok = lambda width: pl.BlockSpec((None, tm, width), lambda b, t: (b, t, 0))
    tab = pl.BlockSpec((tm, LANES), lambda b, t: (t, 0))
    widths = (256, 128, 128, 256, 256, 256, n - 1280)
    shapes = [(bn, l, wd) for wd in widths]
    specs = [tok(wd) for wd in widths]
    if rope:
        for i in (2, 5):
            shapes[i] = (bn, l // LANES, widths[i], LANES)
            specs[i] = pl.BlockSpec((None, tm // LANES, widths[i], LANES), lambda b, t: (b, t, 0, 0))
    return pl.pallas_call(
        functools.partial(_proj_in_kernel, rope=rope),
        out_shape=[jax.ShapeDtypeStruct(sh, BF16) for sh in shapes],
        grid=(bn, l // tm),
        in_specs=[tok(d), row(d), row(d), const((1, d)), _resident((d, n)),
                  const((1, 256)), const((1, 256)), const((1, 256)), const((1, 256)), const((256, 256)),
                  tab, tab, tab],
        out_specs=specs,
        compiler_params=_params("parallel", "parallel"),
        name="proj_in_rope" if rope else "proj_in",
    )(x, shift, scale, g, w, gqa, gka, gqb, gkb, e, *rope_tabs)


def _softmax_pv(s_list, v_list, sink):
    m = s_list[0].max(axis=-1, keepdims=True)
    for s in s_list[1:]:
        m = jnp.maximum(m, s.max(axis=-1, keepdims=True))
    if sink is not None:
        m = jnp.maximum(m, sink)
    den = None
    out = None
    for s, v in zip(s_list, v_list):
        p = jnp.exp(s - m)
        ps = p.sum(axis=-1, keepdims=True)
        den = ps if den is None else den + ps
        o = _dot(p.astype(BF16), v)
        out = o if out is None else out + o
    if sink is not None:
        den = den + jnp.exp(sink - m)
    return out * (1.0 / den)


def _stack_heads(q):
    lo = lax.broadcasted_iota(jnp.int32, q.shape, 1) < HEAD_DIM
    zero = jnp.zeros_like(q)
    return jnp.concatenate([jnp.where(lo, q, zero), jnp.where(lo, zero, q)], axis=0)


def _unstack_heads(o):
    m = o.shape[0] // 2
    lo = lax.broadcasted_iota(jnp.int32, (m, LANES), 1) < HEAD_DIM
    return jnp.where(lo, o[:m], o[m:])


SOFTMAX_CHUNK = 32


def _softmax_keys(s_ref, p_ref, nloc, add_loc, sink):
    nk = s_ref.shape[0]
    ch = SOFTMAX_CHUNK

    def scores(r0):
        s = s_ref[r0:r0 + ch, :]
        return s + add_loc(r0) if r0 < nloc else s

    macc = scores(0)
    for r0 in range(ch, nk, ch):
        macc = jnp.maximum(macc, scores(r0))
    m = macc.max(axis=0, keepdims=True)
    if sink is not None:
        m = jnp.maximum(m, sink)
    sacc = None
    for r0 in range(0, nk, ch):
        p = jnp.exp(scores(r0) - m)
        sacc = p if sacc is None else sacc + p
        p_ref[r0:r0 + ch, :] = p.astype(BF16)
    den = sacc.sum(axis=0, keepdims=True)
    if sink is not None:
        den = den + jnp.exp(sink - m)
    return 1.0 / den


A_STEP_BLOCKS = 4
A_SPAN = 3 * A_BLOCK


def _attn_a_kernel(q_ref, k_ref, vt_ref, kx_ref, vxt_ref, mask_ref, sink_ref, o_ref, s_ref, p_ref):
    l = k_ref.shape[0]
    nb = l // A_BLOCK
    hd = HEAD_DIM
    for u in range(A_STEP_BLOCKS):
        n = pl.program_id(1) * A_STEP_BLOCKS + u
        tile0 = jnp.clip(n - 1, 0, nb - 3)
        start = pl.multiple_of(tile0 * A_BLOCK, A_BLOCK)
        pat = jnp.where(n == 0, 0, jnp.where(n == nb - 1, 2, 1))
        q = q_ref[u * A_BLOCK:(u + 1) * A_BLOCK, :]
        qs = jnp.concatenate([_stack_heads(q[:, :LANES]), _stack_heads(q[:, LANES:])], axis=0)
        sb, pb = s_ref.at[u % 2], p_ref.at[u % 2]
        sb[:A_SPAN, :] = _dot_nt(k_ref[pl.ds(start, A_SPAN), :], qs)
        sb[A_SPAN:, :] = _dot_nt(kx_ref[...], qs)
        r = _softmax_keys(sb, pb, A_SPAN, lambda r0: mask_ref[pat, r0:r0 + SOFTMAX_CHUNK, :], sink_ref[...])
        vt = jnp.concatenate([vt_ref[tile0 + i] for i in range(3)], axis=1)
        ot = (_dot(vt, pb[:A_SPAN, :]) + _dot(vxt_ref[...], pb[A_SPAN:, :])) * r
        ot = jnp.concatenate([ot[(i % 2) * hd:(i % 2 + 1) * hd, i * A_BLOCK:(i + 1) * A_BLOCK]
                              for i in range(A_HEADS)], axis=0)
        o_ref[u * A_BLOCK:(u + 1) * A_BLOCK, :] = ot.T.astype(o_ref.dtype)


def _attn_a_mask():
    i = np.arange(A_BLOCK)[None, :]
    j = np.arange(A_SPAN)[:, None]
    offs = (0, A_BLOCK, 2 * A_BLOCK)
    m = np.stack([np.where(np.abs(j - i - o) <= A_BLOCK, 0.0, -np.inf) for o in offs])
    return jnp.asarray(np.tile(m, (1, 1, A_HEADS)), F32)


def _attn_a(q, k, vt, kx, vx, sink_row):
    bn, l, qw = q.shape
    lc = kx.shape[1]
    kvw = k.shape[2]
    nb = l // A_BLOCK
    qs = A_STEP_BLOCKS * A_BLOCK
    mask = _attn_a_mask()
    vxt = jnp.swapaxes(vx, 1, 2)
    seq = pl.BlockSpec((None, l, kvw), lambda b, s: (b, 0, 0))
    qblk = pl.BlockSpec((None, qs, qw), lambda b, s: (b, s, 0))
    nq = A_HEADS * A_BLOCK
    return pl.pallas_call(
        _attn_a_kernel,
        out_shape=jax.ShapeDtypeStruct((bn, l, qw), BF16),
        grid=(bn, l // qs),
        in_specs=[qblk, seq, pl.BlockSpec((None, nb, kvw, A_BLOCK), lambda b, s: (b, 0, 0, 0)),
                  pl.BlockSpec((None, lc, kvw), lambda b, s: (b, 0, 0)),
                  pl.BlockSpec((None, kvw, lc), lambda b, s: (b, 0, 0)),
                  _resident(mask.shape), _resident(sink_row.shape)],
        out_specs=qblk,
        scratch_shapes=[pltpu.VMEM((2, A_SPAN + lc, nq), F32), pltpu.VMEM((2, A_SPAN + lc, nq), BF16)],
        compiler_params=_params("parallel", "arbitrary"),
        name="window_attn",
    )(q, k, vt, kx, vxt, mask, sink_row)


def _dense_attn_kernel(sink_ref, q_ref, k_ref, v_ref, o_ref, *, kv_of, use_sink):
    for h, kv in enumerate(kv_of):
        hs = slice(h * HEAD_DIM, (h + 1) * HEAD_DIM)
        ks = slice(kv * HEAD_DIM, (kv + 1) * HEAD_DIM)
        s = _dot_nt(q_ref[:, hs], k_ref[:, ks])
        o = _softmax_pv([s], [v_ref[:, ks]], sink_ref[h] if use_sink else None)
        o_ref[:, hs] = o.astype(o_ref.dtype)


def _dense_attn(q, k, v, sink, *, kv_of, use_sink):
    bn, l, qw = q.shape
    kvw = k.shape[2]
    full = lambda wd: pl.BlockSpec((None, l, wd), lambda b: (b, 0, 0))
    return pl.pallas_call(
        functools.partial(_dense_attn_kernel, kv_of=kv_of, use_sink=use_sink),
        out_shape=jax.ShapeDtypeStruct((bn, l, qw), BF16),
        grid=(bn,),
        in_specs=[pl.BlockSpec(memory_space=pltpu.SMEM), full(qw), full(kvw), full(kvw)],
        out_specs=full(qw),
        compiler_params=_params("parallel"),
        name="ctx_attn_sink" if use_sink else "ctx_attn",
    )(sink, q, k, v)


def _rpb_kernel(r_ref, oh_ref, o_ref):
    r = r_ref[...]
    b1 = r.astype(BF16)
    r2 = r - b1.astype(F32)
    b2 = r2.astype(BF16)
    b3 = (r2 - b2.astype(F32)).astype(BF16)
    oh = oh_ref[...]
    o_ref[...] = (_dot(b1, oh) + _dot(b2, oh)) + _dot(b3, oh)


def _rpb_tables(rpb):
    h, nr, nc = rpb.shape
    col = np.arange(GRID_W)
    dc = np.clip(col[None, :] - col[:, None], 1 - NA_COLS, NA_COLS - 1) + NA_COLS - 1
    onehot = (dc.reshape(1, -1) == np.arange(nc)[:, None]).astype(np.float32)
    onehot = np.concatenate([onehot, np.zeros((32 - nc, GRID_W * GRID_W), np.float32)], axis=0)
    rows = 64
    r2 = jnp.zeros((rows, 32), F32).at[:h * nr, :nc].set(rpb.reshape(h * nr, nc))
    full = pl.pallas_call(
        _rpb_kernel,
        out_shape=jax.ShapeDtypeStruct((rows, GRID_W * GRID_W), F32),
        name="rpb_table",
    )(r2, jnp.asarray(onehot, BF16))
    full = full[:h * nr].reshape(h, nr, GRID_W, GRID_W)
    a = np.arange(B_GROUP)[:, None]
    kr = np.arange(B_SLAB)[None, :]
    dr = np.stack([kr - a + NA_ROWS - 1, kr - a + NA_ROWS // 2 - 1, kr - a + (B_GROUP + NA_ROWS - 1 - B_SLAB)])
    lo = np.stack([0 * a + 0 * kr, a + 0 * kr, 0 * a + (B_SLAB - NA_ROWS) + 0 * kr])
    valid = (kr[None] >= lo) & (kr[None] < lo + NA_ROWS)
    col_start = np.clip(col - NA_COLS // 2, 0, GRID_W - NA_COLS)
    col_ok = (col[None, :] >= col_start[:, None]) & (col[None, :] < col_start[:, None] + NA_COLS)
    ok = valid[:, :, None, :, None] & col_ok[None, None, :, None, :]
    tbl = full[:, np.clip(dr, 0, nr - 1)]
    tbl = jnp.transpose(tbl, (1, 0, 2, 4, 3, 5))
    tbl = jnp.where(ok[:, None], tbl, -jnp.inf)
    tbl = tbl.reshape(3, h // 2, 2, B_GROUP, GRID_W, B_SLAB, GRID_W)
    tbl = jnp.transpose(tbl, (0, 1, 5, 6, 2, 3, 4))
    return tbl.reshape(3, h // 2, B_SLAB * GRID_W, 2 * B_GROUP * GRID_W)


B_GROUP = 4
B_SLAB = 12


def _attn_b_kernel(q_ref, k_ref, vt_ref, kx_ref, vxt_ref, tbl_ref, o_ref, s_ref, p_ref):
    g = pl.program_id(1)
    ng = pl.num_programs(1)
    rows = k_ref.shape[0] // GRID_W
    nloc = B_SLAB * GRID_W
    gq = B_GROUP * GRID_W
    base = jnp.clip(g * B_GROUP - NA_ROWS // 2, 0, rows - B_SLAB)
    start = pl.multiple_of(base * GRID_W, LANES)
    tile0 = base // (LANES // GRID_W)
    pat = jnp.where(g == 0, 0, jnp.where(g == ng - 1, 2, 1))
    for t in range(B_HEADS // 2):
        ts = slice(t * LANES, (t + 1) * LANES)
        qs = _stack_heads(q_ref[:, ts])
        sb, pb = s_ref.at[t], p_ref.at[t]
        sb[:nloc, :] = _dot_nt(k_ref[pl.ds(start, nloc), ts], qs)
        sb[nloc:, :] = _dot_nt(kx_ref[:, ts], qs)
        r = _softmax_keys(sb, pb, nloc, lambda r0: tbl_ref[pat, t, r0:r0 + SOFTMAX_CHUNK, :], None)
        vt = jnp.concatenate([vt_ref[tile0 + i, ts, :] for i in range(nloc // LANES)], axis=1)
        ot = (_dot(vt, pb[:nloc, :]) + _dot(vxt_ref[ts, :], pb[nloc:, :])) * r
        ot = jnp.concatenate([ot[:HEAD_DIM, :gq], ot[HEAD_DIM:, gq:]], axis=0)
        o_ref[:, ts] = ot.T.astype(o_ref.dtype)


def _attn_b(q, k, vt, kx, vx, tbl):
    bn, l, w = q.shape
    lc = kx.shape[1]
    gq = B_GROUP * GRID_W
    nk = B_SLAB * GRID_W + lc
    vxt = jnp.swapaxes(vx, 1, 2)
    seq = pl.BlockSpec((None, l, w), lambda b, g: (b, 0, 0))
    qblk = pl.BlockSpec((None, gq, w), lambda b, g: (b, g, 0))
    return pl.pallas_call(
        _attn_b_kernel,
        out_shape=jax.ShapeDtypeStruct((bn, l, w), BF16),
        grid=(bn, l // gq),
        in_specs=[qblk, seq, pl.BlockSpec((None, l // LANES, w, LANES), lambda b, g: (b, 0, 0, 0)),
                  pl.BlockSpec((None, lc, w), lambda b, g: (b, 0, 0)),
                  pl.BlockSpec((None, w, lc), lambda b, g: (b, 0, 0)),
                  _resident(tbl.shape)],
        out_specs=qblk,
        scratch_shapes=[pltpu.VMEM((B_HEADS // 2, nk, 2 * gq), F32), pltpu.VMEM((B_HEADS // 2, nk, 2 * gq), BF16)],
        compiler_params=_params("parallel", "arbitrary"),
        name="nbr_attn",
    )(q, k, vt, kx, vxt, tbl)


def _dft_matrices(l, fh):
    k = jnp.arange(l, dtype=jnp.int32)[:, None]
    n = jnp.arange(l, dtype=jnp.int32)[None, :]
    ang = (((2 * k + 1) * n) % (4 * l)).astype(F32) * (math.pi / (2 * l))
    fre = jnp.cos(ang).reshape(l // fh, fh, l)
    fim = (-jnp.sin(ang)).reshape(l // fh, fh, l)
    f = jnp.concatenate([fre, fim], axis=1).reshape(2 * l, l)
    g = f.T * (1.0 / l)
    return f.astype(BF16), g.astype(BF16)


def _filter_kernel(z_ref, t_ref, w1_ref, b1_ref, fr_ref, w2_ref, b2_ref, w3_ref, dl_ref,
                   ah_ref, al_ref, sh_ref, sl_ref):
    fr = fr_ref[...]
    hid = jnp.sin(fr * (_dot_hp(z_ref[...], w1_ref[...]) + b1_ref[...]))
    hid = jnp.sin(fr * (_dot_hp(hid, w2_ref[...]) + b2_ref[...]))
    cw = dl_ref.shape[1]
    p = z_ref.shape[0] // 3
    decay = jnp.exp(-t_ref[...] * dl_ref[...])
    kf = _dot_hp(hid, w3_ref[:, :cw]) * decay
    kb = _dot_hp(hid, w3_ref[:, cw:]) * decay
    kf0, kf1, kfr = kf[:p], kf[p:2 * p], kf[2 * p:]
    kb0, kb1, kbr = kb[:p], kb[p:2 * p], kb[2 * p:]
    first = lax.broadcasted_iota(jnp.int32, (p, cw), 0) == 0
    drop0 = lambda v: jnp.where(first, 0.0, v)
    kb0 = drop0(kb0)
    colsum = lambda v: jnp.sum(jnp.abs(v), axis=0, keepdims=True)
    inv = 1.0 / (colsum(kf0) + colsum(kf1) + colsum(kb0) + colsum(kb1))
    pairs = ((kf0, kb0), (kf1, drop0(kfr)), (kbr, drop0(kb1)))
    for d, (cp, cm) in enumerate(pairs):
        cols = slice(d * cw, (d + 1) * cw)
        ah_ref[:, cols], al_ref[:, cols] = _split_bf16((cp + cm) * inv)
        sh_ref[:, cols], sl_ref[:, cols] = _split_bf16((cp - cm) * inv)


def _filters(z, t, w1, b1, freq, w2, b2, w3, deltas):
    p = z.shape[0] // 3
    cw = deltas.shape[1]
    fw = w2.shape[0]
    const = lambda shape: pl.BlockSpec(shape, lambda o: (0,) * len(shape))
    out = jax.ShapeDtypeStruct((p, C_ORDER * 3 * cw), BF16)
    oblk = pl.BlockSpec((p, 3 * cw), lambda o: (0, o))
    return pl.pallas_call(
        _filter_kernel,
        out_shape=[out] * 4,
        grid=(C_ORDER,),
        in_specs=[const(z.shape), const(t.shape), const(w1.shape), const((1, fw)), const((1, fw)),
                  const((fw, fw)), const((1, fw)), pl.BlockSpec((fw, C_DIRS * cw), lambda o: (0, o)),
                  const((1, cw))],
        out_specs=[oblk] * 4,
        compiler_params=_params("arbitrary"),
        name="hyena_filter",
    )(z, t, w1, b1, freq, w2, b2, w3, deltas)


def _spectrum_kernel(f_ref, ah_ref, al_ref, sh_ref, sl_ref, kre_ref, kim_ref):
    fh = kre_ref.shape[0]
    fre = f_ref[:fh, :]
    fim = f_ref[fh:, :]
    kre_ref[...] = _dot(fre, ah_ref[...]) + _dot(fre, al_ref[...])
    kim_ref[...] = _dot(fim, sh_ref[...]) + _dot(fim, sl_ref[...])


def _spectrum(f, ah, al, sh, sl, fh):
    p, n = ah.shape
    tn = n // C_ORDER
    taps = pl.BlockSpec((p, tn), lambda o, j: (0, o))
    out = jax.ShapeDtypeStruct((p, n), F32)
    oblk = pl.BlockSpec((fh, tn), lambda o, j: (j, o))
    return pl.pallas_call(
        _spectrum_kernel,
        out_shape=[out, out],
        grid=(C_ORDER, p // fh),
        in_specs=[pl.BlockSpec((2 * fh, p), lambda o, j: (j, 0)), taps, taps, taps, taps],
        out_specs=[oblk, oblk],
        compiler_params=_params("arbitrary", "arbitrary"),
        name="hyena_spectrum",
    )(f, ah, al, sh, sl)


def _short_conv(u, w_ref, b_ref):
    n = u.shape[0]
    row = lax.broadcasted_iota(jnp.int32, u.shape, 0)
    prev = jnp.where(row == 0, 0.0, pltpu.roll(u, 1, 0))
    nxt = jnp.where(row == n - 1, 0.0, pltpu.roll(u, n - 1, 0))
    return prev * w_ref[0:1, :] + u * w_ref[1:2, :] + nxt * w_ref[2:3, :] + b_ref[...]


def _short_conv_wrap(u, w_ref, b_ref):
    n = u.shape[0]
    return (pltpu.roll(u, 1, 0) * w_ref[0:1, :] + u * w_ref[1:2, :] + pltpu.roll(u, n - 1, 0) * w_ref[2:3, :]
            + b_ref[...])


def _longconv_kernel(u_ref, g_ref, cwu_ref, cbu_ref, cwg_ref, cbg_ref, f_ref, gm_ref, kre_ref, kim_ref, d_ref,
                     o_ref, ub_ref, acc_ref, *, conv_u):
    j = pl.program_id(1)
    fh = kre_ref.shape[0]
    p, cw = ub_ref.shape[0], d_ref.shape[1]

    l = 2 * p
    edge = 16

    def edge_conv(ref, w_ref, b_ref):
        head = _short_conv(ref[0:2 * edge, :].astype(F32), w_ref, b_ref)[:edge]
        tail = _short_conv(ref[l - 2 * edge:l, :].astype(F32), w_ref, b_ref)[edge:]
        return head, tail

    @pl.when(j == 0)
    def _():
        if conv_u:
            u = _short_conv_wrap(u_ref[...].astype(F32), cwu_ref, cbu_ref).astype(BF16)
        else:
            u = u_ref[...]
        ub_ref[:, :cw] = u[:p]
        ub_ref[:, cw:] = u[p:]
        if conv_u:
            head, tail = edge_conv(u_ref, cwu_ref, cbu_ref)
            ub_ref[0:edge, :cw] = head.astype(BF16)
            ub_ref[p - edge:p, cw:] = tail.astype(BF16)
        acc_ref[...] = jnp.zeros_like(acc_ref)

    spec = _dot(f_ref[...], ub_ref[...])
    u0r, u1r, u0i, u1i = spec[:fh, :cw], spec[:fh, cw:], spec[fh:, :cw], spec[fh:, cw:]
    tap = lambda ref, d: ref[:, d * cw:(d + 1) * cw]
    c0r, c1r, cmr = (tap(kre_ref, d) for d in range(3))
    c0i, c1i, cmi = (tap(kim_ref, d) for d in range(3))
    y0r = (c0r * u0r - c0i * u0i) + (cmr * u1r - cmi * u1i)
    y0i = (c0r * u0i + c0i * u0r) + (cmr * u1i + cmi * u1r)
    y1r = (c1r * u0r - c1i * u0i) + (c0r * u1r - c0i * u1i)
    y1i = (c1r * u0i + c1i * u0r) + (c0r * u1i + c0i * u1r)
    y = jnp.concatenate([jnp.concatenate([y0r, y1r], axis=1), jnp.concatenate([y0i, y1i], axis=1)], axis=0)
    acc_ref[...] += _dot(gm_ref[...], y.astype(BF16))

    @pl.when(j == pl.num_programs(1) - 1)
    def _():
        gate = _short_conv_wrap(g_ref[...].astype(F32), cwg_ref, cbg_ref)
        y = jnp.concatenate([acc_ref[:, :cw], acc_ref[:, cw:]], axis=0)
        y = y + jnp.concatenate([ub_ref[:, :cw], ub_ref[:, cw:]], axis=0).astype(F32) * d_ref[...]
        o_ref[...] = (gate * y).astype(o_ref.dtype)
        head, tail = edge_conv(g_ref, cwg_ref, cbg_ref)
        o_ref[0:edge, :] = (head * y[:edge]).astype(o_ref.dtype)
        o_ref[l - edge:l, :] = (tail * y[l - edge:]).astype(o_ref.dtype)


def _longconv(u_arr, u_blk, g_arr, g_blk, conv_w, conv_b, f, gm, kre, kim, order, d, *, conv_u, fh):
    bn, l, _ = u_arr.shape
    p = l // 2
    cw = d.shape[1]
    ub = u_blk if conv_u else 0
    tok = lambda blk: pl.BlockSpec((None, l, cw), lambda b, j: (b, 0, blk))
    cpar = lambda rows, blk: pl.BlockSpec((rows, cw), lambda b, j: (0, blk))
    ktab = pl.BlockSpec((fh, 3 * cw), lambda b, j: (j, order))
    return pl.pallas_call(
        functools.partial(_longconv_kernel, conv_u=conv_u),
        out_shape=jax.ShapeDtypeStruct((bn, l, cw), BF16),
        grid=(bn, p // fh),
        in_specs=[tok(u_blk), tok(g_blk), cpar(3, ub), cpar(1, ub), cpar(3, g_blk), cpar(1, g_blk),
                  pl.BlockSpec((2 * fh, p), lambda b, j: (j, 0)),
                  pl.BlockSpec((p, 2 * fh), lambda b, j: (0, j)),
                  ktab, ktab, pl.BlockSpec((1, cw), lambda b, j: (0, 0))],
        out_specs=tok(0),
        scratch_shapes=[pltpu.VMEM((p, 2 * cw), BF16), pltpu.VMEM((p, 2 * cw), F32)],
        compiler_params=_params("parallel", "arbitrary"),
        name="hyena_longconv",
    )(u_arr, g_arr, conv_w, conv_b, conv_w, conv_b, f, gm, kre, kim, d)


def _hyena_consts(l, cw):
    p = l // 2
    e = np.arange(p)
    pos = np.concatenate([e, p + e, p - e])
    t = np.linspace(0.0, 1.0, l, dtype=np.float32)[pos][:, None]
    w = (2.0 * math.pi * np.arange(l, dtype=np.float32) / l).astype(np.float32)[pos][:, None]
    fq = np.linspace(1e-4, C_BANDS - 1, C_BANDS, dtype=np.float32)[None, :]
    wf = jnp.asarray(w) * jnp.asarray(fq)
    z = jnp.concatenate([jnp.asarray(t), jnp.cos(wf), -jnp.sin(wf)], axis=-1)
    z = jnp.pad(z, ((0, 0), (0, C_FILTER_WIDTH - z.shape[1])))
    deltas = np.abs(np.linspace(HYENA_MIN_DECAY, HYENA_MAX_DECAY, cw, dtype=np.float32))[None, :]
    return z, jnp.asarray(t), jnp.asarray(deltas)


def _hyena(hy, hconst, f, gm, fh, conv_w, conv_b, w1p, b1, freq, w2, b2, w3, dbias):
    z, t, deltas = hconst
    ah, al, sh, sl = _filters(z, t, w1p, b1, freq, w2, b2, w3, deltas)
    kre, kim = _spectrum(f, ah, al, sh, sl, fh)
    zz = _longconv(hy, 0, hy, 1, conv_w, conv_b, f, gm, kre, kim, 0, dbias[0:1], conv_u=True, fh=fh)
    return _longconv(zz, 0, hy, 2, conv_w, conv_b, f, gm, kre, kim, 1, dbias[1:2], conv_u=False, fh=fh)


def _mix_ffn_kernel(x_ref, a_ref, b_ref, c_ref, wo_ref, g1_ref, gn_ref, sh_ref, sc_ref, g2_ref,
                    wg_ref, wu_ref, wd_ref, o_ref, *, fc):
    wa, wb = a_ref.shape[1], b_ref.shape[1]
    mix = (_dot(a_ref[...], wo_ref[:wa, :]) + _dot(b_ref[...], wo_ref[wa:wa + wb, :])
           + _dot(c_ref[...], wo_ref[wa + wb:, :]))
    x1 = x_ref[...] + g1_ref[...] * mix
    xb = _rms_mod(x1, gn_ref[...], sh_ref[...], sc_ref[...]).astype(BF16)
    acc = None
    for c0 in range(0, wg_ref.shape[1], fc):
        hg = _dot(xb, wg_ref[:, c0:c0 + fc])
        hu = _dot(xb, wu_ref[:, c0:c0 + fc])
        act = ((hg * jax.nn.sigmoid(hg)) * hu).astype(BF16)
        part = _dot(act, wd_ref[c0:c0 + fc, :])
        acc = part if acc is None else acc + part
    o_ref[...] = x1 + g2_ref[...] * acc


def _mix_ffn(x, a, b, c, wo, g1, gn, sh2, sc2, g2, wg, wu, wd):
    bn, l, d = x.shape
    tm = min(TOKEN_TILE, l)
    row = pl.BlockSpec((None, 1, d), lambda bb, t: (bb, 0, 0))
    tok = lambda width: pl.BlockSpec((None, tm, width), lambda bb, t: (bb, t, 0))
    return pl.pallas_call(
        functools.partial(_mix_ffn_kernel, fc=256),
        out_shape=jax.ShapeDtypeStruct((bn, l, d), F32),
        grid=(bn, l // tm),
        in_specs=[tok(d), tok(a.shape[2]), tok(b.shape[2]), tok(c.shape[2]), _resident(wo.shape),
                  row, pl.BlockSpec((1, d), lambda bb, t: (0, 0)), row, row, row,
                  _resident(wg.shape), _resident(wu.shape), _resident(wd.shape)],
        out_specs=tok(d),
        compiler_params=_params("parallel", "parallel"),
        name="mix_ffn",
    )(x, a, b, c, wo, g1, gn, sh2, sc2, g2, wg, wu, wd)


def _rope_tables(l):
    half = HEAD_DIM // 2
    nfreq = half // 2
    inv = ROPE_BASE ** (-jnp.arange(nfreq, dtype=F32) / nfreq)
    pos = jnp.arange(l)
    rows, cols = pos // GRID_W, pos % GRID_W
    ang = jnp.concatenate([rows.astype(F32)[:, None] * inv[None, :]] * 2
                          + [cols.astype(F32)[:, None] * inv[None, :]] * 2, axis=-1)
    first = (np.arange(HEAD_DIM) % half) < nfreq
    cos, sin = jnp.cos(ang), jnp.sin(ang)
    sin_a = jnp.where(first[None, :], -sin, 0.0)
    sin_b = jnp.where(first[None, :], 0.0, sin)
    reps = LANES // HEAD_DIM
    return tuple(jnp.tile(tb, (1, reps)) for tb in (cos, sin_a, sin_b))


def kernel(x, c, ctx, c_ctx, ada_w, ada_b, norm1_g, norm2_g, w_in, qnorm_a, knorm_a, sink_a, qnorm_b, knorm_b,
           rpb_b, conv_w, conv_b, filt_w1, filt_b1, filt_freq, filt_w2, filt_b2, filt_w3, hyena_bias, w_out,
           ffn_w_gate, ffn_w_up, ffn_w_down):
    bn, l, d = x.shape
    lc = ctx.shape[1]
    depth = ada_w.shape[0]
    cw = hyena_bias.shape[2]
    assert l % (A_STEP_BLOCKS * A_BLOCK) == 0 and l >= 3 * A_SPAN and lc % 256 == 0
    assert l % (B_GROUP * GRID_W) == 0 and l // GRID_W >= 3 * B_GROUP

    mod_rows = 8 * (-(-(bn + 1) // 8))
    c_all = jnp.zeros((mod_rows, d), F32).at[:bn].set(c).at[bn].set(c_ctx)
    mod = _modulation(c_all, ada_w, ada_b)

    a_order = (0, 2, 1, 3)
    a_kv = tuple(hh // (A_HEADS // A_KV_HEADS) for hh in a_order)
    heads_a = lambda t, axis: [lax.slice_in_dim(t, hh * HEAD_DIM, (hh + 1) * HEAD_DIM, axis=axis) for hh in a_order]
    qa_w = A_HEADS * HEAD_DIM
    w_in_b = jnp.concatenate(heads_a(w_in, 2) + [w_in[:, :, qa_w:]], axis=2).astype(BF16)
    w_out_b = jnp.concatenate(heads_a(w_out, 1) + [w_out[:, qa_w:]], axis=1).astype(BF16)
    sink_p = jnp.stack([sink_a[:, hh] for hh in a_order], axis=1)
    wg_b, wu_b, wd_b = ffn_w_gate.astype(BF16), ffn_w_up.astype(BF16), ffn_w_down.astype(BF16)

    lane = np.arange(256)
    e_heads = jnp.asarray((lane[:, None] // HEAD_DIM) == (lane[None, :] // HEAD_DIM), BF16)
    tile4 = lambda g: jnp.tile(g.reshape(1, HEAD_DIM), (1, 256 // HEAD_DIM))
    rope_lat = _rope_tables(l)
    rope_ctx = tuple(tb[:lc] for tb in rope_lat)

    fh_lat, fh_ctx = min(FREQ_TILE, l // 2), min(FREQ_TILE, lc // 2)
    f_lat, g_lat = _dft_matrices(l // 2, fh_lat)
    f_ctx, g_ctx = _dft_matrices(lc // 2, fh_ctx)
    hc_lat = _hyena_consts(l, cw)
    hc_ctx = _hyena_consts(lc, cw)

    xc = ctx
    for i in range(depth):
        last = i == depth - 1
        m = mod[i]
        part = lambda rows, k: rows[:, None, k * d:(k + 1) * d]
        lat = [part(m[:bn], k) for k in range(6)]
        cx = [jnp.broadcast_to(part(m[bn:bn + 1], k), (bn, 1, d)) for k in range(6)]
        g1n = norm1_g[i].reshape(1, d)
        g2n = norm2_g[i].reshape(1, d)
        gqa, gka, gqb, gkb = tile4(qnorm_a[i]), tile4(knorm_a[i]), tile4(qnorm_b[i]), tile4(knorm_b[i])

        qa, ka, va, qb, kb, vb, hy = _proj_in(x, lat[0], lat[1], g1n, w_in_b[i], gqa, gka, gqb, gkb, e_heads,
                                              rope_lat, rope=True)
        qa_c, ka_c, va_c, qb_c, kb_c, vb_c, hy_c = _proj_in(xc, cx[0], cx[1], g1n, w_in_b[i], gqa, gka, gqb, gkb,
                                                            e_heads, rope_ctx, rope=False)
        w1p = jnp.pad(filt_w1[i], ((0, C_FILTER_WIDTH - filt_w1.shape[1]), (0, 0)))
        hy_args = (conv_w[i], conv_b[i].reshape(1, -1), w1p, filt_b1[i].reshape(1, -1),
                   filt_freq[i].reshape(1, -1), filt_w2[i], filt_b2[i].reshape(1, -1), filt_w3[i], hyena_bias[i])

        sink_row = jnp.repeat(sink_p[i], A_BLOCK).reshape(1, A_HEADS * A_BLOCK)
        out_a = _attn_a(qa, ka, va, ka_c, va_c, sink_row)
        out_b = _attn_b(qb, kb, vb, kb_c, vb_c, _rpb_tables(rpb_b[i]))
        out_c = _hyena(hy, hc_lat, f_lat, g_lat, fh_lat, *hy_args)
        x = _mix_ffn(x, out_a, out_b, out_c, w_out_b[i], lat[2], g2n, lat[3], lat[4], lat[5],
                     wg_b[i], wu_b[i], wd_b[i])

        if not last:
            oa_c = _dense_attn(qa_c, ka_c, va_c, sink_p[i], kv_of=a_kv, use_sink=True)
            ob_c = _dense_attn(qb_c, kb_c, vb_c, sink_p[i], kv_of=tuple(range(B_HEADS)), use_sink=False)
            oc_c = _hyena(hy_c, hc_ctx, f_ctx, g_ctx, fh_ctx, *hy_args)
            xc = _mix_ffn(xc, oa_c, ob_c, oc_c, w_out_b[i], cx[2], g2n, cx[3], cx[4], cx[5],
                          wg_b[i], wu_b[i], wd_b[i])
    return x
```

```python
import functools
import math

import jax
import jax.numpy as jnp
import numpy as np
from jax import lax
from jax.experimental import pallas as pl
from jax.experimental.pallas import tpu as pltpu

F32 = jnp.float32
BF16 = jnp.bfloat16

GRID_W = 64
HEAD_DIM = 64
A_HEADS = 4
A_KV_HEADS = 2
A_BLOCK = 128
B_HEADS = 4
NA_ROWS = 8
NA_COLS = 16
C_ORDER = 2
C_DIRS = 2
C_FILTER_WIDTH = 64
C_BANDS = 16
ROPE_BASE = 10000.0
EPS = 1e-6
NEG_INF = -1e30
LOG2E = math.log2(math.e)
HYENA_MIN_DECAY = math.log(1e-2) / 1.5
HYENA_MAX_DECAY = math.log(1e-2) / 0.3

V7X_VMEM_BYTES = 64 * 1024 * 1024
VMEM_LIMIT = V7X_VMEM_BYTES - 8 * 1024 * 1024
LANES = 128
TOKEN_TILE = 512
FREQ_TILE = 256


def _params(*sem):
    return pltpu.CompilerParams(dimension_semantics=sem, vmem_limit_bytes=VMEM_LIMIT)


def _dot(a, b):
    return jnp.dot(a, b, preferred_element_type=F32)


def _dot_nt(a, b):
    return lax.dot_general(a, b, (((1,), (1,)), ((), ())), preferred_element_type=F32)


def _dot_hp(a, b):
    ah, al = _split_bf16(a)
    bh, bl = _split_bf16(b)
    return _dot(ah, bh) + (_dot(ah, bl) + _dot(al, bh))


def _split_bf16(v):
    hi = v.astype(BF16)
    lo = (v - hi.astype(F32)).astype(BF16)
    return hi, lo


def _resident(shape):
    nd = len(shape)
    return pl.BlockSpec(shape, lambda *_: (0,) * nd, pipeline_mode=pl.Buffered(1))


def _mod_kernel(c_ref, w_ref, b_ref, o_ref):
    cv = c_ref[...]
    sc = (cv * jax.nn.sigmoid(cv)).astype(BF16)
    o_ref[...] = _dot(sc, w_ref[...].astype(BF16)) + b_ref[...]


def _modulation(c_all, ada_w, ada_b):
    depth, d, n = ada_w.shape
    rows = c_all.shape[0]
    tn = 512
    return pl.pallas_call(
        _mod_kernel,
        out_shape=jax.ShapeDtypeStruct((depth, rows, n), F32),
        grid=(depth, n // tn),
        in_specs=[
            pl.BlockSpec((rows, d), lambda i, j: (0, 0)),
            pl.BlockSpec((None, d, tn), lambda i, j: (i, 0, j)),
            pl.BlockSpec((None, 1, tn), lambda i, j: (i, 0, j)),
        ],
        out_specs=pl.BlockSpec((None, rows, tn), lambda i, j: (i, 0, j)),
        compiler_params=_params("arbitrary", "arbitrary"),
        name="modulation",
    )(c_all, ada_w, ada_b.reshape(depth, 1, n))


def _rms_mod(x, g, shift, scale):
    y = x * lax.rsqrt(jnp.mean(x * x, axis=-1, keepdims=True) + EPS)
    return (y * g) * (1 + scale) + shift


def _head_norm(h, g, e):
    ss = _dot((h * h).astype(BF16), e)
    return (h * lax.rsqrt(ss * (1.0 / HEAD_DIM) + EPS)) * g


def _rope(t, cos, sin_a, sin_b):
    outs = []
    for c in range(t.shape[1] // LANES):
        tc = t[:, c * LANES:(c + 1) * LANES]
        outs.append(tc * cos + pltpu.roll(tc, LANES - 16, 1) * sin_a + pltpu.roll(tc, 16, 1) * sin_b)
    return outs[0] if len(outs) == 1 else jnp.concatenate(outs, axis=1)


PROJ_SPLIT = 2


def _proj_in_kernel(x_ref, shift_ref, scale_ref, g_ref, w_ref, gqa_ref, gka_ref, gqb_ref, gkb_ref, e_ref, *rest,
                    rope):
    if rope:
        cos_ref, sa_ref, sb_ref = rest[:3]
        rest = rest[3:]
    qa_ref, ka_ref, va_ref, qb_ref, kb_ref, vb_ref, hy_ref = rest
    qscale = LOG2E * HEAD_DIM ** -0.5
    hm = x_ref.shape[0] // PROJ_SPLIT
    for part in range(PROJ_SPLIT):
        rows = slice(part * hm, (part + 1) * hm)
        xb = _rms_mod(x_ref[rows, :], g_ref[...], shift_ref[...], scale_ref[...]).astype(BF16)

        def proj(lo, hi):
            return _dot(xb, w_ref[:, lo:hi])

        def maybe_rope(t):
            return _rope(t, cos_ref[rows, :], sa_ref[rows, :], sb_ref[rows, :]) if rope else t

        def store_v(ref, v):
            if rope:
                for i in range(hm // LANES):
                    ref[part * (hm // LANES) + i] = v[i * LANES:(i + 1) * LANES, :].T.astype(BF16)
            else:
                ref[rows, :] = v.astype(BF16)

        qa = maybe_rope(_head_norm(proj(0, 256), gqa_ref[...], e_ref[...]))
        qa_ref[rows, :] = (qa * qscale).astype(BF16)
        ka = maybe_rope(_head_norm(proj(256, 384), gka_ref[:, :128], e_ref[:128, :128]))
        ka_ref[rows, :] = ka.astype(BF16)
        store_v(va_ref, proj(384, 512))
        qb = _head_norm(proj(512, 768), gqb_ref[...], e_ref[...])
        qb_ref[rows, :] = (qb * qscale).astype(BF16)
        kb_ref[rows, :] = _head_norm(proj(768, 1024), gkb_ref[...], e_ref[...]).astype(BF16)
        store_v(vb_ref, proj(1024, 1280))
        hy_ref[rows, :] = proj(1280, 2816).astype(BF16)


def _proj_in(x, shift, scale, g, w, gqa, gka, gqb, gkb, e, rope_tabs):
    bn, l, d = x.shape
    n = w.shape[1]
    rope = rope_tabs is not None
    tm = min(PROJ_SPLIT * TOKEN_TILE, l)
    row = lambda width: pl.BlockSpec((None, 1, width), lambda b, t: (b, 0, 0))
    const = lambda shape: pl.BlockSpec(shape, lambda b, t: (0,) * len(shape))
    tok = lambda width: pl.BlockSpec((None, tm, width), lambda b, t: (b, t, 0))
    tab = pl.BlockSpec((tm, LANES), lambda b, t: (t, 0))
    widths = (256, 128, 128, 256, 256, 256, n - 1280)
    shapes = [(bn, l, wd) for wd in widths]
    specs = [tok(wd) for wd in widths]
    if rope:
        for i in (2, 5):
            shapes[i] = (bn, l // LANES, widths[i], LANES)
            specs[i] = pl.BlockSpec((None, tm // LANES, widths[i], LANES), lambda b, t: (b, t, 0, 0))
    return pl.pallas_call(
        functools.partial(_proj_in_kernel, rope=rope),
        out_shape=[jax.ShapeDtypeStruct(sh, BF16) for sh in shapes],
        grid=(bn, l // tm),
        in_specs=[tok(d), row(d), row(d), const((1, d)), _resident((d, n)),
                  const((1, 256)), const((1, 256)), const((1, 256)), const((1, 256)), const((256, 256))]
                 + ([tab, tab, tab] if rope else []),
        out_specs=specs,
        compiler_params=_params("parallel", "parallel"),
        name="proj_in_rope" if rope else "proj_in",
    )(x, shift, scale, g, w, gqa, gka, gqb, gkb, e, *(rope_tabs or ()))


def _softmax_pv(s_list, v_list, sink):
    m = s_list[0].max(axis=-1, keepdims=True)
    for s in s_list[1:]:
        m = jnp.maximum(m, s.max(axis=-1, keepdims=True))
    if sink is not None:
        m = jnp.maximum(m, sink)
    den = None
    out = None
    for s, v in zip(s_list, v_list):
        p = jnp.exp2(s - m)
        ps = p.sum(axis=-1, keepdims=True)
        den = ps if den is None else den + ps
        o = _dot(p.astype(BF16), v)
        out = o if out is None else out + o
    if sink is not None:
        den = den + jnp.exp2(sink - m)
    return out * (1.0 / den)


def _stack_heads(q):
    lo = lax.broadcasted_iota(jnp.int32, q.shape, 1) < HEAD_DIM
    zero = jnp.zeros_like(q)
    return jnp.concatenate([jnp.where(lo, q, zero), jnp.where(lo, zero, q)], axis=0)


def _unstack_heads(o):
    m = o.shape[0] // 2
    lo = lax.broadcasted_iota(jnp.int32, (m, LANES), 1) < HEAD_DIM
    return jnp.where(lo, o[:m], o[m:])


SOFTMAX_CHUNK = 32


def _softmax_keys(s_ref, p_ref, nloc, add_loc, sink):
    nk = s_ref.shape[0]
    ch = SOFTMAX_CHUNK
    macc = None
    for r0 in range(0, nk, ch):
        s = s_ref[r0:r0 + ch, :]
        if r0 < nloc:
            s = s + add_loc(r0)
            s_ref[r0:r0 + ch, :] = s
        macc = s if macc is None else jnp.maximum(macc, s)
    m = macc.max(axis=0, keepdims=True)
    if sink is not None:
        m = jnp.maximum(m, sink)
    sacc = None
    for r0 in range(0, nk, ch):
        p = jnp.exp2(s_ref[r0:r0 + ch, :] - m)
        sacc = p if sacc is None else sacc + p
        p_ref[r0:r0 + ch, :] = p.astype(BF16)
    den = sacc.sum(axis=0, keepdims=True)
    if sink is not None:
        den = den + jnp.exp2(sink - m)
    return 1.0 / den


A_STEP_BLOCKS = 4
A_SPAN = 3 * A_BLOCK


def _attn_a_kernel(q_ref, k_ref, vt_ref, kx_ref, vxt_ref, mask_ref, sink_ref, o_ref, s_ref, p_ref):
    l = k_ref.shape[0]
    nb = l // A_BLOCK
    hd = HEAD_DIM
    for u in range(A_STEP_BLOCKS):
        n = pl.program_id(1) * A_STEP_BLOCKS + u
        tile0 = jnp.clip(n - 1, 0, nb - 3)
        start = pl.multiple_of(tile0 * A_BLOCK, A_BLOCK)
        pat = jnp.where(n == 0, 0, jnp.where(n == nb - 1, 2, 1))
        q = q_ref[u * A_BLOCK:(u + 1) * A_BLOCK, :]
        qs = jnp.concatenate([_stack_heads(q[:, :LANES]), _stack_heads(q[:, LANES:])], axis=0)
        sb, pb = s_ref.at[u % 2], p_ref.at[u % 2]
        sb[:A_SPAN, :] = _dot_nt(k_ref[pl.ds(start, A_SPAN), :], qs)
        sb[A_SPAN:, :] = _dot_nt(kx_ref[...], qs)
        r = _softmax_keys(sb, pb, A_SPAN, lambda r0: mask_ref[pat, r0:r0 + SOFTMAX_CHUNK, :], sink_ref[...])
        vt = jnp.concatenate([vt_ref[tile0 + i] for i in range(3)], axis=1)
        ot = (_dot(vt, pb[:A_SPAN, :]) + _dot(vxt_ref[...], pb[A_SPAN:, :])) * r
        ot = jnp.concatenate([ot[(i % 2) * hd:(i % 2 + 1) * hd, i * A_BLOCK:(i + 1) * A_BLOCK]
                              for i in range(A_HEADS)], axis=0)
        o_ref[u * A_BLOCK:(u + 1) * A_BLOCK, :] = ot.T.astype(o_ref.dtype)


def _attn_a_mask():
    i = np.arange(A_BLOCK)[None, :]
    j = np.arange(A_SPAN)[:, None]
    offs = (0, A_BLOCK, 2 * A_BLOCK)
    m = np.stack([np.where(np.abs(j - i - o) <= A_BLOCK, 0.0, -np.inf) for o in offs])
    return jnp.asarray(np.tile(m, (1, 1, A_HEADS)), F32)


def _attn_a(q, k, vt, kx, vx, sink_row):
    bn, l, qw = q.shape
    lc = kx.shape[1]
    kvw = k.shape[2]
    nb = l // A_BLOCK
    qs = A_STEP_BLOCKS * A_BLOCK
    mask = _attn_a_mask()
    vxt = jnp.swapaxes(vx, 1, 2)
    seq = pl.BlockSpec((None, l, kvw), lambda b, s: (b, 0, 0))
    qblk = pl.BlockSpec((None, qs, qw), lambda b, s: (b, s, 0))
    nq = A_HEADS * A_BLOCK
    return pl.pallas_call(
        _attn_a_kernel,
        out_shape=jax.ShapeDtypeStruct((bn, l, qw), BF16),
        grid=(bn, l // qs),
        in_specs=[qblk, seq, pl.BlockSpec((None, nb, kvw, A_BLOCK), lambda b, s: (b, 0, 0, 0)),
                  pl.BlockSpec((None, lc, kvw), lambda b, s: (b, 0, 0)),
                  pl.BlockSpec((None, kvw, lc), lambda b, s: (b, 0, 0)),
                  _resident(mask.shape), _resident(sink_row.shape)],
        out_specs=qblk,
        scratch_shapes=[pltpu.VMEM((2, A_SPAN + lc, nq), F32), pltpu.VMEM((2, A_SPAN + lc, nq), BF16)],
        compiler_params=_params("parallel", "arbitrary"),
        name="window_attn",
    )(q, k, vt, kx, vxt, mask, sink_row)


def _dense_attn_kernel(sink_ref, q_ref, k_ref, v_ref, o_ref, *, kv_of, use_sink):
    for h, kv in enumerate(kv_of):
        hs = slice(h * HEAD_DIM, (h + 1) * HEAD_DIM)
        ks = slice(kv * HEAD_DIM, (kv + 1) * HEAD_DIM)
        s = _dot_nt(q_ref[:, hs], k_ref[:, ks])
        o = _softmax_pv([s], [v_ref[:, ks]], sink_ref[h] if use_sink else None)
        o_ref[:, hs] = o.astype(o_ref.dtype)


def _dense_attn(q, k, v, sink, *, kv_of, use_sink):
    bn, l, qw = q.shape
    kvw = k.shape[2]
    full = lambda wd: pl.BlockSpec((None, l, wd), lambda b: (b, 0, 0))
    return pl.pallas_call(
        functools.partial(_dense_attn_kernel, kv_of=kv_of, use_sink=use_sink),
        out_shape=jax.ShapeDtypeStruct((bn, l, qw), BF16),
        grid=(bn,),
        in_specs=[pl.BlockSpec(memory_space=pltpu.SMEM), full(qw), full(kvw), full(kvw)],
        out_specs=full(qw),
        compiler_params=_params("parallel"),
        name="ctx_attn_sink" if use_sink else "ctx_attn",
    )(sink, q, k, v)


def _rpb_kernel(r_ref, oh_ref, o_ref):
    r = r_ref[...]
    b1 = r.astype(BF16)
    r2 = r - b1.astype(F32)
    b2 = r2.astype(BF16)
    b3 = (r2 - b2.astype(F32)).astype(BF16)
    oh = oh_ref[...]
    o_ref[...] = (_dot(b1, oh) + _dot(b2, oh)) + _dot(b3, oh)


def _rpb_tables(rpb):
    h, nr, nc = rpb.shape
    col = np.arange(GRID_W)
    dc = np.clip(col[None, :] - col[:, None], 1 - NA_COLS, NA_COLS - 1) + NA_COLS - 1
    onehot = (dc.reshape(1, -1) == np.arange(nc)[:, None]).astype(np.float32)
    onehot = np.concatenate([onehot, np.zeros((32 - nc, GRID_W * GRID_W), np.float32)], axis=0)
    rows = 64
    r2 = jnp.zeros((rows, 32), F32).at[:h * nr, :nc].set(rpb.reshape(h * nr, nc))
    full = pl.pallas_call(
        _rpb_kernel,
        out_shape=jax.ShapeDtypeStruct((rows, GRID_W * GRID_W), F32),
        name="rpb_table",
    )(r2, jnp.asarray(onehot, BF16))
    full = full[:h * nr].reshape(h, nr, GRID_W, GRID_W)
    a = np.arange(B_GROUP)[:, None]
    kr = np.arange(B_SLAB)[None, :]
    dr = np.stack([kr - a + NA_ROWS - 1, kr - a + NA_ROWS // 2 - 1, kr - a + (B_GROUP + NA_ROWS - 1 - B_SLAB)])
    lo = np.stack([0 * a + 0 * kr, a + 0 * kr, 0 * a + (B_SLAB - NA_ROWS) + 0 * kr])
    valid = (kr[None] >= lo) & (kr[None] < lo + NA_ROWS)
    col_start = np.clip(col - NA_COLS // 2, 0, GRID_W - NA_COLS)
    col_ok = (col[None, :] >= col_start[:, None]) & (col[None, :] < col_start[:, None] + NA_COLS)
    ok = valid[:, :, None, :, None] & col_ok[None, None, :, None, :]
    tbl = full[:, np.clip(dr, 0, nr - 1)]
    tbl = jnp.transpose(tbl, (1, 0, 2, 4, 3, 5))
    tbl = jnp.where(ok[:, None], tbl * LOG2E, -jnp.inf)
    tbl = tbl.reshape(3, h // 2, 2, B_GROUP, GRID_W, B_SLAB, GRID_W)
    tbl = jnp.transpose(tbl, (0, 1, 5, 6, 2, 3, 4))
    return tbl.reshape(3, h // 2, B_SLAB * GRID_W, 2 * B_GROUP * GRID_W)


B_GROUP = 4
B_SLAB = 12


def _attn_b_kernel(q_ref, k_ref, vt_ref, kx_ref, vxt_ref, tbl_ref, o_ref, s_ref, p_ref):
    g = pl.program_id(1)
    ng = pl.num_programs(1)
    rows = k_ref.shape[0] // GRID_W
    nloc = B_SLAB * GRID_W
    gq = B_GROUP * GRID_W
    base = jnp.clip(g * B_GROUP - NA_ROWS // 2, 0, rows - B_SLAB)
    start = pl.multiple_of(base * GRID_W, LANES)
    tile0 = base // (LANES // GRID_W)
    pat = jnp.where(g == 0, 0, jnp.where(g == ng - 1, 2, 1))
    for t in range(B_HEADS // 2):
        ts = slice(t * LANES, (t + 1) * LANES)
        qs = _stack_heads(q_ref[:, ts])
        sb, pb = s_ref.at[t], p_ref.at[t]
        sb[:nloc, :] = _dot_nt(k_ref[pl.ds(start, nloc), ts], qs)
        sb[nloc:, :] = _dot_nt(kx_ref[:, ts], qs)
        r = _softmax_keys(sb, pb, nloc, lambda r0: tbl_ref[pat, t, r0:r0 + SOFTMAX_CHUNK, :], None)
        vt = jnp.concatenate([vt_ref[tile0 + i, ts, :] for i in range(nloc // LANES)], axis=1)
        ot = (_dot(vt, pb[:nloc, :]) + _dot(vxt_ref[ts, :], pb[nloc:, :])) * r
        ot = jnp.concatenate([ot[:HEAD_DIM, :gq], ot[HEAD_DIM:, gq:]], axis=0)
        o_ref[:, ts] = ot.T.astype(o_ref.dtype)


def _attn_b(q, k, vt, kx, vx, tbl):
    bn, l, w = q.shape
    lc = kx.shape[1]
    gq = B_GROUP * GRID_W
    nk = B_SLAB * GRID_W + lc
    vxt = jnp.swapaxes(vx, 1, 2)
    seq = pl.BlockSpec((None, l, w), lambda b, g: (b, 0, 0))
    qblk = pl.BlockSpec((None, gq, w), lambda b, g: (b, g, 0))
    return pl.pallas_call(
        _attn_b_kernel,
        out_shape=jax.ShapeDtypeStruct((bn, l, w), BF16),
        grid=(bn, l // gq),
        in_specs=[qblk, seq, pl.BlockSpec((None, l // LANES, w, LANES), lambda b, g: (b, 0, 0, 0)),
                  pl.BlockSpec((None, lc, w), lambda b, g: (b, 0, 0)),
                  pl.BlockSpec((None, w, lc), lambda b, g: (b, 0, 0)),
                  _resident(tbl.shape)],
        out_specs=qblk,
        scratch_shapes=[pltpu.VMEM((B_HEADS // 2, nk, 2 * gq), F32), pltpu.VMEM((B_HEADS // 2, nk, 2 * gq), BF16)],
        compiler_params=_params("parallel", "arbitrary"),
        name="nbr_attn",
    )(q, k, vt, kx, vxt, tbl)


def _dft_matrices(l, fh):
    k = jnp.arange(l, dtype=jnp.int32)[:, None]
    n = jnp.arange(l, dtype=jnp.int32)[None, :]
    ang = (((2 * k + 1) * n) % (4 * l)).astype(F32) * (math.pi / (2 * l))
    fre = jnp.cos(ang).reshape(l // fh, fh, l)
    fim = (-jnp.sin(ang)).reshape(l // fh, fh, l)
    f = jnp.concatenate([fre, fim], axis=1).reshape(2 * l, l)
    g = f.T * (1.0 / l)
    return f.astype(BF16), g.astype(BF16)


def _filter_kernel(z_ref, t_ref, w1_ref, b1_ref, fr_ref, w2_ref, b2_ref, w3_ref, dl_ref,
                   ah_ref, al_ref, sh_ref, sl_ref):
    fr = fr_ref[...]
    hid = jnp.sin(fr * (_dot_hp(z_ref[...], w1_ref[...]) + b1_ref[...]))
    hid = jnp.sin(fr * (_dot_hp(hid, w2_ref[...]) + b2_ref[...]))
    cw = dl_ref.shape[1]
    p = z_ref.shape[0] // 3
    decay = jnp.exp(-t_ref[...] * dl_ref[...])
    kf = _dot_hp(hid, w3_ref[:, :cw]) * decay
    kb = _dot_hp(hid, w3_ref[:, cw:]) * decay
    kf0, kf1, kfr = kf[:p], kf[p:2 * p], kf[2 * p:]
    kb0, kb1, kbr = kb[:p], kb[p:2 * p], kb[2 * p:]
    first = lax.broadcasted_iota(jnp.int32, (p, cw), 0) == 0
    drop0 = lambda v: jnp.where(first, 0.0, v)
    kb0 = drop0(kb0)
    colsum = lambda v: jnp.sum(jnp.abs(v), axis=0, keepdims=True)
    inv = 1.0 / (colsum(kf0) + colsum(kf1) + colsum(kb0) + colsum(kb1))
    pairs = ((kf0, kb0), (kf1, drop0(kfr)), (kbr, drop0(kb1)))
    for d, (cp, cm) in enumerate(pairs):
        cols = slice(d * cw, (d + 1) * cw)
        ah_ref[:, cols], al_ref[:, cols] = _split_bf16((cp + cm) * inv)
        sh_ref[:, cols], sl_ref[:, cols] = _split_bf16((cp - cm) * inv)


def _filters(z, t, w1, b1, freq, w2, b2, w3, deltas):
    p = z.shape[0] // 3
    cw = deltas.shape[1]
    fw = w2.shape[0]
    const = lambda shape: pl.BlockSpec(shape, lambda o: (0,) * len(shape))
    out = jax.ShapeDtypeStruct((p, C_ORDER * 3 * cw), BF16)
    oblk = pl.BlockSpec((p, 3 * cw), lambda o: (0, o))
    return pl.pallas_call(
        _filter_kernel,
        out_shape=[out] * 4,
        grid=(C_ORDER,),
        in_specs=[const(z.shape), const(t.shape), const(w1.shape), const((1, fw)), const((1, fw)),
                  const((fw, fw)), const((1, fw)), pl.BlockSpec((fw, C_DIRS * cw), lambda o: (0, o)),
                  const((1, cw))],
        out_specs=[oblk] * 4,
        compiler_params=_params("arbitrary"),
        name="hyena_filter",
    )(z, t, w1, b1, freq, w2, b2, w3, deltas)


def _spectrum_kernel(f_ref, ah_ref, al_ref, sh_ref, sl_ref, kre_ref, kim_ref):
    fh = kre_ref.shape[0]
    fre = f_ref[:fh, :]
    fim = f_ref[fh:, :]
    kre_ref[...] = _dot(fre, ah_ref[...]) + _dot(fre, al_ref[...])
    kim_ref[...] = _dot(fim, sh_ref[...]) + _dot(fim, sl_ref[...])


def _spectrum(f, ah, al, sh, sl, fh):
    p, n = ah.shape
    tn = n // C_ORDER
    taps = pl.BlockSpec((p, tn), lambda o, j: (0, o))
    out = jax.ShapeDtypeStruct((p, n), F32)
    oblk = pl.BlockSpec((fh, tn), lambda o, j: (j, o))
    return pl.pallas_call(
        _spectrum_kernel,
        out_shape=[out, out],
        grid=(C_ORDER, p // fh),
        in_specs=[pl.BlockSpec((2 * fh, p), lambda o, j: (j, 0)), taps, taps, taps, taps],
        out_specs=[oblk, oblk],
        compiler_params=_params("arbitrary", "arbitrary"),
        name="hyena_spectrum",
    )(f, ah, al, sh, sl)


def _short_conv(u, w_ref, b_ref):
    n = u.shape[0]
    row = lax.broadcasted_iota(jnp.int32, u.shape, 0)
    prev = jnp.where(row == 0, 0.0, pltpu.roll(u, 1, 0))
    nxt = jnp.where(row == n - 1, 0.0, pltpu.roll(u, n - 1, 0))
    return prev * w_ref[0:1, :] + u * w_ref[1:2, :] + nxt * w_ref[2:3, :] + b_ref[...]


def _short_conv_wrap(u, w_ref, b_ref):
    n = u.shape[0]
    return (pltpu.roll(u, 1, 0) * w_ref[0:1, :] + u * w_ref[1:2, :] + pltpu.roll(u, n - 1, 0) * w_ref[2:3, :]
            + b_ref[...])


def _longconv_kernel(u_ref, g_ref, cwu_ref, cbu_ref, cwg_ref, cbg_ref, f_ref, gm_ref, kre_ref, kim_ref, d_ref,
                     o_ref, ub_ref, acc_ref, *, conv_u):
    j = pl.program_id(1)
    fh = kre_ref.shape[0]
    p, cw = ub_ref.shape[0], d_ref.shape[1]

    l = 2 * p
    edge = 16

    def edge_conv(ref, w_ref, b_ref):
        head = _short_conv(ref[0:2 * edge, :].astype(F32), w_ref, b_ref)[:edge]
        tail = _short_conv(ref[l - 2 * edge:l, :].astype(F32), w_ref, b_ref)[edge:]
        return head, tail

    @pl.when(j == 0)
    def _():
        if conv_u:
            u = _short_conv_wrap(u_ref[...].astype(F32), cwu_ref, cbu_ref).astype(BF16)
        else:
            u = u_ref[...]
        ub_ref[:, :cw] = u[:p]
        ub_ref[:, cw:] = u[p:]
        if conv_u:
            head, tail = edge_conv(u_ref, cwu_ref, cbu_ref)
            ub_ref[0:edge, :cw] = head.astype(BF16)
            ub_ref[p - edge:p, cw:] = tail.astype(BF16)
        acc_ref[...] = jnp.zeros_like(acc_ref)

    spec = _dot(f_ref[...], ub_ref[...])
    u0r, u1r, u0i, u1i = spec[:fh, :cw], spec[:fh, cw:], spec[fh:, :cw], spec[fh:, cw:]
    tap = lambda ref, d: ref[:, d * cw:(d + 1) * cw]
    c0r, c1r, cmr = (tap(kre_ref, d) for d in range(3))
    c0i, c1i, cmi = (tap(kim_ref, d) for d in range(3))
    y0r = (c0r * u0r - c0i * u0i) + (cmr * u1r - cmi * u1i)
    y0i = (c0r * u0i + c0i * u0r) + (cmr * u1i + cmi * u1r)
    y1r = (c1r * u0r - c1i * u0i) + (c0r * u1r - c0i * u1i)
    y1i = (c1r * u0i + c1i * u0r) + (c0r * u1i + c0i * u1r)
    y = jnp.concatenate([jnp.concatenate([y0r, y1r], axis=1), jnp.concatenate([y0i, y1i], axis=1)], axis=0)
    acc_ref[...] += _dot(gm_ref[...], y.astype(BF16))

    @pl.when(j == pl.num_programs(1) - 1)
    def _():
        gate = _short_conv_wrap(g_ref[...].astype(F32), cwg_ref, cbg_ref)
        y = jnp.concatenate([acc_ref[:, :cw], acc_ref[:, cw:]], axis=0)
        y = y + jnp.concatenate([ub_ref[:, :cw], ub_ref[:, cw:]], axis=0).astype(F32) * d_ref[...]
        o_ref[...] = (gate * y).astype(o_ref.dtype)
        head, tail = edge_conv(g_ref, cwg_ref, cbg_ref)
        o_ref[0:edge, :] = (head * y[:edge]).astype(o_ref.dtype)
        o_ref[l - edge:l, :] = (tail * y[l - edge:]).astype(o_ref.dtype)


def _longconv(u_arr, u_blk, g_arr, g_blk, conv_w, conv_b, f, gm, kre, kim, order, d, *, conv_u, fh):
    bn, l, _ = u_arr.shape
    p = l // 2
    cw = d.shape[1]
    ub = u_blk if conv_u else 0
    tok = lambda blk: pl.BlockSpec((None, l, cw), lambda b, j: (b, 0, blk))
    cpar = lambda rows, blk: pl.BlockSpec((rows, cw), lambda b, j: (0, blk))
    ktab = pl.BlockSpec((fh, 3 * cw), lambda b, j: (j, order))
    return pl.pallas_call(
        functools.partial(_longconv_kernel, conv_u=conv_u),
        out_shape=jax.ShapeDtypeStruct((bn, l, cw), BF16),
        grid=(bn, p // fh),
        in_specs=[tok(u_blk), tok(g_blk), cpar(3, ub), cpar(1, ub), cpar(3, g_blk), cpar(1, g_blk),
                  pl.BlockSpec((2 * fh, p), lambda b, j: (j, 0)),
                  pl.BlockSpec((p, 2 * fh), lambda b, j: (0, j)),
                  ktab, ktab, pl.BlockSpec((1, cw), lambda b, j: (0, 0))],
        out_specs=tok(0),
        scratch_shapes=[pltpu.VMEM((p, 2 * cw), BF16), pltpu.VMEM((p, 2 * cw), F32)],
        compiler_params=_params("parallel", "arbitrary"),
        name="hyena_longconv",
    )(u_arr, g_arr, conv_w, conv_b, conv_w, conv_b, f, gm, kre, kim, d)


def _hyena_consts(l, cw):
    p = l // 2
    e = np.arange(p)
    pos = np.concatenate([e, p + e, p - e])
    t = np.linspace(0.0, 1.0, l, dtype=np.float32)[pos][:, None]
    w = (2.0 * math.pi * np.arange(l, dtype=np.float32) / l).astype(np.float32)[pos][:, None]
    fq = np.linspace(1e-4, C_BANDS - 1, C_BANDS, dtype=np.float32)[None, :]
    wf = jnp.asarray(w) * jnp.asarray(fq)
    z = jnp.concatenate([jnp.asarray(t), jnp.cos(wf), -jnp.sin(wf)], axis=-1)
    z = jnp.pad(z, ((0, 0), (0, C_FILTER_WIDTH - z.shape[1])))
    deltas = np.abs(np.linspace(HYENA_MIN_DECAY, HYENA_MAX_DECAY, cw, dtype=np.float32))[None, :]
    return z, jnp.asarray(t), jnp.asarray(deltas)


def _hyena(hy, hconst, f, gm, fh, conv_w, conv_b, w1p, b1, freq, w2, b2, w3, dbias):
    z, t, deltas = hconst
    ah, al, sh, sl = _filters(z, t, w1p, b1, freq, w2, b2, w3, deltas)
    kre, kim = _spectrum(f, ah, al, sh, sl, fh)
    zz = _longconv(hy, 0, hy, 1, conv_w, conv_b, f, gm, kre, kim, 0, dbias[0:1], conv_u=True, fh=fh)
    return _longconv(zz, 0, hy, 2, conv_w, conv_b, f, gm, kre, kim, 1, dbias[1:2], conv_u=False, fh=fh)


def _mix_ffn_kernel(x_ref, a_ref, b_ref, c_ref, wo_ref, g1_ref, gn_ref, sh_ref, sc_ref, g2_ref,
                    wg_ref, wu_ref, wd_ref, o_ref, *, fc):
    wa, wb = a_ref.shape[1], b_ref.shape[1]
    mix = (_dot(a_ref[...], wo_ref[:wa, :]) + _dot(b_ref[...], wo_ref[wa:wa + wb, :])
           + _dot(c_ref[...], wo_ref[wa + wb:, :]))
    x1 = x_ref[...] + g1_ref[...] * mix
    xb = _rms_mod(x1, gn_ref[...], sh_ref[...], sc_ref[...]).astype(BF16)
    acc = None
    for c0 in range(0, wg_ref.shape[1], fc):
        hg = _dot(xb, wg_ref[:, c0:c0 + fc])
        hu = _dot(xb, wu_ref[:, c0:c0 + fc])
        act = ((hg * jax.nn.sigmoid(hg)) * hu).astype(BF16)
        part = _dot(act, wd_ref[c0:c0 + fc, :])
        acc = part if acc is None else acc + part
    o_ref[...] = x1 + g2_ref[...] * acc


def _mix_ffn(x, a, b, c, wo, g1, gn, sh2, sc2, g2, wg, wu, wd):
    bn, l, d = x.shape
    tm = min(TOKEN_TILE, l)
    row = pl.BlockSpec((None, 1, d), lambda bb, t: (bb, 0, 0))
    tok = lambda width: pl.BlockSpec((None, tm, width), lambda bb, t: (bb, t, 0))
    return pl.pallas_call(
        functools.partial(_mix_ffn_kernel, fc=256),
        out_shape=jax.ShapeDtypeStruct((bn, l, d), F32),
        grid=(bn, l // tm),
        in_specs=[tok(d), tok(a.shape[2]), tok(b.shape[2]), tok(c.shape[2]), _resident(wo.shape),
                  row, pl.BlockSpec((1, d), lambda bb, t: (0, 0)), row, row, row,
                  _resident(wg.shape), _resident(wu.shape), _resident(wd.shape)],
        out_specs=tok(d),
        compiler_params=_params("parallel", "parallel"),
        name="mix_ffn",
    )(x, a, b, c, wo, g1, gn, sh2, sc2, g2, wg, wu, wd)


def _rope_tables(l):
    half = HEAD_DIM // 2
    nfreq = half // 2
    inv = ROPE_BASE ** (-jnp.arange(nfreq, dtype=F32) / nfreq)
    pos = jnp.arange(l)
    rows, cols = pos // GRID_W, pos % GRID_W
    ang = jnp.concatenate([rows.astype(F32)[:, None] * inv[None, :]] * 2
                          + [cols.astype(F32)[:, None] * inv[None, :]] * 2, axis=-1)
    first = (np.arange(HEAD_DIM) % half) < nfreq
    cos, sin = jnp.cos(ang), jnp.sin(ang)
    sin_a = jnp.where(first[None, :], -sin, 0.0)
    sin_b = jnp.where(first[None, :], 0.0, sin)
    reps = LANES // HEAD_DIM
    return tuple(jnp.tile(tb, (1, reps)) for tb in (cos, sin_a, sin_b))


def kernel(x, c, ctx, c_ctx, ada_w, ada_b, norm1_g, norm2_g, w_in, qnorm_a, knorm_a, sink_a, qnorm_b, knorm_b,
           rpb_b, conv_w, conv_b, filt_w1, filt_b1, filt_freq, filt_w2, filt_b2, filt_w3, hyena_bias, w_out,
           ffn_w_gate, ffn_w_up, ffn_w_down):
    bn, l, d = x.shape
    lc = ctx.shape[1]
    depth = ada_w.shape[0]
    cw = hyena_bias.shape[2]
    assert l % (A_STEP_BLOCKS * A_BLOCK) == 0 and l >= 3 * A_SPAN and lc % 256 == 0
    assert l % (B_GROUP * GRID_W) == 0 and l // GRID_W >= 3 * B_GROUP

    mod_rows = 8 * (-(-(bn + 1) // 8))
    c_all = jnp.zeros((mod_rows, d), F32).at[:bn].set(c).at[bn].set(c_ctx)
    mod = _modulation(c_all, ada_w, ada_b)

    a_order = (0, 2, 1, 3)
    a_kv = tuple(hh // (A_HEADS // A_KV_HEADS) for hh in a_order)
    heads_a = lambda t, axis: [lax.slice_in_dim(t, hh * HEAD_DIM, (hh + 1) * HEAD_DIM, axis=axis) for hh in a_order]
    qa_w = A_HEADS * HEAD_DIM
    w_in_b = jnp.concatenate(heads_a(w_in, 2) + [w_in[:, :, qa_w:]], axis=2).astype(BF16)
    w_out_b = jnp.concatenate(heads_a(w_out, 1) + [w_out[:, qa_w:]], axis=1).astype(BF16)
    sink_p = LOG2E * jnp.stack([sink_a[:, hh] for hh in a_order], axis=1)
    wg_b, wu_b, wd_b = ffn_w_gate.astype(BF16), ffn_w_up.astype(BF16), ffn_w_down.astype(BF16)

    lane = np.arange(256)
    e_heads = jnp.asarray((lane[:, None] // HEAD_DIM) == (lane[None, :] // HEAD_DIM), BF16)
    tile4 = lambda g: jnp.tile(g.reshape(1, HEAD_DIM), (1, 256 // HEAD_DIM))
    rope_lat = _rope_tables(l)

    fh_lat, fh_ctx = min(FREQ_TILE, l // 2), min(FREQ_TILE, lc // 2)
    f_lat, g_lat = _dft_matrices(l // 2, fh_lat)
    f_ctx, g_ctx = _dft_matrices(lc // 2, fh_ctx)
    hc_lat = _hyena_consts(l, cw)
    hc_ctx = _hyena_consts(lc, cw)

    xc = ctx
    for i in range(depth):
        last = i == depth - 1
        m = mod[i]
        part = lambda rows, k: rows[:, None, k * d:(k + 1) * d]
        lat = [part(m[:bn], k) for k in range(6)]
        cx = [part(m[bn:bn + 1], k) for k in range(6)]
        g1n = norm1_g[i].reshape(1, d)
        g2n = norm2_g[i].reshape(1, d)
        gqa, gka, gqb, gkb = tile4(qnorm_a[i]), tile4(knorm_a[i]), tile4(qnorm_b[i]), tile4(knorm_b[i])

        qa, ka, va, qb, kb, vb, hy = _proj_in(x, lat[0], lat[1], g1n, w_in_b[i], gqa, gka, gqb, gkb, e_heads,
                                              rope_lat)
        ctx_out = _proj_in(xc.reshape(1, bn * lc, d), cx[0], cx[1], g1n, w_in_b[i], gqa, gka, gqb, gkb, e_heads, None)
        qa_c, ka_c, va_c, qb_c, kb_c, vb_c, hy_c = [t.reshape(bn, lc, -1) for t in ctx_out]
        w1p = jnp.pad(filt_w1[i], ((0, C_FILTER_WIDTH - filt_w1.shape[1]), (0, 0)))
        hy_args = (conv_w[i], conv_b[i].reshape(1, -1), w1p, filt_b1[i].reshape(1, -1),
                   filt_freq[i].reshape(1, -1), filt_w2[i], filt_b2[i].reshape(1, -1), filt_w3[i], hyena_bias[i])

        sink_row = jnp.repeat(sink_p[i], A_BLOCK).reshape(1, A_HEADS * A_BLOCK)
        out_a = _attn_a(qa, ka, va, ka_c, va_c, sink_row)
        out_b = _attn_b(qb, kb, vb, kb_c, vb_c, _rpb_tables(rpb_b[i]))
        out_c = _hyena(hy, hc_lat, f_lat, g_lat, fh_lat, *hy_args)
        x = _mix_ffn(x, out_a, out_b, out_c, w_out_b[i], lat[2], g2n, lat[3], lat[4], lat[5],
                     wg_b[i], wu_b[i], wd_b[i])

        if not last:
            oa_c = _dense_attn(qa_c, ka_c, va_c, sink_p[i], kv_of=a_kv, use_sink=True)
            ob_c = _dense_attn(qb_c, kb_c, vb_c, sink_p[i], kv_of=tuple(range(B_HEADS)), use_sink=False)
            oc_c = _hyena(hy_c, hc_ctx, f_ctx, g_ctx, fh_ctx, *hy_args)
            flat = lambda t: t.reshape(1, bn * lc, -1)
            xc = _mix_ffn(flat(xc), flat(oa_c), flat(ob_c), flat(oc_c), w_out_b[i], cx[2], g2n, cx[3], cx[4], cx[5],
                          wg_b[i], wu_b[i], wd_b[i]).reshape(bn, lc, d)
    return x
```

```python
import functools
import math

import jax
import jax.numpy as jnp
import numpy as np
from jax import lax
from jax.experimental import pallas as pl
from jax.experimental.pallas import tpu as pltpu

F32 = jnp.float32
BF16 = jnp.bfloat16

GRID_W = 64
HEAD_DIM = 64
A_HEADS = 4
A_KV_HEADS = 2
A_BLOCK = 128
B_HEADS = 4
NA_ROWS = 8
NA_COLS = 16
C_ORDER = 2
C_DIRS = 2
C_FILTER_WIDTH = 64
C_BANDS = 16
ROPE_BASE = 10000.0
EPS = 1e-6
NEG_INF = -1e30
LOG2E = math.log2(math.e)
HYENA_MIN_DECAY = math.log(1e-2) / 1.5
HYENA_MAX_DECAY = math.log(1e-2) / 0.3

V7X_VMEM_BYTES = 64 * 1024 * 1024
VMEM_LIMIT = V7X_VMEM_BYTES - 8 * 1024 * 1024
LANES = 128
TOKEN_TILE = 512
FREQ_TILE = 512


def _params(*sem):
    return pltpu.CompilerParams(dimension_semantics=sem, vmem_limit_bytes=VMEM_LIMIT)


def _dot(a, b):
    return jnp.dot(a, b, preferred_element_type=F32)


def _dot_nt(a, b):
    return lax.dot_general(a, b, (((1,), (1,)), ((), ())), preferred_element_type=F32)


def _dot_hp(a, b):
    ah, al = _split_bf16(a)
    bh, bl = _split_bf16(b)
    return _dot(ah, bh) + (_dot(ah, bl) + _dot(al, bh))


def _split_bf16(v):
    hi = v.astype(BF16)
    lo = (v - hi.astype(F32)).astype(BF16)
    return hi, lo


def _resident(shape):
    nd = len(shape)
    return pl.BlockSpec(shape, lambda *_: (0,) * nd, pipeline_mode=pl.Buffered(1))


def _mod_kernel(c_ref, w_ref, b_ref, o_ref):
    cv = c_ref[...]
    sc = (cv * jax.nn.sigmoid(cv)).astype(BF16)
    o_ref[...] = _dot(sc, w_ref[...].astype(BF16)) + b_ref[...]


def _modulation(c_all, ada_w, ada_b):
    depth, d, n = ada_w.shape
    rows = c_all.shape[0]
    tn = 512
    return pl.pallas_call(
        _mod_kernel,
        out_shape=jax.ShapeDtypeStruct((depth, rows, n), F32),
        grid=(depth, n // tn),
        in_specs=[
            pl.BlockSpec((rows, d), lambda i, j: (0, 0)),
            pl.BlockSpec((None, d, tn), lambda i, j: (i, 0, j)),
            pl.BlockSpec((None, 1, tn), lambda i, j: (i, 0, j)),
        ],
        out_specs=pl.BlockSpec((None, rows, tn), lambda i, j: (i, 0, j)),
        compiler_params=_params("arbitrary", "arbitrary"),
        name="modulation",
    )(c_all, ada_w, ada_b.reshape(depth, 1, n))


def _rms_mod(x, g, shift, scale):
    y = x * lax.rsqrt(jnp.mean(x * x, axis=-1, keepdims=True) + EPS)
    return (y * g) * (1 + scale) + shift


def _head_norm(h, g, e):
    ss = _dot((h * h).astype(BF16), e)
    return (h * lax.rsqrt(ss * (1.0 / HEAD_DIM) + EPS)) * g


def _rope(t, cos, sin_a, sin_b):
    outs = []
    for c in range(t.shape[1] // LANES):
        tc = t[:, c * LANES:(c + 1) * LANES]
        outs.append(tc * cos + pltpu.roll(tc, LANES - 16, 1) * sin_a + pltpu.roll(tc, 16, 1) * sin_b)
    return outs[0] if len(outs) == 1 else jnp.concatenate(outs, axis=1)


PROJ_SPLIT = 2


def _proj_in_kernel(x_ref, shift_ref, scale_ref, g_ref, w_ref, gqa_ref, gka_ref, gqb_ref, gkb_ref, e_ref, *rest,
                    rope):
    if rope:
        cos_ref, sa_ref, sb_ref = rest[:3]
        rest = rest[3:]
    qa_ref, ka_ref, va_ref, qb_ref, kb_ref, vb_ref, hy_ref = rest
    qscale = LOG2E * HEAD_DIM ** -0.5
    hm = x_ref.shape[0] // PROJ_SPLIT
    for part in range(PROJ_SPLIT):
        rows = slice(part * hm, (part + 1) * hm)
        xb = _rms_mod(x_ref[rows, :], g_ref[...], shift_ref[...], scale_ref[...]).astype(BF16)

        def proj(lo, hi):
            return _dot(xb, w_ref[:, lo:hi])

        def maybe_rope(t):
            return _rope(t, cos_ref[rows, :], sa_ref[rows, :], sb_ref[rows, :]) if rope else t

        def store_v(ref, v):
            if rope:
                for i in range(hm // LANES):
                    ref[part * (hm // LANES) + i] = v[i * LANES:(i + 1) * LANES, :].T.astype(BF16)
            else:
                ref[rows, :] = v.astype(BF16)

        qa = maybe_rope(_head_norm(proj(0, 256), gqa_ref[...], e_ref[...]))
        qa_ref[rows, :] = (qa * qscale).astype(BF16)
        ka = maybe_rope(_head_norm(proj(256, 384), gka_ref[:, :128], e_ref[:128, :128]))
        ka_ref[rows, :] = ka.astype(BF16)
        store_v(va_ref, proj(384, 512))
        qb = _head_norm(proj(512, 768), gqb_ref[...], e_ref[...])
        qb_ref[rows, :] = (qb * qscale).astype(BF16)
        kb_ref[rows, :] = _head_norm(proj(768, 1024), gkb_ref[...], e_ref[...]).astype(BF16)
        store_v(vb_ref, proj(1024, 1280))
        hy_ref[rows, :] = proj(1280, 2816).astype(BF16)


def _proj_in(x, shift, scale, g, w, gqa, gka, gqb, gkb, e, rope_tabs):
    bn, l, d = x.shape
    n = w.shape[1]
    rope = rope_tabs is not None
    tm = min(PROJ_SPLIT * TOKEN_TILE, l)
    row = lambda width: pl.BlockSpec((None, 1, width), lambda b, t: (b, 0, 0))
    const = lambda shape: pl.BlockSpec(shape, lambda b, t: (0,) * len(shape))
    tok = lambda width: pl.BlockSpec((None, tm, width), lambda b, t: (b, t, 0))
    tab = pl.BlockSpec((tm, LANES), lambda b, t: (t, 0))
    widths = (256, 128, 128, 256, 256, 256, n - 1280)
    shapes = [(bn, l, wd) for wd in widths]
    specs = [tok(wd) for wd in widths]
    if rope:
        for i in (2, 5):
            shapes[i] = (bn, l // LANES, widths[i], LANES)
            specs[i] = pl.BlockSpec((None, tm // LANES, widths[i], LANES), lambda b, t: (b, t, 0, 0))
    return pl.pallas_call(
        functools.partial(_proj_in_kernel, rope=rope),
        out_shape=[jax.ShapeDtypeStruct(sh, BF16) for sh in shapes],
        grid=(bn, l // tm),
        in_specs=[tok(d), row(d), row(d), const((1, d)), _resident((d, n)),
                  const((1, 256)), const((1, 256)), const((1, 256)), const((1, 256)), const((256, 256))]
                 + ([tab, tab, tab] if rope else []),
        out_specs=specs,
        compiler_params=_params("parallel", "parallel"),
        name="proj_in_rope" if rope else "proj_in",
    )(x, shift, scale, g, w, gqa, gka, gqb, gkb, e, *(rope_tabs or ()))


def _softmax_pv(s_list, v_list, sink):
    m = s_list[0].max(axis=-1, keepdims=True)
    for s in s_list[1:]:
        m = jnp.maximum(m, s.max(axis=-1, keepdims=True))
    if sink is not None:
        m = jnp.maximum(m, sink)
    den = None
    out = None
    for s, v in zip(s_list, v_list):
        p = jnp.exp2(s - m)
        ps = p.sum(axis=-1, keepdims=True)
        den = ps if den is None else den + ps
        o = _dot(p.astype(BF16), v)
        out = o if out is None else out + o
    if sink is not None:
        den = den + jnp.exp2(sink - m)
    return out * (1.0 / den)


def _stack_heads(q):
    lo = lax.broadcasted_iota(jnp.int32, q.shape, 1) < HEAD_DIM
    zero = jnp.zeros_like(q)
    return jnp.concatenate([jnp.where(lo, q, zero), jnp.where(lo, zero, q)], axis=0)


def _unstack_heads(o):
    m = o.shape[0] // 2
    lo = lax.broadcasted_iota(jnp.int32, (m, LANES), 1) < HEAD_DIM
    return jnp.where(lo, o[:m], o[m:])


SOFTMAX_CHUNK = 32


def _softmax_keys(s_ref, p_ref, nloc, add_loc, sink):
    nk = s_ref.shape[0]
    ch = SOFTMAX_CHUNK
    macc = None
    for r0 in range(0, nk, ch):
        s = s_ref[r0:r0 + ch, :]
        if r0 < nloc:
            s = s + add_loc(r0)
            s_ref[r0:r0 + ch, :] = s
        macc = s if macc is None else jnp.maximum(macc, s)
    m = macc.max(axis=0, keepdims=True)
    if sink is not None:
        m = jnp.maximum(m, sink)
    sacc = None
    for r0 in range(0, nk, ch):
        p = jnp.exp2(s_ref[r0:r0 + ch, :] - m)
        sacc = p if sacc is None else sacc + p
        p_ref[r0:r0 + ch, :] = p.astype(BF16)
    den = sacc.sum(axis=0, keepdims=True)
    if sink is not None:
        den = den + jnp.exp2(sink - m)
    return 1.0 / den


A_STEP_BLOCKS = 8
A_SPAN = 3 * A_BLOCK


def _attn_a_kernel(q_ref, k_ref, vt_ref, kx_ref, vxt_ref, mask_ref, sink_ref, o_ref, s_ref, p_ref):
    l = k_ref.shape[0]
    nb = l // A_BLOCK
    hd = HEAD_DIM
    for u in range(A_STEP_BLOCKS):
        n = pl.program_id(1) * A_STEP_BLOCKS + u
        tile0 = jnp.clip(n - 1, 0, nb - 3)
        start = pl.multiple_of(tile0 * A_BLOCK, A_BLOCK)
        pat = jnp.where(n == 0, 0, jnp.where(n == nb - 1, 2, 1))
        q = q_ref[u * A_BLOCK:(u + 1) * A_BLOCK, :]
        qs = jnp.concatenate([_stack_heads(q[:, :LANES]), _stack_heads(q[:, LANES:])], axis=0)
        sb, pb = s_ref.at[u % 2], p_ref.at[u % 2]
        sb[:A_SPAN, :] = _dot_nt(k_ref[pl.ds(start, A_SPAN), :], qs)
        sb[A_SPAN:, :] = _dot_nt(kx_ref[...], qs)
        r = _softmax_keys(sb, pb, A_SPAN, lambda r0: mask_ref[pat, r0:r0 + SOFTMAX_CHUNK, :], sink_ref[...])
        vt = jnp.concatenate([vt_ref[tile0 + i] for i in range(3)], axis=1)
        ot = (_dot(vt, pb[:A_SPAN, :]) + _dot(vxt_ref[...], pb[A_SPAN:, :])) * r
        ot = jnp.concatenate([ot[(i % 2) * hd:(i % 2 + 1) * hd, i * A_BLOCK:(i + 1) * A_BLOCK]
                              for i in range(A_HEADS)], axis=0)
        o_ref[u * A_BLOCK:(u + 1) * A_BLOCK, :] = ot.T.astype(o_ref.dtype)


def _attn_a_mask():
    i = np.arange(A_BLOCK)[None, :]
    j = np.arange(A_SPAN)[:, None]
    offs = (0, A_BLOCK, 2 * A_BLOCK)
    m = np.stack([np.where(np.abs(j - i - o) <= A_BLOCK, 0.0, -np.inf) for o in offs])
    return jnp.asarray(np.tile(m, (1, 1, A_HEADS)), F32)


def _attn_a(q, k, vt, kx, vx, sink_row):
    bn, l, qw = q.shape
    lc = kx.shape[1]
    kvw = k.shape[2]
    nb = l // A_BLOCK
    qs = A_STEP_BLOCKS * A_BLOCK
    mask = _attn_a_mask()
    vxt = jnp.swapaxes(vx, 1, 2)
    seq = pl.BlockSpec((None, l, kvw), lambda b, s: (b, 0, 0))
    qblk = pl.BlockSpec((None, qs, qw), lambda b, s: (b, s, 0))
    nq = A_HEADS * A_BLOCK
    return pl.pallas_call(
        _attn_a_kernel,
        out_shape=jax.ShapeDtypeStruct((bn, l, qw), BF16),
        grid=(bn, l // qs),
        in_specs=[qblk, seq, pl.BlockSpec((None, nb, kvw, A_BLOCK), lambda b, s: (b, 0, 0, 0)),
                  pl.BlockSpec((None, lc, kvw), lambda b, s: (b, 0, 0)),
                  pl.BlockSpec((None, kvw, lc), lambda b, s: (b, 0, 0)),
                  _resident(mask.shape), _resident(sink_row.shape)],
        out_specs=qblk,
        scratch_shapes=[pltpu.VMEM((2, A_SPAN + lc, nq), F32), pltpu.VMEM((2, A_SPAN + lc, nq), BF16)],
        compiler_params=_params("parallel", "arbitrary"),
        name="window_attn",
    )(q, k, vt, kx, vxt, mask, sink_row)


def _dense_attn_kernel(sink_ref, q_ref, k_ref, v_ref, o_ref, *, kv_of, use_sink):
    for h, kv in enumerate(kv_of):
        hs = slice(h * HEAD_DIM, (h + 1) * HEAD_DIM)
        ks = slice(kv * HEAD_DIM, (kv + 1) * HEAD_DIM)
        s = _dot_nt(q_ref[:, hs], k_ref[:, ks])
        o = _softmax_pv([s], [v_ref[:, ks]], sink_ref[h] if use_sink else None)
        o_ref[:, hs] = o.astype(o_ref.dtype)


def _dense_attn(q, k, v, sink, *, kv_of, use_sink):
    bn, l, qw = q.shape
    kvw = k.shape[2]
    full = lambda wd: pl.BlockSpec((None, l, wd), lambda b: (b, 0, 0))
    return pl.pallas_call(
        functools.partial(_dense_attn_kernel, kv_of=kv_of, use_sink=use_sink),
        out_shape=jax.ShapeDtypeStruct((bn, l, qw), BF16),
        grid=(bn,),
        in_specs=[pl.BlockSpec(memory_space=pltpu.SMEM), full(qw), full(kvw), full(kvw)],
        out_specs=full(qw),
        compiler_params=_params("parallel"),
        name="ctx_attn_sink" if use_sink else "ctx_attn",
    )(sink, q, k, v)


def _rpb_kernel(r_ref, oh_ref, o_ref):
    r = r_ref[...]
    b1 = r.astype(BF16)
    r2 = r - b1.astype(F32)
    b2 = r2.astype(BF16)
    b3 = (r2 - b2.astype(F32)).astype(BF16)
    oh = oh_ref[...]
    o_ref[...] = (_dot(b1, oh) + _dot(b2, oh)) + _dot(b3, oh)


def _rpb_tables(rpb):
    h, nr, nc = rpb.shape
    col = np.arange(GRID_W)
    dc = np.clip(col[None, :] - col[:, None], 1 - NA_COLS, NA_COLS - 1) + NA_COLS - 1
    onehot = (dc.reshape(1, -1) == np.arange(nc)[:, None]).astype(np.float32)
    onehot = np.concatenate([onehot, np.zeros((32 - nc, GRID_W * GRID_W), np.float32)], axis=0)
    rows = 64
    r2 = jnp.zeros((rows, 32), F32).at[:h * nr, :nc].set(rpb.reshape(h * nr, nc))
    full = pl.pallas_call(
        _rpb_kernel,
        out_shape=jax.ShapeDtypeStruct((rows, GRID_W * GRID_W), F32),
        name="rpb_table",
    )(r2, jnp.asarray(onehot, BF16))
    full = full[:h * nr].reshape(h, nr, GRID_W, GRID_W)
    a = np.arange(B_GROUP)[:, None]
    kr = np.arange(B_SLAB)[None, :]
    dr = np.stack([kr - a + NA_ROWS - 1, kr - a + NA_ROWS // 2 - 1, kr - a + (B_GROUP + NA_ROWS - 1 - B_SLAB)])
    lo = np.stack([0 * a + 0 * kr, a + 0 * kr, 0 * a + (B_SLAB - NA_ROWS) + 0 * kr])
    valid = (kr[None] >= lo) & (kr[None] < lo + NA_ROWS)
    col_start = np.clip(col - NA_COLS // 2, 0, GRID_W - NA_COLS)
    col_ok = (col[None, :] >= col_start[:, None]) & (col[None, :] < col_start[:, None] + NA_COLS)
    ok = valid[:, :, None, :, None] & col_ok[None, None, :, None, :]
    tbl = full[:, np.clip(dr, 0, nr - 1)]
    tbl = jnp.transpose(tbl, (1, 0, 2, 4, 3, 5))
    tbl = jnp.where(ok[:, None], tbl * LOG2E, -jnp.inf)
    tbl = tbl.reshape(3, h // 2, 2, B_GROUP, GRID_W, B_SLAB, GRID_W)
    tbl = jnp.transpose(tbl, (0, 1, 5, 6, 2, 3, 4))
    return tbl.reshape(3, h // 2, B_SLAB * GRID_W, 2 * B_GROUP * GRID_W)


B_GROUP = 4
B_STEP_GROUPS = 2
B_SLAB = 12


def _attn_b_kernel(q_ref, k_ref, vt_ref, kx_ref, vxt_ref, tbl_ref, o_ref, s_ref, p_ref):
    ng = pl.num_programs(1) * B_STEP_GROUPS
    rows = k_ref.shape[0] // GRID_W
    nloc = B_SLAB * GRID_W
    gq = B_GROUP * GRID_W
    for u in range(B_STEP_GROUPS):
        g = pl.program_id(1) * B_STEP_GROUPS + u
        base = jnp.clip(g * B_GROUP - NA_ROWS // 2, 0, rows - B_SLAB)
        start = pl.multiple_of(base * GRID_W, LANES)
        tile0 = base // (LANES // GRID_W)
        pat = jnp.where(g == 0, 0, jnp.where(g == ng - 1, 2, 1))
        for t in range(B_HEADS // 2):
            ts = slice(t * LANES, (t + 1) * LANES)
            qs = _stack_heads(q_ref[u * gq:(u + 1) * gq, ts])
            sb, pb = s_ref.at[u, t], p_ref.at[u, t]
            sb[:nloc, :] = _dot_nt(k_ref[pl.ds(start, nloc), ts], qs)
            sb[nloc:, :] = _dot_nt(kx_ref[:, ts], qs)
            r = _softmax_keys(sb, pb, nloc, lambda r0: tbl_ref[pat, t, r0:r0 + SOFTMAX_CHUNK, :], None)
            vt = jnp.concatenate([vt_ref[tile0 + i, ts, :] for i in range(nloc // LANES)], axis=1)
            ot = (_dot(vt, pb[:nloc, :]) + _dot(vxt_ref[ts, :], pb[nloc:, :])) * r
            ot = jnp.concatenate([ot[:HEAD_DIM, :gq], ot[HEAD_DIM:, gq:]], axis=0)
            o_ref[u * gq:(u + 1) * gq, ts] = ot.T.astype(o_ref.dtype)


def _attn_b(q, k, vt, kx, vx, tbl):
    bn, l, w = q.shape
    lc = kx.shape[1]
    gq = B_GROUP * GRID_W
    sq = B_STEP_GROUPS * gq
    nk = B_SLAB * GRID_W + lc
    vxt = jnp.swapaxes(vx, 1, 2)
    seq = pl.BlockSpec((None, l, w), lambda b, g: (b, 0, 0))
    qblk = pl.BlockSpec((None, sq, w), lambda b, g: (b, g, 0))
    return pl.pallas_call(
        _attn_b_kernel,
        out_shape=jax.ShapeDtypeStruct((bn, l, w), BF16),
        grid=(bn, l // sq),
        in_specs=[qblk, seq, pl.BlockSpec((None, l // LANES, w, LANES), lambda b, g: (b, 0, 0, 0)),
                  pl.BlockSpec((None, lc, w), lambda b, g: (b, 0, 0)),
                  pl.BlockSpec((None, w, lc), lambda b, g: (b, 0, 0)),
                  _resident(tbl.shape)],
        out_specs=qblk,
        scratch_shapes=[pltpu.VMEM((B_STEP_GROUPS, B_HEADS // 2, nk, 2 * gq), F32),
                        pltpu.VMEM((B_STEP_GROUPS, B_HEADS // 2, nk, 2 * gq), BF16)],
        compiler_params=_params("parallel", "arbitrary"),
        name="nbr_attn",
    )(q, k, vt, kx, vxt, tbl)


def _dft_matrices(l, fh):
    k = jnp.arange(l, dtype=jnp.int32)[:, None]
    n = jnp.arange(l, dtype=jnp.int32)[None, :]
    ang = (((2 * k + 1) * n) % (4 * l)).astype(F32) * (math.pi / (2 * l))
    fre = jnp.cos(ang).reshape(l // fh, fh, l)
    fim = (-jnp.sin(ang)).reshape(l // fh, fh, l)
    f = jnp.concatenate([fre, fim], axis=1).reshape(2 * l, l)
    g = f.T * (1.0 / l)
    return f.astype(BF16), g.astype(BF16)


def _filter_kernel(z_ref, t_ref, w1_ref, b1_ref, fr_ref, w2_ref, b2_ref, w3_ref, dl_ref,
                   ah_ref, al_ref, sh_ref, sl_ref):
    fr = fr_ref[...]
    hid = jnp.sin(fr * (_dot_hp(z_ref[...], w1_ref[...]) + b1_ref[...]))
    hid = jnp.sin(fr * (_dot_hp(hid, w2_ref[...]) + b2_ref[...]))
    cw = dl_ref.shape[1]
    p = z_ref.shape[0] // 3
    decay = jnp.exp(-t_ref[...] * dl_ref[...])
    kf = _dot_hp(hid, w3_ref[:, :cw]) * decay
    kb = _dot_hp(hid, w3_ref[:, cw:]) * decay
    kf0, kf1, kfr = kf[:p], kf[p:2 * p], kf[2 * p:]
    kb0, kb1, kbr = kb[:p], kb[p:2 * p], kb[2 * p:]
    first = lax.broadcasted_iota(jnp.int32, (p, cw), 0) == 0
    drop0 = lambda v: jnp.where(first, 0.0, v)
    kb0 = drop0(kb0)
    colsum = lambda v: jnp.sum(jnp.abs(v), axis=0, keepdims=True)
    inv = 1.0 / (colsum(kf0) + colsum(kf1) + colsum(kb0) + colsum(kb1))
    pairs = ((kf0, kb0), (kf1, drop0(kfr)), (kbr, drop0(kb1)))
    for d, (cp, cm) in enumerate(pairs):
        cols = slice(d * cw, (d + 1) * cw)
        ah_ref[:, cols], al_ref[:, cols] = _split_bf16((cp + cm) * inv)
        sh_ref[:, cols], sl_ref[:, cols] = _split_bf16((cp - cm) * inv)


def _filters(z, t, w1, b1, freq, w2, b2, w3, deltas):
    p = z.shape[0] // 3
    cw = deltas.shape[1]
    fw = w2.shape[0]
    const = lambda shape: pl.BlockSpec(shape, lambda o: (0,) * len(shape))
    out = jax.ShapeDtypeStruct((p, C_ORDER * 3 * cw), BF16)
    oblk = pl.BlockSpec((p, 3 * cw), lambda o: (0, o))
    return pl.pallas_call(
        _filter_kernel,
        out_shape=[out] * 4,
        grid=(C_ORDER,),
        in_specs=[const(z.shape), const(t.shape), const(w1.shape), const((1, fw)), const((1, fw)),
                  const((fw, fw)), const((1, fw)), pl.BlockSpec((fw, C_DIRS * cw), lambda o: (0, o)),
                  const((1, cw))],
        out_specs=[oblk] * 4,
        compiler_params=_params("arbitrary"),
        name="hyena_filter",
    )(z, t, w1, b1, freq, w2, b2, w3, deltas)


def _spectrum_kernel(f_ref, ah_ref, al_ref, sh_ref, sl_ref, kre_ref, kim_ref):
    fh = kre_ref.shape[0]
    fre = f_ref[:fh, :]
    fim = f_ref[fh:, :]
    kre_ref[...] = _dot(fre, ah_ref[...]) + _dot(fre, al_ref[...])
    kim_ref[...] = _dot(fim, sh_ref[...]) + _dot(fim, sl_ref[...])


def _spectrum(f, ah, al, sh, sl, fh):
    p, n = ah.shape
    tn = n // C_ORDER
    taps = pl.BlockSpec((p, tn), lambda o, j: (0, o))
    out = jax.ShapeDtypeStruct((p, n), F32)
    oblk = pl.BlockSpec((fh, tn), lambda o, j: (j, o))
    return pl.pallas_call(
        _spectrum_kernel,
        out_shape=[out, out],
        grid=(C_ORDER, p // fh),
        in_specs=[pl.BlockSpec((2 * fh, p), lambda o, j: (j, 0)), taps, taps, taps, taps],
        out_specs=[oblk, oblk],
        compiler_params=_params("arbitrary", "arbitrary"),
        name="hyena_spectrum",
    )(f, ah, al, sh, sl)


def _short_conv(u, w_ref, b_ref):
    n = u.shape[0]
    row = lax.broadcasted_iota(jnp.int32, u.shape, 0)
    prev = jnp.where(row == 0, 0.0, pltpu.roll(u, 1, 0))
    nxt = jnp.where(row == n - 1, 0.0, pltpu.roll(u, n - 1, 0))
    return prev * w_ref[0:1, :] + u * w_ref[1:2, :] + nxt * w_ref[2:3, :] + b_ref[...]


def _short_conv_wrap(u, w_ref, b_ref):
    n = u.shape[0]
    return (pltpu.roll(u, 1, 0) * w_ref[0:1, :] + u * w_ref[1:2, :] + pltpu.roll(u, n - 1, 0) * w_ref[2:3, :]
            + b_ref[...])


CONV_CHUNKS = 2


def _longconv_kernel(u_ref, g_ref, cwu_ref, cbu_ref, cwg_ref, cbg_ref, f_ref, gm_ref, kre_ref, kim_ref, d_ref,
                     o_ref, ub_ref, acc_ref, *, conv_u, n_steps):
    j = pl.program_id(1)
    fh = kre_ref.shape[0]
    p, cw = ub_ref.shape[1], d_ref.shape[1]
    cc = cw // CONV_CHUNKS
    l = 2 * p
    edge = 16

    def edge_conv(ref, w_ref, b_ref, cs):
        head = _short_conv(ref[0:2 * edge, cs].astype(F32), w_ref.at[:, cs], b_ref.at[:, cs])[:edge]
        tail = _short_conv(ref[l - 2 * edge:l, cs].astype(F32), w_ref.at[:, cs], b_ref.at[:, cs])[edge:]
        return head, tail

    def load_u(c, cs):
        ub = ub_ref.at[c]
        if conv_u:
            u = _short_conv_wrap(u_ref[:, cs].astype(F32), cwu_ref.at[:, cs], cbu_ref.at[:, cs]).astype(BF16)
        else:
            u = u_ref[:, cs]
        ub[:, :cc] = u[:p]
        ub[:, cc:] = u[p:]
        if conv_u:
            head, tail = edge_conv(u_ref, cwu_ref, cbu_ref, cs)
            ub[0:edge, :cc] = head.astype(BF16)
            ub[p - edge:p, cc:] = tail.astype(BF16)

    def spectral(c, cs):
        spec = _dot(f_ref[...], ub_ref[c])
        u0r, u1r, u0i, u1i = spec[:fh, :cc], spec[:fh, cc:], spec[fh:, :cc], spec[fh:, cc:]
        tap = lambda ref, d: ref[:, d * cw + c * cc:d * cw + (c + 1) * cc]
        c0r, c1r, cmr = (tap(kre_ref, d) for d in range(3))
        c0i, c1i, cmi = (tap(kim_ref, d) for d in range(3))
        y0r = (c0r * u0r - c0i * u0i) + (cmr * u1r - cmi * u1i)
        y0i = (c0r * u0i + c0i * u0r) + (cmr * u1i + cmi * u1r)
        y1r = (c1r * u0r - c1i * u0i) + (c0r * u1r - c0i * u1i)
        y1i = (c1r * u0i + c1i * u0r) + (c0r * u1i + c0i * u1r)
        y = jnp.concatenate([jnp.concatenate([y0r, y1r], axis=1), jnp.concatenate([y0i, y1i], axis=1)], axis=0)
        return _dot(gm_ref[...], y.astype(BF16))

    def gated_out(c, cs, conv):
        gate = _short_conv_wrap(g_ref[:, cs].astype(F32), cwg_ref.at[:, cs], cbg_ref.at[:, cs])
        ub = ub_ref.at[c]
        y = jnp.concatenate([conv[:, :cc], conv[:, cc:]], axis=0)
        y = y + jnp.concatenate([ub[:, :cc], ub[:, cc:]], axis=0).astype(F32) * d_ref[:, cs]
        o_ref[:, cs] = (gate * y).astype(o_ref.dtype)
        head, tail = edge_conv(g_ref, cwg_ref, cbg_ref, cs)
        o_ref[0:edge, cs] = (head * y[:edge]).astype(o_ref.dtype)
        o_ref[l - edge:l, cs] = (tail * y[l - edge:]).astype(o_ref.dtype)

    def step(first, last):
        for c in range(CONV_CHUNKS):
            cs = slice(c * cc, (c + 1) * cc)
            if first:
                load_u(c, cs)
            conv = spectral(c, cs)
            if not first:
                conv = acc_ref[c] + conv
            if last:
                gated_out(c, cs, conv)
            else:
                acc_ref[c] = conv

    if n_steps == 1:
        step(True, True)
    else:
        pl.when(j == 0)(lambda: step(True, False))
        pl.when(j == n_steps - 1)(lambda: step(False, True))
        if n_steps > 2:
            pl.when(jnp.logical_and(j > 0, j < n_steps - 1))(lambda: step(False, False))


def _longconv(u_arr, u_blk, g_arr, g_blk, conv_w, conv_b, f, gm, kre, kim, order, d, *, conv_u, fh):
    bn, l, _ = u_arr.shape
    p = l // 2
    cw = d.shape[1]
    ub = u_blk if conv_u else 0
    tok = lambda blk: pl.BlockSpec((None, l, cw), lambda b, j: (b, 0, blk))
    cpar = lambda rows, blk: pl.BlockSpec((rows, cw), lambda b, j: (0, blk))
    ktab = pl.BlockSpec((fh, 3 * cw), lambda b, j: (j, order))
    return pl.pallas_call(
        functools.partial(_longconv_kernel, conv_u=conv_u, n_steps=p // fh),
        out_shape=jax.ShapeDtypeStruct((bn, l, cw), BF16),
        grid=(bn, p // fh),
        in_specs=[tok(u_blk), tok(g_blk), cpar(3, ub), cpar(1, ub), cpar(3, g_blk), cpar(1, g_blk),
                  pl.BlockSpec((2 * fh, p), lambda b, j: (j, 0)),
                  pl.BlockSpec((p, 2 * fh), lambda b, j: (0, j)),
                  ktab, ktab, pl.BlockSpec((1, cw), lambda b, j: (0, 0))],
        out_specs=tok(0),
        scratch_shapes=[pltpu.VMEM((CONV_CHUNKS, p, 2 * cw // CONV_CHUNKS), BF16),
                        pltpu.VMEM((CONV_CHUNKS, p, 2 * cw // CONV_CHUNKS), F32)],
        compiler_params=_params("parallel", "arbitrary"),
        name="hyena_longconv",
    )(u_arr, g_arr, conv_w, conv_b, conv_w, conv_b, f, gm, kre, kim, d)


def _hyena_consts(l, cw):
    p = l // 2
    e = np.arange(p)
    pos = np.concatenate([e, p + e, p - e])
    t = np.linspace(0.0, 1.0, l, dtype=np.float32)[pos][:, None]
    w = (2.0 * math.pi * np.arange(l, dtype=np.float32) / l).astype(np.float32)[pos][:, None]
    fq = np.linspace(1e-4, C_BANDS - 1, C_BANDS, dtype=np.float32)[None, :]
    wf = jnp.asarray(w) * jnp.asarray(fq)
    z = jnp.concatenate([jnp.asarray(t), jnp.cos(wf), -jnp.sin(wf)], axis=-1)
    z = jnp.pad(z, ((0, 0), (0, C_FILTER_WIDTH - z.shape[1])))
    deltas = np.abs(np.linspace(HYENA_MIN_DECAY, HYENA_MAX_DECAY, cw, dtype=np.float32))[None, :]
    return z, jnp.asarray(t), jnp.asarray(deltas)


def _hyena(hy, hconst, f, gm, fh, conv_w, conv_b, w1p, b1, freq, w2, b2, w3, dbias):
    z, t, deltas = hconst
    ah, al, sh, sl = _filters(z, t, w1p, b1, freq, w2, b2, w3, deltas)
    kre, kim = _spectrum(f, ah, al, sh, sl, fh)
    zz = _longconv(hy, 0, hy, 1, conv_w, conv_b, f, gm, kre, kim, 0, dbias[0:1], conv_u=True, fh=fh)
    return _longconv(zz, 0, hy, 2, conv_w, conv_b, f, gm, kre, kim, 1, dbias[1:2], conv_u=False, fh=fh)


def _mix_ffn_kernel(x_ref, a_ref, b_ref, c_ref, wo_ref, g1_ref, gn_ref, sh_ref, sc_ref, g2_ref,
                    wg_ref, wu_ref, wd_ref, o_ref, *, fc):
    wa, wb = a_ref.shape[1], b_ref.shape[1]
    mix = (_dot(a_ref[...], wo_ref[:wa, :]) + _dot(b_ref[...], wo_ref[wa:wa + wb, :])
           + _dot(c_ref[...], wo_ref[wa + wb:, :]))
    x1 = x_ref[...] + g1_ref[...] * mix
    xb = _rms_mod(x1, gn_ref[...], sh_ref[...], sc_ref[...]).astype(BF16)
    acc = None
    for c0 in range(0, wg_ref.shape[1], fc):
        hg = _dot(xb, wg_ref[:, c0:c0 + fc])
        hu = _dot(xb, wu_ref[:, c0:c0 + fc])
        act = ((hg * jax.nn.sigmoid(hg)) * hu).astype(BF16)
        part = _dot(act, wd_ref[c0:c0 + fc, :])
        acc = part if acc is None else acc + part
    o_ref[...] = x1 + g2_ref[...] * acc


def _mix_ffn(x, a, b, c, wo, g1, gn, sh2, sc2, g2, wg, wu, wd):
    bn, l, d = x.shape
    tm = min(TOKEN_TILE, l)
    row = pl.BlockSpec((None, 1, d), lambda bb, t: (bb, 0, 0))
    tok = lambda width: pl.BlockSpec((None, tm, width), lambda bb, t: (bb, t, 0))
    return pl.pallas_call(
        functools.partial(_mix_ffn_kernel, fc=256),
        out_shape=jax.ShapeDtypeStruct((bn, l, d), F32),
        grid=(bn, l // tm),
        in_specs=[tok(d), tok(a.shape[2]), tok(b.shape[2]), tok(c.shape[2]), _resident(wo.shape),
                  row, pl.BlockSpec((1, d), lambda bb, t: (0, 0)), row, row, row,
                  _resident(wg.shape), _resident(wu.shape), _resident(wd.shape)],
        out_specs=tok(d),
        compiler_params=_params("parallel", "parallel"),
        name="mix_ffn",
    )(x, a, b, c, wo, g1, gn, sh2, sc2, g2, wg, wu, wd)


def _rope_tables(l):
    half = HEAD_DIM // 2
    nfreq = half // 2
    inv = ROPE_BASE ** (-jnp.arange(nfreq, dtype=F32) / nfreq)
    pos = jnp.arange(l)
    rows, cols = pos // GRID_W, pos % GRID_W
    ang = jnp.concatenate([rows.astype(F32)[:, None] * inv[None, :]] * 2
                          + [cols.astype(F32)[:, None] * inv[None, :]] * 2, axis=-1)
    first = (np.arange(HEAD_DIM) % half) < nfreq
    cos, sin = jnp.cos(ang), jnp.sin(ang)
    sin_a = jnp.where(first[None, :], -sin, 0.0)
    sin_b = jnp.where(first[None, :], 0.0, sin)
    reps = LANES // HEAD_DIM
    return tuple(jnp.tile(tb, (1, reps)) for tb in (cos, sin_a, sin_b))


def kernel(x, c, ctx, c_ctx, ada_w, ada_b, norm1_g, norm2_g, w_in, qnorm_a, knorm_a, sink_a, qnorm_b, knorm_b,
           rpb_b, conv_w, conv_b, filt_w1, filt_b1, filt_freq, filt_w2, filt_b2, filt_w3, hyena_bias, w_out,
           ffn_w_gate, ffn_w_up, ffn_w_down):
    bn, l, d = x.shape
    lc = ctx.shape[1]
    depth = ada_w.shape[0]
    cw = hyena_bias.shape[2]
    assert l % (A_STEP_BLOCKS * A_BLOCK) == 0 and l >= 3 * A_SPAN and lc % 256 == 0
    assert l % (B_STEP_GROUPS * B_GROUP * GRID_W) == 0 and l // GRID_W >= 3 * B_GROUP

    mod_rows = 8 * (-(-(bn + 1) // 8))
    c_all = jnp.zeros((mod_rows, d), F32).at[:bn].set(c).at[bn].set(c_ctx)
    mod = _modulation(c_all, ada_w, ada_b)

    a_order = (0, 2, 1, 3)
    a_kv = tuple(hh // (A_HEADS // A_KV_HEADS) for hh in a_order)
    heads_a = lambda t, axis: [lax.slice_in_dim(t, hh * HEAD_DIM, (hh + 1) * HEAD_DIM, axis=axis) for hh in a_order]
    qa_w = A_HEADS * HEAD_DIM
    w_in_b = jnp.concatenate(heads_a(w_in, 2) + [w_in[:, :, qa_w:]], axis=2).astype(BF16)
    w_out_b = jnp.concatenate(heads_a(w_out, 1) + [w_out[:, qa_w:]], axis=1).astype(BF16)
    sink_p = LOG2E * jnp.stack([sink_a[:, hh] for hh in a_order], axis=1)
    wg_b, wu_b, wd_b = ffn_w_gate.astype(BF16), ffn_w_up.astype(BF16), ffn_w_down.astype(BF16)

    lane = np.arange(256)
    e_heads = jnp.asarray((lane[:, None] // HEAD_DIM) == (lane[None, :] // HEAD_DIM), BF16)
    tile4 = lambda g: jnp.tile(g.reshape(1, HEAD_DIM), (1, 256 // HEAD_DIM))
    rope_lat = _rope_tables(l)

    fh_lat, fh_ctx = min(FREQ_TILE, l // 2), min(FREQ_TILE, lc // 2)
    f_lat, g_lat = _dft_matrices(l // 2, fh_lat)
    f_ctx, g_ctx = _dft_matrices(lc // 2, fh_ctx)
    hc_lat = _hyena_consts(l, cw)
    hc_ctx = _hyena_consts(lc, cw)

    xc = ctx
    for i in range(depth):
        last = i == depth - 1
        m = mod[i]
        part = lambda rows, k: rows[:, None, k * d:(k + 1) * d]
        lat = [part(m[:bn], k) for k in range(6)]
        cx = [part(m[bn:bn + 1], k) for k in range(6)]
        g1n = norm1_g[i].reshape(1, d)
        g2n = norm2_g[i].reshape(1, d)
        gqa, gka, gqb, gkb = tile4(qnorm_a[i]), tile4(knorm_a[i]), tile4(qnorm_b[i]), tile4(knorm_b[i])

        qa, ka, va, qb, kb, vb, hy = _proj_in(x, lat[0], lat[1], g1n, w_in_b[i], gqa, gka, gqb, gkb, e_heads,
                                              rope_lat)
        ctx_out = _proj_in(xc.reshape(1, bn * lc, d), cx[0], cx[1], g1n, w_in_b[i], gqa, gka, gqb, gkb, e_heads, None)
        qa_c, ka_c, va_c, qb_c, kb_c, vb_c, hy_c = [t.reshape(bn, lc, -1) for t in ctx_out]
        w1p = jnp.pad(filt_w1[i], ((0, C_FILTER_WIDTH - filt_w1.shape[1]), (0, 0)))
        hy_args = (conv_w[i], conv_b[i].reshape(1, -1), w1p, filt_b1[i].reshape(1, -1),
                   filt_freq[i].reshape(1, -1), filt_w2[i], filt_b2[i].reshape(1, -1), filt_w3[i], hyena_bias[i])

        sink_row = jnp.repeat(sink_p[i], A_BLOCK).reshape(1, A_HEADS * A_BLOCK)
        out_a = _attn_a(qa, ka, va, ka_c, va_c, sink_row)
        out_b = _attn_b(qb, kb, vb, kb_c, vb_c, _rpb_tables(rpb_b[i]))
        out_c = _hyena(hy, hc_lat, f_lat, g_lat, fh_lat, *hy_args)
        x = _mix_ffn(x, out_a, out_b, out_c, w_out_b[i], lat[2], g2n, lat[3], lat[4], lat[5],
                     wg_b[i], wu_b[i], wd_b[i])

        if not last:
            oa_c = _dense_attn(qa_c, ka_c, va_c, sink_p[i], kv_of=a_kv, use_sink=True)
            ob_c = _dense_attn(qb_c, kb_c, vb_c, sink_p[i], kv_of=tuple(range(B_HEADS)), use_sink=False)
            oc_c = _hyena(hy_c, hc_ctx, f_ctx, g_ctx, fh_ctx, *hy_args)
            flat = lambda t: t.reshape(1, bn * lc, -1)
            xc = _mix_ffn(flat(xc), flat(oa_c), flat(ob_c), flat(oc_c), w_out_b[i], cx[2], g2n, cx[3], cx[4], cx[5],
                          wg_b[i], wu_b[i], wd_b[i]).reshape(bn, lc, d)
    return x
```

```python
import functools
import math

import jax
import jax.numpy as jnp
import numpy as np
from jax import lax
from jax.experimental import pallas as pl
from jax.experimental.pallas import tpu as pltpu

F32 = jnp.float32
BF16 = jnp.bfloat16

GRID_W = 64
HEAD_DIM = 64
A_HEADS = 4
A_KV_HEADS = 2
A_BLOCK = 128
B_HEADS = 4
NA_ROWS = 8
NA_COLS = 16
C_ORDER = 2
C_DIRS = 2
C_FILTER_WIDTH = 64
C_BANDS = 16
ROPE_BASE = 10000.0
EPS = 1e-6
NEG_INF = -1e30
LOG2E = math.log2(math.e)
HYENA_MIN_DECAY = math.log(1e-2) / 1.5
HYENA_MAX_DECAY = math.log(1e-2) / 0.3

V7X_VMEM_BYTES = 64 * 1024 * 1024
VMEM_LIMIT = V7X_VMEM_BYTES - 8 * 1024 * 1024
LANES = 128
TOKEN_TILE = 512
FREQ_TILE = 512


def _params(*sem):
    return pltpu.CompilerParams(dimension_semantics=sem, vmem_limit_bytes=VMEM_LIMIT)


def _dot(a, b):
    return jnp.dot(a, b, preferred_element_type=F32)


def _dot_nt(a, b):
    return lax.dot_general(a, b, (((1,), (1,)), ((), ())), preferred_element_type=F32)


def _dot_hp(a, b):
    ah, al = _split_bf16(a)
    bh, bl = _split_bf16(b)
    return _dot(ah, bh) + (_dot(ah, bl) + _dot(al, bh))


def _split_bf16(v):
    hi = v.astype(BF16)
    lo = (v - hi.astype(F32)).astype(BF16)
    return hi, lo


def _layer_of(stack, layer):
    return pl.BlockSpec((None,) + stack.shape[1:], lambda *_: (layer, 0, 0), pipeline_mode=pl.Buffered(1))


def _resident(shape):
    nd = len(shape)
    return pl.BlockSpec(shape, lambda *_: (0,) * nd, pipeline_mode=pl.Buffered(1))


def _mod_kernel(c_ref, w_ref, b_ref, o_ref):
    cv = c_ref[...]
    sc = (cv * jax.nn.sigmoid(cv)).astype(BF16)
    o_ref[...] = _dot(sc, w_ref[...].astype(BF16)) + b_ref[...]


def _modulation(c_all, ada_w, ada_b):
    depth, d, n = ada_w.shape
    rows = c_all.shape[0]
    tn = 512
    return pl.pallas_call(
        _mod_kernel,
        out_shape=jax.ShapeDtypeStruct((depth, rows, n), F32),
        grid=(depth, n // tn),
        in_specs=[
            pl.BlockSpec((rows, d), lambda i, j: (0, 0)),
            pl.BlockSpec((None, d, tn), lambda i, j: (i, 0, j)),
            pl.BlockSpec((None, 1, tn), lambda i, j: (i, 0, j)),
        ],
        out_specs=pl.BlockSpec((None, rows, tn), lambda i, j: (i, 0, j)),
        compiler_params=_params("arbitrary", "arbitrary"),
        name="modulation",
    )(c_all, ada_w, ada_b.reshape(depth, 1, n))


def _rms_mod(x, g, shift, scale):
    y = x * lax.rsqrt(jnp.mean(x * x, axis=-1, keepdims=True) + EPS)
    return (y * g) * (1 + scale) + shift


def _head_norm(h, g, e):
    ss = _dot((h * h).astype(BF16), e)
    return (h * lax.rsqrt(ss * (1.0 / HEAD_DIM) + EPS)) * g


def _rope(t, cos, sin_a, sin_b):
    outs = []
    for c in range(t.shape[1] // LANES):
        tc = t[:, c * LANES:(c + 1) * LANES]
        outs.append(tc * cos + pltpu.roll(tc, LANES - 16, 1) * sin_a + pltpu.roll(tc, 16, 1) * sin_b)
    return outs[0] if len(outs) == 1 else jnp.concatenate(outs, axis=1)


PROJ_SPLIT = 2


def _proj_in_kernel(x_ref, shift_ref, scale_ref, g_ref, w_ref, gqa_ref, gka_ref, gqb_ref, gkb_ref, e_ref, *rest,
                    rope):
    if rope:
        cos_ref, sa_ref, sb_ref = rest[:3]
        rest = rest[3:]
    qa_ref, ka_ref, va_ref, qb_ref, kb_ref, vb_ref, hy_ref = rest
    qscale = LOG2E * HEAD_DIM ** -0.5
    hm = x_ref.shape[0] // PROJ_SPLIT
    for part in range(PROJ_SPLIT):
        rows = slice(part * hm, (part + 1) * hm)
        xb = _rms_mod(x_ref[rows, :], g_ref[...], shift_ref[...], scale_ref[...]).astype(BF16)

        def proj(lo, hi):
            return _dot(xb, w_ref[:, lo:hi])

        def maybe_rope(t):
            return _rope(t, cos_ref[rows, :], sa_ref[rows, :], sb_ref[rows, :]) if rope else t

        def store_v(ref, v):
            if rope:
                for i in range(hm // LANES):
                    ref[part * (hm // LANES) + i] = v[i * LANES:(i + 1) * LANES, :].T.astype(BF16)
            else:
                ref[rows, :] = v.astype(BF16)

        qa = maybe_rope(_head_norm(proj(0, 256), gqa_ref[...], e_ref[...])) * qscale
        t0, t1 = qa[:, :LANES], qa[:, LANES:]
        lo = lax.broadcasted_iota(jnp.int32, t0.shape, 1) < HEAD_DIM
        qa_ref[rows, :LANES] = jnp.where(lo, t0, pltpu.roll(t1, HEAD_DIM, 1)).astype(BF16)
        qa_ref[rows, LANES:] = jnp.where(lo, pltpu.roll(t0, HEAD_DIM, 1), t1).astype(BF16)
        ka = maybe_rope(_head_norm(proj(256, 384), gka_ref[:, :128], e_ref[:128, :128]))
        ka_ref[rows, :] = ka.astype(BF16)
        store_v(va_ref, proj(384, 512))
        qb = _head_norm(proj(512, 768), gqb_ref[...], e_ref[...])
        qb_ref[rows, :] = (qb * qscale).astype(BF16)
        kb_ref[rows, :] = _head_norm(proj(768, 1024), gkb_ref[...], e_ref[...]).astype(BF16)
        store_v(vb_ref, proj(1024, 1280))
        hy_ref[rows, :] = proj(1280, 2816).astype(BF16)


def _proj_in(x, shift, scale, g, w, layer, gqa, gka, gqb, gkb, e, rope_tabs):
    bn, l, d = x.shape
    n = w.shape[2]
    rope = rope_tabs is not None
    tm = min(PROJ_SPLIT * TOKEN_TILE, l)
    row = lambda width: pl.BlockSpec((None, 1, width), lambda b, t: (b, 0, 0))
    const = lambda shape: pl.BlockSpec(shape, lambda b, t: (0,) * len(shape))
    tok = lambda width: pl.BlockSpec((None, tm, width), lambda b, t: (b, t, 0))
    tab = pl.BlockSpec((tm, LANES), lambda b, t: (t, 0))
    widths = (256, 128, 128, 256, 256, 256, n - 1280)
    shapes = [(bn, l, wd) for wd in widths]
    specs = [tok(wd) for wd in widths]
    if rope:
        for i in (2, 5):
            shapes[i] = (bn, l // LANES, widths[i], LANES)
            specs[i] = pl.BlockSpec((None, tm // LANES, widths[i], LANES), lambda b, t: (b, t, 0, 0))
    return pl.pallas_call(
        functools.partial(_proj_in_kernel, rope=rope),
        out_shape=[jax.ShapeDtypeStruct(sh, BF16) for sh in shapes],
        grid=(bn, l // tm),
        in_specs=[tok(d), row(d), row(d), const((1, d)), _layer_of(w, layer),
                  const((1, 256)), const((1, 256)), const((1, 256)), const((1, 256)), const((256, 256))]
                 + ([tab, tab, tab] if rope else []),
        out_specs=specs,
        compiler_params=_params("parallel", "parallel"),
        name="proj_in_rope" if rope else "proj_in",
    )(x, shift, scale, g, w, gqa, gka, gqb, gkb, e, *(rope_tabs or ()))


def _softmax_pv(s_list, v_list, sink):
    m = s_list[0].max(axis=-1, keepdims=True)
    for s in s_list[1:]:
        m = jnp.maximum(m, s.max(axis=-1, keepdims=True))
    if sink is not None:
        m = jnp.maximum(m, sink)
    den = None
    out = None
    for s, v in zip(s_list, v_list):
        p = jnp.exp2(s - m)
        ps = p.sum(axis=-1, keepdims=True)
        den = ps if den is None else den + ps
        o = _dot(p.astype(BF16), v)
        out = o if out is None else out + o
    if sink is not None:
        den = den + jnp.exp2(sink - m)
    return out * (1.0 / den)


def _stack_heads(q):
    lo = lax.broadcasted_iota(jnp.int32, q.shape, 1) < HEAD_DIM
    zero = jnp.zeros_like(q)
    return jnp.concatenate([jnp.where(lo, q, zero), jnp.where(lo, zero, q)], axis=0)


def _unstack_heads(o):
    m = o.shape[0] // 2
    lo = lax.broadcasted_iota(jnp.int32, (m, LANES), 1) < HEAD_DIM
    return jnp.where(lo, o[:m], o[m:])


SOFTMAX_CHUNK = 32


def _softmax_keys(s_ref, p_ref, nloc, add_loc, sink):
    nk = s_ref.shape[0]
    ch = SOFTMAX_CHUNK
    macc = None
    for r0 in range(0, nk, ch):
        s = s_ref[r0:r0 + ch, :]
        if r0 < nloc:
            s = s + add_loc(r0)
            s_ref[r0:r0 + ch, :] = s
        macc = s if macc is None else jnp.maximum(macc, s)
    m = macc.max(axis=0, keepdims=True)
    if sink is not None:
        m = jnp.maximum(m, sink)
    sacc = None
    for r0 in range(0, nk, ch):
        p = jnp.exp2(s_ref[r0:r0 + ch, :] - m)
        sacc = p if sacc is None else sacc + p
        p_ref[r0:r0 + ch, :] = p.astype(BF16)
    den = sacc.sum(axis=0, keepdims=True)
    if sink is not None:
        den = den + jnp.exp2(sink - m)
    return 1.0 / den


A_STEP_BLOCKS = 8
A_SPAN = 3 * A_BLOCK


def _attn_a_kernel(q_ref, k_ref, vt_ref, kx_ref, vxt_ref, mask_ref, sink_ref, o_ref, s_ref, p_ref):
    l = k_ref.shape[0]
    nb = l // A_BLOCK
    hd = HEAD_DIM
    for u in range(A_STEP_BLOCKS):
        n = pl.program_id(1) * A_STEP_BLOCKS + u
        tile0 = jnp.clip(n - 1, 0, nb - 3)
        start = pl.multiple_of(tile0 * A_BLOCK, A_BLOCK)
        pat = jnp.where(n == 0, 0, jnp.where(n == nb - 1, 2, 1))
        q = q_ref[u * A_BLOCK:(u + 1) * A_BLOCK, :]
        qs = jnp.concatenate([_stack_heads(q[:, :LANES]), _stack_heads(q[:, LANES:])], axis=0)
        sb, pb = s_ref.at[u % 2], p_ref.at[u % 2]
        sb[:A_SPAN, :] = _dot_nt(k_ref[pl.ds(start, A_SPAN), :], qs)
        sb[A_SPAN:, :] = _dot_nt(kx_ref[...], qs)
        r = _softmax_keys(sb, pb, A_SPAN, lambda r0: mask_ref[pat, r0:r0 + SOFTMAX_CHUNK, :], sink_ref[...])
        vt = jnp.concatenate([vt_ref[tile0 + i] for i in range(3)], axis=1)
        ot = (_dot(vt, pb[:A_SPAN, :]) + _dot(vxt_ref[...], pb[A_SPAN:, :])) * r
        ot = jnp.concatenate([ot[(i % 2) * hd:(i % 2 + 1) * hd, i * A_BLOCK:(i + 1) * A_BLOCK]
                              for i in range(A_HEADS)], axis=0)
        o_ref[u * A_BLOCK:(u + 1) * A_BLOCK, :] = ot.T.astype(o_ref.dtype)


def _attn_a_mask():
    i = np.arange(A_BLOCK)[None, :]
    j = np.arange(A_SPAN)[:, None]
    offs = (0, A_BLOCK, 2 * A_BLOCK)
    m = np.stack([np.where(np.abs(j - i - o) <= A_BLOCK, 0.0, -np.inf) for o in offs])
    return jnp.asarray(np.tile(m, (1, 1, A_HEADS)), F32)


def _attn_a(q, k, vt, kx, vx, sink_row):
    bn, l, qw = q.shape
    lc = kx.shape[1]
    kvw = k.shape[2]
    nb = l // A_BLOCK
    qs = A_STEP_BLOCKS * A_BLOCK
    mask = _attn_a_mask()
    vxt = jnp.swapaxes(vx, 1, 2)
    seq = pl.BlockSpec((None, l, kvw), lambda b, s: (b, 0, 0))
    qblk = pl.BlockSpec((None, qs, qw), lambda b, s: (b, s, 0))
    nq = A_HEADS * A_BLOCK
    return pl.pallas_call(
        _attn_a_kernel,
        out_shape=jax.ShapeDtypeStruct((bn, l, qw), BF16),
        grid=(bn, l // qs),
        in_specs=[qblk, seq, pl.BlockSpec((None, nb, kvw, A_BLOCK), lambda b, s: (b, 0, 0, 0)),
                  pl.BlockSpec((None, lc, kvw), lambda b, s: (b, 0, 0)),
                  pl.BlockSpec((None, kvw, lc), lambda b, s: (b, 0, 0)),
                  _resident(mask.shape), _resident(sink_row.shape)],
        out_specs=qblk,
        scratch_shapes=[pltpu.VMEM((2, A_SPAN + lc, nq), F32), pltpu.VMEM((2, A_SPAN + lc, nq), BF16)],
        compiler_params=_params("parallel", "arbitrary"),
        name="window_attn",
    )(q, k, vt, kx, vxt, mask, sink_row)


def _dense_attn_kernel(sink_ref, q_ref, k_ref, v_ref, o_ref, *, kv_of, use_sink):
    for h, kv in enumerate(kv_of):
        hs = slice(h * HEAD_DIM, (h + 1) * HEAD_DIM)
        ks = slice(kv * HEAD_DIM, (kv + 1) * HEAD_DIM)
        s = _dot_nt(q_ref[:, hs], k_ref[:, ks])
        o = _softmax_pv([s], [v_ref[:, ks]], sink_ref[h] if use_sink else None)
        o_ref[:, hs] = o.astype(o_ref.dtype)


def _dense_attn(q, k, v, sink, *, kv_of, use_sink):
    bn, l, qw = q.shape
    kvw = k.shape[2]
    full = lambda wd: pl.BlockSpec((None, l, wd), lambda b: (b, 0, 0))
    return pl.pallas_call(
        functools.partial(_dense_attn_kernel, kv_of=kv_of, use_sink=use_sink),
        out_shape=jax.ShapeDtypeStruct((bn, l, qw), BF16),
        grid=(bn,),
        in_specs=[pl.BlockSpec(memory_space=pltpu.SMEM), full(qw), full(kvw), full(kvw)],
        out_specs=full(qw),
        compiler_params=_params("parallel"),
        name="ctx_attn_sink" if use_sink else "ctx_attn",
    )(sink, q, k, v)


def _rpb_kernel(r_ref, oh_ref, ok_ref, o_ref):
    r = r_ref[...]
    b1 = r.astype(BF16)
    r2 = r - b1.astype(F32)
    b2 = r2.astype(BF16)
    b3 = (r2 - b2.astype(F32)).astype(BF16)
    oh = oh_ref[...]
    bias = (_dot(b1, oh) + _dot(b2, oh)) + _dot(b3, oh)
    o_ref[...] = jnp.where(ok_ref[...] > 0.5, bias * LOG2E, -jnp.inf)


def _rpb_slots():
    a = np.arange(B_GROUP)[:, None]
    kr = np.arange(B_SLAB)[None, :]
    dr = np.stack([kr - a + NA_ROWS - 1, kr - a + NA_ROWS // 2 - 1, kr - a + (B_GROUP + NA_ROWS - 1 - B_SLAB)])
    lo = np.stack([0 * a + 0 * kr, a + 0 * kr, 0 * a + (B_SLAB - NA_ROWS) + 0 * kr])
    valid = (kr[None] >= lo) & (kr[None] < lo + NA_ROWS)
    return np.where(valid, dr, -1)


def _rpb_assemble_kernel(t_ref, o_ref, *, n_dr):
    slots = _rpb_slots()
    w = GRID_W
    blank = jnp.full((w, w), -jnp.inf, F32)
    for pat in range(3):
        for t in range(o_ref.shape[1]):
            for kr in range(B_SLAB):
                for hh in range(2):
                    for a in range(B_GROUP):
                        dr = int(slots[pat, a, kr])
                        tile = t_ref[(2 * t + hh) * n_dr + dr] if dr >= 0 else blank
                        c0 = (hh * B_GROUP + a) * w
                        o_ref[pat, t, kr * w:(kr + 1) * w, c0:c0 + w] = tile


def _rpb_tables(rpb):
    h, nr, nc = rpb.shape
    col = np.arange(GRID_W)
    dc = np.clip(col[:, None] - col[None, :], 1 - NA_COLS, NA_COLS - 1) + NA_COLS - 1
    onehot = (dc.reshape(1, -1) == np.arange(nc)[:, None]).astype(np.float32)
    onehot = np.concatenate([onehot, np.zeros((32 - nc, GRID_W * GRID_W), np.float32)], axis=0)
    col_start = np.clip(col - NA_COLS // 2, 0, GRID_W - NA_COLS)
    col_ok = (col[:, None] >= col_start[None, :]) & (col[:, None] < col_start[None, :] + NA_COLS)
    rows = 64
    r2 = jnp.zeros((rows, 32), F32).at[:h * nr, :nc].set(rpb.reshape(h * nr, nc))
    tiles = pl.pallas_call(
        _rpb_kernel,
        out_shape=jax.ShapeDtypeStruct((rows, GRID_W * GRID_W), F32),
        name="rpb_table",
    )(r2, jnp.asarray(onehot, BF16), jnp.asarray(col_ok.reshape(1, -1), F32))
    tiles = tiles.reshape(rows, GRID_W, GRID_W)
    return pl.pallas_call(
        functools.partial(_rpb_assemble_kernel, n_dr=nr),
        out_shape=jax.ShapeDtypeStruct((3, h // 2, B_SLAB * GRID_W, 2 * B_GROUP * GRID_W), F32),
        compiler_params=_params(),
        name="rpb_assemble",
    )(tiles)


B_GROUP = 4
B_STEP_GROUPS = 2
B_SLAB = 12


def _attn_b_kernel(q_ref, k_ref, vt_ref, kx_ref, vxt_ref, tbl_ref, o_ref, s_ref, p_ref):
    ng = pl.num_programs(1) * B_STEP_GROUPS
    rows = k_ref.shape[0] // GRID_W
    nloc = B_SLAB * GRID_W
    gq = B_GROUP * GRID_W
    for u in range(B_STEP_GROUPS):
        g = pl.program_id(1) * B_STEP_GROUPS + u
        base = jnp.clip(g * B_GROUP - NA_ROWS // 2, 0, rows - B_SLAB)
        start = pl.multiple_of(base * GRID_W, LANES)
        tile0 = base // (LANES // GRID_W)
        pat = jnp.where(g == 0, 0, jnp.where(g == ng - 1, 2, 1))
        for t in range(B_HEADS // 2):
            ts = slice(t * LANES, (t + 1) * LANES)
            qs = _stack_heads(q_ref[u * gq:(u + 1) * gq, ts])
            sb, pb = s_ref.at[u, t], p_ref.at[u, t]
            sb[:nloc, :] = _dot_nt(k_ref[pl.ds(start, nloc), ts], qs)
            sb[nloc:, :] = _dot_nt(kx_ref[:, ts], qs)
            r = _softmax_keys(sb, pb, nloc, lambda r0: tbl_ref[pat, t, r0:r0 + SOFTMAX_CHUNK, :], None)
            vt = jnp.concatenate([vt_ref[tile0 + i, ts, :] for i in range(nloc // LANES)], axis=1)
            ot = (_dot(vt, pb[:nloc, :]) + _dot(vxt_ref[ts, :], pb[nloc:, :])) * r
            ot = jnp.concatenate([ot[:HEAD_DIM, :gq], ot[HEAD_DIM:, gq:]], axis=0)
            o_ref[u * gq:(u + 1) * gq, ts] = ot.T.astype(o_ref.dtype)


def _attn_b(q, k, vt, kx, vx, tbl):
    bn, l, w = q.shape
    lc = kx.shape[1]
    gq = B_GROUP * GRID_W
    sq = B_STEP_GROUPS * gq
    nk = B_SLAB * GRID_W + lc
    vxt = jnp.swapaxes(vx, 1, 2)
    seq = pl.BlockSpec((None, l, w), lambda b, g: (b, 0, 0))
    qblk = pl.BlockSpec((None, sq, w), lambda b, g: (b, g, 0))
    return pl.pallas_call(
        _attn_b_kernel,
        out_shape=jax.ShapeDtypeStruct((bn, l, w), BF16),
        grid=(bn, l // sq),
        in_specs=[qblk, seq, pl.BlockSpec((None, l // LANES, w, LANES), lambda b, g: (b, 0, 0, 0)),
                  pl.BlockSpec((None, lc, w), lambda b, g: (b, 0, 0)),
                  pl.BlockSpec((None, w, lc), lambda b, g: (b, 0, 0)),
                  _resident(tbl.shape)],
        out_specs=qblk,
        scratch_shapes=[pltpu.VMEM((B_STEP_GROUPS, B_HEADS // 2, nk, 2 * gq), F32),
                        pltpu.VMEM((B_STEP_GROUPS, B_HEADS // 2, nk, 2 * gq), BF16)],
        compiler_params=_params("parallel", "arbitrary"),
        name="nbr_attn",
    )(q, k, vt, kx, vxt, tbl)


def _dft_matrices(l, fh):
    k = jnp.arange(l, dtype=jnp.int32)[:, None]
    n = jnp.arange(l, dtype=jnp.int32)[None, :]
    ang = (((2 * k + 1) * n) % (4 * l)).astype(F32) * (math.pi / (2 * l))
    fre = jnp.cos(ang).reshape(l // fh, fh, l)
    fim = (-jnp.sin(ang)).reshape(l // fh, fh, l)
    f = jnp.concatenate([fre, fim], axis=1).reshape(2 * l, l)
    g = f.T * (1.0 / l)
    return f.astype(BF16), g.astype(BF16)


def _filter_kernel(z_ref, t_ref, w1_ref, b1_ref, fr_ref, w2_ref, b2_ref, w3_ref, dl_ref,
                   ah_ref, al_ref, sh_ref, sl_ref):
    fr = fr_ref[...]
    hid = jnp.sin(fr * (_dot_hp(z_ref[...], w1_ref[...]) + b1_ref[...]))
    hid = jnp.sin(fr * (_dot_hp(hid, w2_ref[...]) + b2_ref[...]))
    cw = dl_ref.shape[1]
    p = z_ref.shape[0] // 3
    decay = jnp.exp(-t_ref[...] * dl_ref[...])
    kf = _dot_hp(hid, w3_ref[:, :cw]) * decay
    kb = _dot_hp(hid, w3_ref[:, cw:]) * decay
    kf0, kf1, kfr = kf[:p], kf[p:2 * p], kf[2 * p:]
    kb0, kb1, kbr = kb[:p], kb[p:2 * p], kb[2 * p:]
    first = lax.broadcasted_iota(jnp.int32, (p, cw), 0) == 0
    drop0 = lambda v: jnp.where(first, 0.0, v)
    kb0 = drop0(kb0)
    colsum = lambda v: jnp.sum(jnp.abs(v), axis=0, keepdims=True)
    inv = 1.0 / (colsum(kf0) + colsum(kf1) + colsum(kb0) + colsum(kb1))
    pairs = ((kf0, kb0), (kf1, drop0(kfr)), (kbr, drop0(kb1)))
    for d, (cp, cm) in enumerate(pairs):
        cols = slice(d * cw, (d + 1) * cw)
        ah_ref[:, cols], al_ref[:, cols] = _split_bf16((cp + cm) * inv)
        sh_ref[:, cols], sl_ref[:, cols] = _split_bf16((cp - cm) * inv)


def _filters(z, t, w1, b1, freq, w2, b2, w3, deltas):
    p = z.shape[0] // 3
    cw = deltas.shape[1]
    fw = w2.shape[0]
    const = lambda shape: pl.BlockSpec(shape, lambda o: (0,) * len(shape))
    out = jax.ShapeDtypeStruct((p, C_ORDER * 3 * cw), BF16)
    oblk = pl.BlockSpec((p, 3 * cw), lambda o: (0, o))
    return pl.pallas_call(
        _filter_kernel,
        out_shape=[out] * 4,
        grid=(C_ORDER,),
        in_specs=[const(z.shape), const(t.shape), const(w1.shape), const((1, fw)), const((1, fw)),
                  const((fw, fw)), const((1, fw)), pl.BlockSpec((fw, C_DIRS * cw), lambda o: (0, o)),
                  const((1, cw))],
        out_specs=[oblk] * 4,
        compiler_params=_params("arbitrary"),
        name="hyena_filter",
    )(z, t, w1, b1, freq, w2, b2, w3, deltas)


def _spectrum_kernel(f_ref, ah_ref, al_ref, sh_ref, sl_ref, kre_ref, kim_ref):
    fh = kre_ref.shape[0]
    fre = f_ref[:fh, :]
    fim = f_ref[fh:, :]
    kre_ref[...] = _dot(fre, ah_ref[...]) + _dot(fre, al_ref[...])
    kim_ref[...] = _dot(fim, sh_ref[...]) + _dot(fim, sl_ref[...])


def _spectrum(f, ah, al, sh, sl, fh):
    p, n = ah.shape
    tn = n // C_ORDER
    taps = pl.BlockSpec((p, tn), lambda o, j: (0, o))
    out = jax.ShapeDtypeStruct((p, n), F32)
    oblk = pl.BlockSpec((fh, tn), lambda o, j: (j, o))
    return pl.pallas_call(
        _spectrum_kernel,
        out_shape=[out, out],
        grid=(C_ORDER, p // fh),
        in_specs=[pl.BlockSpec((2 * fh, p), lambda o, j: (j, 0)), taps, taps, taps, taps],
        out_specs=[oblk, oblk],
        compiler_params=_params("arbitrary", "arbitrary"),
        name="hyena_spectrum",
    )(f, ah, al, sh, sl)


def _short_conv(u, w_ref, b_ref):
    n = u.shape[0]
    row = lax.broadcasted_iota(jnp.int32, u.shape, 0)
    prev = jnp.where(row == 0, 0.0, pltpu.roll(u, 1, 0))
    nxt = jnp.where(row == n - 1, 0.0, pltpu.roll(u, n - 1, 0))
    return prev * w_ref[0:1, :] + u * w_ref[1:2, :] + nxt * w_ref[2:3, :] + b_ref[...]


def _short_conv_wrap(u, w_ref, b_ref):
    n = u.shape[0]
    return (pltpu.roll(u, 1, 0) * w_ref[0:1, :] + u * w_ref[1:2, :] + pltpu.roll(u, n - 1, 0) * w_ref[2:3, :]
            + b_ref[...])


CONV_CHUNKS = 2


def _longconv_kernel(u_ref, g_ref, cwu_ref, cbu_ref, cwg_ref, cbg_ref, f_ref, gm_ref, kre_ref, kim_ref, d_ref,
                     o_ref, ub_ref, acc_ref, *, conv_u, n_steps):
    j = pl.program_id(1)
    fh = kre_ref.shape[0]
    p, cw = ub_ref.shape[1], d_ref.shape[1]
    cc = cw // CONV_CHUNKS
    l = 2 * p
    edge = 16

    def edge_conv(ref, w_ref, b_ref, cs):
        head = _short_conv(ref[0:2 * edge, cs].astype(F32), w_ref.at[:, cs], b_ref.at[:, cs])[:edge]
        tail = _short_conv(ref[l - 2 * edge:l, cs].astype(F32), w_ref.at[:, cs], b_ref.at[:, cs])[edge:]
        return head, tail

    def load_u(c, cs):
        ub = ub_ref.at[c]
        if conv_u:
            u = _short_conv_wrap(u_ref[:, cs].astype(F32), cwu_ref.at[:, cs], cbu_ref.at[:, cs]).astype(BF16)
        else:
            u = u_ref[:, cs]
        ub[:, :cc] = u[:p]
        ub[:, cc:] = u[p:]
        if conv_u:
            head, tail = edge_conv(u_ref, cwu_ref, cbu_ref, cs)
            ub[0:edge, :cc] = head.astype(BF16)
            ub[p - edge:p, cc:] = tail.astype(BF16)

    def spectral(c, cs):
        spec = _dot(f_ref[...], ub_ref[c])
        u0r, u1r, u0i, u1i = spec[:fh, :cc], spec[:fh, cc:], spec[fh:, :cc], spec[fh:, cc:]
        tap = lambda ref, d: ref[:, d * cw + c * cc:d * cw + (c + 1) * cc]
        c0r, c1r, cmr = (tap(kre_ref, d) for d in range(3))
        c0i, c1i, cmi = (tap(kim_ref, d) for d in range(3))
        y0r = (c0r * u0r - c0i * u0i) + (cmr * u1r - cmi * u1i)
        y0i = (c0r * u0i + c0i * u0r) + (cmr * u1i + cmi * u1r)
        y1r = (c1r * u0r - c1i * u0i) + (c0r * u1r - c0i * u1i)
        y1i = (c1r * u0i + c1i * u0r) + (c0r * u1i + c0i * u1r)
        y = jnp.concatenate([jnp.concatenate([y0r, y1r], axis=1), jnp.concatenate([y0i, y1i], axis=1)], axis=0)
        return _dot(gm_ref[...], y.astype(BF16))

    def gated_out(c, cs, conv):
        gate = _short_conv_wrap(g_ref[:, cs].astype(F32), cwg_ref.at[:, cs], cbg_ref.at[:, cs])
        ub = ub_ref.at[c]
        y = jnp.concatenate([conv[:, :cc], conv[:, cc:]], axis=0)
        y = y + jnp.concatenate([ub[:, :cc], ub[:, cc:]], axis=0).astype(F32) * d_ref[:, cs]
        o_ref[:, cs] = (gate * y).astype(o_ref.dtype)
        head, tail = edge_conv(g_ref, cwg_ref, cbg_ref, cs)
        o_ref[0:edge, cs] = (head * y[:edge]).astype(o_ref.dtype)
        o_ref[l - edge:l, cs] = (tail * y[l - edge:]).astype(o_ref.dtype)

    def step(first, last):
        for c in range(CONV_CHUNKS):
            cs = slice(c * cc, (c + 1) * cc)
            if first:
                load_u(c, cs)
            conv = spectral(c, cs)
            if not first:
                conv = acc_ref[c] + conv
            if last:
                gated_out(c, cs, conv)
            else:
                acc_ref[c] = conv

    if n_steps == 1:
        step(True, True)
    else:
        pl.when(j == 0)(lambda: step(True, False))
        pl.when(j == n_steps - 1)(lambda: step(False, True))
        if n_steps > 2:
            pl.when(jnp.logical_and(j > 0, j < n_steps - 1))(lambda: step(False, False))


def _longconv(u_arr, u_blk, g_arr, g_blk, conv_w, conv_b, f, gm, kre, kim, order, d, *, conv_u, fh):
    bn, l, _ = u_arr.shape
    p = l // 2
    cw = d.shape[1]
    ub = u_blk if conv_u else 0
    tok = lambda blk: pl.BlockSpec((None, l, cw), lambda b, j: (b, 0, blk))
    cpar = lambda rows, blk: pl.BlockSpec((rows, cw), lambda b, j: (0, blk))
    ktab = pl.BlockSpec((fh, 3 * cw), lambda b, j: (j, order))
    return pl.pallas_call(
        functools.partial(_longconv_kernel, conv_u=conv_u, n_steps=p // fh),
        out_shape=jax.ShapeDtypeStruct((bn, l, cw), BF16),
        grid=(bn, p // fh),
        in_specs=[tok(u_blk), tok(g_blk), cpar(3, ub), cpar(1, ub), cpar(3, g_blk), cpar(1, g_blk),
                  pl.BlockSpec((2 * fh, p), lambda b, j: (j, 0)),
                  pl.BlockSpec((p, 2 * fh), lambda b, j: (0, j)),
                  ktab, ktab, pl.BlockSpec((1, cw), lambda b, j: (0, 0))],
        out_specs=tok(0),
        scratch_shapes=[pltpu.VMEM((CONV_CHUNKS, p, 2 * cw // CONV_CHUNKS), BF16),
                        pltpu.VMEM((CONV_CHUNKS, p, 2 * cw // CONV_CHUNKS), F32)],
        compiler_params=_params("parallel", "arbitrary"),
        name="hyena_longconv",
    )(u_arr, g_arr, conv_w, conv_b, conv_w, conv_b, f, gm, kre, kim, d)


def _hyena_consts(l, cw):
    p = l // 2
    e = np.arange(p)
    pos = np.concatenate([e, p + e, p - e])
    t = np.linspace(0.0, 1.0, l, dtype=np.float32)[pos][:, None]
    w = (2.0 * math.pi * np.arange(l, dtype=np.float32) / l).astype(np.float32)[pos][:, None]
    fq = np.linspace(1e-4, C_BANDS - 1, C_BANDS, dtype=np.float32)[None, :]
    wf = jnp.asarray(w) * jnp.asarray(fq)
    z = jnp.concatenate([jnp.asarray(t), jnp.cos(wf), -jnp.sin(wf)], axis=-1)
    z = jnp.pad(z, ((0, 0), (0, C_FILTER_WIDTH - z.shape[1])))
    deltas = np.abs(np.linspace(HYENA_MIN_DECAY, HYENA_MAX_DECAY, cw, dtype=np.float32))[None, :]
    return z, jnp.asarray(t), jnp.asarray(deltas)


def _hyena(hy, hconst, f, gm, fh, conv_w, conv_b, w1p, b1, freq, w2, b2, w3, dbias):
    z, t, deltas = hconst
    ah, al, sh, sl = _filters(z, t, w1p, b1, freq, w2, b2, w3, deltas)
    kre, kim = _spectrum(f, ah, al, sh, sl, fh)
    zz = _longconv(hy, 0, hy, 1, conv_w, conv_b, f, gm, kre, kim, 0, dbias[0:1], conv_u=True, fh=fh)
    return _longconv(zz, 0, hy, 2, conv_w, conv_b, f, gm, kre, kim, 1, dbias[1:2], conv_u=False, fh=fh)


def _mix_ffn_kernel(x_ref, a_ref, b_ref, c_ref, wo_ref, g1_ref, gn_ref, sh_ref, sc_ref, g2_ref,
                    wg_ref, wu_ref, wd_ref, o_ref, *, fc):
    wa, wb = a_ref.shape[1], b_ref.shape[1]
    mix = (_dot(a_ref[...], wo_ref[:wa, :]) + _dot(b_ref[...], wo_ref[wa:wa + wb, :])
           + _dot(c_ref[...], wo_ref[wa + wb:, :]))
    x1 = x_ref[...] + g1_ref[...] * mix
    xb = _rms_mod(x1, gn_ref[...], sh_ref[...], sc_ref[...]).astype(BF16)
    acc = None
    for c0 in range(0, wg_ref.shape[1], fc):
        hg = _dot(xb, wg_ref[:, c0:c0 + fc])
        hu = _dot(xb, wu_ref[:, c0:c0 + fc])
        act = ((hg * jax.nn.sigmoid(hg)) * hu).astype(BF16)
        part = _dot(act, wd_ref[c0:c0 + fc, :])
        acc = part if acc is None else acc + part
    o_ref[...] = x1 + g2_ref[...] * acc


def _mix_ffn(x, a, b, c, wo, g1, gn, sh2, sc2, g2, wg, wu, wd, layer):
    bn, l, d = x.shape
    tm = min(TOKEN_TILE, l)
    row = pl.BlockSpec((None, 1, d), lambda bb, t: (bb, 0, 0))
    tok = lambda width: pl.BlockSpec((None, tm, width), lambda bb, t: (bb, t, 0))
    return pl.pallas_call(
        functools.partial(_mix_ffn_kernel, fc=256),
        out_shape=jax.ShapeDtypeStruct((bn, l, d), F32),
        grid=(bn, l // tm),
        in_specs=[tok(d), tok(a.shape[2]), tok(b.shape[2]), tok(c.shape[2]), _layer_of(wo, layer),
                  row, pl.BlockSpec((1, d), lambda bb, t: (0, 0)), row, row, row,
                  _layer_of(wg, layer), _layer_of(wu, layer), _layer_of(wd, layer)],
        out_specs=tok(d),
        compiler_params=_params("parallel", "parallel"),
        name="mix_ffn",
    )(x, a, b, c, wo, g1, gn, sh2, sc2, g2, wg, wu, wd)


def _rope_tables(l):
    half = HEAD_DIM // 2
    nfreq = half // 2
    inv = ROPE_BASE ** (-jnp.arange(nfreq, dtype=F32) / nfreq)
    pos = jnp.arange(l)
    rows, cols = pos // GRID_W, pos % GRID_W
    ang = jnp.concatenate([rows.astype(F32)[:, None] * inv[None, :]] * 2
                          + [cols.astype(F32)[:, None] * inv[None, :]] * 2, axis=-1)
    first = (np.arange(HEAD_DIM) % half) < nfreq
    cos, sin = jnp.cos(ang), jnp.sin(ang)
    sin_a = jnp.where(first[None, :], -sin, 0.0)
    sin_b = jnp.where(first[None, :], 0.0, sin)
    reps = LANES // HEAD_DIM
    return tuple(jnp.tile(tb, (1, reps)) for tb in (cos, sin_a, sin_b))


def kernel(x, c, ctx, c_ctx, ada_w, ada_b, norm1_g, norm2_g, w_in, qnorm_a, knorm_a, sink_a, qnorm_b, knorm_b,
           rpb_b, conv_w, conv_b, filt_w1, filt_b1, filt_freq, filt_w2, filt_b2, filt_w3, hyena_bias, w_out,
           ffn_w_gate, ffn_w_up, ffn_w_down):
    bn, l, d = x.shape
    lc = ctx.shape[1]
    depth = ada_w.shape[0]
    cw = hyena_bias.shape[2]
    assert l % (A_STEP_BLOCKS * A_BLOCK) == 0 and l >= 3 * A_SPAN and lc % 256 == 0
    assert l % (B_STEP_GROUPS * B_GROUP * GRID_W) == 0 and l // GRID_W >= 3 * B_GROUP

    mod_rows = 8 * (-(-(bn + 1) // 8))
    c_all = jnp.zeros((mod_rows, d), F32).at[:bn].set(c).at[bn].set(c_ctx)
    mod = _modulation(c_all, ada_w, ada_b)

    a_order = (0, 2, 1, 3)
    a_kv = tuple(hh // (A_HEADS // A_KV_HEADS) for hh in a_order)
    heads_a = lambda t, axis: [lax.slice_in_dim(t, hh * HEAD_DIM, (hh + 1) * HEAD_DIM, axis=axis) for hh in a_order]
    qa_w = A_HEADS * HEAD_DIM
    w_in_b = w_in.astype(BF16)
    w_out_b = jnp.concatenate(heads_a(w_out, 1) + [w_out[:, qa_w:]], axis=1).astype(BF16)
    sink_p = LOG2E * jnp.stack([sink_a[:, hh] for hh in a_order], axis=1)
    wg_b, wu_b, wd_b = ffn_w_gate.astype(BF16), ffn_w_up.astype(BF16), ffn_w_down.astype(BF16)

    lane = np.arange(256)
    e_heads = jnp.asarray((lane[:, None] // HEAD_DIM) == (lane[None, :] // HEAD_DIM), BF16)
    tile4 = lambda g: jnp.tile(g.reshape(1, HEAD_DIM), (1, 256 // HEAD_DIM))
    rope_lat = _rope_tables(l)

    fh_lat, fh_ctx = min(FREQ_TILE, l // 2), min(FREQ_TILE, lc // 2)
    f_lat, g_lat = _dft_matrices(l // 2, fh_lat)
    f_ctx, g_ctx = _dft_matrices(lc // 2, fh_ctx)
    hc_lat = _hyena_consts(l, cw)
    hc_ctx = _hyena_consts(lc, cw)

    xc = ctx
    for i in range(depth):
        last = i == depth - 1
        m = mod[i]
        part = lambda rows, k: rows[:, None, k * d:(k + 1) * d]
        lat = [part(m[:bn], k) for k in range(6)]
        cx = [part(m[bn:bn + 1], k) for k in range(6)]
        g1n = norm1_g[i].reshape(1, d)
        g2n = norm2_g[i].reshape(1, d)
        gqa, gka, gqb, gkb = tile4(qnorm_a[i]), tile4(knorm_a[i]), tile4(qnorm_b[i]), tile4(knorm_b[i])

        qa, ka, va, qb, kb, vb, hy = _proj_in(x, lat[0], lat[1], g1n, w_in_b, i, gqa, gka, gqb, gkb, e_heads,
                                              rope_lat)
        ctx_out = _proj_in(xc.reshape(1, bn * lc, d), cx[0], cx[1], g1n, w_in_b, i, gqa, gka, gqb, gkb, e_heads,
                           None)
        qa_c, ka_c, va_c, qb_c, kb_c, vb_c, hy_c = [t.reshape(bn, lc, -1) for t in ctx_out]
        w1p = jnp.pad(filt_w1[i], ((0, C_FILTER_WIDTH - filt_w1.shape[1]), (0, 0)))
        hy_args = (conv_w[i], conv_b[i].reshape(1, -1), w1p, filt_b1[i].reshape(1, -1),
                   filt_freq[i].reshape(1, -1), filt_w2[i], filt_b2[i].reshape(1, -1), filt_w3[i], hyena_bias[i])

        sink_row = jnp.repeat(sink_p[i], A_BLOCK).reshape(1, A_HEADS * A_BLOCK)
        out_a = _attn_a(qa, ka, va, ka_c, va_c, sink_row)
        out_b = _attn_b(qb, kb, vb, kb_c, vb_c, _rpb_tables(rpb_b[i]))
        out_c = _hyena(hy, hc_lat, f_lat, g_lat, fh_lat, *hy_args)
        x = _mix_ffn(x, out_a, out_b, out_c, w_out_b, lat[2], g2n, lat[3], lat[4], lat[5], wg_b, wu_b, wd_b, i)

        if not last:
            oa_c = _dense_attn(qa_c, ka_c, va_c, sink_p[i], kv_of=a_kv, use_sink=True)
            ob_c = _dense_attn(qb_c, kb_c, vb_c, sink_p[i], kv_of=tuple(range(B_HEADS)), use_sink=False)
            oc_c = _hyena(hy_c, hc_ctx, f_ctx, g_ctx, fh_ctx, *hy_args)
            flat = lambda t: t.reshape(1, bn * lc, -1)
            xc = _mix_ffn(flat(xc), flat(oa_c), flat(ob_c), flat(oc_c), w_out_b, cx[2], g2n, cx[3], cx[4], cx[5],
                          wg_b, wu_b, wd_b, i).reshape(bn, lc, d)
    return x
```

```python
import functools
import math

import jax
import jax.numpy as jnp
import numpy as np
from jax import lax
from jax.experimental import pallas as pl
from jax.experimental.pallas import tpu as pltpu

F32 = jnp.float32
BF16 = jnp.bfloat16

GRID_W = 64
HEAD_DIM = 64
A_HEADS = 4
A_KV_HEADS = 2
A_BLOCK = 128
B_HEADS = 4
NA_ROWS = 8
NA_COLS = 16
C_ORDER = 2
C_DIRS = 2
C_FILTER_WIDTH = 64
C_BANDS = 16
ROPE_BASE = 10000.0
EPS = 1e-6
NEG_INF = -1e30
LOG2E = math.log2(math.e)
HYENA_MIN_DECAY = math.log(1e-2) / 1.5
HYENA_MAX_DECAY = math.log(1e-2) / 0.3

V7X_VMEM_BYTES = 64 * 1024 * 1024
VMEM_LIMIT = V7X_VMEM_BYTES - 8 * 1024 * 1024
LANES = 128
TOKEN_TILE = 512
FREQ_TILE = 512


def _params(*sem):
    return pltpu.CompilerParams(dimension_semantics=sem, vmem_limit_bytes=VMEM_LIMIT)


def _dot(a, b):
    return jnp.dot(a, b, preferred_element_type=F32)


def _dot_nt(a, b):
    return lax.dot_general(a, b, (((1,), (1,)), ((), ())), preferred_element_type=F32)


def _dot_hp(a, b):
    ah, al = _split_bf16(a)
    bh, bl = _split_bf16(b)
    return _dot(ah, bh) + (_dot(ah, bl) + _dot(al, bh))


def _split_bf16(v):
    hi = v.astype(BF16)
    lo = (v - hi.astype(F32)).astype(BF16)
    return hi, lo


def _layer_of(stack, layer):
    return pl.BlockSpec((None,) + stack.shape[1:], lambda *_: (layer, 0, 0), pipeline_mode=pl.Buffered(1))


def _resident(shape):
    nd = len(shape)
    return pl.BlockSpec(shape, lambda *_: (0,) * nd, pipeline_mode=pl.Buffered(1))


def _mod_kernel(c_ref, w_ref, b_ref, o_ref):
    cv = c_ref[...]
    sc = (cv * jax.nn.sigmoid(cv)).astype(BF16)
    o_ref[...] = _dot(sc, w_ref[...].astype(BF16)) + b_ref[...]


def _modulation(c_all, ada_w, ada_b):
    depth, d, n = ada_w.shape
    rows = c_all.shape[0]
    tn = 1536
    return pl.pallas_call(
        _mod_kernel,
        out_shape=jax.ShapeDtypeStruct((depth, rows, n), F32),
        grid=(depth, n // tn),
        in_specs=[
            pl.BlockSpec((rows, d), lambda i, j: (0, 0)),
            pl.BlockSpec((None, d, tn), lambda i, j: (i, 0, j)),
            pl.BlockSpec((None, 1, tn), lambda i, j: (i, 0, j)),
        ],
        out_specs=pl.BlockSpec((None, rows, tn), lambda i, j: (i, 0, j)),
        compiler_params=_params("arbitrary", "arbitrary"),
        name="modulation",
    )(c_all, ada_w, ada_b.reshape(depth, 1, n))


def _rms_mod(x, g, shift, scale):
    y = x * lax.rsqrt(jnp.mean(x * x, axis=-1, keepdims=True) + EPS)
    return (y * g) * (1 + scale) + shift


def _head_norm(h, g, e):
    ss = _dot((h * h).astype(BF16), e)
    return (h * lax.rsqrt(ss * (1.0 / HEAD_DIM) + EPS)) * g


def _rope(t, cos, sin_a, sin_b):
    outs = []
    for c in range(t.shape[1] // LANES):
        tc = t[:, c * LANES:(c + 1) * LANES]
        outs.append(tc * cos + pltpu.roll(tc, LANES - 16, 1) * sin_a + pltpu.roll(tc, 16, 1) * sin_b)
    return outs[0] if len(outs) == 1 else jnp.concatenate(outs, axis=1)


PROJ_ROWS = 1024
PROJ_SPLIT = 2


def _proj_in_kernel(x_ref, shift_ref, scale_ref, g_ref, w_ref, gqa_ref, gka_ref, gqb_ref, gkb_ref, e_ref, *rest,
                    rope):
    if rope:
        cos_ref, sa_ref, sb_ref = rest[:3]
        rest = rest[3:]
    qa_ref, ka_ref, va_ref, qb_ref, kb_ref, vb_ref, hy_ref = rest
    qscale = LOG2E * HEAD_DIM ** -0.5
    hm = x_ref.shape[0] // PROJ_SPLIT
    for part in range(PROJ_SPLIT):
        rows = slice(part * hm, (part + 1) * hm)
        xb = _rms_mod(x_ref[rows, :], g_ref[...], shift_ref[...], scale_ref[...]).astype(BF16)

        h_all = _dot(xb, w_ref[...])

        def proj(lo, hi):
            return h_all[:, lo:hi]

        def maybe_rope(t):
            return _rope(t, cos_ref[rows, :], sa_ref[rows, :], sb_ref[rows, :]) if rope else t

        def store_v(ref, v):
            if rope:
                for i in range(hm // LANES):
                    ref[part * (hm // LANES) + i] = v[i * LANES:(i + 1) * LANES, :].T.astype(BF16)
            else:
                ref[rows, :] = v.astype(BF16)

        qa = maybe_rope(_head_norm(proj(0, 256), gqa_ref[...], e_ref[...])) * qscale
        t0, t1 = qa[:, :LANES], qa[:, LANES:]
        lo = lax.broadcasted_iota(jnp.int32, t0.shape, 1) < HEAD_DIM
        qa_ref[rows, :LANES] = jnp.where(lo, t0, pltpu.roll(t1, HEAD_DIM, 1)).astype(BF16)
        qa_ref[rows, LANES:] = jnp.where(lo, pltpu.roll(t0, HEAD_DIM, 1), t1).astype(BF16)
        kva = proj(256, 512)
        ka = maybe_rope(_head_norm(kva[:, :LANES], gka_ref[:, :128], e_ref[:128, :128]))
        ka_ref[rows, :] = ka.astype(BF16)
        store_v(va_ref, kva[:, LANES:])
        qb = _head_norm(proj(512, 768), gqb_ref[...], e_ref[...])
        qb_ref[rows, :] = (qb * qscale).astype(BF16)
        kb_ref[rows, :] = _head_norm(proj(768, 1024), gkb_ref[...], e_ref[...]).astype(BF16)
        store_v(vb_ref, proj(1024, 1280))
        hy_ref[rows, :] = proj(1280, 2816).astype(BF16)


def _proj_in(x, shift, scale, g, w, layer, gqa, gka, gqb, gkb, e, rope_tabs):
    bn, l, d = x.shape
    n = w.shape[2]
    rope = rope_tabs is not None
    tm = min(PROJ_ROWS, l)
    row = lambda width: pl.BlockSpec((None, 1, width), lambda b, t: (b, 0, 0))
    const = lambda shape: pl.BlockSpec(shape, lambda b, t: (0,) * len(shape))
    tok = lambda width: pl.BlockSpec((None, tm, width), lambda b, t: (b, t, 0))
    tab = pl.BlockSpec((tm, LANES), lambda b, t: (t, 0))
    widths = (256, 128, 128, 256, 256, 256, n - 1280)
    shapes = [(bn, l, wd) for wd in widths]
    specs = [tok(wd) for wd in widths]
    if rope:
        for i in (2, 5):
            shapes[i] = (bn, l // LANES, widths[i], LANES)
            specs[i] = pl.BlockSpec((None, tm // LANES, widths[i], LANES), lambda b, t: (b, t, 0, 0))
    return pl.pallas_call(
        functools.partial(_proj_in_kernel, rope=rope),
        out_shape=[jax.ShapeDtypeStruct(sh, BF16) for sh in shapes],
        grid=(bn, l // tm),
        in_specs=[tok(d), row(d), row(d), const((1, d)), _layer_of(w, layer),
                  const((1, 256)), const((1, 256)), const((1, 256)), const((1, 256)), const((256, 256))]
                 + ([tab, tab, tab] if rope else []),
        out_specs=specs,
        compiler_params=_params("parallel", "parallel"),
        name="proj_in_rope" if rope else "proj_in",
    )(x, shift, scale, g, w, gqa, gka, gqb, gkb, e, *(rope_tabs or ()))


def _softmax_pv(s_list, v_list, sink):
    m = s_list[0].max(axis=-1, keepdims=True)
    for s in s_list[1:]:
        m = jnp.maximum(m, s.max(axis=-1, keepdims=True))
    if sink is not None:
        m = jnp.maximum(m, sink)
    den = None
    out = None
    for s, v in zip(s_list, v_list):
        p = jnp.exp2(s - m)
        ps = p.sum(axis=-1, keepdims=True)
        den = ps if den is None else den + ps
        o = _dot(p.astype(BF16), v)
        out = o if out is None else out + o
    if sink is not None:
        den = den + jnp.exp2(sink - m)
    return out * (1.0 / den)


def _stack_heads(q):
    lo = lax.broadcasted_iota(jnp.int32, q.shape, 1) < HEAD_DIM
    zero = jnp.zeros_like(q)
    return jnp.concatenate([jnp.where(lo, q, zero), jnp.where(lo, zero, q)], axis=0)


def _unstack_heads(o):
    m = o.shape[0] // 2
    lo = lax.broadcasted_iota(jnp.int32, (m, LANES), 1) < HEAD_DIM
    return jnp.where(lo, o[:m], o[m:])


SOFTMAX_CHUNK = 32


def _softmax_keys(s_ref, p_ref, nloc, add_loc, sink):
    nk = s_ref.shape[0]
    ch = SOFTMAX_CHUNK
    macc = None
    for r0 in range(0, nk, ch):
        s = s_ref[r0:r0 + ch, :]
        if r0 < nloc:
            s = s + add_loc(r0)
            s_ref[r0:r0 + ch, :] = s
        macc = s if macc is None else jnp.maximum(macc, s)
    m = macc.max(axis=0, keepdims=True)
    if sink is not None:
        m = jnp.maximum(m, sink)
    sacc = None
    for r0 in range(0, nk, ch):
        p = jnp.exp2(s_ref[r0:r0 + ch, :] - m)
        sacc = p if sacc is None else sacc + p
        p_ref[r0:r0 + ch, :] = p.astype(BF16)
    den = sacc.sum(axis=0, keepdims=True)
    if sink is not None:
        den = den + jnp.exp2(sink - m)
    return 1.0 / den


A_STEP_BLOCKS = 8
A_SPAN = 3 * A_BLOCK


def _attn_a_kernel(q_ref, k_ref, vt_ref, kx_ref, vxt_ref, mask_ref, sink_ref, o_ref, s_ref, p_ref):
    l = k_ref.shape[0]
    nb = l // A_BLOCK
    hd = HEAD_DIM
    for u in range(A_STEP_BLOCKS):
        n = pl.program_id(1) * A_STEP_BLOCKS + u
        tile0 = jnp.clip(n - 1, 0, nb - 3)
        start = pl.multiple_of(tile0 * A_BLOCK, A_BLOCK)
        pat = jnp.where(n == 0, 0, jnp.where(n == nb - 1, 2, 1))
        q = q_ref[u * A_BLOCK:(u + 1) * A_BLOCK, :]
        qs = jnp.concatenate([_stack_heads(q[:, :LANES]), _stack_heads(q[:, LANES:])], axis=0)
        sb, pb = s_ref.at[u % 2], p_ref.at[u % 2]
        sb[:A_SPAN, :] = _dot_nt(k_ref[pl.ds(start, A_SPAN), :], qs)
        sb[A_SPAN:, :] = _dot_nt(kx_ref[...], qs)
        r = _softmax_keys(sb, pb, A_SPAN, lambda r0: mask_ref[pat, r0:r0 + SOFTMAX_CHUNK, :], sink_ref[...])
        vt = jnp.concatenate([vt_ref[tile0 + i] for i in range(3)], axis=1)
        ot = (_dot(vt, pb[:A_SPAN, :]) + _dot(vxt_ref[...], pb[A_SPAN:, :])) * r
        ot = jnp.concatenate([ot[(i % 2) * hd:(i % 2 + 1) * hd, i * A_BLOCK:(i + 1) * A_BLOCK]
                              for i in range(A_HEADS)], axis=0)
        o_ref[u * A_BLOCK:(u + 1) * A_BLOCK, :] = ot.T.astype(o_ref.dtype)


def _attn_a_mask():
    i = np.arange(A_BLOCK)[None, :]
    j = np.arange(A_SPAN)[:, None]
    offs = (0, A_BLOCK, 2 * A_BLOCK)
    m = np.stack([np.where(np.abs(j - i - o) <= A_BLOCK, 0.0, -np.inf) for o in offs])
    return jnp.asarray(np.tile(m, (1, 1, A_HEADS)), F32)


def _attn_a(q, k, vt, kx, vx, sink_row):
    bn, l, qw = q.shape
    lc = kx.shape[1]
    kvw = k.shape[2]
    nb = l // A_BLOCK
    qs = A_STEP_BLOCKS * A_BLOCK
    mask = _attn_a_mask()
    vxt = jnp.swapaxes(vx, 1, 2)
    seq = pl.BlockSpec((None, l, kvw), lambda b, s: (b, 0, 0))
    qblk = pl.BlockSpec((None, qs, qw), lambda b, s: (b, s, 0))
    nq = A_HEADS * A_BLOCK
    return pl.pallas_call(
        _attn_a_kernel,
        out_shape=jax.ShapeDtypeStruct((bn, l, qw), BF16),
        grid=(bn, l // qs),
        in_specs=[qblk, seq, pl.BlockSpec((None, nb, kvw, A_BLOCK), lambda b, s: (b, 0, 0, 0)),
                  pl.BlockSpec((None, lc, kvw), lambda b, s: (b, 0, 0)),
                  pl.BlockSpec((None, kvw, lc), lambda b, s: (b, 0, 0)),
                  _resident(mask.shape), _resident(sink_row.shape)],
        out_specs=qblk,
        scratch_shapes=[pltpu.VMEM((2, A_SPAN + lc, nq), F32), pltpu.VMEM((2, A_SPAN + lc, nq), BF16)],
        compiler_params=_params("parallel", "arbitrary"),
        name="window_attn",
    )(q, k, vt, kx, vxt, mask, sink_row)


def _dense_attn_kernel(sink_ref, q_ref, k_ref, v_ref, o_ref, *, kv_of, use_sink):
    for h, kv in enumerate(kv_of):
        hs = slice(h * HEAD_DIM, (h + 1) * HEAD_DIM)
        ks = slice(kv * HEAD_DIM, (kv + 1) * HEAD_DIM)
        s = _dot_nt(q_ref[:, hs], k_ref[:, ks])
        o = _softmax_pv([s], [v_ref[:, ks]], sink_ref[h] if use_sink else None)
        o_ref[:, hs] = o.astype(o_ref.dtype)


def _dense_attn(q, k, v, sink, *, kv_of, use_sink):
    bn, l, qw = q.shape
    kvw = k.shape[2]
    full = lambda wd: pl.BlockSpec((None, l, wd), lambda b: (b, 0, 0))
    return pl.pallas_call(
        functools.partial(_dense_attn_kernel, kv_of=kv_of, use_sink=use_sink),
        out_shape=jax.ShapeDtypeStruct((bn, l, qw), BF16),
        grid=(bn,),
        in_specs=[pl.BlockSpec(memory_space=pltpu.SMEM), full(qw), full(kvw), full(kvw)],
        out_specs=full(qw),
        compiler_params=_params("parallel"),
        name="ctx_attn_sink" if use_sink else "ctx_attn",
    )(sink, q, k, v)


def _rpb_kernel(r_ref, oh_ref, ok_ref, o_ref):
    r = r_ref[...]
    b1 = r.astype(BF16)
    r2 = r - b1.astype(F32)
    b2 = r2.astype(BF16)
    b3 = (r2 - b2.astype(F32)).astype(BF16)
    oh = oh_ref[...]
    bias = (_dot(b1, oh) + _dot(b2, oh)) + _dot(b3, oh)
    o_ref[...] = jnp.where(ok_ref[...] > 0.5, bias * LOG2E, -jnp.inf)


def _rpb_slots():
    a = np.arange(B_GROUP)[:, None]
    kr = np.arange(B_SLAB)[None, :]
    dr = np.stack([kr - a + NA_ROWS - 1, kr - a + NA_ROWS // 2 - 1, kr - a + (B_GROUP + NA_ROWS - 1 - B_SLAB)])
    lo = np.stack([0 * a + 0 * kr, a + 0 * kr, 0 * a + (B_SLAB - NA_ROWS) + 0 * kr])
    valid = (kr[None] >= lo) & (kr[None] < lo + NA_ROWS)
    return np.where(valid, dr, -1)


def _rpb_assemble_kernel(t_ref, o_ref, *, n_dr):
    slots = _rpb_slots()
    w = GRID_W
    blank = jnp.full((w, w), -jnp.inf, F32)
    for pat in range(3):
        for t in range(o_ref.shape[1]):
            for kr in range(B_SLAB):
                for hh in range(2):
                    for a in range(B_GROUP):
                        dr = int(slots[pat, a, kr])
                        tile = t_ref[(2 * t + hh) * n_dr + dr] if dr >= 0 else blank
                        c0 = (hh * B_GROUP + a) * w
                        o_ref[pat, t, kr * w:(kr + 1) * w, c0:c0 + w] = tile


def _rpb_tables(rpb):
    h, nr, nc = rpb.shape
    col = np.arange(GRID_W)
    dc = np.clip(col[:, None] - col[None, :], 1 - NA_COLS, NA_COLS - 1) + NA_COLS - 1
    onehot = (dc.reshape(1, -1) == np.arange(nc)[:, None]).astype(np.float32)
    onehot = np.concatenate([onehot, np.zeros((32 - nc, GRID_W * GRID_W), np.float32)], axis=0)
    col_start = np.clip(col - NA_COLS // 2, 0, GRID_W - NA_COLS)
    col_ok = (col[:, None] >= col_start[None, :]) & (col[:, None] < col_start[None, :] + NA_COLS)
    rows = 64
    r2 = jnp.zeros((rows, 32), F32).at[:h * nr, :nc].set(rpb.reshape(h * nr, nc))
    tiles = pl.pallas_call(
        _rpb_kernel,
        out_shape=jax.ShapeDtypeStruct((rows, GRID_W * GRID_W), F32),
        name="rpb_table",
    )(r2, jnp.asarray(onehot, BF16), jnp.asarray(col_ok.reshape(1, -1), F32))
    tiles = tiles.reshape(rows, GRID_W, GRID_W)
    return pl.pallas_call(
        functools.partial(_rpb_assemble_kernel, n_dr=nr),
        out_shape=jax.ShapeDtypeStruct((3, h // 2, B_SLAB * GRID_W, 2 * B_GROUP * GRID_W), F32),
        compiler_params=_params(),
        name="rpb_assemble",
    )(tiles)


B_GROUP = 4
B_STEP_GROUPS = 2
B_SLAB = 12


def _attn_b_kernel(q_ref, k_ref, vt_ref, kx_ref, vxt_ref, tbl_ref, o_ref, s_ref, p_ref):
    ng = pl.num_programs(1) * B_STEP_GROUPS
    rows = k_ref.shape[0] // GRID_W
    nloc = B_SLAB * GRID_W
    gq = B_GROUP * GRID_W
    for u in range(B_STEP_GROUPS):
        g = pl.program_id(1) * B_STEP_GROUPS + u
        base = jnp.clip(g * B_GROUP - NA_ROWS // 2, 0, rows - B_SLAB)
        start = pl.multiple_of(base * GRID_W, LANES)
        tile0 = base // (LANES // GRID_W)
        pat = jnp.where(g == 0, 0, jnp.where(g == ng - 1, 2, 1))
        for t in range(B_HEADS // 2):
            ts = slice(t * LANES, (t + 1) * LANES)
            qs = _stack_heads(q_ref[u * gq:(u + 1) * gq, ts])
            sb, pb = s_ref.at[u, t], p_ref.at[u, t]
            sb[:nloc, :] = _dot_nt(k_ref[pl.ds(start, nloc), ts], qs)
            sb[nloc:, :] = _dot_nt(kx_ref[:, ts], qs)
            r = _softmax_keys(sb, pb, nloc, lambda r0: tbl_ref[pat, t, r0:r0 + SOFTMAX_CHUNK, :], None)
            vt = jnp.concatenate([vt_ref[tile0 + i, ts, :] for i in range(nloc // LANES)], axis=1)
            ot = (_dot(vt, pb[:nloc, :]) + _dot(vxt_ref[ts, :], pb[nloc:, :])) * r
            ot = jnp.concatenate([ot[:HEAD_DIM, :gq], ot[HEAD_DIM:, gq:]], axis=0)
            o_ref[u * gq:(u + 1) * gq, ts] = ot.T.astype(o_ref.dtype)


def _attn_b(q, k, vt, kx, vx, tbl):
    bn, l, w = q.shape
    lc = kx.shape[1]
    gq = B_GROUP * GRID_W
    sq = B_STEP_GROUPS * gq
    nk = B_SLAB * GRID_W + lc
    vxt = jnp.swapaxes(vx, 1, 2)
    seq = pl.BlockSpec((None, l, w), lambda b, g: (b, 0, 0))
    qblk = pl.BlockSpec((None, sq, w), lambda b, g: (b, g, 0))
    return pl.pallas_call(
        _attn_b_kernel,
        out_shape=jax.ShapeDtypeStruct((bn, l, w), BF16),
        grid=(bn, l // sq),
        in_specs=[qblk, seq, pl.BlockSpec((None, l // LANES, w, LANES), lambda b, g: (b, 0, 0, 0)),
                  pl.BlockSpec((None, lc, w), lambda b, g: (b, 0, 0)),
                  pl.BlockSpec((None, w, lc), lambda b, g: (b, 0, 0)),
                  _resident(tbl.shape)],
        out_specs=qblk,
        scratch_shapes=[pltpu.VMEM((B_STEP_GROUPS, B_HEADS // 2, nk, 2 * gq), F32),
                        pltpu.VMEM((B_STEP_GROUPS, B_HEADS // 2, nk, 2 * gq), BF16)],
        compiler_params=_params("parallel", "arbitrary"),
        name="nbr_attn",
    )(q, k, vt, kx, vxt, tbl)


def _dft_matrices(l, fh):
    k = jnp.arange(l, dtype=jnp.int32)[:, None]
    n = jnp.arange(l, dtype=jnp.int32)[None, :]
    ang = (((2 * k + 1) * n) % (4 * l)).astype(F32) * (math.pi / (2 * l))
    fre = jnp.cos(ang).reshape(l // fh, fh, l)
    fim = (-jnp.sin(ang)).reshape(l // fh, fh, l)
    f = jnp.concatenate([fre, fim], axis=1).reshape(2 * l, l)
    g = f.T * (1.0 / l)
    return f.astype(BF16), g.astype(BF16)


def _filter_kernel(z_ref, t_ref, w1_ref, b1_ref, fr_ref, w2_ref, b2_ref, w3_ref, dl_ref,
                   ah_ref, al_ref, sh_ref, sl_ref, hid_ref):
    @pl.when(pl.program_id(0) == 0)
    def _():
        fr = fr_ref[...]
        hid = jnp.sin(fr * (_dot_hp(z_ref[...], w1_ref[...]) + b1_ref[...]))
        hid_ref[...] = jnp.sin(fr * (_dot_hp(hid, w2_ref[...]) + b2_ref[...]))

    cw = dl_ref.shape[1]
    p = t_ref.shape[0] // 3
    hid = hid_ref[...]
    w3 = w3_ref[...]
    none = jnp.zeros_like(w3)
    taps = jnp.concatenate([_dot_hp(hid, jnp.concatenate([w3, none], axis=0)),
                            _dot_hp(hid, jnp.concatenate([none, w3], axis=0))], axis=0)
    decay = jnp.exp(-t_ref[...] * dl_ref[...])
    kf = taps[:, :cw] * decay
    kb = taps[:, cw:] * decay
    kf0, kf1, kfr = kf[:p], kf[p:2 * p], kf[2 * p:]
    kb0, kb1, kbr = kb[:p], kb[p:2 * p], kb[2 * p:]
    first = lax.broadcasted_iota(jnp.int32, (p, cw), 0) == 0
    drop0 = lambda v: jnp.where(first, 0.0, v)
    kb0 = drop0(kb0)
    colsum = lambda v: jnp.sum(jnp.abs(v), axis=0, keepdims=True)
    inv = 1.0 / (colsum(kf0) + colsum(kf1) + colsum(kb0) + colsum(kb1))
    pairs = ((kf0, kb0), (kf1, drop0(kfr)), (kbr, drop0(kb1)))
    for d, (cp, cm) in enumerate(pairs):
        cols = slice(d * cw, (d + 1) * cw)
        ah_ref[:, cols], al_ref[:, cols] = _split_bf16((cp + cm) * inv)
        sh_ref[:, cols], sl_ref[:, cols] = _split_bf16((cp - cm) * inv)


def _filters(z, t, w1, b1, freq, w2, b2, w3, deltas):
    p = t.shape[0] // 3
    cw = deltas.shape[1]
    fw = w2.shape[0]
    const = lambda shape: pl.BlockSpec(shape, lambda o: (0,) * len(shape))
    out = jax.ShapeDtypeStruct((p, C_ORDER * 3 * cw), BF16)
    oblk = pl.BlockSpec((p, 3 * cw), lambda o: (0, o))
    return pl.pallas_call(
        _filter_kernel,
        out_shape=[out] * 4,
        grid=(C_ORDER,),
        in_specs=[const(z.shape), const(t.shape), const(w1.shape), const((1, fw)), const((1, fw)),
                  const((fw, fw)), const((1, fw)), pl.BlockSpec((w3.shape[0], C_DIRS * cw), lambda o: (0, o)),
                  const((1, cw))],
        out_specs=[oblk] * 4,
        scratch_shapes=[pltpu.VMEM(z.shape, F32)],
        compiler_params=_params("arbitrary"),
        name="hyena_filter",
    )(z, t, w1, b1, freq, w2, b2, w3, deltas)


def _spectrum_kernel(f_ref, ah_ref, al_ref, sh_ref, sl_ref, kre_ref, kim_ref):
    fh = kre_ref.shape[0]
    fre = f_ref[:fh, :]
    fim = f_ref[fh:, :]
    kre_ref[...] = _dot(fre, ah_ref[...]) + _dot(fre, al_ref[...])
    kim_ref[...] = _dot(fim, sh_ref[...]) + _dot(fim, sl_ref[...])


def _spectrum(f, ah, al, sh, sl, fh):
    p, n = ah.shape
    tn = n // C_ORDER
    taps = pl.BlockSpec((p, tn), lambda o, j: (0, o))
    out = jax.ShapeDtypeStruct((p, n), F32)
    oblk = pl.BlockSpec((fh, tn), lambda o, j: (j, o))
    return pl.pallas_call(
        _spectrum_kernel,
        out_shape=[out, out],
        grid=(C_ORDER, p // fh),
        in_specs=[pl.BlockSpec((2 * fh, p), lambda o, j: (j, 0)), taps, taps, taps, taps],
        out_specs=[oblk, oblk],
        compiler_params=_params("arbitrary", "arbitrary"),
        name="hyena_spectrum",
    )(f, ah, al, sh, sl)


def _short_conv(u, w_ref, b_ref):
    n = u.shape[0]
    row = lax.broadcasted_iota(jnp.int32, u.shape, 0)
    prev = jnp.where(row == 0, 0.0, pltpu.roll(u, 1, 0))
    nxt = jnp.where(row == n - 1, 0.0, pltpu.roll(u, n - 1, 0))
    return prev * w_ref[0:1, :] + u * w_ref[1:2, :] + nxt * w_ref[2:3, :] + b_ref[...]


def _short_conv_wrap(u, w_ref, b_ref):
    n = u.shape[0]
    return (pltpu.roll(u, 1, 0) * w_ref[0:1, :] + u * w_ref[1:2, :] + pltpu.roll(u, n - 1, 0) * w_ref[2:3, :]
            + b_ref[...])


CONV_CHUNKS = 2


def _longconv_kernel(u_ref, g_ref, cwu_ref, cbu_ref, cwg_ref, cbg_ref, f_ref, gm_ref, kre_ref, kim_ref, d_ref,
                     o_ref, ub_ref, acc_ref, *, conv_u, n_steps):
    j = pl.program_id(1)
    fh = kre_ref.shape[0]
    p, cw = ub_ref.shape[1], d_ref.shape[1]
    cc = cw // CONV_CHUNKS
    l = 2 * p
    edge = 16

    def edge_conv(ref, w_ref, b_ref, cs):
        head = _short_conv(ref[0:2 * edge, cs].astype(F32), w_ref.at[:, cs], b_ref.at[:, cs])[:edge]
        tail = _short_conv(ref[l - 2 * edge:l, cs].astype(F32), w_ref.at[:, cs], b_ref.at[:, cs])[edge:]
        return head, tail

    def load_u(c, cs):
        ub = ub_ref.at[c]
        if conv_u:
            u = _short_conv_wrap(u_ref[:, cs].astype(F32), cwu_ref.at[:, cs], cbu_ref.at[:, cs]).astype(BF16)
        else:
            u = u_ref[:, cs]
        ub[:, :cc] = u[:p]
        ub[:, cc:] = u[p:]
        if conv_u:
            head, tail = edge_conv(u_ref, cwu_ref, cbu_ref, cs)
            ub[0:edge, :cc] = head.astype(BF16)
            ub[p - edge:p, cc:] = tail.astype(BF16)

    def spectral(c, cs):
        spec = _dot(f_ref[...], ub_ref[c])
        u0r, u1r, u0i, u1i = spec[:fh, :cc], spec[:fh, cc:], spec[fh:, :cc], spec[fh:, cc:]
        tap = lambda ref, d: ref[:, d * cw + c * cc:d * cw + (c + 1) * cc]
        c0r, c1r, cmr = (tap(kre_ref, d) for d in range(3))
        c0i, c1i, cmi = (tap(kim_ref, d) for d in range(3))
        y0r = (c0r * u0r - c0i * u0i) + (cmr * u1r - cmi * u1i)
        y0i = (c0r * u0i + c0i * u0r) + (cmr * u1i + cmi * u1r)
        y1r = (c1r * u0r - c1i * u0i) + (c0r * u1r - c0i * u1i)
        y1i = (c1r * u0i + c1i * u0r) + (c0r * u1i + c0i * u1r)
        y = jnp.concatenate([jnp.concatenate([y0r, y1r], axis=1), jnp.concatenate([y0i, y1i], axis=1)], axis=0)
        return _dot(gm_ref[...], y.astype(BF16))

    def gated_out(c, cs, conv):
        gate = _short_conv_wrap(g_ref[:, cs].astype(F32), cwg_ref.at[:, cs], cbg_ref.at[:, cs])
        ub = ub_ref.at[c]
        y = jnp.concatenate([conv[:, :cc], conv[:, cc:]], axis=0)
        y = y + jnp.concatenate([ub[:, :cc], ub[:, cc:]], axis=0).astype(F32) * d_ref[:, cs]
        o_ref[:, cs] = (gate * y).astype(o_ref.dtype)
        head, tail = edge_conv(g_ref, cwg_ref, cbg_ref, cs)
        o_ref[0:edge, cs] = (head * y[:edge]).astype(o_ref.dtype)
        o_ref[l - edge:l, cs] = (tail * y[l - edge:]).astype(o_ref.dtype)

    def step(first, last):
        for c in range(CONV_CHUNKS):
            cs = slice(c * cc, (c + 1) * cc)
            if first:
                load_u(c, cs)
            conv = spectral(c, cs)
            if not first:
                conv = acc_ref[c] + conv
            if last:
                gated_out(c, cs, conv)
            else:
                acc_ref[c] = conv

    if n_steps == 1:
        step(True, True)
    else:
        pl.when(j == 0)(lambda: step(True, False))
        pl.when(j == n_steps - 1)(lambda: step(False, True))
        if n_steps > 2:
            pl.when(jnp.logical_and(j > 0, j < n_steps - 1))(lambda: step(False, False))


def _longconv(u_arr, u_blk, g_arr, g_blk, conv_w, conv_b, f, gm, kre, kim, order, d, *, conv_u, fh):
    bn, l, _ = u_arr.shape
    p = l // 2
    cw = d.shape[1]
    ub = u_blk if conv_u else 0
    tok = lambda blk: pl.BlockSpec((None, l, cw), lambda b, j: (b, 0, blk))
    cpar = lambda rows, blk: pl.BlockSpec((rows, cw), lambda b, j: (0, blk))
    ktab = pl.BlockSpec((fh, 3 * cw), lambda b, j: (j, order))
    return pl.pallas_call(
        functools.partial(_longconv_kernel, conv_u=conv_u, n_steps=p // fh),
        out_shape=jax.ShapeDtypeStruct((bn, l, cw), BF16),
        grid=(bn, p // fh),
        in_specs=[tok(u_blk), tok(g_blk), cpar(3, ub), cpar(1, ub), cpar(3, g_blk), cpar(1, g_blk),
                  pl.BlockSpec((2 * fh, p), lambda b, j: (j, 0)),
                  pl.BlockSpec((p, 2 * fh), lambda b, j: (0, j)),
                  ktab, ktab, pl.BlockSpec((1, cw), lambda b, j: (0, 0))],
        out_specs=tok(0),
        scratch_shapes=[pltpu.VMEM((CONV_CHUNKS, p, 2 * cw // CONV_CHUNKS), BF16),
                        pltpu.VMEM((CONV_CHUNKS, p, 2 * cw // CONV_CHUNKS), F32)],
        compiler_params=_params("parallel", "arbitrary"),
        name="hyena_longconv",
    )(u_arr, g_arr, conv_w, conv_b, conv_w, conv_b, f, gm, kre, kim, d)


def _hyena_consts(l, cw):
    p = l // 2
    e = np.arange(p)
    pos = np.concatenate([e, p + e, p - e])
    t = np.linspace(0.0, 1.0, l, dtype=np.float32)[pos][:, None]
    w = (2.0 * math.pi * np.arange(l, dtype=np.float32) / l).astype(np.float32)[pos][:, None]
    fq = np.linspace(1e-4, C_BANDS - 1, C_BANDS, dtype=np.float32)[None, :]
    wf = jnp.asarray(w) * jnp.asarray(fq)
    z = jnp.concatenate([jnp.asarray(t), jnp.cos(wf), -jnp.sin(wf)], axis=-1)
    z = jnp.pad(z, ((0, 0), (0, C_FILTER_WIDTH - z.shape[1])))
    z = jnp.concatenate([z[:3 * p // 2], z[3 * p // 2:]], axis=1)
    deltas = np.abs(np.linspace(HYENA_MIN_DECAY, HYENA_MAX_DECAY, cw, dtype=np.float32))[None, :]
    return z, jnp.asarray(t), jnp.asarray(deltas)


def _pack_filter_mlp(w1, b1, freq, w2, b2):
    fw = w2.shape[0]
    w1p = jnp.pad(w1, ((0, fw - w1.shape[0]), (0, 0)))
    zero = jnp.zeros((fw, fw), F32)
    diag = lambda w: jnp.concatenate([jnp.concatenate([w, zero], axis=1), jnp.concatenate([zero, w], axis=1)], axis=0)
    twice = lambda v: jnp.tile(v.reshape(1, fw), (1, 2))
    return diag(w1p), twice(b1), twice(freq), diag(w2), twice(b2)


def _hyena(hy, hconst, f, gm, fh, conv_w, conv_b, w1p, b1, freq, w2, b2, w3, dbias):
    z, t, deltas = hconst
    ah, al, sh, sl = _filters(z, t, w1p, b1, freq, w2, b2, w3, deltas)
    kre, kim = _spectrum(f, ah, al, sh, sl, fh)
    zz = _longconv(hy, 0, hy, 1, conv_w, conv_b, f, gm, kre, kim, 0, dbias[0:1], conv_u=True, fh=fh)
    return _longconv(zz, 0, hy, 2, conv_w, conv_b, f, gm, kre, kim, 1, dbias[1:2], conv_u=False, fh=fh)


def _mix_ffn_kernel(x_ref, a_ref, b_ref, c_ref, wo_ref, g1_ref, gn_ref, sh_ref, sc_ref, g2_ref,
                    wg_ref, wu_ref, wd_ref, o_ref, *, fc):
    wa, wb = a_ref.shape[1], b_ref.shape[1]
    mix = (_dot(a_ref[...], wo_ref[:wa, :]) + _dot(b_ref[...], wo_ref[wa:wa + wb, :])
           + _dot(c_ref[...], wo_ref[wa + wb:, :]))
    x1 = x_ref[...] + g1_ref[...] * mix
    xb = _rms_mod(x1, gn_ref[...], sh_ref[...], sc_ref[...]).astype(BF16)
    acc = None
    for c0 in range(0, wg_ref.shape[1], fc):
        hg = _dot(xb, wg_ref[:, c0:c0 + fc])
        hu = _dot(xb, wu_ref[:, c0:c0 + fc])
        act = ((hg * jax.nn.sigmoid(hg)) * hu).astype(BF16)
        part = _dot(act, wd_ref[c0:c0 + fc, :])
        acc = part if acc is None else acc + part
    o_ref[...] = x1 + g2_ref[...] * acc


def _mix_ffn(x, a, b, c, wo, g1, gn, sh2, sc2, g2, wg, wu, wd, layer):
    bn, l, d = x.shape
    tm = min(TOKEN_TILE, l)
    row = pl.BlockSpec((None, 1, d), lambda bb, t: (bb, 0, 0))
    tok = lambda width: pl.BlockSpec((None, tm, width), lambda bb, t: (bb, t, 0))
    return pl.pallas_call(
        functools.partial(_mix_ffn_kernel, fc=256),
        out_shape=jax.ShapeDtypeStruct((bn, l, d), F32),
        grid=(bn, l // tm),
        in_specs=[tok(d), tok(a.shape[2]), tok(b.shape[2]), tok(c.shape[2]), _layer_of(wo, layer),
                  row, pl.BlockSpec((1, d), lambda bb, t: (0, 0)), row, row, row,
                  _layer_of(wg, layer), _layer_of(wu, layer), _layer_of(wd, layer)],
        out_specs=tok(d),
        compiler_params=_params("parallel", "parallel"),
        name="mix_ffn",
    )(x, a, b, c, wo, g1, gn, sh2, sc2, g2, wg, wu, wd)


def _rope_tables(l):
    half = HEAD_DIM // 2
    nfreq = half // 2
    inv = ROPE_BASE ** (-jnp.arange(nfreq, dtype=F32) / nfreq)
    pos = jnp.arange(l)
    rows, cols = pos // GRID_W, pos % GRID_W
    ang = jnp.concatenate([rows.astype(F32)[:, None] * inv[None, :]] * 2
                          + [cols.astype(F32)[:, None] * inv[None, :]] * 2, axis=-1)
    first = (np.arange(HEAD_DIM) % half) < nfreq
    cos, sin = jnp.cos(ang), jnp.sin(ang)
    sin_a = jnp.where(first[None, :], -sin, 0.0)
    sin_b = jnp.where(first[None, :], 0.0, sin)
    reps = LANES // HEAD_DIM
    return tuple(jnp.tile(tb, (1, reps)) for tb in (cos, sin_a, sin_b))


def kernel(x, c, ctx, c_ctx, ada_w, ada_b, norm1_g, norm2_g, w_in, qnorm_a, knorm_a, sink_a, qnorm_b, knorm_b,
           rpb_b, conv_w, conv_b, filt_w1, filt_b1, filt_freq, filt_w2, filt_b2, filt_w3, hyena_bias, w_out,
           ffn_w_gate, ffn_w_up, ffn_w_down):
    bn, l, d = x.shape
    lc = ctx.shape[1]
    depth = ada_w.shape[0]
    cw = hyena_bias.shape[2]
    assert l % (A_STEP_BLOCKS * A_BLOCK) == 0 and l >= 3 * A_SPAN and lc % 256 == 0
    assert l % (B_STEP_GROUPS * B_GROUP * GRID_W) == 0 and l // GRID_W >= 3 * B_GROUP

    mod_rows = 8 * (-(-(bn + 1) // 8))
    c_all = jnp.zeros((mod_rows, d), F32).at[:bn].set(c).at[bn].set(c_ctx)
    mod = _modulation(c_all, ada_w, ada_b)

    a_order = (0, 2, 1, 3)
    a_kv = tuple(hh // (A_HEADS // A_KV_HEADS) for hh in a_order)
    heads_a = lambda t, axis: [lax.slice_in_dim(t, hh * HEAD_DIM, (hh + 1) * HEAD_DIM, axis=axis) for hh in a_order]
    qa_w = A_HEADS * HEAD_DIM
    w_in_b = w_in.astype(BF16)
    w_out_b = jnp.concatenate(heads_a(w_out, 1) + [w_out[:, qa_w:]], axis=1).astype(BF16)
    sink_p = LOG2E * jnp.stack([sink_a[:, hh] for hh in a_order], axis=1)
    wg_b, wu_b, wd_b = ffn_w_gate.astype(BF16), ffn_w_up.astype(BF16), ffn_w_down.astype(BF16)

    lane = np.arange(256)
    e_heads = jnp.asarray((lane[:, None] // HEAD_DIM) == (lane[None, :] // HEAD_DIM), BF16)
    tile4 = lambda g: jnp.tile(g.reshape(1, HEAD_DIM), (1, 256 // HEAD_DIM))
    rope_lat = _rope_tables(l)

    fh_lat, fh_ctx = min(FREQ_TILE, l // 2), min(FREQ_TILE, lc // 2)
    f_lat, g_lat = _dft_matrices(l // 2, fh_lat)
    f_ctx, g_ctx = _dft_matrices(lc // 2, fh_ctx)
    hc_lat = _hyena_consts(l, cw)
    hc_ctx = _hyena_consts(lc, cw)

    xc = ctx
    for i in range(depth):
        last = i == depth - 1
        m = mod[i]
        part = lambda rows, k: rows[:, None, k * d:(k + 1) * d]
        lat = [part(m[:bn], k) for k in range(6)]
        cx = [part(m[bn:bn + 1], k) for k in range(6)]
        g1n = norm1_g[i].reshape(1, d)
        g2n = norm2_g[i].reshape(1, d)
        gqa, gka, gqb, gkb = tile4(qnorm_a[i]), tile4(knorm_a[i]), tile4(qnorm_b[i]), tile4(knorm_b[i])

        qa, ka, va, qb, kb, vb, hy = _proj_in(x, lat[0], lat[1], g1n, w_in_b, i, gqa, gka, gqb, gkb, e_heads,
                                              rope_lat)
        ctx_out = _proj_in(xc.reshape(1, bn * lc, d), cx[0], cx[1], g1n, w_in_b, i, gqa, gka, gqb, gkb, e_heads,
                           None)
        qa_c, ka_c, va_c, qb_c, kb_c, vb_c, hy_c = [t.reshape(bn, lc, -1) for t in ctx_out]
        hy_args = (conv_w[i], conv_b[i].reshape(1, -1),
                   *_pack_filter_mlp(filt_w1[i], filt_b1[i], filt_freq[i], filt_w2[i], filt_b2[i]),
                   filt_w3[i], hyena_bias[i])

        sink_row = jnp.repeat(sink_p[i], A_BLOCK).reshape(1, A_HEADS * A_BLOCK)
        out_a = _attn_a(qa, ka, va, ka_c, va_c, sink_row)
        out_b = _attn_b(qb, kb, vb, kb_c, vb_c, _rpb_tables(rpb_b[i]))
        out_c = _hyena(hy, hc_lat, f_lat, g_lat, fh_lat, *hy_args)
        x = _mix_ffn(x, out_a, out_b, out_c, w_out_b, lat[2], g2n, lat[3], lat[4], lat[5], wg_b, wu_b, wd_b, i)

        if not last:
            oa_c = _dense_attn(qa_c, ka_c, va_c, sink_p[i], kv_of=a_kv, use_sink=True)
            ob_c = _dense_attn(qb_c, kb_c, vb_c, sink_p[i], kv_of=tuple(range(B_HEADS)), use_sink=False)
            oc_c = _hyena(hy_c, hc_ctx, f_ctx, g_ctx, fh_ctx, *hy_args)
            flat = lambda t: t.reshape(1, bn * lc, -1)
            xc = _mix_ffn(flat(xc), flat(oa_c), flat(ob_c), flat(oc_c), w_out_b, cx[2], g2n, cx[3], cx[4], cx[5],
                          wg_b, wu_b, wd_b, i).reshape(bn, lc, d)
    return x
```

```python
import functools
import math

import jax
import jax.numpy as jnp
import numpy as np
from jax import lax
from jax.experimental import pallas as pl
from jax.experimental.pallas import tpu as pltpu

F32 = jnp.float32
BF16 = jnp.bfloat16

GRID_W = 64
HEAD_DIM = 64
A_HEADS = 4
A_KV_HEADS = 2
A_BLOCK = 128
B_HEADS = 4
NA_ROWS = 8
NA_COLS = 16
C_ORDER = 2
C_DIRS = 2
C_FILTER_WIDTH = 64
C_BANDS = 16
ROPE_BASE = 10000.0
EPS = 1e-6
NEG_INF = -1e30
LOG2E = math.log2(math.e)
HYENA_MIN_DECAY = math.log(1e-2) / 1.5
HYENA_MAX_DECAY = math.log(1e-2) / 0.3

V7X_VMEM_BYTES = 64 * 1024 * 1024
VMEM_LIMIT = V7X_VMEM_BYTES - 8 * 1024 * 1024
LANES = 128
TOKEN_TILE = 512
FREQ_TILE = 512


def _params(*sem):
    return pltpu.CompilerParams(dimension_semantics=sem, vmem_limit_bytes=VMEM_LIMIT)


def _dot(a, b):
    return jnp.dot(a, b, preferred_element_type=F32)


def _dot_nt(a, b):
    return lax.dot_general(a, b, (((1,), (1,)), ((), ())), preferred_element_type=F32)


def _dot_hp(a, b):
    ah, al = _split_bf16(a)
    bh, bl = _split_bf16(b)
    return _dot(ah, bh) + (_dot(ah, bl) + _dot(al, bh))


def _split_bf16(v):
    hi = v.astype(BF16)
    lo = (v - hi.astype(F32)).astype(BF16)
    return hi, lo


def _layer_of(stack, layer):
    return pl.BlockSpec((None,) + stack.shape[1:], lambda *_: (layer, 0, 0), pipeline_mode=pl.Buffered(1))


def _resident(shape):
    nd = len(shape)
    return pl.BlockSpec(shape, lambda *_: (0,) * nd, pipeline_mode=pl.Buffered(1))


def _mod_kernel(c_ref, w_ref, b_ref, o_ref):
    cv = c_ref[...]
    sc = (cv * jax.nn.sigmoid(cv)).astype(BF16)
    o_ref[...] = _dot(sc, w_ref[...].astype(BF16)) + b_ref[...]


def _modulation(c_all, ada_w, ada_b):
    depth, d, n = ada_w.shape
    rows = c_all.shape[0]
    tn = 1536
    return pl.pallas_call(
        _mod_kernel,
        out_shape=jax.ShapeDtypeStruct((depth, rows, n), F32),
        grid=(depth, n // tn),
        in_specs=[
            pl.BlockSpec((rows, d), lambda i, j: (0, 0)),
            pl.BlockSpec((None, d, tn), lambda i, j: (i, 0, j)),
            pl.BlockSpec((None, 1, tn), lambda i, j: (i, 0, j)),
        ],
        out_specs=pl.BlockSpec((None, rows, tn), lambda i, j: (i, 0, j)),
        compiler_params=_params("arbitrary", "arbitrary"),
        name="modulation",
    )(c_all, ada_w, ada_b.reshape(depth, 1, n))


def _rms_mod(x, g, shift, scale):
    y = x * lax.rsqrt(jnp.mean(x * x, axis=-1, keepdims=True) + EPS)
    return (y * g) * (1 + scale) + shift


def _head_norm(h, g, e):
    ss = _dot((h * h).astype(BF16), e)
    return (h * lax.rsqrt(ss * (1.0 / HEAD_DIM) + EPS)) * g


def _rope(t, cos, sin_a, sin_b):
    outs = []
    for c in range(t.shape[1] // LANES):
        tc = t[:, c * LANES:(c + 1) * LANES]
        outs.append(tc * cos + pltpu.roll(tc, LANES - 16, 1) * sin_a + pltpu.roll(tc, 16, 1) * sin_b)
    return outs[0] if len(outs) == 1 else jnp.concatenate(outs, axis=1)


PROJ_ROWS = 1024
PROJ_SPLIT = 2


def _proj_in_kernel(x_ref, shift_ref, scale_ref, g_ref, w_ref, gqa_ref, gka_ref, gqb_ref, gkb_ref, e_ref, *rest,
                    rope):
    if rope:
        cos_ref, sa_ref, sb_ref = rest[:3]
        rest = rest[3:]
    qa_ref, ka_ref, va_ref, qb_ref, kb_ref, vb_ref, hy_ref = rest
    qscale = LOG2E * HEAD_DIM ** -0.5
    hm = x_ref.shape[0] // PROJ_SPLIT
    for part in range(PROJ_SPLIT):
        rows = slice(part * hm, (part + 1) * hm)
        xb = _rms_mod(x_ref[rows, :], g_ref[...], shift_ref[...], scale_ref[...]).astype(BF16)

        h_all = _dot(xb, w_ref[...])

        def proj(lo, hi):
            return h_all[:, lo:hi]

        def maybe_rope(t):
            return _rope(t, cos_ref[rows, :], sa_ref[rows, :], sb_ref[rows, :]) if rope else t

        def store_v(ref, v):
            if rope:
                for i in range(hm // LANES):
                    ref[part * (hm // LANES) + i] = v[i * LANES:(i + 1) * LANES, :].T.astype(BF16)
            else:
                ref[rows, :] = v.astype(BF16)

        qa = maybe_rope(_head_norm(proj(0, 256), gqa_ref[...], e_ref[...])) * qscale
        t0, t1 = qa[:, :LANES], qa[:, LANES:]
        lo = lax.broadcasted_iota(jnp.int32, t0.shape, 1) < HEAD_DIM
        qa_ref[rows, :LANES] = jnp.where(lo, t0, pltpu.roll(t1, HEAD_DIM, 1)).astype(BF16)
        qa_ref[rows, LANES:] = jnp.where(lo, pltpu.roll(t0, HEAD_DIM, 1), t1).astype(BF16)
        kva = proj(256, 512)
        ka = maybe_rope(_head_norm(kva[:, :LANES], gka_ref[:, :128], e_ref[:128, :128]))
        ka_ref[rows, :] = ka.astype(BF16)
        store_v(va_ref, kva[:, LANES:])
        qb = _head_norm(proj(512, 768), gqb_ref[...], e_ref[...])
        qb_ref[rows, :] = (qb * qscale).astype(BF16)
        kb_ref[rows, :] = _head_norm(proj(768, 1024), gkb_ref[...], e_ref[...]).astype(BF16)
        store_v(vb_ref, proj(1024, 1280))
        hy_ref[rows, :] = proj(1280, 2816).astype(BF16)


def _proj_in(x, shift, scale, g, w, layer, gqa, gka, gqb, gkb, e, rope_tabs):
    bn, l, d = x.shape
    n = w.shape[2]
    rope = rope_tabs is not None
    tm = min(PROJ_ROWS, l)
    row = lambda width: pl.BlockSpec((None, 1, width), lambda b, t: (b, 0, 0))
    const = lambda shape: pl.BlockSpec(shape, lambda b, t: (0,) * len(shape))
    tok = lambda width: pl.BlockSpec((None, tm, width), lambda b, t: (b, t, 0))
    tab = pl.BlockSpec((tm, LANES), lambda b, t: (t, 0))
    widths = (256, 128, 128, 256, 256, 256, n - 1280)
    shapes = [(bn, l, wd) for wd in widths]
    specs = [tok(wd) for wd in widths]
    if rope:
        for i in (2, 5):
            shapes[i] = (bn, l // LANES, widths[i], LANES)
            specs[i] = pl.BlockSpec((None, tm // LANES, widths[i], LANES), lambda b, t: (b, t, 0, 0))
    return pl.pallas_call(
        functools.partial(_proj_in_kernel, rope=rope),
        out_shape=[jax.ShapeDtypeStruct(sh, BF16) for sh in shapes],
        grid=(bn, l // tm),
        in_specs=[tok(d), row(d), row(d), const((1, d)), _layer_of(w, layer),
                  const((1, 256)), const((1, 256)), const((1, 256)), const((1, 256)), const((256, 256))]
                 + ([tab, tab, tab] if rope else []),
        out_specs=specs,
        compiler_params=_params("parallel", "parallel"),
        name="proj_in_rope" if rope else "proj_in",
    )(x, shift, scale, g, w, gqa, gka, gqb, gkb, e, *(rope_tabs or ()))


def _softmax_pv(s_list, v_list, sink):
    m = s_list[0].max(axis=-1, keepdims=True)
    for s in s_list[1:]:
        m = jnp.maximum(m, s.max(axis=-1, keepdims=True))
    if sink is not None:
        m = jnp.maximum(m, sink)
    den = None
    out = None
    for s, v in zip(s_list, v_list):
        p = jnp.exp2(s - m)
        ps = p.sum(axis=-1, keepdims=True)
        den = ps if den is None else den + ps
        o = _dot(p.astype(BF16), v)
        out = o if out is None else out + o
    if sink is not None:
        den = den + jnp.exp2(sink - m)
    return out * (1.0 / den)


def _stack_heads(q):
    lo = lax.broadcasted_iota(jnp.int32, q.shape, 1) < HEAD_DIM
    zero = jnp.zeros_like(q)
    return jnp.concatenate([jnp.where(lo, q, zero), jnp.where(lo, zero, q)], axis=0)


def _unstack_heads(o):
    m = o.shape[0] // 2
    lo = lax.broadcasted_iota(jnp.int32, (m, LANES), 1) < HEAD_DIM
    return jnp.where(lo, o[:m], o[m:])


SOFTMAX_CHUNK = 32


def _softmax_keys(s_ref, p_ref, nloc, add_loc, sink):
    nk = s_ref.shape[0]
    ch = SOFTMAX_CHUNK
    macc = None
    for r0 in range(0, nk, ch):
        s = s_ref[r0:r0 + ch, :]
        if r0 < nloc:
            s = s + add_loc(r0)
            s_ref[r0:r0 + ch, :] = s
        macc = s if macc is None else jnp.maximum(macc, s)
    m = macc.max(axis=0, keepdims=True)
    if sink is not None:
        m = jnp.maximum(m, sink)
    sacc = None
    for r0 in range(0, nk, ch):
        p = jnp.exp2(s_ref[r0:r0 + ch, :] - m)
        sacc = p if sacc is None else sacc + p
        p_ref[r0:r0 + ch, :] = p.astype(BF16)
    den = sacc.sum(axis=0, keepdims=True)
    if sink is not None:
        den = den + jnp.exp2(sink - m)
    return 1.0 / den


A_STEP_BLOCKS = 16
A_SPAN = 3 * A_BLOCK


def _attn_a_kernel(q_ref, k_ref, vt_ref, kx_ref, vxt_ref, mask_ref, sink_ref, o_ref, s_ref, p_ref):
    l = k_ref.shape[0]
    nb = l // A_BLOCK
    hd = HEAD_DIM
    def window(u):
        n = pl.program_id(1) * A_STEP_BLOCKS + u
        tile0 = jnp.clip(n - 1, 0, nb - 3)
        pat = jnp.where(n == 0, 0, jnp.where(n == nb - 1, 2, 1))
        return tile0, pat

    def scores(u):
        tile0, _ = window(u)
        start = pl.multiple_of(tile0 * A_BLOCK, A_BLOCK)
        q = q_ref[u * A_BLOCK:(u + 1) * A_BLOCK, :]
        qs = jnp.concatenate([_stack_heads(q[:, :LANES]), _stack_heads(q[:, LANES:])], axis=0)
        s_ref[u % 2] = _dot_nt(jnp.concatenate([k_ref[pl.ds(start, A_SPAN), :], kx_ref[...]], axis=0), qs)

    def output(u, r):
        tile0, _ = window(u)
        vt = jnp.concatenate([vt_ref[tile0 + i] for i in range(3)] + [vxt_ref[...]], axis=1)
        ot = _dot(vt, p_ref[u % 2]) * r
        ot = jnp.concatenate([ot[(i % 2) * hd:(i % 2 + 1) * hd, i * A_BLOCK:(i + 1) * A_BLOCK]
                              for i in range(A_HEADS)], axis=0)
        o_ref[u * A_BLOCK:(u + 1) * A_BLOCK, :] = ot.T.astype(o_ref.dtype)

    scores(0)
    pending = None
    for u in range(A_STEP_BLOCKS):
        if u + 1 < A_STEP_BLOCKS:
            scores(u + 1)
        _, pat = window(u)
        r = _softmax_keys(s_ref.at[u % 2], p_ref.at[u % 2], A_SPAN,
                          lambda r0: mask_ref[pat, r0:r0 + SOFTMAX_CHUNK, :], sink_ref[...])
        if pending is not None:
            output(*pending)
        pending = (u, r)
    output(*pending)


def _attn_a_mask():
    i = np.arange(A_BLOCK)[None, :]
    j = np.arange(A_SPAN)[:, None]
    offs = (0, A_BLOCK, 2 * A_BLOCK)
    m = np.stack([np.where(np.abs(j - i - o) <= A_BLOCK, 0.0, -np.inf) for o in offs])
    return jnp.asarray(np.tile(m, (1, 1, A_HEADS)), F32)


def _attn_a(q, k, vt, kx, vx, sink_row):
    bn, l, qw = q.shape
    lc = kx.shape[1]
    kvw = k.shape[2]
    nb = l // A_BLOCK
    qs = A_STEP_BLOCKS * A_BLOCK
    mask = _attn_a_mask()
    vxt = jnp.swapaxes(vx, 1, 2)
    seq = pl.BlockSpec((None, l, kvw), lambda b, s: (b, 0, 0))
    qblk = pl.BlockSpec((None, qs, qw), lambda b, s: (b, s, 0))
    nq = A_HEADS * A_BLOCK
    return pl.pallas_call(
        _attn_a_kernel,
        out_shape=jax.ShapeDtypeStruct((bn, l, qw), BF16),
        grid=(bn, l // qs),
        in_specs=[qblk, seq, pl.BlockSpec((None, nb, kvw, A_BLOCK), lambda b, s: (b, 0, 0, 0)),
                  pl.BlockSpec((None, lc, kvw), lambda b, s: (b, 0, 0)),
                  pl.BlockSpec((None, kvw, lc), lambda b, s: (b, 0, 0)),
                  _resident(mask.shape), _resident(sink_row.shape)],
        out_specs=qblk,
        scratch_shapes=[pltpu.VMEM((2, A_SPAN + lc, nq), F32), pltpu.VMEM((2, A_SPAN + lc, nq), BF16)],
        compiler_params=_params("parallel", "arbitrary"),
        name="window_attn",
    )(q, k, vt, kx, vxt, mask, sink_row)


def _dense_attn_kernel(sink_ref, q_ref, k_ref, v_ref, o_ref, *, kv_of, use_sink):
    for h, kv in enumerate(kv_of):
        hs = slice(h * HEAD_DIM, (h + 1) * HEAD_DIM)
        ks = slice(kv * HEAD_DIM, (kv + 1) * HEAD_DIM)
        s = _dot_nt(q_ref[:, hs], k_ref[:, ks])
        o = _softmax_pv([s], [v_ref[:, ks]], sink_ref[h] if use_sink else None)
        o_ref[:, hs] = o.astype(o_ref.dtype)


def _dense_attn(q, k, v, sink, *, kv_of, use_sink):
    bn, l, qw = q.shape
    kvw = k.shape[2]
    full = lambda wd: pl.BlockSpec((None, l, wd), lambda b: (b, 0, 0))
    return pl.pallas_call(
        functools.partial(_dense_attn_kernel, kv_of=kv_of, use_sink=use_sink),
        out_shape=jax.ShapeDtypeStruct((bn, l, qw), BF16),
        grid=(bn,),
        in_specs=[pl.BlockSpec(memory_space=pltpu.SMEM), full(qw), full(kvw), full(kvw)],
        out_specs=full(qw),
        compiler_params=_params("parallel"),
        name="ctx_attn_sink" if use_sink else "ctx_attn",
    )(sink, q, k, v)


def _rpb_kernel(r_ref, oh_ref, ok_ref, o_ref):
    r = r_ref[...]
    b1 = r.astype(BF16)
    r2 = r - b1.astype(F32)
    b2 = r2.astype(BF16)
    b3 = (r2 - b2.astype(F32)).astype(BF16)
    oh = oh_ref[...]
    bias = (_dot(b1, oh) + _dot(b2, oh)) + _dot(b3, oh)
    o_ref[...] = jnp.where(ok_ref[...] > 0.5, bias * LOG2E, -jnp.inf)


def _rpb_slots():
    a = np.arange(B_GROUP)[:, None]
    kr = np.arange(B_SLAB)[None, :]
    dr = np.stack([kr - a + NA_ROWS - 1, kr - a + NA_ROWS // 2 - 1, kr - a + (B_GROUP + NA_ROWS - 1 - B_SLAB)])
    lo = np.stack([0 * a + 0 * kr, a + 0 * kr, 0 * a + (B_SLAB - NA_ROWS) + 0 * kr])
    valid = (kr[None] >= lo) & (kr[None] < lo + NA_ROWS)
    return np.where(valid, dr, -1)


def _rpb_assemble_kernel(t_ref, o_ref, *, n_dr):
    slots = _rpb_slots()
    w = GRID_W
    blank = jnp.full((w, w), -jnp.inf, F32)
    for pat in range(3):
        for t in range(o_ref.shape[1]):
            for kr in range(B_SLAB):
                for hh in range(2):
                    for a in range(B_GROUP):
                        dr = int(slots[pat, a, kr])
                        tile = t_ref[(2 * t + hh) * n_dr + dr] if dr >= 0 else blank
                        c0 = (hh * B_GROUP + a) * w
                        o_ref[pat, t, kr * w:(kr + 1) * w, c0:c0 + w] = tile


def _rpb_tables(rpb):
    h, nr, nc = rpb.shape
    col = np.arange(GRID_W)
    dc = np.clip(col[:, None] - col[None, :], 1 - NA_COLS, NA_COLS - 1) + NA_COLS - 1
    onehot = (dc.reshape(1, -1) == np.arange(nc)[:, None]).astype(np.float32)
    onehot = np.concatenate([onehot, np.zeros((32 - nc, GRID_W * GRID_W), np.float32)], axis=0)
    col_start = np.clip(col - NA_COLS // 2, 0, GRID_W - NA_COLS)
    col_ok = (col[:, None] >= col_start[None, :]) & (col[:, None] < col_start[None, :] + NA_COLS)
    rows = 64
    r2 = jnp.zeros((rows, 32), F32).at[:h * nr, :nc].set(rpb.reshape(h * nr, nc))
    tiles = pl.pallas_call(
        _rpb_kernel,
        out_shape=jax.ShapeDtypeStruct((rows, GRID_W * GRID_W), F32),
        name="rpb_table",
    )(r2, jnp.asarray(onehot, BF16), jnp.asarray(col_ok.reshape(1, -1), F32))
    tiles = tiles.reshape(rows, GRID_W, GRID_W)
    return pl.pallas_call(
        functools.partial(_rpb_assemble_kernel, n_dr=nr),
        out_shape=jax.ShapeDtypeStruct((3, h // 2, B_SLAB * GRID_W, 2 * B_GROUP * GRID_W), F32),
        compiler_params=_params(),
        name="rpb_assemble",
    )(tiles)


B_GROUP = 4
B_STEP_GROUPS = 4
B_SLAB = 12


def _attn_b_kernel(q_ref, k_ref, vt_ref, kx_ref, vxt_ref, tbl_ref, o_ref, s_ref, p_ref):
    ng = pl.num_programs(1) * B_STEP_GROUPS
    rows = k_ref.shape[0] // GRID_W
    nloc = B_SLAB * GRID_W
    gq = B_GROUP * GRID_W
    units = [(u, t) for u in range(B_STEP_GROUPS) for t in range(B_HEADS // 2)]

    def slab(u):
        g = pl.program_id(1) * B_STEP_GROUPS + u
        base = jnp.clip(g * B_GROUP - NA_ROWS // 2, 0, rows - B_SLAB)
        pat = jnp.where(g == 0, 0, jnp.where(g == ng - 1, 2, 1))
        return base, pat

    def scores(u, t):
        base, _ = slab(u)
        start = pl.multiple_of(base * GRID_W, LANES)
        ts = slice(t * LANES, (t + 1) * LANES)
        qs = _stack_heads(q_ref[u * gq:(u + 1) * gq, ts])
        s_ref[u, t] = _dot_nt(jnp.concatenate([k_ref[pl.ds(start, nloc), ts], kx_ref[:, ts]], axis=0), qs)

    scores(*units[0])
    for i, (u, t) in enumerate(units):
        if i + 1 < len(units):
            scores(*units[i + 1])
        base, pat = slab(u)
        tile0 = base // (LANES // GRID_W)
        ts = slice(t * LANES, (t + 1) * LANES)
        sb, pb = s_ref.at[u, t], p_ref.at[u, t]
        r = _softmax_keys(sb, pb, nloc, lambda r0: tbl_ref[pat, t, r0:r0 + SOFTMAX_CHUNK, :], None)
        vt = jnp.concatenate([vt_ref[tile0 + j, ts, :] for j in range(nloc // LANES)] + [vxt_ref[ts, :]],
                             axis=1)
        ot = _dot(vt, pb[...]) * r
        ot = jnp.concatenate([ot[:HEAD_DIM, :gq], ot[HEAD_DIM:, gq:]], axis=0)
        o_ref[u * gq:(u + 1) * gq, ts] = ot.T.astype(o_ref.dtype)


def _attn_b(q, k, vt, kx, vx, tbl):
    bn, l, w = q.shape
    lc = kx.shape[1]
    gq = B_GROUP * GRID_W
    sq = B_STEP_GROUPS * gq
    nk = B_SLAB * GRID_W + lc
    vxt = jnp.swapaxes(vx, 1, 2)
    seq = pl.BlockSpec((None, l, w), lambda b, g: (b, 0, 0))
    qblk = pl.BlockSpec((None, sq, w), lambda b, g: (b, g, 0))
    return pl.pallas_call(
        _attn_b_kernel,
        out_shape=jax.ShapeDtypeStruct((bn, l, w), BF16),
        grid=(bn, l // sq),
        in_specs=[qblk, seq, pl.BlockSpec((None, l // LANES, w, LANES), lambda b, g: (b, 0, 0, 0)),
                  pl.BlockSpec((None, lc, w), lambda b, g: (b, 0, 0)),
                  pl.BlockSpec((None, w, lc), lambda b, g: (b, 0, 0)),
                  _resident(tbl.shape)],
        out_specs=qblk,
        scratch_shapes=[pltpu.VMEM((B_STEP_GROUPS, B_HEADS // 2, nk, 2 * gq), F32),
                        pltpu.VMEM((B_STEP_GROUPS, B_HEADS // 2, nk, 2 * gq), BF16)],
        compiler_params=_params("parallel", "arbitrary"),
        name="nbr_attn",
    )(q, k, vt, kx, vxt, tbl)


def _dft_matrices(l, fh):
    k = jnp.arange(l, dtype=jnp.int32)[:, None]
    n = jnp.arange(l, dtype=jnp.int32)[None, :]
    ang = (((2 * k + 1) * n) % (4 * l)).astype(F32) * (math.pi / (2 * l))
    fre = jnp.cos(ang).reshape(l // fh, fh, l)
    fim = (-jnp.sin(ang)).reshape(l // fh, fh, l)
    f = jnp.concatenate([fre, fim], axis=1).reshape(2 * l, l)
    g = f.T * (1.0 / l)
    return f.astype(BF16), g.astype(BF16)


def _filter_kernel(z_ref, t_ref, w1_ref, b1_ref, fr_ref, w2_ref, b2_ref, w3_ref, dl_ref,
                   ah_ref, al_ref, sh_ref, sl_ref, hid_ref):
    @pl.when(pl.program_id(0) == 0)
    def _():
        fr = fr_ref[...]
        hid = jnp.sin(fr * (_dot_hp(z_ref[...], w1_ref[...]) + b1_ref[...]))
        hid_ref[...] = jnp.sin(fr * (_dot_hp(hid, w2_ref[...]) + b2_ref[...]))

    cw = dl_ref.shape[1]
    p = t_ref.shape[0] // 3
    hid = hid_ref[...]
    w3 = w3_ref[...]
    none = jnp.zeros_like(w3)
    taps = jnp.concatenate([_dot_hp(hid, jnp.concatenate([w3, none], axis=0)),
                            _dot_hp(hid, jnp.concatenate([none, w3], axis=0))], axis=0)
    decay = jnp.exp(-t_ref[...] * dl_ref[...])
    kf = taps[:, :cw] * decay
    kb = taps[:, cw:] * decay
    kf0, kf1, kfr = kf[:p], kf[p:2 * p], kf[2 * p:]
    kb0, kb1, kbr = kb[:p], kb[p:2 * p], kb[2 * p:]
    first = lax.broadcasted_iota(jnp.int32, (p, cw), 0) == 0
    drop0 = lambda v: jnp.where(first, 0.0, v)
    kb0 = drop0(kb0)
    colsum = lambda v: jnp.sum(jnp.abs(v), axis=0, keepdims=True)
    inv = 1.0 / (colsum(kf0) + colsum(kf1) + colsum(kb0) + colsum(kb1))
    pairs = ((kf0, kb0), (kf1, drop0(kfr)), (kbr, drop0(kb1)))
    for d, (cp, cm) in enumerate(pairs):
        cols = slice(d * cw, (d + 1) * cw)
        ah_ref[:, cols], al_ref[:, cols] = _split_bf16((cp + cm) * inv)
        sh_ref[:, cols], sl_ref[:, cols] = _split_bf16((cp - cm) * inv)


def _filters(z, t, w1, b1, freq, w2, b2, w3, deltas):
    p = t.shape[0] // 3
    cw = deltas.shape[1]
    fw = w2.shape[0]
    const = lambda shape: pl.BlockSpec(shape, lambda o: (0,) * len(shape))
    out = jax.ShapeDtypeStruct((p, C_ORDER * 3 * cw), BF16)
    oblk = pl.BlockSpec((p, 3 * cw), lambda o: (0, o))
    return pl.pallas_call(
        _filter_kernel,
        out_shape=[out] * 4,
        grid=(C_ORDER,),
        in_specs=[const(z.shape), const(t.shape), const(w1.shape), const((1, fw)), const((1, fw)),
                  const((fw, fw)), const((1, fw)), pl.BlockSpec((w3.shape[0], C_DIRS * cw), lambda o: (0, o)),
                  const((1, cw))],
        out_specs=[oblk] * 4,
        scratch_shapes=[pltpu.VMEM(z.shape, F32)],
        compiler_params=_params("arbitrary"),
        name="hyena_filter",
    )(z, t, w1, b1, freq, w2, b2, w3, deltas)


def _spectrum_kernel(f_ref, ah_ref, al_ref, sh_ref, sl_ref, kre_ref, kim_ref):
    fh = kre_ref.shape[0]
    fre = f_ref[:fh, :]
    fim = f_ref[fh:, :]
    kre_ref[...] = _dot(fre, ah_ref[...]) + _dot(fre, al_ref[...])
    kim_ref[...] = _dot(fim, sh_ref[...]) + _dot(fim, sl_ref[...])


def _spectrum(f, ah, al, sh, sl, fh):
    p, n = ah.shape
    tn = n // C_ORDER
    taps = pl.BlockSpec((p, tn), lambda o, j: (0, o))
    out = jax.ShapeDtypeStruct((p, n), F32)
    oblk = pl.BlockSpec((fh, tn), lambda o, j: (j, o))
    return pl.pallas_call(
        _spectrum_kernel,
        out_shape=[out, out],
        grid=(C_ORDER, p // fh),
        in_specs=[pl.BlockSpec((2 * fh, p), lambda o, j: (j, 0)), taps, taps, taps, taps],
        out_specs=[oblk, oblk],
        compiler_params=_params("arbitrary", "arbitrary"),
        name="hyena_spectrum",
    )(f, ah, al, sh, sl)


def _short_conv(u, w_ref, b_ref):
    n = u.shape[0]
    row = lax.broadcasted_iota(jnp.int32, u.shape, 0)
    prev = jnp.where(row == 0, 0.0, pltpu.roll(u, 1, 0))
    nxt = jnp.where(row == n - 1, 0.0, pltpu.roll(u, n - 1, 0))
    return prev * w_ref[0:1, :] + u * w_ref[1:2, :] + nxt * w_ref[2:3, :] + b_ref[...]


def _short_conv_wrap(u, w_ref, b_ref):
    n = u.shape[0]
    return (pltpu.roll(u, 1, 0) * w_ref[0:1, :] + u * w_ref[1:2, :] + pltpu.roll(u, n - 1, 0) * w_ref[2:3, :]
            + b_ref[...])


CONV_CHUNKS = 2


def _longconv_kernel(u_ref, g_ref, cwu_ref, cbu_ref, cwg_ref, cbg_ref, f_ref, gm_ref, kre_ref, kim_ref, d_ref,
                     o_ref, ub_ref, acc_ref, *, conv_u, n_steps):
    j = pl.program_id(1)
    fh = kre_ref.shape[0]
    p, cw = ub_ref.shape[1], d_ref.shape[1]
    cc = cw // CONV_CHUNKS
    l = 2 * p
    edge = 16

    def edge_conv(ref, w_ref, b_ref, cs):
        head = _short_conv(ref[0:2 * edge, cs].astype(F32), w_ref.at[:, cs], b_ref.at[:, cs])[:edge]
        tail = _short_conv(ref[l - 2 * edge:l, cs].astype(F32), w_ref.at[:, cs], b_ref.at[:, cs])[edge:]
        return head, tail

    def load_u(c, cs):
        ub = ub_ref.at[c]
        if conv_u:
            u = _short_conv_wrap(u_ref[:, cs].astype(F32), cwu_ref.at[:, cs], cbu_ref.at[:, cs]).astype(BF16)
        else:
            u = u_ref[:, cs]
        ub[:, :cc] = u[:p]
        ub[:, cc:] = u[p:]
        if conv_u:
            head, tail = edge_conv(u_ref, cwu_ref, cbu_ref, cs)
            ub[0:edge, :cc] = head.astype(BF16)
            ub[p - edge:p, cc:] = tail.astype(BF16)

    def forward(c):
        return _dot(f_ref[...], ub_ref[c])

    def inverse(c, spec):
        u0r, u1r, u0i, u1i = spec[:fh, :cc], spec[:fh, cc:], spec[fh:, :cc], spec[fh:, cc:]
        tap = lambda ref, d: ref[:, d * cw + c * cc:d * cw + (c + 1) * cc]
        c0r, c1r, cmr = (tap(kre_ref, d) for d in range(3))
        c0i, c1i, cmi = (tap(kim_ref, d) for d in range(3))
        y0r = (c0r * u0r - c0i * u0i) + (cmr * u1r - cmi * u1i)
        y0i = (c0r * u0i + c0i * u0r) + (cmr * u1i + cmi * u1r)
        y1r = (c1r * u0r - c1i * u0i) + (c0r * u1r - c0i * u1i)
        y1i = (c1r * u0i + c1i * u0r) + (c0r * u1i + c0i * u1r)
        y = jnp.concatenate([jnp.concatenate([y0r, y1r], axis=1), jnp.concatenate([y0i, y1i], axis=1)], axis=0)
        return _dot(gm_ref[...], y.astype(BF16))

    def gated_out(c, cs, conv):
        gate = _short_conv_wrap(g_ref[:, cs].astype(F32), cwg_ref.at[:, cs], cbg_ref.at[:, cs])
        ub = ub_ref.at[c]
        y = jnp.concatenate([conv[:, :cc], conv[:, cc:]], axis=0)
        y = y + jnp.concatenate([ub[:, :cc], ub[:, cc:]], axis=0).astype(F32) * d_ref[:, cs]
        o_ref[:, cs] = (gate * y).astype(o_ref.dtype)
        head, tail = edge_conv(g_ref, cwg_ref, cbg_ref, cs)
        o_ref[0:edge, cs] = (head * y[:edge]).astype(o_ref.dtype)
        o_ref[l - edge:l, cs] = (tail * y[l - edge:]).astype(o_ref.dtype)

    def step(first, last):
        chunks = [(c, slice(c * cc, (c + 1) * cc)) for c in range(CONV_CHUNKS)]

        def start(c, cs):
            if first:
                load_u(c, cs)
            return forward(c)

        spec = start(*chunks[0])
        pending = None
        for i, (c, cs) in enumerate(chunks):
            nxt = start(*chunks[i + 1]) if i + 1 < len(chunks) else None
            conv = inverse(c, spec)
            spec = nxt
            if not first:
                conv = acc_ref[c] + conv
            if last:
                if pending is not None:
                    gated_out(*pending)
                pending = (c, cs, conv)
            else:
                acc_ref[c] = conv
        if last:
            gated_out(*pending)

    if n_steps == 1:
        step(True, True)
    else:
        pl.when(j == 0)(lambda: step(True, False))
        pl.when(j == n_steps - 1)(lambda: step(False, True))
        if n_steps > 2:
            pl.when(jnp.logical_and(j > 0, j < n_steps - 1))(lambda: step(False, False))


def _longconv(u_arr, u_blk, g_arr, g_blk, conv_w, conv_b, f, gm, kre, kim, order, d, *, conv_u, fh):
    bn, l, _ = u_arr.shape
    p = l // 2
    cw = d.shape[1]
    ub = u_blk if conv_u else 0
    tok = lambda blk: pl.BlockSpec((None, l, cw), lambda b, j: (b, 0, blk))
    cpar = lambda rows, blk: pl.BlockSpec((rows, cw), lambda b, j: (0, blk))
    ktab = pl.BlockSpec((fh, 3 * cw), lambda b, j: (j, order))
    return pl.pallas_call(
        functools.partial(_longconv_kernel, conv_u=conv_u, n_steps=p // fh),
        out_shape=jax.ShapeDtypeStruct((bn, l, cw), BF16),
        grid=(bn, p // fh),
        in_specs=[tok(u_blk), tok(g_blk), cpar(3, ub), cpar(1, ub), cpar(3, g_blk), cpar(1, g_blk),
                  pl.BlockSpec((2 * fh, p), lambda b, j: (j, 0)),
                  pl.BlockSpec((p, 2 * fh), lambda b, j: (0, j)),
                  ktab, ktab, pl.BlockSpec((1, cw), lambda b, j: (0, 0))],
        out_specs=tok(0),
        scratch_shapes=[pltpu.VMEM((CONV_CHUNKS, p, 2 * cw // CONV_CHUNKS), BF16),
                        pltpu.VMEM((CONV_CHUNKS, p, 2 * cw // CONV_CHUNKS), F32)],
        compiler_params=_params("parallel", "arbitrary"),
        name="hyena_longconv",
    )(u_arr, g_arr, conv_w, conv_b, conv_w, conv_b, f, gm, kre, kim, d)


def _hyena_consts(l, cw):
    p = l // 2
    e = np.arange(p)
    pos = np.concatenate([e, p + e, p - e])
    t = np.linspace(0.0, 1.0, l, dtype=np.float32)[pos][:, None]
    w = (2.0 * math.pi * np.arange(l, dtype=np.float32) / l).astype(np.float32)[pos][:, None]
    fq = np.linspace(1e-4, C_BANDS - 1, C_BANDS, dtype=np.float32)[None, :]
    wf = jnp.asarray(w) * jnp.asarray(fq)
    z = jnp.concatenate([jnp.asarray(t), jnp.cos(wf), -jnp.sin(wf)], axis=-1)
    z = jnp.pad(z, ((0, 0), (0, C_FILTER_WIDTH - z.shape[1])))
    z = jnp.concatenate([z[:3 * p // 2], z[3 * p // 2:]], axis=1)
    deltas = np.abs(np.linspace(HYENA_MIN_DECAY, HYENA_MAX_DECAY, cw, dtype=np.float32))[None, :]
    return z, jnp.asarray(t), jnp.asarray(deltas)


def _pack_filter_mlp(w1, b1, freq, w2, b2):
    fw = w2.shape[0]
    w1p = jnp.pad(w1, ((0, fw - w1.shape[0]), (0, 0)))
    zero = jnp.zeros((fw, fw), F32)
    diag = lambda w: jnp.concatenate([jnp.concatenate([w, zero], axis=1), jnp.concatenate([zero, w], axis=1)], axis=0)
    twice = lambda v: jnp.tile(v.reshape(1, fw), (1, 2))
    return diag(w1p), twice(b1), twice(freq), diag(w2), twice(b2)


def _hyena(hy, hconst, f, gm, fh, conv_w, conv_b, w1p, b1, freq, w2, b2, w3, dbias):
    z, t, deltas = hconst
    ah, al, sh, sl = _filters(z, t, w1p, b1, freq, w2, b2, w3, deltas)
    kre, kim = _spectrum(f, ah, al, sh, sl, fh)
    zz = _longconv(hy, 0, hy, 1, conv_w, conv_b, f, gm, kre, kim, 0, dbias[0:1], conv_u=True, fh=fh)
    return _longconv(zz, 0, hy, 2, conv_w, conv_b, f, gm, kre, kim, 1, dbias[1:2], conv_u=False, fh=fh)


def _mix_ffn_kernel(x_ref, a_ref, b_ref, c_ref, wo_ref, g1_ref, gn_ref, sh_ref, sc_ref, g2_ref,
                    wg_ref, wu_ref, wd_ref, o_ref, *, fc):
    wa, wb = a_ref.shape[1], b_ref.shape[1]
    mix = (_dot(a_ref[...], wo_ref[:wa, :]) + _dot(b_ref[...], wo_ref[wa:wa + wb, :])
           + _dot(c_ref[...], wo_ref[wa + wb:, :]))
    x1 = x_ref[...] + g1_ref[...] * mix
    xb = _rms_mod(x1, gn_ref[...], sh_ref[...], sc_ref[...]).astype(BF16)
    acc = None
    for c0 in range(0, wg_ref.shape[1], fc):
        hg = _dot(xb, wg_ref[:, c0:c0 + fc])
        hu = _dot(xb, wu_ref[:, c0:c0 + fc])
        act = ((hg * jax.nn.sigmoid(hg)) * hu).astype(BF16)
        part = _dot(act, wd_ref[c0:c0 + fc, :])
        acc = part if acc is None else acc + part
    o_ref[...] = x1 + g2_ref[...] * acc


def _mix_ffn(x, a, b, c, wo, g1, gn, sh2, sc2, g2, wg, wu, wd, layer):
    bn, l, d = x.shape
    tm = min(TOKEN_TILE, l)
    row = pl.BlockSpec((None, 1, d), lambda bb, t: (bb, 0, 0))
    tok = lambda width: pl.BlockSpec((None, tm, width), lambda bb, t: (bb, t, 0))
    return pl.pallas_call(
        functools.partial(_mix_ffn_kernel, fc=256),
        out_shape=jax.ShapeDtypeStruct((bn, l, d), F32),
        grid=(bn, l // tm),
        in_specs=[tok(d), tok(a.shape[2]), tok(b.shape[2]), tok(c.shape[2]), _layer_of(wo, layer),
                  row, pl.BlockSpec((1, d), lambda bb, t: (0, 0)), row, row, row,
                  _layer_of(wg, layer), _layer_of(wu, layer), _layer_of(wd, layer)],
        out_specs=tok(d),
        compiler_params=_params("parallel", "parallel"),
        name="mix_ffn",
    )(x, a, b, c, wo, g1, gn, sh2, sc2, g2, wg, wu, wd)


def _rope_tables(l):
    half = HEAD_DIM // 2
    nfreq = half // 2
    inv = ROPE_BASE ** (-jnp.arange(nfreq, dtype=F32) / nfreq)
    pos = jnp.arange(l)
    rows, cols = pos // GRID_W, pos % GRID_W
    ang = jnp.concatenate([rows.astype(F32)[:, None] * inv[None, :]] * 2
                          + [cols.astype(F32)[:, None] * inv[None, :]] * 2, axis=-1)
    first = (np.arange(HEAD_DIM) % half) < nfreq
    cos, sin = jnp.cos(ang), jnp.sin(ang)
    sin_a = jnp.where(first[None, :], -sin, 0.0)
    sin_b = jnp.where(first[None, :], 0.0, sin)
    reps = LANES // HEAD_DIM
    return tuple(jnp.tile(tb, (1, reps)) for tb in (cos, sin_a, sin_b))


def kernel(x, c, ctx, c_ctx, ada_w, ada_b, norm1_g, norm2_g, w_in, qnorm_a, knorm_a, sink_a, qnorm_b, knorm_b,
           rpb_b, conv_w, conv_b, filt_w1, filt_b1, filt_freq, filt_w2, filt_b2, filt_w3, hyena_bias, w_out,
           ffn_w_gate, ffn_w_up, ffn_w_down):
    bn, l, d = x.shape
    lc = ctx.shape[1]
    depth = ada_w.shape[0]
    cw = hyena_bias.shape[2]
    assert l % (A_STEP_BLOCKS * A_BLOCK) == 0 and l >= 3 * A_SPAN and lc % 256 == 0
    assert l % (B_STEP_GROUPS * B_GROUP * GRID_W) == 0 and l // GRID_W >= 3 * B_GROUP

    mod_rows = 8 * (-(-(bn + 1) // 8))
    c_all = jnp.zeros((mod_rows, d), F32).at[:bn].set(c).at[bn].set(c_ctx)
    mod = _modulation(c_all, ada_w, ada_b)

    a_order = (0, 2, 1, 3)
    a_kv = tuple(hh // (A_HEADS // A_KV_HEADS) for hh in a_order)
    heads_a = lambda t, axis: [lax.slice_in_dim(t, hh * HEAD_DIM, (hh + 1) * HEAD_DIM, axis=axis) for hh in a_order]
    qa_w = A_HEADS * HEAD_DIM
    w_in_b = w_in.astype(BF16)
    w_out_b = jnp.concatenate(heads_a(w_out, 1) + [w_out[:, qa_w:]], axis=1).astype(BF16)
    sink_p = LOG2E * jnp.stack([sink_a[:, hh] for hh in a_order], axis=1)
    wg_b, wu_b, wd_b = ffn_w_gate.astype(BF16), ffn_w_up.astype(BF16), ffn_w_down.astype(BF16)

    lane = np.arange(256)
    e_heads = jnp.asarray((lane[:, None] // HEAD_DIM) == (lane[None, :] // HEAD_DIM), BF16)
    tile4 = lambda g: jnp.tile(g.reshape(1, HEAD_DIM), (1, 256 // HEAD_DIM))
    rope_lat = _rope_tables(l)

    fh_lat, fh_ctx = min(FREQ_TILE, l // 2), min(FREQ_TILE, lc // 2)
    f_lat, g_lat = _dft_matrices(l // 2, fh_lat)
    f_ctx, g_ctx = _dft_matrices(lc // 2, fh_ctx)
    hc_lat = _hyena_consts(l, cw)
    hc_ctx = _hyena_consts(lc, cw)

    xc = ctx
    for i in range(depth):
        last = i == depth - 1
        m = mod[i]
        part = lambda rows, k: rows[:, None, k * d:(k + 1) * d]
        lat = [part(m[:bn], k) for k in range(6)]
        cx = [part(m[bn:bn + 1], k) for k in range(6)]
        g1n = norm1_g[i].reshape(1, d)
        g2n = norm2_g[i].reshape(1, d)
        gqa, gka, gqb, gkb = tile4(qnorm_a[i]), tile4(knorm_a[i]), tile4(qnorm_b[i]), tile4(knorm_b[i])

        qa, ka, va, qb, kb, vb, hy = _proj_in(x, lat[0], lat[1], g1n, w_in_b, i, gqa, gka, gqb, gkb, e_heads,
                                              rope_lat)
        ctx_out = _proj_in(xc.reshape(1, bn * lc, d), cx[0], cx[1], g1n, w_in_b, i, gqa, gka, gqb, gkb, e_heads,
                           None)
        qa_c, ka_c, va_c, qb_c, kb_c, vb_c, hy_c = [t.reshape(bn, lc, -1) for t in ctx_out]
        hy_args = (conv_w[i], conv_b[i].reshape(1, -1),
                   *_pack_filter_mlp(filt_w1[i], filt_b1[i], filt_freq[i], filt_w2[i], filt_b2[i]),
                   filt_w3[i], hyena_bias[i])

        sink_row = jnp.repeat(sink_p[i], A_BLOCK).reshape(1, A_HEADS * A_BLOCK)
        out_a = _attn_a(qa, ka, va, ka_c, va_c, sink_row)
        out_b = _attn_b(qb, kb, vb, kb_c, vb_c, _rpb_tables(rpb_b[i]))
        out_c = _hyena(hy, hc_lat, f_lat, g_lat, fh_lat, *hy_args)
        x = _mix_ffn(x, out_a, out_b, out_c, w_out_b, lat[2], g2n, lat[3], lat[4], lat[5], wg_b, wu_b, wd_b, i)

        if not last:
            oa_c = _dense_attn(qa_c, ka_c, va_c, sink_p[i], kv_of=a_kv, use_sink=True)
            ob_c = _dense_attn(qb_c, kb_c, vb_c, sink_p[i], kv_of=tuple(range(B_HEADS)), use_sink=False)
            oc_c = _hyena(hy_c, hc_ctx, f_ctx, g_ctx, fh_ctx, *hy_args)
            flat = lambda t: t.reshape(1, bn * lc, -1)
            xc = _mix_ffn(flat(xc), flat(oa_c), flat(ob_c), flat(oc_c), w_out_b, cx[2], g2n, cx[3], cx[4], cx[5],
                          wg_b, wu_b, wd_b, i).reshape(bn, lc, d)
    return x
```

```python
import functools
import math

import jax
import jax.numpy as jnp
import numpy as np
from jax import lax
from jax.experimental import pallas as pl
from jax.experimental.pallas import tpu as pltpu

F32 = jnp.float32
BF16 = jnp.bfloat16

GRID_W = 64
HEAD_DIM = 64
A_HEADS = 4
A_KV_HEADS = 2
A_BLOCK = 128
B_HEADS = 4
NA_ROWS = 8
NA_COLS = 16
C_ORDER = 2
C_DIRS = 2
C_FILTER_WIDTH = 64
C_BANDS = 16
ROPE_BASE = 10000.0
EPS = 1e-6
NEG_INF = -1e30
LOG2E = math.log2(math.e)
HYENA_MIN_DECAY = math.log(1e-2) / 1.5
HYENA_MAX_DECAY = math.log(1e-2) / 0.3

V7X_VMEM_BYTES = 64 * 1024 * 1024
VMEM_LIMIT = V7X_VMEM_BYTES - 8 * 1024 * 1024
LANES = 128
TOKEN_TILE = 512
FREQ_TILE = 512


def _params(*sem):
    return pltpu.CompilerParams(dimension_semantics=sem, vmem_limit_bytes=VMEM_LIMIT)


def _dot(a, b):
    return jnp.dot(a, b, preferred_element_type=F32)


def _dot_nt(a, b):
    return lax.dot_general(a, b, (((1,), (1,)), ((), ())), preferred_element_type=F32)


def _dot_hp(a, b):
    ah, al = _split_bf16(a)
    bh, bl = _split_bf16(b)
    return _dot(ah, bh) + (_dot(ah, bl) + _dot(al, bh))


def _split_bf16(v):
    hi = v.astype(BF16)
    lo = (v - hi.astype(F32)).astype(BF16)
    return hi, lo


def _layer_of(stack, layer):
    return pl.BlockSpec((None,) + stack.shape[1:], lambda *_: (layer, 0, 0), pipeline_mode=pl.Buffered(1))


def _resident(shape):
    nd = len(shape)
    return pl.BlockSpec(shape, lambda *_: (0,) * nd, pipeline_mode=pl.Buffered(1))


def _mod_kernel(c_ref, w_ref, b_ref, o_ref):
    cv = c_ref[...]
    sc = (cv * jax.nn.sigmoid(cv)).astype(BF16)
    o_ref[...] = _dot(sc, w_ref[...].astype(BF16)) + b_ref[...]


def _modulation(c_all, ada_w, ada_b):
    depth, d, n = ada_w.shape
    rows = c_all.shape[0]
    tn = 1536
    return pl.pallas_call(
        _mod_kernel,
        out_shape=jax.ShapeDtypeStruct((depth, rows, n), F32),
        grid=(depth, n // tn),
        in_specs=[
            pl.BlockSpec((rows, d), lambda i, j: (0, 0)),
            pl.BlockSpec((None, d, tn), lambda i, j: (i, 0, j)),
            pl.BlockSpec((None, 1, tn), lambda i, j: (i, 0, j)),
        ],
        out_specs=pl.BlockSpec((None, rows, tn), lambda i, j: (i, 0, j)),
        compiler_params=_params("arbitrary", "arbitrary"),
        name="modulation",
    )(c_all, ada_w, ada_b.reshape(depth, 1, n))


def _rms_mod(x, g, shift, scale):
    y = x * lax.rsqrt(jnp.mean(x * x, axis=-1, keepdims=True) + EPS)
    return (y * g) * (1 + scale) + shift


def _head_norm(h, g, e):
    ss = _dot((h * h).astype(BF16), e)
    return (h * lax.rsqrt(ss * (1.0 / HEAD_DIM) + EPS)) * g


def _rope(t, cos, sin_a, sin_b):
    outs = []
    for c in range(t.shape[1] // LANES):
        tc = t[:, c * LANES:(c + 1) * LANES]
        outs.append(tc * cos + pltpu.roll(tc, LANES - 16, 1) * sin_a + pltpu.roll(tc, 16, 1) * sin_b)
    return outs[0] if len(outs) == 1 else jnp.concatenate(outs, axis=1)


PROJ_ROWS = 1024
PROJ_SPLIT = 2


def _proj_in_kernel(x_ref, shift_ref, scale_ref, g_ref, w_ref, gqa_ref, gka_ref, gqb_ref, gkb_ref, e_ref, *rest,
                    rope):
    if rope:
        cos_ref, sa_ref, sb_ref = rest[:3]
        rest = rest[3:]
    qa_ref, ka_ref, va_ref, qb_ref, kb_ref, vb_ref, hy_ref = rest
    qscale = LOG2E * HEAD_DIM ** -0.5
    hm = x_ref.shape[0] // PROJ_SPLIT
    for part in range(PROJ_SPLIT):
        rows = slice(part * hm, (part + 1) * hm)
        xb = _rms_mod(x_ref[rows, :], g_ref[...], shift_ref[...], scale_ref[...]).astype(BF16)

        h_all = _dot(xb, w_ref[...])

        def proj(lo, hi):
            return h_all[:, lo:hi]

        def maybe_rope(t):
            return _rope(t, cos_ref[rows, :], sa_ref[rows, :], sb_ref[rows, :]) if rope else t

        def store_v(ref, v):
            if rope:
                for i in range(hm // LANES):
                    ref[part * (hm // LANES) + i] = v[i * LANES:(i + 1) * LANES, :].T.astype(BF16)
            else:
                ref[rows, :] = v.astype(BF16)

        qa = maybe_rope(_head_norm(proj(0, 256), gqa_ref[...], e_ref[...])) * qscale
        t0, t1 = qa[:, :LANES], qa[:, LANES:]
        lo = lax.broadcasted_iota(jnp.int32, t0.shape, 1) < HEAD_DIM
        qa_ref[rows, :LANES] = jnp.where(lo, t0, pltpu.roll(t1, HEAD_DIM, 1)).astype(BF16)
        qa_ref[rows, LANES:] = jnp.where(lo, pltpu.roll(t0, HEAD_DIM, 1), t1).astype(BF16)
        kva = proj(256, 512)
        ka = maybe_rope(_head_norm(kva[:, :LANES], gka_ref[:, :128], e_ref[:128, :128]))
        ka_ref[rows, :] = ka.astype(BF16)
        store_v(va_ref, kva[:, LANES:])
        qb = _head_norm(proj(512, 768), gqb_ref[...], e_ref[...])
        qb_ref[rows, :] = (qb * qscale).astype(BF16)
        kb_ref[rows, :] = _head_norm(proj(768, 1024), gkb_ref[...], e_ref[...]).astype(BF16)
        store_v(vb_ref, proj(1024, 1280))
        hy_ref[rows, :] = proj(1280, 2816).astype(BF16)


def _proj_in(x, shift, scale, g, w, layer, gqa, gka, gqb, gkb, e, rope_tabs):
    bn, l, d = x.shape
    n = w.shape[2]
    rope = rope_tabs is not None
    tm = min(PROJ_ROWS, l)
    row = lambda width: pl.BlockSpec((None, 1, width), lambda b, t: (b, 0, 0))
    const = lambda shape: pl.BlockSpec(shape, lambda b, t: (0,) * len(shape))
    tok = lambda width: pl.BlockSpec((None, tm, width), lambda b, t: (b, t, 0))
    tab = pl.BlockSpec((tm, LANES), lambda b, t: (t, 0))
    widths = (256, 128, 128, 256, 256, 256, n - 1280)
    shapes = [(bn, l, wd) for wd in widths]
    specs = [tok(wd) for wd in widths]
    if rope:
        for i in (2, 5):
            shapes[i] = (bn, l // LANES, widths[i], LANES)
            specs[i] = pl.BlockSpec((None, tm // LANES, widths[i], LANES), lambda b, t: (b, t, 0, 0))
    return pl.pallas_call(
        functools.partial(_proj_in_kernel, rope=rope),
        out_shape=[jax.ShapeDtypeStruct(sh, BF16) for sh in shapes],
        grid=(bn, l // tm),
        in_specs=[tok(d), row(d), row(d), const((1, d)), _layer_of(w, layer),
                  const((1, 256)), const((1, 256)), const((1, 256)), const((1, 256)), const((256, 256))]
                 + ([tab, tab, tab] if rope else []),
        out_specs=specs,
        compiler_params=_params("parallel", "parallel"),
        name="proj_in_rope" if rope else "proj_in",
    )(x, shift, scale, g, w, gqa, gka, gqb, gkb, e, *(rope_tabs or ()))


def _softmax_pv(s_list, v_list, sink):
    m = s_list[0].max(axis=-1, keepdims=True)
    for s in s_list[1:]:
        m = jnp.maximum(m, s.max(axis=-1, keepdims=True))
    if sink is not None:
        m = jnp.maximum(m, sink)
    den = None
    out = None
    for s, v in zip(s_list, v_list):
        p = jnp.exp2(s - m)
        ps = p.sum(axis=-1, keepdims=True)
        den = ps if den is None else den + ps
        o = _dot(p.astype(BF16), v)
        out = o if out is None else out + o
    if sink is not None:
        den = den + jnp.exp2(sink - m)
    return out * (1.0 / den)


def _stack_heads(q):
    lo = lax.broadcasted_iota(jnp.int32, q.shape, 1) < HEAD_DIM
    zero = jnp.zeros_like(q)
    return jnp.concatenate([jnp.where(lo, q, zero), jnp.where(lo, zero, q)], axis=0)


def _unstack_heads(o):
    m = o.shape[0] // 2
    lo = lax.broadcasted_iota(jnp.int32, (m, LANES), 1) < HEAD_DIM
    return jnp.where(lo, o[:m], o[m:])


SOFTMAX_CHUNK = 32


def _softmax_keys(s_ref, p_ref, nloc, add_loc, sink):
    nk = s_ref.shape[0]
    ch = SOFTMAX_CHUNK
    macc = None
    for r0 in range(0, nk, ch):
        s = s_ref[r0:r0 + ch, :]
        if r0 < nloc:
            s = s + add_loc(r0)
            s_ref[r0:r0 + ch, :] = s
        macc = s if macc is None else jnp.maximum(macc, s)
    m = macc.max(axis=0, keepdims=True)
    if sink is not None:
        m = jnp.maximum(m, sink)
    sacc = None
    for r0 in range(0, nk, ch):
        p = jnp.exp2(s_ref[r0:r0 + ch, :] - m)
        sacc = p if sacc is None else sacc + p
        p_ref[r0:r0 + ch, :] = p.astype(BF16)
    den = sacc.sum(axis=0, keepdims=True)
    if sink is not None:
        den = den + jnp.exp2(sink - m)
    return 1.0 / den


A_STEP_BLOCKS = 16
A_SPAN = 3 * A_BLOCK


def _attn_a_kernel(q_ref, k_ref, vt_ref, kx_ref, vxt_ref, mask_ref, sink_ref, o_ref, s_ref, p_ref):
    l = k_ref.shape[0]
    nb = l // A_BLOCK
    hd = HEAD_DIM
    def window(u):
        n = pl.program_id(1) * A_STEP_BLOCKS + u
        tile0 = jnp.clip(n - 1, 0, nb - 3)
        pat = jnp.where(n == 0, 0, jnp.where(n == nb - 1, 2, 1))
        return tile0, pat

    def scores(u):
        tile0, _ = window(u)
        start = pl.multiple_of(tile0 * A_BLOCK, A_BLOCK)
        q = q_ref[u * A_BLOCK:(u + 1) * A_BLOCK, :]
        qs = jnp.concatenate([_stack_heads(q[:, :LANES]), _stack_heads(q[:, LANES:])], axis=0)
        s_ref[u % 2] = _dot_nt(jnp.concatenate([k_ref[pl.ds(start, A_SPAN), :], kx_ref[...]], axis=0), qs)

    def output(u, r):
        tile0, _ = window(u)
        vt = jnp.concatenate([vt_ref[tile0 + i] for i in range(3)] + [vxt_ref[...]], axis=1)
        ot = _dot(vt, p_ref[u % 2]) * r
        ot = jnp.concatenate([ot[(i % 2) * hd:(i % 2 + 1) * hd, i * A_BLOCK:(i + 1) * A_BLOCK]
                              for i in range(A_HEADS)], axis=0)
        o_ref[u * A_BLOCK:(u + 1) * A_BLOCK, :] = ot.T.astype(o_ref.dtype)

    scores(0)
    pending = None
    for u in range(A_STEP_BLOCKS):
        if u + 1 < A_STEP_BLOCKS:
            scores(u + 1)
        _, pat = window(u)
        r = _softmax_keys(s_ref.at[u % 2], p_ref.at[u % 2], A_SPAN,
                          lambda r0: mask_ref[pat, r0:r0 + SOFTMAX_CHUNK, :], sink_ref[...])
        if pending is not None:
            output(*pending)
        pending = (u, r)
    output(*pending)


def _attn_a_mask():
    i = np.arange(A_BLOCK)[None, :]
    j = np.arange(A_SPAN)[:, None]
    offs = (0, A_BLOCK, 2 * A_BLOCK)
    m = np.stack([np.where(np.abs(j - i - o) <= A_BLOCK, 0.0, -np.inf) for o in offs])
    return jnp.asarray(np.tile(m, (1, 1, A_HEADS)), F32)


def _attn_a(q, k, vt, kx, vx, sink_row):
    bn, l, qw = q.shape
    lc = kx.shape[1]
    kvw = k.shape[2]
    nb = l // A_BLOCK
    qs = A_STEP_BLOCKS * A_BLOCK
    mask = _attn_a_mask()
    vxt = jnp.swapaxes(vx, 1, 2)
    seq = pl.BlockSpec((None, l, kvw), lambda b, s: (b, 0, 0))
    qblk = pl.BlockSpec((None, qs, qw), lambda b, s: (b, s, 0))
    nq = A_HEADS * A_BLOCK
    return pl.pallas_call(
        _attn_a_kernel,
        out_shape=jax.ShapeDtypeStruct((bn, l, qw), BF16),
        grid=(bn, l // qs),
        in_specs=[qblk, seq, pl.BlockSpec((None, nb, kvw, A_BLOCK), lambda b, s: (b, 0, 0, 0)),
                  pl.BlockSpec((None, lc, kvw), lambda b, s: (b, 0, 0)),
                  pl.BlockSpec((None, kvw, lc), lambda b, s: (b, 0, 0)),
                  _resident(mask.shape), _resident(sink_row.shape)],
        out_specs=qblk,
        scratch_shapes=[pltpu.VMEM((2, A_SPAN + lc, nq), F32), pltpu.VMEM((2, A_SPAN + lc, nq), BF16)],
        compiler_params=_params("parallel", "arbitrary"),
        name="window_attn",
    )(q, k, vt, kx, vxt, mask, sink_row)


def _dense_attn_kernel(sink_ref, q_ref, k_ref, v_ref, o_ref, *, kv_of, use_sink):
    for h, kv in enumerate(kv_of):
        hs = slice(h * HEAD_DIM, (h + 1) * HEAD_DIM)
        ks = slice(kv * HEAD_DIM, (kv + 1) * HEAD_DIM)
        s = _dot_nt(q_ref[:, hs], k_ref[:, ks])
        o = _softmax_pv([s], [v_ref[:, ks]], sink_ref[h] if use_sink else None)
        o_ref[:, hs] = o.astype(o_ref.dtype)


def _dense_attn(q, k, v, sink, *, kv_of, use_sink):
    bn, l, qw = q.shape
    kvw = k.shape[2]
    full = lambda wd: pl.BlockSpec((None, l, wd), lambda b: (b, 0, 0))
    return pl.pallas_call(
        functools.partial(_dense_attn_kernel, kv_of=kv_of, use_sink=use_sink),
        out_shape=jax.ShapeDtypeStruct((bn, l, qw), BF16),
        grid=(bn,),
        in_specs=[pl.BlockSpec(memory_space=pltpu.SMEM), full(qw), full(kvw), full(kvw)],
        out_specs=full(qw),
        compiler_params=_params("parallel"),
        name="ctx_attn_sink" if use_sink else "ctx_attn",
    )(sink, q, k, v)


def _rpb_kernel(r_ref, oh_ref, ok_ref, o_ref):
    r = r_ref[...]
    b1 = r.astype(BF16)
    r2 = r - b1.astype(F32)
    b2 = r2.astype(BF16)
    b3 = (r2 - b2.astype(F32)).astype(BF16)
    oh = oh_ref[...]
    bias = (_dot(b1, oh) + _dot(b2, oh)) + _dot(b3, oh)
    o_ref[...] = jnp.where(ok_ref[...] > 0.5, bias * LOG2E, -jnp.inf)


def _rpb_slots():
    a = np.arange(B_GROUP)[:, None]
    kr = np.arange(B_SLAB)[None, :]
    dr = np.stack([kr - a + NA_ROWS - 1, kr - a + NA_ROWS // 2 - 1, kr - a + (B_GROUP + NA_ROWS - 1 - B_SLAB)])
    lo = np.stack([0 * a + 0 * kr, a + 0 * kr, 0 * a + (B_SLAB - NA_ROWS) + 0 * kr])
    valid = (kr[None] >= lo) & (kr[None] < lo + NA_ROWS)
    return np.where(valid, dr, -1)


def _rpb_assemble_kernel(t_ref, o_ref, *, n_dr):
    slots = _rpb_slots()
    w = GRID_W
    blank = jnp.full((w, w), -jnp.inf, F32)
    for pat in range(3):
        for t in range(o_ref.shape[1]):
            for kr in range(B_SLAB):
                for hh in range(2):
                    for a in range(B_GROUP):
                        dr = int(slots[pat, a, kr])
                        tile = t_ref[(2 * t + hh) * n_dr + dr] if dr >= 0 else blank
                        c0 = (hh * B_GROUP + a) * w
                        o_ref[pat, t, kr * w:(kr + 1) * w, c0:c0 + w] = tile


def _rpb_tables(rpb):
    h, nr, nc = rpb.shape
    col = np.arange(GRID_W)
    dc = np.clip(col[:, None] - col[None, :], 1 - NA_COLS, NA_COLS - 1) + NA_COLS - 1
    onehot = (dc.reshape(1, -1) == np.arange(nc)[:, None]).astype(np.float32)
    onehot = np.concatenate([onehot, np.zeros((32 - nc, GRID_W * GRID_W), np.float32)], axis=0)
    col_start = np.clip(col - NA_COLS // 2, 0, GRID_W - NA_COLS)
    col_ok = (col[:, None] >= col_start[None, :]) & (col[:, None] < col_start[None, :] + NA_COLS)
    rows = 64
    r2 = jnp.zeros((rows, 32), F32).at[:h * nr, :nc].set(rpb.reshape(h * nr, nc))
    tiles = pl.pallas_call(
        _rpb_kernel,
        out_shape=jax.ShapeDtypeStruct((rows, GRID_W * GRID_W), F32),
        name="rpb_table",
    )(r2, jnp.asarray(onehot, BF16), jnp.asarray(col_ok.reshape(1, -1), F32))
    tiles = tiles.reshape(rows, GRID_W, GRID_W)
    return pl.pallas_call(
        functools.partial(_rpb_assemble_kernel, n_dr=nr),
        out_shape=jax.ShapeDtypeStruct((3, h // 2, B_SLAB * GRID_W, 2 * B_GROUP * GRID_W), F32),
        compiler_params=_params(),
        name="rpb_assemble",
    )(tiles)


B_GROUP = 4
B_STEP_GROUPS = 4
B_SLAB = 12


def _attn_b_kernel(q_ref, k_ref, vt_ref, kx_ref, vxt_ref, tbl_ref, o_ref, s_ref, p_ref):
    ng = pl.num_programs(1) * B_STEP_GROUPS
    rows = k_ref.shape[0] // GRID_W
    nloc = B_SLAB * GRID_W
    gq = B_GROUP * GRID_W
    units = [(u, t) for u in range(B_STEP_GROUPS) for t in range(B_HEADS // 2)]

    def slab(u):
        g = pl.program_id(1) * B_STEP_GROUPS + u
        base = jnp.clip(g * B_GROUP - NA_ROWS // 2, 0, rows - B_SLAB)
        pat = jnp.where(g == 0, 0, jnp.where(g == ng - 1, 2, 1))
        return base, pat

    def scores(u, t):
        base, _ = slab(u)
        start = pl.multiple_of(base * GRID_W, LANES)
        ts = slice(t * LANES, (t + 1) * LANES)
        qs = _stack_heads(q_ref[u * gq:(u + 1) * gq, ts])
        s_ref[u, t] = _dot_nt(jnp.concatenate([k_ref[pl.ds(start, nloc), ts], kx_ref[:, ts]], axis=0), qs)

    def output(u, t, r):
        base, _ = slab(u)
        tile0 = base // (LANES // GRID_W)
        ts = slice(t * LANES, (t + 1) * LANES)
        vt = jnp.concatenate([vt_ref[tile0 + j, ts, :] for j in range(nloc // LANES)] + [vxt_ref[ts, :]],
                             axis=1)
        ot = _dot(vt, p_ref[u, t]) * r
        ot = jnp.concatenate([ot[:HEAD_DIM, :gq], ot[HEAD_DIM:, gq:]], axis=0)
        o_ref[u * gq:(u + 1) * gq, ts] = ot.T.astype(o_ref.dtype)

    scores(*units[0])
    pending = None
    for i, (u, t) in enumerate(units):
        if i + 1 < len(units):
            scores(*units[i + 1])
        _, pat = slab(u)
        r = _softmax_keys(s_ref.at[u, t], p_ref.at[u, t], nloc,
                          lambda r0: tbl_ref[pat, t, r0:r0 + SOFTMAX_CHUNK, :], None)
        if pending is not None:
            output(*pending)
        pending = (u, t, r)
    output(*pending)


def _attn_b(q, k, vt, kx, vx, tbl):
    bn, l, w = q.shape
    lc = kx.shape[1]
    gq = B_GROUP * GRID_W
    sq = B_STEP_GROUPS * gq
    nk = B_SLAB * GRID_W + lc
    vxt = jnp.swapaxes(vx, 1, 2)
    seq = pl.BlockSpec((None, l, w), lambda b, g: (b, 0, 0))
    qblk = pl.BlockSpec((None, sq, w), lambda b, g: (b, g, 0))
    return pl.pallas_call(
        _attn_b_kernel,
        out_shape=jax.ShapeDtypeStruct((bn, l, w), BF16),
        grid=(bn, l // sq),
        in_specs=[qblk, seq, pl.BlockSpec((None, l // LANES, w, LANES), lambda b, g: (b, 0, 0, 0)),
                  pl.BlockSpec((None, lc, w), lambda b, g: (b, 0, 0)),
                  pl.BlockSpec((None, w, lc), lambda b, g: (b, 0, 0)),
                  _resident(tbl.shape)],
        out_specs=qblk,
        scratch_shapes=[pltpu.VMEM((B_STEP_GROUPS, B_HEADS // 2, nk, 2 * gq), F32),
                        pltpu.VMEM((B_STEP_GROUPS, B_HEADS // 2, nk, 2 * gq), BF16)],
        compiler_params=_params("parallel", "arbitrary"),
        name="nbr_attn",
    )(q, k, vt, kx, vxt, tbl)


def _dft_matrices(l, fh):
    k = jnp.arange(l, dtype=jnp.int32)[:, None]
    n = jnp.arange(l, dtype=jnp.int32)[None, :]
    ang = (((2 * k + 1) * n) % (4 * l)).astype(F32) * (math.pi / (2 * l))
    fre = jnp.cos(ang).reshape(l // fh, fh, l)
    fim = (-jnp.sin(ang)).reshape(l // fh, fh, l)
    f = jnp.concatenate([fre, fim], axis=1).reshape(2 * l, l)
    g = f.T * (1.0 / l)
    return f.astype(BF16), g.astype(BF16)


def _filter_kernel(z_ref, t_ref, w1_ref, b1_ref, fr_ref, w2_ref, b2_ref, w3_ref, dl_ref,
                   ah_ref, al_ref, sh_ref, sl_ref, hid_ref):
    @pl.when(pl.program_id(0) == 0)
    def _():
        fr = fr_ref[...]
        hid = jnp.sin(fr * (_dot_hp(z_ref[...], w1_ref[...]) + b1_ref[...]))
        hid_ref[...] = jnp.sin(fr * (_dot_hp(hid, w2_ref[...]) + b2_ref[...]))

    cw = dl_ref.shape[1]
    p = t_ref.shape[0] // 3
    hid = hid_ref[...]
    w3 = w3_ref[...]
    none = jnp.zeros_like(w3)
    taps = jnp.concatenate([_dot_hp(hid, jnp.concatenate([w3, none], axis=0)),
                            _dot_hp(hid, jnp.concatenate([none, w3], axis=0))], axis=0)
    decay = jnp.exp(-t_ref[...] * dl_ref[...])
    kf = taps[:, :cw] * decay
    kb = taps[:, cw:] * decay
    kf0, kf1, kfr = kf[:p], kf[p:2 * p], kf[2 * p:]
    kb0, kb1, kbr = kb[:p], kb[p:2 * p], kb[2 * p:]
    first = lax.broadcasted_iota(jnp.int32, (p, cw), 0) == 0
    drop0 = lambda v: jnp.where(first, 0.0, v)
    kb0 = drop0(kb0)
    colsum = lambda v: jnp.sum(jnp.abs(v), axis=0, keepdims=True)
    inv = 1.0 / (colsum(kf0) + colsum(kf1) + colsum(kb0) + colsum(kb1))
    pairs = ((kf0, kb0), (kf1, drop0(kfr)), (kbr, drop0(kb1)))
    for d, (cp, cm) in enumerate(pairs):
        cols = slice(d * cw, (d + 1) * cw)
        ah_ref[:, cols], al_ref[:, cols] = _split_bf16((cp + cm) * inv)
        sh_ref[:, cols], sl_ref[:, cols] = _split_bf16((cp - cm) * inv)


def _filters(z, t, w1, b1, freq, w2, b2, w3, deltas):
    p = t.shape[0] // 3
    cw = deltas.shape[1]
    fw = w2.shape[0]
    const = lambda shape: pl.BlockSpec(shape, lambda o: (0,) * len(shape))
    out = jax.ShapeDtypeStruct((p, C_ORDER * 3 * cw), BF16)
    oblk = pl.BlockSpec((p, 3 * cw), lambda o: (0, o))
    return pl.pallas_call(
        _filter_kernel,
        out_shape=[out] * 4,
        grid=(C_ORDER,),
        in_specs=[const(z.shape), const(t.shape), const(w1.shape), const((1, fw)), const((1, fw)),
                  const((fw, fw)), const((1, fw)), pl.BlockSpec((w3.shape[0], C_DIRS * cw), lambda o: (0, o)),
                  const((1, cw))],
        out_specs=[oblk] * 4,
        scratch_shapes=[pltpu.VMEM(z.shape, F32)],
        compiler_params=_params("arbitrary"),
        name="hyena_filter",
    )(z, t, w1, b1, freq, w2, b2, w3, deltas)


def _spectrum_kernel(f_ref, ah_ref, al_ref, sh_ref, sl_ref, kre_ref, kim_ref):
    fh = kre_ref.shape[0]
    fre = f_ref[:fh, :]
    fim = f_ref[fh:, :]
    kre_ref[...] = _dot(fre, ah_ref[...]) + _dot(fre, al_ref[...])
    kim_ref[...] = _dot(fim, sh_ref[...]) + _dot(fim, sl_ref[...])


def _spectrum(f, ah, al, sh, sl, fh):
    p, n = ah.shape
    tn = n // C_ORDER
    taps = pl.BlockSpec((p, tn), lambda o, j: (0, o))
    out = jax.ShapeDtypeStruct((p, n), F32)
    oblk = pl.BlockSpec((fh, tn), lambda o, j: (j, o))
    return pl.pallas_call(
        _spectrum_kernel,
        out_shape=[out, out],
        grid=(C_ORDER, p // fh),
        in_specs=[pl.BlockSpec((2 * fh, p), lambda o, j: (j, 0)), taps, taps, taps, taps],
        out_specs=[oblk, oblk],
        compiler_params=_params("arbitrary", "arbitrary"),
        name="hyena_spectrum",
    )(f, ah, al, sh, sl)


def _short_conv(u, w_ref, b_ref):
    n = u.shape[0]
    row = lax.broadcasted_iota(jnp.int32, u.shape, 0)
    prev = jnp.where(row == 0, 0.0, pltpu.roll(u, 1, 0))
    nxt = jnp.where(row == n - 1, 0.0, pltpu.roll(u, n - 1, 0))
    return prev * w_ref[0:1, :] + u * w_ref[1:2, :] + nxt * w_ref[2:3, :] + b_ref[...]


def _short_conv_wrap(u, w_ref, b_ref):
    n = u.shape[0]
    return (pltpu.roll(u, 1, 0) * w_ref[0:1, :] + u * w_ref[1:2, :] + pltpu.roll(u, n - 1, 0) * w_ref[2:3, :]
            + b_ref[...])


LONGCONV_ROWS = 1024
CONV_CHUNKS = 2


def _longconv_kernel(u_ref, g_ref, cwu_ref, cbu_ref, cwg_ref, cbg_ref, f_ref, gm_ref, kre_ref, kim_ref, d_ref,
                     o_ref, ub_ref, acc_ref, *, conv_u, n_steps):
    j = pl.program_id(1)
    fh = kre_ref.shape[0]
    nb, p, cw = u_ref.shape[0], ub_ref.shape[2], d_ref.shape[1]
    cc = cw // CONV_CHUNKS
    l = 2 * p
    edge = 16

    def edge_conv(ref, w_ref, b_ref, cs):
        head = _short_conv(ref[0:2 * edge, cs].astype(F32), w_ref.at[:, cs], b_ref.at[:, cs])[:edge]
        tail = _short_conv(ref[l - 2 * edge:l, cs].astype(F32), w_ref.at[:, cs], b_ref.at[:, cs])[edge:]
        return head, tail

    def load_u(ib, c, cs):
        ub = ub_ref.at[ib, c]
        if conv_u:
            u = _short_conv_wrap(u_ref[ib, :, cs].astype(F32), cwu_ref.at[:, cs], cbu_ref.at[:, cs]).astype(BF16)
        else:
            u = u_ref[ib, :, cs]
        ub[:, :cc] = u[:p]
        ub[:, cc:] = u[p:]
        if conv_u:
            head, tail = edge_conv(u_ref.at[ib], cwu_ref, cbu_ref, cs)
            ub[0:edge, :cc] = head.astype(BF16)
            ub[p - edge:p, cc:] = tail.astype(BF16)

    def forward(ib, c):
        return _dot(f_ref[...], ub_ref[ib, c])

    def inverse(c, spec):
        u0r, u1r, u0i, u1i = spec[:fh, :cc], spec[:fh, cc:], spec[fh:, :cc], spec[fh:, cc:]
        tap = lambda ref, d: ref[:, d * cw + c * cc:d * cw + (c + 1) * cc]
        c0r, c1r, cmr = (tap(kre_ref, d) for d in range(3))
        c0i, c1i, cmi = (tap(kim_ref, d) for d in range(3))
        y0r = (c0r * u0r - c0i * u0i) + (cmr * u1r - cmi * u1i)
        y0i = (c0r * u0i + c0i * u0r) + (cmr * u1i + cmi * u1r)
        y1r = (c1r * u0r - c1i * u0i) + (c0r * u1r - c0i * u1i)
        y1i = (c1r * u0i + c1i * u0r) + (c0r * u1i + c0i * u1r)
        y = jnp.concatenate([jnp.concatenate([y0r, y1r], axis=1), jnp.concatenate([y0i, y1i], axis=1)], axis=0)
        return _dot(gm_ref[...], y.astype(BF16))

    def gated_out(ib, c, cs, conv):
        gate = _short_conv_wrap(g_ref[ib, :, cs].astype(F32), cwg_ref.at[:, cs], cbg_ref.at[:, cs])
        ub = ub_ref.at[ib, c]
        y = jnp.concatenate([conv[:, :cc], conv[:, cc:]], axis=0)
        y = y + jnp.concatenate([ub[:, :cc], ub[:, cc:]], axis=0).astype(F32) * d_ref[:, cs]
        o_ref[ib, :, cs] = (gate * y).astype(o_ref.dtype)
        head, tail = edge_conv(g_ref.at[ib], cwg_ref, cbg_ref, cs)
        o_ref[ib, 0:edge, cs] = (head * y[:edge]).astype(o_ref.dtype)
        o_ref[ib, l - edge:l, cs] = (tail * y[l - edge:]).astype(o_ref.dtype)

    def step(first, last):
        chunks = [(ib, c, slice(c * cc, (c + 1) * cc)) for ib in range(nb) for c in range(CONV_CHUNKS)]

        def start(ib, c, cs):
            if first:
                load_u(ib, c, cs)
            return forward(ib, c)

        spec = start(*chunks[0])
        pending = None
        for i, (ib, c, cs) in enumerate(chunks):
            nxt = start(*chunks[i + 1]) if i + 1 < len(chunks) else None
            conv = inverse(c, spec)
            spec = nxt
            if not first:
                conv = acc_ref[ib, c] + conv
            if last:
                if pending is not None:
                    gated_out(*pending)
                pending = (ib, c, cs, conv)
            else:
                acc_ref[ib, c] = conv
        if last:
            gated_out(*pending)

    if n_steps == 1:
        step(True, True)
    else:
        pl.when(j == 0)(lambda: step(True, False))
        pl.when(j == n_steps - 1)(lambda: step(False, True))
        if n_steps > 2:
            pl.when(jnp.logical_and(j > 0, j < n_steps - 1))(lambda: step(False, False))


def _longconv(u_arr, u_blk, g_arr, g_blk, conv_w, conv_b, f, gm, kre, kim, order, d, *, conv_u, fh):
    bn, l, _ = u_arr.shape
    p = l // 2
    cw = d.shape[1]
    ub = u_blk if conv_u else 0
    nb = math.gcd(bn, max(1, LONGCONV_ROWS // l))
    tok = lambda blk: pl.BlockSpec((nb, l, cw), lambda b, j: (b, 0, blk))
    cpar = lambda rows, blk: pl.BlockSpec((rows, cw), lambda b, j: (0, blk))
    ktab = pl.BlockSpec((fh, 3 * cw), lambda b, j: (j, order))
    return pl.pallas_call(
        functools.partial(_longconv_kernel, conv_u=conv_u, n_steps=p // fh),
        out_shape=jax.ShapeDtypeStruct((bn, l, cw), BF16),
        grid=(bn // nb, p // fh),
        in_specs=[tok(u_blk), tok(g_blk), cpar(3, ub), cpar(1, ub), cpar(3, g_blk), cpar(1, g_blk),
                  pl.BlockSpec((2 * fh, p), lambda b, j: (j, 0)),
                  pl.BlockSpec((p, 2 * fh), lambda b, j: (0, j)),
                  ktab, ktab, pl.BlockSpec((1, cw), lambda b, j: (0, 0))],
        out_specs=tok(0),
        scratch_shapes=[pltpu.VMEM((nb, CONV_CHUNKS, p, 2 * cw // CONV_CHUNKS), BF16),
                        pltpu.VMEM((nb, CONV_CHUNKS, p, 2 * cw // CONV_CHUNKS), F32)],
        compiler_params=_params("parallel", "arbitrary"),
        name="hyena_longconv",
    )(u_arr, g_arr, conv_w, conv_b, conv_w, conv_b, f, gm, kre, kim, d)


def _hyena_consts(l, cw):
    p = l // 2
    e = np.arange(p)
    pos = np.concatenate([e, p + e, p - e])
    t = np.linspace(0.0, 1.0, l, dtype=np.float32)[pos][:, None]
    w = (2.0 * math.pi * np.arange(l, dtype=np.float32) / l).astype(np.float32)[pos][:, None]
    fq = np.linspace(1e-4, C_BANDS - 1, C_BANDS, dtype=np.float32)[None, :]
    wf = jnp.asarray(w) * jnp.asarray(fq)
    z = jnp.concatenate([jnp.asarray(t), jnp.cos(wf), -jnp.sin(wf)], axis=-1)
    z = jnp.pad(z, ((0, 0), (0, C_FILTER_WIDTH - z.shape[1])))
    z = jnp.concatenate([z[:3 * p // 2], z[3 * p // 2:]], axis=1)
    deltas = np.abs(np.linspace(HYENA_MIN_DECAY, HYENA_MAX_DECAY, cw, dtype=np.float32))[None, :]
    return z, jnp.asarray(t), jnp.asarray(deltas)


def _pack_filter_mlp(w1, b1, freq, w2, b2):
    fw = w2.shape[0]
    w1p = jnp.pad(w1, ((0, fw - w1.shape[0]), (0, 0)))
    zero = jnp.zeros((fw, fw), F32)
    diag = lambda w: jnp.concatenate([jnp.concatenate([w, zero], axis=1), jnp.concatenate([zero, w], axis=1)], axis=0)
    twice = lambda v: jnp.tile(v.reshape(1, fw), (1, 2))
    return diag(w1p), twice(b1), twice(freq), diag(w2), twice(b2)


def _hyena(hy, hconst, f, gm, fh, conv_w, conv_b, w1p, b1, freq, w2, b2, w3, dbias):
    z, t, deltas = hconst
    ah, al, sh, sl = _filters(z, t, w1p, b1, freq, w2, b2, w3, deltas)
    kre, kim = _spectrum(f, ah, al, sh, sl, fh)
    zz = _longconv(hy, 0, hy, 1, conv_w, conv_b, f, gm, kre, kim, 0, dbias[0:1], conv_u=True, fh=fh)
    return _longconv(zz, 0, hy, 2, conv_w, conv_b, f, gm, kre, kim, 1, dbias[1:2], conv_u=False, fh=fh)


FFN_CHUNK = 256


def _mix_ffn_kernel(x_ref, a_ref, b_ref, c_ref, wo_ref, g1_ref, gn_ref, sh_ref, sc_ref, g2_ref,
                    wg_ref, wu_ref, wd_ref, o_ref, *, fc):
    wa, wb = a_ref.shape[1], b_ref.shape[1]
    mix = (_dot(a_ref[...], wo_ref[:wa, :]) + _dot(b_ref[...], wo_ref[wa:wa + wb, :])
           + _dot(c_ref[...], wo_ref[wa + wb:, :]))
    x1 = x_ref[...] + g1_ref[...] * mix
    xb = _rms_mod(x1, gn_ref[...], sh_ref[...], sc_ref[...]).astype(BF16)
    acc = None
    for c0 in range(0, wg_ref.shape[1], fc):
        hg = _dot(xb, wg_ref[:, c0:c0 + fc])
        hu = _dot(xb, wu_ref[:, c0:c0 + fc])
        act = ((hg * jax.nn.sigmoid(hg)) * hu).astype(BF16)
        part = _dot(act, wd_ref[c0:c0 + fc, :])
        acc = part if acc is None else acc + part
    o_ref[...] = x1 + g2_ref[...] * acc


def _mix_ffn(x, a, b, c, wo, g1, gn, sh2, sc2, g2, wg, wu, wd, layer):
    bn, l, d = x.shape
    tm = min(TOKEN_TILE, l)
    row = pl.BlockSpec((None, 1, d), lambda bb, t: (bb, 0, 0))
    tok = lambda width: pl.BlockSpec((None, tm, width), lambda bb, t: (bb, t, 0))
    return pl.pallas_call(
        functools.partial(_mix_ffn_kernel, fc=FFN_CHUNK),
        out_shape=jax.ShapeDtypeStruct((bn, l, d), F32),
        grid=(bn, l // tm),
        in_specs=[tok(d), tok(a.shape[2]), tok(b.shape[2]), tok(c.shape[2]), _layer_of(wo, layer),
                  row, pl.BlockSpec((1, d), lambda bb, t: (0, 0)), row, row, row,
                  _layer_of(wg, layer), _layer_of(wu, layer), _layer_of(wd, layer)],
        out_specs=tok(d),
        compiler_params=_params("parallel", "parallel"),
        name="mix_ffn",
    )(x, a, b, c, wo, g1, gn, sh2, sc2, g2, wg, wu, wd)


def _rope_tables(l):
    half = HEAD_DIM // 2
    nfreq = half // 2
    inv = ROPE_BASE ** (-jnp.arange(nfreq, dtype=F32) / nfreq)
    pos = jnp.arange(l)
    rows, cols = pos // GRID_W, pos % GRID_W
    ang = jnp.concatenate([rows.astype(F32)[:, None] * inv[None, :]] * 2
                          + [cols.astype(F32)[:, None] * inv[None, :]] * 2, axis=-1)
    first = (np.arange(HEAD_DIM) % half) < nfreq
    cos, sin = jnp.cos(ang), jnp.sin(ang)
    sin_a = jnp.where(first[None, :], -sin, 0.0)
    sin_b = jnp.where(first[None, :], 0.0, sin)
    reps = LANES // HEAD_DIM
    return tuple(jnp.tile(tb, (1, reps)) for tb in (cos, sin_a, sin_b))


def kernel(x, c, ctx, c_ctx, ada_w, ada_b, norm1_g, norm2_g, w_in, qnorm_a, knorm_a, sink_a, qnorm_b, knorm_b,
           rpb_b, conv_w, conv_b, filt_w1, filt_b1, filt_freq, filt_w2, filt_b2, filt_w3, hyena_bias, w_out,
           ffn_w_gate, ffn_w_up, ffn_w_down):
    bn, l, d = x.shape
    lc = ctx.shape[1]
    depth = ada_w.shape[0]
    cw = hyena_bias.shape[2]
    assert l % (A_STEP_BLOCKS * A_BLOCK) == 0 and l >= 3 * A_SPAN and lc % 256 == 0
    assert l % (B_STEP_GROUPS * B_GROUP * GRID_W) == 0 and l // GRID_W >= 3 * B_GROUP

    mod_rows = 8 * (-(-(bn + 1) // 8))
    c_all = jnp.zeros((mod_rows, d), F32).at[:bn].set(c).at[bn].set(c_ctx)
    mod = _modulation(c_all, ada_w, ada_b)

    a_order = (0, 2, 1, 3)
    a_kv = tuple(hh // (A_HEADS // A_KV_HEADS) for hh in a_order)
    heads_a = lambda t, axis: [lax.slice_in_dim(t, hh * HEAD_DIM, (hh + 1) * HEAD_DIM, axis=axis) for hh in a_order]
    qa_w = A_HEADS * HEAD_DIM
    w_in_b = w_in.astype(BF16)
    w_out_b = jnp.concatenate(heads_a(w_out, 1) + [w_out[:, qa_w:]], axis=1).astype(BF16)
    sink_p = LOG2E * jnp.stack([sink_a[:, hh] for hh in a_order], axis=1)
    wg_b, wu_b, wd_b = ffn_w_gate.astype(BF16), ffn_w_up.astype(BF16), ffn_w_down.astype(BF16)

    lane = np.arange(256)
    e_heads = jnp.asarray((lane[:, None] // HEAD_DIM) == (lane[None, :] // HEAD_DIM), BF16)
    tile4 = lambda g: jnp.tile(g.reshape(1, HEAD_DIM), (1, 256 // HEAD_DIM))
    rope_lat = _rope_tables(l)

    fh_lat, fh_ctx = min(FREQ_TILE, l // 2), min(FREQ_TILE, lc // 2)
    f_lat, g_lat = _dft_matrices(l // 2, fh_lat)
    f_ctx, g_ctx = _dft_matrices(lc // 2, fh_ctx)
    hc_lat = _hyena_consts(l, cw)
    hc_ctx = _hyena_consts(lc, cw)

    xc = ctx
    for i in range(depth):
        last = i == depth - 1
        m = mod[i]
        part = lambda rows, k: rows[:, None, k * d:(k + 1) * d]
        lat = [part(m[:bn], k) for k in range(6)]
        cx = [part(m[bn:bn + 1], k) for k in range(6)]
        g1n = norm1_g[i].reshape(1, d)
        g2n = norm2_g[i].reshape(1, d)
        gqa, gka, gqb, gkb = tile4(qnorm_a[i]), tile4(knorm_a[i]), tile4(qnorm_b[i]), tile4(knorm_b[i])

        qa, ka, va, qb, kb, vb, hy = _proj_in(x, lat[0], lat[1], g1n, w_in_b, i, gqa, gka, gqb, gkb, e_heads,
                                              rope_lat)
        ctx_out = _proj_in(xc.reshape(1, bn * lc, d), cx[0], cx[1], g1n, w_in_b, i, gqa, gka, gqb, gkb, e_heads,
                           None)
        qa_c, ka_c, va_c, qb_c, kb_c, vb_c, hy_c = [t.reshape(bn, lc, -1) for t in ctx_out]
        hy_args = (conv_w[i], conv_b[i].reshape(1, -1),
                   *_pack_filter_mlp(filt_w1[i], filt_b1[i], filt_freq[i], filt_w2[i], filt_b2[i]),
                   filt_w3[i], hyena_bias[i])

        sink_row = jnp.repeat(sink_p[i], A_BLOCK).reshape(1, A_HEADS * A_BLOCK)
        out_a = _attn_a(qa, ka, va, ka_c, va_c, sink_row)
        out_b = _attn_b(qb, kb, vb, kb_c, vb_c, _rpb_tables(rpb_b[i]))
        out_c = _hyena(hy, hc_lat, f_lat, g_lat, fh_lat, *hy_args)
        x = _mix_ffn(x, out_a, out_b, out_c, w_out_b, lat[2], g2n, lat[3], lat[4], lat[5], wg_b, wu_b, wd_b, i)

        if not last:
            oa_c = _dense_attn(qa_c, ka_c, va_c, sink_p[i], kv_of=a_kv, use_sink=True)
            ob_c = _dense_attn(qb_c, kb_c, vb_c, sink_p[i], kv_of=tuple(range(B_HEADS)), use_sink=False)
            oc_c = _hyena(hy_c, hc_ctx, f_ctx, g_ctx, fh_ctx, *hy_args)
            flat = lambda t: t.reshape(1, bn * lc, -1)
            xc = _mix_ffn(flat(xc), flat(oa_c), flat(ob_c), flat(oc_c), w_out_b, cx[2], g2n, cx[3], cx[4], cx[5],
                          wg_b, wu_b, wd_b, i).reshape(bn, lc, d)
    return x
```

```python
import functools
import math

import jax
import jax.numpy as jnp
import numpy as np
from jax import lax
from jax.experimental import pallas as pl
from jax.experimental.pallas import tpu as pltpu

F32 = jnp.float32
BF16 = jnp.bfloat16

GRID_W = 64
HEAD_DIM = 64
A_HEADS = 4
A_KV_HEADS = 2
A_BLOCK = 128
B_HEADS = 4
NA_ROWS = 8
NA_COLS = 16
C_ORDER = 2
C_DIRS = 2
C_FILTER_WIDTH = 64
C_BANDS = 16
ROPE_BASE = 10000.0
EPS = 1e-6
LOG2E = math.log2(math.e)
HYENA_MIN_DECAY = math.log(1e-2) / 1.5
HYENA_MAX_DECAY = math.log(1e-2) / 0.3

V7X_VMEM_BYTES = 64 * 1024 * 1024
VMEM_LIMIT = V7X_VMEM_BYTES - 8 * 1024 * 1024
LANES = 128
TOKEN_TILE = 512
FREQ_TILE = 512


def _params(*sem):
    return pltpu.CompilerParams(dimension_semantics=sem, vmem_limit_bytes=VMEM_LIMIT)


def _dot(a, b):
    return jnp.dot(a, b, preferred_element_type=F32)


def _dot_nt(a, b):
    return lax.dot_general(a, b, (((1,), (1,)), ((), ())), preferred_element_type=F32)


def _dot_hp(a, b):
    ah, al = _split_bf16(a)
    bh, bl = _split_bf16(b)
    return _dot(ah, bh) + (_dot(ah, bl) + _dot(al, bh))


def _split_bf16(v):
    hi = v.astype(BF16)
    lo = (v - hi.astype(F32)).astype(BF16)
    return hi, lo


def _layer_of(stack, layer):
    return pl.BlockSpec((None,) + stack.shape[1:], lambda *_: (layer, 0, 0), pipeline_mode=pl.Buffered(1))


def _resident(shape):
    nd = len(shape)
    return pl.BlockSpec(shape, lambda *_: (0,) * nd, pipeline_mode=pl.Buffered(1))


def _mod_kernel(c_ref, w_ref, b_ref, o_ref):
    cv = c_ref[...]
    sc = (cv * jax.nn.sigmoid(cv)).astype(BF16)
    o_ref[...] = _dot(sc, w_ref[...].astype(BF16)) + b_ref[...]


def _modulation(c_all, ada_w, ada_b):
    depth, d, n = ada_w.shape
    rows = c_all.shape[0]
    tn = 1536
    return pl.pallas_call(
        _mod_kernel,
        out_shape=jax.ShapeDtypeStruct((depth, rows, n), F32),
        grid=(depth, n // tn),
        in_specs=[
            pl.BlockSpec((rows, d), lambda i, j: (0, 0)),
            pl.BlockSpec((None, d, tn), lambda i, j: (i, 0, j)),
            pl.BlockSpec((None, 1, tn), lambda i, j: (i, 0, j)),
        ],
        out_specs=pl.BlockSpec((None, rows, tn), lambda i, j: (i, 0, j)),
        compiler_params=_params("arbitrary", "arbitrary"),
        name="modulation",
    )(c_all, ada_w, ada_b.reshape(depth, 1, n))


def _rms_mod(x, g, shift, scale):
    y = x * lax.rsqrt(jnp.mean(x * x, axis=-1, keepdims=True) + EPS)
    return (y * g) * (1 + scale) + shift


def _head_norm(h, g, e):
    ss = _dot((h * h).astype(BF16), e)
    return (h * lax.rsqrt(ss * (1.0 / HEAD_DIM) + EPS)) * g


def _rope(t, cos, sin_a, sin_b):
    outs = []
    for c in range(t.shape[1] // LANES):
        tc = t[:, c * LANES:(c + 1) * LANES]
        outs.append(tc * cos + pltpu.roll(tc, LANES - 16, 1) * sin_a + pltpu.roll(tc, 16, 1) * sin_b)
    return outs[0] if len(outs) == 1 else jnp.concatenate(outs, axis=1)


PROJ_ROWS = 1024
PROJ_SPLIT = 2


def _proj_in_kernel(x_ref, shift_ref, scale_ref, g_ref, w_ref, gqa_ref, gka_ref, gqb_ref, gkb_ref, e_ref, *rest,
                    rope):
    if rope:
        cos_ref, sa_ref, sb_ref = rest[:3]
        rest = rest[3:]
    qa_ref, ka_ref, va_ref, qb_ref, kb_ref, vb_ref, hy_ref = rest
    qscale = LOG2E * HEAD_DIM ** -0.5
    hm = x_ref.shape[0] // PROJ_SPLIT
    for part in range(PROJ_SPLIT):
        rows = slice(part * hm, (part + 1) * hm)
        xb = _rms_mod(x_ref[rows, :], g_ref[...], shift_ref[...], scale_ref[...]).astype(BF16)

        h_all = _dot(xb, w_ref[...])

        def proj(lo, hi):
            return h_all[:, lo:hi]

        def maybe_rope(t):
            return _rope(t, cos_ref[rows, :], sa_ref[rows, :], sb_ref[rows, :]) if rope else t

        def store_v(ref, v):
            if rope:
                for i in range(hm // LANES):
                    ref[part * (hm // LANES) + i] = v[i * LANES:(i + 1) * LANES, :].T.astype(BF16)
            else:
                ref[rows, :] = v.astype(BF16)

        qa = maybe_rope(_head_norm(proj(0, 256), gqa_ref[...], e_ref[...])) * qscale
        t0, t1 = qa[:, :LANES], qa[:, LANES:]
        lo = lax.broadcasted_iota(jnp.int32, t0.shape, 1) < HEAD_DIM
        qa_ref[rows, :LANES] = jnp.where(lo, t0, pltpu.roll(t1, HEAD_DIM, 1)).astype(BF16)
        qa_ref[rows, LANES:] = jnp.where(lo, pltpu.roll(t0, HEAD_DIM, 1), t1).astype(BF16)
        kva = proj(256, 512)
        ka = maybe_rope(_head_norm(kva[:, :LANES], gka_ref[:, :128], e_ref[:128, :128]))
        ka_ref[rows, :] = ka.astype(BF16)
        store_v(va_ref, kva[:, LANES:])
        qb = _head_norm(proj(512, 768), gqb_ref[...], e_ref[...])
        qb_ref[rows, :] = (qb * qscale).astype(BF16)
        kb_ref[rows, :] = _head_norm(proj(768, 1024), gkb_ref[...], e_ref[...]).astype(BF16)
        store_v(vb_ref, proj(1024, 1280))
        hy_ref[rows, :] = proj(1280, 2816).astype(BF16)


def _proj_in(x, shift, scale, g, w, layer, gqa, gka, gqb, gkb, e, rope_tabs):
    bn, l, d = x.shape
    n = w.shape[2]
    rope = rope_tabs is not None
    tm = min(PROJ_ROWS, l)
    row = lambda width: pl.BlockSpec((None, 1, width), lambda b, t: (b, 0, 0))
    const = lambda shape: pl.BlockSpec(shape, lambda b, t: (0,) * len(shape))
    tok = lambda width: pl.BlockSpec((None, tm, width), lambda b, t: (b, t, 0))
    tab = pl.BlockSpec((tm, LANES), lambda b, t: (t, 0))
    widths = (256, 128, 128, 256, 256, 256, n - 1280)
    shapes = [(bn, l, wd) for wd in widths]
    specs = [tok(wd) for wd in widths]
    if rope:
        for i in (2, 5):
            shapes[i] = (bn, l // LANES, widths[i], LANES)
            specs[i] = pl.BlockSpec((None, tm // LANES, widths[i], LANES), lambda b, t: (b, t, 0, 0))
    return pl.pallas_call(
        functools.partial(_proj_in_kernel, rope=rope),
        out_shape=[jax.ShapeDtypeStruct(sh, BF16) for sh in shapes],
        grid=(bn, l // tm),
        in_specs=[tok(d), row(d), row(d), const((1, d)), _layer_of(w, layer),
                  const((1, 256)), const((1, 256)), const((1, 256)), const((1, 256)), const((256, 256))]
                 + ([tab, tab, tab] if rope else []),
        out_specs=specs,
        compiler_params=_params("parallel", "parallel"),
        name="proj_in_rope" if rope else "proj_in",
    )(x, shift, scale, g, w, gqa, gka, gqb, gkb, e, *(rope_tabs or ()))


def _softmax_pv(s_list, v_list, sink):
    m = s_list[0].max(axis=-1, keepdims=True)
    for s in s_list[1:]:
        m = jnp.maximum(m, s.max(axis=-1, keepdims=True))
    if sink is not None:
        m = jnp.maximum(m, sink)
    den = None
    out = None
    for s, v in zip(s_list, v_list):
        p = jnp.exp2(s - m)
        ps = p.sum(axis=-1, keepdims=True)
        den = ps if den is None else den + ps
        o = _dot(p.astype(BF16), v)
        out = o if out is None else out + o
    if sink is not None:
        den = den + jnp.exp2(sink - m)
    return out * (1.0 / den)


def _stack_heads(q):
    lo = lax.broadcasted_iota(jnp.int32, q.shape, 1) < HEAD_DIM
    zero = jnp.zeros_like(q)
    return jnp.concatenate([jnp.where(lo, q, zero), jnp.where(lo, zero, q)], axis=0)


SOFTMAX_CHUNK = 32


def _softmax_keys(s_ref, p_ref, nloc, add_loc, sink):
    nk = s_ref.shape[0]
    ch = SOFTMAX_CHUNK
    macc = None
    for r0 in range(0, nk, ch):
        s = s_ref[r0:r0 + ch, :]
        if r0 < nloc:
            s = s + add_loc(r0)
            s_ref[r0:r0 + ch, :] = s
        macc = s if macc is None else jnp.maximum(macc, s)
    m = macc.max(axis=0, keepdims=True)
    if sink is not None:
        m = jnp.maximum(m, sink)
    sacc = None
    for r0 in range(0, nk, ch):
        p = jnp.exp2(s_ref[r0:r0 + ch, :] - m)
        sacc = p if sacc is None else sacc + p
        p_ref[r0:r0 + ch, :] = p.astype(BF16)
    den = sacc.sum(axis=0, keepdims=True)
    if sink is not None:
        den = den + jnp.exp2(sink - m)
    return 1.0 / den


A_STEP_BLOCKS = 16
A_SPAN = 3 * A_BLOCK
A_SCORE_BUFS = 2


def _attn_a_kernel(q_ref, k_ref, vt_ref, kx_ref, vxt_ref, mask_ref, sink_ref, o_ref, s_ref, p_ref):
    l = k_ref.shape[0]
    nb = l // A_BLOCK
    hd = HEAD_DIM
    def window(u):
        n = pl.program_id(1) * A_STEP_BLOCKS + u
        tile0 = jnp.clip(n - 1, 0, nb - 3)
        pat = jnp.where(n == 0, 0, jnp.where(n == nb - 1, 2, 1))
        return tile0, pat

    def scores(u):
        tile0, _ = window(u)
        start = pl.multiple_of(tile0 * A_BLOCK, A_BLOCK)
        q = q_ref[u * A_BLOCK:(u + 1) * A_BLOCK, :]
        qs = jnp.concatenate([_stack_heads(q[:, :LANES]), _stack_heads(q[:, LANES:])], axis=0)
        s_ref[u % A_SCORE_BUFS] = _dot_nt(jnp.concatenate([k_ref[pl.ds(start, A_SPAN), :], kx_ref[...]], axis=0), qs)

    def output(u, r):
        tile0, _ = window(u)
        vt = jnp.concatenate([vt_ref[tile0 + i] for i in range(3)] + [vxt_ref[...]], axis=1)
        ot = _dot(vt, p_ref[u % 2]) * r
        ot = jnp.concatenate([ot[(i % 2) * hd:(i % 2 + 1) * hd, i * A_BLOCK:(i + 1) * A_BLOCK]
                              for i in range(A_HEADS)], axis=0)
        o_ref[u * A_BLOCK:(u + 1) * A_BLOCK, :] = ot.T.astype(o_ref.dtype)

    ahead = A_SCORE_BUFS - 1
    for u in range(ahead):
        scores(u)
    pending = None
    for u in range(A_STEP_BLOCKS):
        if u + ahead < A_STEP_BLOCKS:
            scores(u + ahead)
        _, pat = window(u)
        r = _softmax_keys(s_ref.at[u % A_SCORE_BUFS], p_ref.at[u % 2], A_SPAN,
                          lambda r0: mask_ref[pat, r0:r0 + SOFTMAX_CHUNK, :], sink_ref[...])
        if pending is not None:
            output(*pending)
        pending = (u, r)
    output(*pending)


def _attn_a_mask():
    i = np.arange(A_BLOCK)[None, :]
    j = np.arange(A_SPAN)[:, None]
    offs = (0, A_BLOCK, 2 * A_BLOCK)
    m = np.stack([np.where(np.abs(j - i - o) <= A_BLOCK, 0.0, -np.inf) for o in offs])
    return jnp.asarray(np.tile(m, (1, 1, A_HEADS)), F32)


def _attn_a(q, k, vt, kx, vx, sink_row):
    bn, l, qw = q.shape
    lc = kx.shape[1]
    kvw = k.shape[2]
    nb = l // A_BLOCK
    qs = A_STEP_BLOCKS * A_BLOCK
    mask = _attn_a_mask()
    vxt = jnp.swapaxes(vx, 1, 2)
    seq = pl.BlockSpec((None, l, kvw), lambda b, s: (b, 0, 0))
    qblk = pl.BlockSpec((None, qs, qw), lambda b, s: (b, s, 0))
    nq = A_HEADS * A_BLOCK
    return pl.pallas_call(
        _attn_a_kernel,
        out_shape=jax.ShapeDtypeStruct((bn, l, qw), BF16),
        grid=(bn, l // qs),
        in_specs=[qblk, seq, pl.BlockSpec((None, nb, kvw, A_BLOCK), lambda b, s: (b, 0, 0, 0)),
                  pl.BlockSpec((None, lc, kvw), lambda b, s: (b, 0, 0)),
                  pl.BlockSpec((None, kvw, lc), lambda b, s: (b, 0, 0)),
                  _resident(mask.shape), _resident(sink_row.shape)],
        out_specs=qblk,
        scratch_shapes=[pltpu.VMEM((A_SCORE_BUFS, A_SPAN + lc, nq), F32), pltpu.VMEM((2, A_SPAN + lc, nq), BF16)],
        compiler_params=_params("parallel", "arbitrary"),
        name="window_attn",
    )(q, k, vt, kx, vxt, mask, sink_row)


def _dense_attn_kernel(sink_ref, q_ref, k_ref, v_ref, o_ref, *, kv_of, use_sink):
    for h, kv in enumerate(kv_of):
        hs = slice(h * HEAD_DIM, (h + 1) * HEAD_DIM)
        ks = slice(kv * HEAD_DIM, (kv + 1) * HEAD_DIM)
        s = _dot_nt(q_ref[:, hs], k_ref[:, ks])
        o = _softmax_pv([s], [v_ref[:, ks]], sink_ref[h] if use_sink else None)
        o_ref[:, hs] = o.astype(o_ref.dtype)


def _dense_attn(q, k, v, sink, *, kv_of, use_sink):
    bn, l, qw = q.shape
    kvw = k.shape[2]
    full = lambda wd: pl.BlockSpec((None, l, wd), lambda b: (b, 0, 0))
    return pl.pallas_call(
        functools.partial(_dense_attn_kernel, kv_of=kv_of, use_sink=use_sink),
        out_shape=jax.ShapeDtypeStruct((bn, l, qw), BF16),
        grid=(bn,),
        in_specs=[pl.BlockSpec(memory_space=pltpu.SMEM), full(qw), full(kvw), full(kvw)],
        out_specs=full(qw),
        compiler_params=_params("parallel"),
        name="ctx_attn_sink" if use_sink else "ctx_attn",
    )(sink, q, k, v)


def _rpb_kernel(r_ref, oh_ref, ok_ref, o_ref):
    r = r_ref[...]
    b1 = r.astype(BF16)
    r2 = r - b1.astype(F32)
    b2 = r2.astype(BF16)
    b3 = (r2 - b2.astype(F32)).astype(BF16)
    oh = oh_ref[...]
    bias = (_dot(b1, oh) + _dot(b2, oh)) + _dot(b3, oh)
    o_ref[...] = jnp.where(ok_ref[...] > 0.5, bias * LOG2E, -jnp.inf)


def _rpb_slots():
    a = np.arange(B_GROUP)[:, None]
    kr = np.arange(B_SLAB)[None, :]
    dr = np.stack([kr - a + NA_ROWS - 1, kr - a + NA_ROWS // 2 - 1, kr - a + (B_GROUP + NA_ROWS - 1 - B_SLAB)])
    lo = np.stack([0 * a + 0 * kr, a + 0 * kr, 0 * a + (B_SLAB - NA_ROWS) + 0 * kr])
    valid = (kr[None] >= lo) & (kr[None] < lo + NA_ROWS)
    return np.where(valid, dr, -1)


def _rpb_assemble_kernel(t_ref, o_ref, *, n_dr):
    slots = _rpb_slots()
    w = GRID_W
    blank = jnp.full((w, w), -jnp.inf, F32)
    for pat in range(3):
        for t in range(o_ref.shape[1]):
            for kr in range(B_SLAB):
                for hh in range(2):
                    for a in range(B_GROUP):
                        dr = int(slots[pat, a, kr])
                        tile = t_ref[(2 * t + hh) * n_dr + dr] if dr >= 0 else blank
                        c0 = (hh * B_GROUP + a) * w
                        o_ref[pat, t, kr * w:(kr + 1) * w, c0:c0 + w] = tile


def _rpb_tables(rpb):
    h, nr, nc = rpb.shape
    col = np.arange(GRID_W)
    dc = np.clip(col[:, None] - col[None, :], 1 - NA_COLS, NA_COLS - 1) + NA_COLS - 1
    onehot = (dc.reshape(1, -1) == np.arange(nc)[:, None]).astype(np.float32)
    onehot = np.concatenate([onehot, np.zeros((32 - nc, GRID_W * GRID_W), np.float32)], axis=0)
    col_start = np.clip(col - NA_COLS // 2, 0, GRID_W - NA_COLS)
    col_ok = (col[:, None] >= col_start[None, :]) & (col[:, None] < col_start[None, :] + NA_COLS)
    rows = 64
    r2 = jnp.zeros((rows, 32), F32).at[:h * nr, :nc].set(rpb.reshape(h * nr, nc))
    tiles = pl.pallas_call(
        _rpb_kernel,
        out_shape=jax.ShapeDtypeStruct((rows, GRID_W * GRID_W), F32),
        name="rpb_table",
    )(r2, jnp.asarray(onehot, BF16), jnp.asarray(col_ok.reshape(1, -1), F32))
    tiles = tiles.reshape(rows, GRID_W, GRID_W)
    return pl.pallas_call(
        functools.partial(_rpb_assemble_kernel, n_dr=nr),
        out_shape=jax.ShapeDtypeStruct((3, h // 2, B_SLAB * GRID_W, 2 * B_GROUP * GRID_W), F32),
        compiler_params=_params(),
        name="rpb_assemble",
    )(tiles)


B_GROUP = 4
B_SCORE_AHEAD = 2
B_STEP_GROUPS = 4
B_SLAB = 12


def _attn_b_kernel(q_ref, k_ref, vt_ref, kx_ref, vxt_ref, tbl_ref, o_ref, s_ref, p_ref):
    ng = pl.num_programs(1) * B_STEP_GROUPS
    rows = k_ref.shape[0] // GRID_W
    nloc = B_SLAB * GRID_W
    gq = B_GROUP * GRID_W
    units = [(u, t) for u in range(B_STEP_GROUPS) for t in range(B_HEADS // 2)]

    def slab(u):
        g = pl.program_id(1) * B_STEP_GROUPS + u
        base = jnp.clip(g * B_GROUP - NA_ROWS // 2, 0, rows - B_SLAB)
        pat = jnp.where(g == 0, 0, jnp.where(g == ng - 1, 2, 1))
        return base, pat

    def scores(u, t):
        base, _ = slab(u)
        start = pl.multiple_of(base * GRID_W, LANES)
        ts = slice(t * LANES, (t + 1) * LANES)
        qs = _stack_heads(q_ref[u * gq:(u + 1) * gq, ts])
        s_ref[u, t] = _dot_nt(jnp.concatenate([k_ref[pl.ds(start, nloc), ts], kx_ref[:, ts]], axis=0), qs)

    def output(u, t, r):
        base, _ = slab(u)
        tile0 = base // (LANES // GRID_W)
        ts = slice(t * LANES, (t + 1) * LANES)
        vt = jnp.concatenate([vt_ref[tile0 + j, ts, :] for j in range(nloc // LANES)] + [vxt_ref[ts, :]],
                             axis=1)
        ot = _dot(vt, p_ref[u, t]) * r
        ot = jnp.concatenate([ot[:HEAD_DIM, :gq], ot[HEAD_DIM:, gq:]], axis=0)
        o_ref[u * gq:(u + 1) * gq, ts] = ot.T.astype(o_ref.dtype)

    ahead = B_SCORE_AHEAD
    for unit in units[:ahead]:
        scores(*unit)
    pending = None
    for i, (u, t) in enumerate(units):
        if i + ahead < len(units):
            scores(*units[i + ahead])
        _, pat = slab(u)
        r = _softmax_keys(s_ref.at[u, t], p_ref.at[u, t], nloc,
                          lambda r0: tbl_ref[pat, t, r0:r0 + SOFTMAX_CHUNK, :], None)
        if pending is not None:
            output(*pending)
        pending = (u, t, r)
    output(*pending)


def _attn_b(q, k, vt, kx, vx, tbl):
    bn, l, w = q.shape
    lc = kx.shape[1]
    gq = B_GROUP * GRID_W
    sq = B_STEP_GROUPS * gq
    nk = B_SLAB * GRID_W + lc
    vxt = jnp.swapaxes(vx, 1, 2)
    seq = pl.BlockSpec((None, l, w), lambda b, g: (b, 0, 0))
    qblk = pl.BlockSpec((None, sq, w), lambda b, g: (b, g, 0))
    return pl.pallas_call(
        _attn_b_kernel,
        out_shape=jax.ShapeDtypeStruct((bn, l, w), BF16),
        grid=(bn, l // sq),
        in_specs=[qblk, seq, pl.BlockSpec((None, l // LANES, w, LANES), lambda b, g: (b, 0, 0, 0)),
                  pl.BlockSpec((None, lc, w), lambda b, g: (b, 0, 0)),
                  pl.BlockSpec((None, w, lc), lambda b, g: (b, 0, 0)),
                  _resident(tbl.shape)],
        out_specs=qblk,
        scratch_shapes=[pltpu.VMEM((B_STEP_GROUPS, B_HEADS // 2, nk, 2 * gq), F32),
                        pltpu.VMEM((B_STEP_GROUPS, B_HEADS // 2, nk, 2 * gq), BF16)],
        compiler_params=_params("parallel", "arbitrary"),
        name="nbr_attn",
    )(q, k, vt, kx, vxt, tbl)


def _dft_matrices(l, fh):
    k = jnp.arange(l, dtype=jnp.int32)[:, None]
    n = jnp.arange(l, dtype=jnp.int32)[None, :]
    ang = (((2 * k + 1) * n) % (4 * l)).astype(F32) * (math.pi / (2 * l))
    fre = jnp.cos(ang).reshape(l // fh, fh, l)
    fim = (-jnp.sin(ang)).reshape(l // fh, fh, l)
    f = jnp.concatenate([fre, fim], axis=1).reshape(2 * l, l)
    g = f.T * (1.0 / l)
    return f.astype(BF16), g.astype(BF16)


def _filter_kernel(z_ref, t_ref, w1_ref, b1_ref, fr_ref, w2_ref, b2_ref, w3_ref, dl_ref,
                   ah_ref, al_ref, sh_ref, sl_ref, hid_ref):
    @pl.when(pl.program_id(0) == 0)
    def _():
        fr = fr_ref[...]
        hid = jnp.sin(fr * (_dot_hp(z_ref[...], w1_ref[...]) + b1_ref[...]))
        hid_ref[...] = jnp.sin(fr * (_dot_hp(hid, w2_ref[...]) + b2_ref[...]))

    cw = dl_ref.shape[1]
    p = t_ref.shape[0] // 3
    hid = hid_ref[...]
    w3 = w3_ref[...]
    none = jnp.zeros_like(w3)
    taps = jnp.concatenate([_dot_hp(hid, jnp.concatenate([w3, none], axis=0)),
                            _dot_hp(hid, jnp.concatenate([none, w3], axis=0))], axis=0)
    decay = jnp.exp(-t_ref[...] * dl_ref[...])
    kf = taps[:, :cw] * decay
    kb = taps[:, cw:] * decay
    kf0, kf1, kfr = kf[:p], kf[p:2 * p], kf[2 * p:]
    kb0, kb1, kbr = kb[:p], kb[p:2 * p], kb[2 * p:]
    first = lax.broadcasted_iota(jnp.int32, (p, cw), 0) == 0
    drop0 = lambda v: jnp.where(first, 0.0, v)
    kb0 = drop0(kb0)
    colsum = lambda v: jnp.sum(jnp.abs(v), axis=0, keepdims=True)
    inv = 1.0 / (colsum(kf0) + colsum(kf1) + colsum(kb0) + colsum(kb1))
    pairs = ((kf0, kb0), (kf1, drop0(kfr)), (kbr, drop0(kb1)))
    for d, (cp, cm) in enumerate(pairs):
        cols = slice(d * cw, (d + 1) * cw)
        ah_ref[:, cols], al_ref[:, cols] = _split_bf16((cp + cm) * inv)
        sh_ref[:, cols], sl_ref[:, cols] = _split_bf16((cp - cm) * inv)


def _filters(z, t, w1, b1, freq, w2, b2, w3, deltas):
    p = t.shape[0] // 3
    cw = deltas.shape[1]
    fw = w2.shape[0]
    const = lambda shape: pl.BlockSpec(shape, lambda o: (0,) * len(shape))
    out = jax.ShapeDtypeStruct((p, C_ORDER * 3 * cw), BF16)
    oblk = pl.BlockSpec((p, 3 * cw), lambda o: (0, o))
    return pl.pallas_call(
        _filter_kernel,
        out_shape=[out] * 4,
        grid=(C_ORDER,),
        in_specs=[const(z.shape), const(t.shape), const(w1.shape), const((1, fw)), const((1, fw)),
                  const((fw, fw)), const((1, fw)), pl.BlockSpec((w3.shape[0], C_DIRS * cw), lambda o: (0, o)),
                  const((1, cw))],
        out_specs=[oblk] * 4,
        scratch_shapes=[pltpu.VMEM(z.shape, F32)],
        compiler_params=_params("arbitrary"),
        name="hyena_filter",
    )(z, t, w1, b1, freq, w2, b2, w3, deltas)


def _spectrum_kernel(f_ref, ah_ref, al_ref, sh_ref, sl_ref, kre_ref, kim_ref):
    fh = kre_ref.shape[0]
    fre = f_ref[:fh, :]
    fim = f_ref[fh:, :]
    kre_ref[...] = _dot(fre, ah_ref[...]) + _dot(fre, al_ref[...])
    kim_ref[...] = _dot(fim, sh_ref[...]) + _dot(fim, sl_ref[...])


def _spectrum(f, ah, al, sh, sl, fh):
    p, n = ah.shape
    tn = n // C_ORDER
    taps = pl.BlockSpec((p, tn), lambda o, j: (0, o))
    out = jax.ShapeDtypeStruct((p, n), F32)
    oblk = pl.BlockSpec((fh, tn), lambda o, j: (j, o))
    return pl.pallas_call(
        _spectrum_kernel,
        out_shape=[out, out],
        grid=(C_ORDER, p // fh),
        in_specs=[pl.BlockSpec((2 * fh, p), lambda o, j: (j, 0)), taps, taps, taps, taps],
        out_specs=[oblk, oblk],
        compiler_params=_params("arbitrary", "arbitrary"),
        name="hyena_spectrum",
    )(f, ah, al, sh, sl)


def _short_conv(u, w_ref, b_ref):
    n = u.shape[0]
    row = lax.broadcasted_iota(jnp.int32, u.shape, 0)
    prev = jnp.where(row == 0, 0.0, pltpu.roll(u, 1, 0))
    nxt = jnp.where(row == n - 1, 0.0, pltpu.roll(u, n - 1, 0))
    return prev * w_ref[0:1, :] + u * w_ref[1:2, :] + nxt * w_ref[2:3, :] + b_ref[...]


def _short_conv_wrap(u, w_ref, b_ref):
    n = u.shape[0]
    return (pltpu.roll(u, 1, 0) * w_ref[0:1, :] + u * w_ref[1:2, :] + pltpu.roll(u, n - 1, 0) * w_ref[2:3, :]
            + b_ref[...])


LONGCONV_ROWS = 1024
CONV_CHUNKS = 2


def _longconv_kernel(u_ref, g_ref, cwu_ref, cbu_ref, cwg_ref, cbg_ref, f_ref, gm_ref, kre_ref, kim_ref, d_ref,
                     o_ref, ub_ref, acc_ref, *, conv_u, n_steps):
    j = pl.program_id(1)
    fh = kre_ref.shape[0]
    nb, p, cw = u_ref.shape[0], ub_ref.shape[2], d_ref.shape[1]
    cc = cw // CONV_CHUNKS
    l = 2 * p
    edge = 16

    def edge_conv(ref, w_ref, b_ref, cs):
        head = _short_conv(ref[0:2 * edge, cs].astype(F32), w_ref.at[:, cs], b_ref.at[:, cs])[:edge]
        tail = _short_conv(ref[l - 2 * edge:l, cs].astype(F32), w_ref.at[:, cs], b_ref.at[:, cs])[edge:]
        return head, tail

    def load_u(ib, c, cs):
        ub = ub_ref.at[ib, c]
        if conv_u:
            u = _short_conv_wrap(u_ref[ib, :, cs].astype(F32), cwu_ref.at[:, cs], cbu_ref.at[:, cs]).astype(BF16)
        else:
            u = u_ref[ib, :, cs]
        ub[:, :cc] = u[:p]
        ub[:, cc:] = u[p:]
        if conv_u:
            head, tail = edge_conv(u_ref.at[ib], cwu_ref, cbu_ref, cs)
            ub[0:edge, :cc] = head.astype(BF16)
            ub[p - edge:p, cc:] = tail.astype(BF16)

    def forward(ib, c):
        return _dot(f_ref[...], ub_ref[ib, c])

    def inverse(c, spec):
        u0r, u1r, u0i, u1i = spec[:fh, :cc], spec[:fh, cc:], spec[fh:, :cc], spec[fh:, cc:]
        tap = lambda ref, d: ref[:, d * cw + c * cc:d * cw + (c + 1) * cc]
        c0r, c1r, cmr = (tap(kre_ref, d) for d in range(3))
        c0i, c1i, cmi = (tap(kim_ref, d) for d in range(3))
        y0r = (c0r * u0r - c0i * u0i) + (cmr * u1r - cmi * u1i)
        y0i = (c0r * u0i + c0i * u0r) + (cmr * u1i + cmi * u1r)
        y1r = (c1r * u0r - c1i * u0i) + (c0r * u1r - c0i * u1i)
        y1i = (c1r * u0i + c1i * u0r) + (c0r * u1i + c0i * u1r)
        y = jnp.concatenate([jnp.concatenate([y0r, y1r], axis=1), jnp.concatenate([y0i, y1i], axis=1)], axis=0)
        return _dot(gm_ref[...], y.astype(BF16))

    def gated_out(ib, c, cs, conv):
        gate = _short_conv_wrap(g_ref[ib, :, cs].astype(F32), cwg_ref.at[:, cs], cbg_ref.at[:, cs])
        ub = ub_ref.at[ib, c]
        y = jnp.concatenate([conv[:, :cc], conv[:, cc:]], axis=0)
        y = y + jnp.concatenate([ub[:, :cc], ub[:, cc:]], axis=0).astype(F32) * d_ref[:, cs]
        o_ref[ib, :, cs] = (gate * y).astype(o_ref.dtype)
        head, tail = edge_conv(g_ref.at[ib], cwg_ref, cbg_ref, cs)
        o_ref[ib, 0:edge, cs] = (head * y[:edge]).astype(o_ref.dtype)
        o_ref[ib, l - edge:l, cs] = (tail * y[l - edge:]).astype(o_ref.dtype)

    def step(first, last):
        chunks = [(ib, c, slice(c * cc, (c + 1) * cc)) for ib in range(nb) for c in range(CONV_CHUNKS)]

        def start(ib, c, cs):
            if first:
                load_u(ib, c, cs)
            return forward(ib, c)

        spec = start(*chunks[0])
        pending = None
        for i, (ib, c, cs) in enumerate(chunks):
            nxt = start(*chunks[i + 1]) if i + 1 < len(chunks) else None
            conv = inverse(c, spec)
            spec = nxt
            if not first:
                conv = acc_ref[ib, c] + conv
            if last:
                if pending is not None:
                    gated_out(*pending)
                pending = (ib, c, cs, conv)
            else:
                acc_ref[ib, c] = conv
        if last:
            gated_out(*pending)

    if n_steps == 1:
        step(True, True)
    else:
        pl.when(j == 0)(lambda: step(True, False))
        pl.when(j == n_steps - 1)(lambda: step(False, True))
        if n_steps > 2:
            pl.when(jnp.logical_and(j > 0, j < n_steps - 1))(lambda: step(False, False))


def _longconv(u_arr, u_blk, g_arr, g_blk, conv_w, conv_b, f, gm, kre, kim, order, d, *, conv_u, fh):
    bn, l, _ = u_arr.shape
    p = l // 2
    cw = d.shape[1]
    ub = u_blk if conv_u else 0
    nb = math.gcd(bn, max(1, LONGCONV_ROWS // l))
    tok = lambda blk: pl.BlockSpec((nb, l, cw), lambda b, j: (b, 0, blk))
    cpar = lambda rows, blk: pl.BlockSpec((rows, cw), lambda b, j: (0, blk))
    ktab = pl.BlockSpec((fh, 3 * cw), lambda b, j: (j, order))
    return pl.pallas_call(
        functools.partial(_longconv_kernel, conv_u=conv_u, n_steps=p // fh),
        out_shape=jax.ShapeDtypeStruct((bn, l, cw), BF16),
        grid=(bn // nb, p // fh),
        in_specs=[tok(u_blk), tok(g_blk), cpar(3, ub), cpar(1, ub), cpar(3, g_blk), cpar(1, g_blk),
                  pl.BlockSpec((2 * fh, p), lambda b, j: (j, 0)),
                  pl.BlockSpec((p, 2 * fh), lambda b, j: (0, j)),
                  ktab, ktab, pl.BlockSpec((1, cw), lambda b, j: (0, 0))],
        out_specs=tok(0),
        scratch_shapes=[pltpu.VMEM((nb, CONV_CHUNKS, p, 2 * cw // CONV_CHUNKS), BF16),
                        pltpu.VMEM((nb, CONV_CHUNKS, p, 2 * cw // CONV_CHUNKS), F32)],
        compiler_params=_params("parallel", "arbitrary"),
        name="hyena_longconv",
    )(u_arr, g_arr, conv_w, conv_b, conv_w, conv_b, f, gm, kre, kim, d)


def _hyena_consts(l, cw):
    p = l // 2
    e = np.arange(p)
    pos = np.concatenate([e, p + e, p - e])
    t = np.linspace(0.0, 1.0, l, dtype=np.float32)[pos][:, None]
    w = (2.0 * math.pi * np.arange(l, dtype=np.float32) / l).astype(np.float32)[pos][:, None]
    fq = np.linspace(1e-4, C_BANDS - 1, C_BANDS, dtype=np.float32)[None, :]
    wf = jnp.asarray(w) * jnp.asarray(fq)
    z = jnp.concatenate([jnp.asarray(t), jnp.cos(wf), -jnp.sin(wf)], axis=-1)
    z = jnp.pad(z, ((0, 0), (0, C_FILTER_WIDTH - z.shape[1])))
    z = jnp.concatenate([z[:3 * p // 2], z[3 * p // 2:]], axis=1)
    deltas = np.abs(np.linspace(HYENA_MIN_DECAY, HYENA_MAX_DECAY, cw, dtype=np.float32))[None, :]
    return z, jnp.asarray(t), jnp.asarray(deltas)


def _pack_filter_mlp(w1, b1, freq, w2, b2):
    fw = w2.shape[0]
    w1p = jnp.pad(w1, ((0, fw - w1.shape[0]), (0, 0)))
    zero = jnp.zeros((fw, fw), F32)
    diag = lambda w: jnp.concatenate([jnp.concatenate([w, zero], axis=1), jnp.concatenate([zero, w], axis=1)], axis=0)
    twice = lambda v: jnp.tile(v.reshape(1, fw), (1, 2))
    return diag(w1p), twice(b1), twice(freq), diag(w2), twice(b2)


def _hyena(hy, hconst, f, gm, fh, conv_w, conv_b, w1p, b1, freq, w2, b2, w3, dbias):
    z, t, deltas = hconst
    ah, al, sh, sl = _filters(z, t, w1p, b1, freq, w2, b2, w3, deltas)
    kre, kim = _spectrum(f, ah, al, sh, sl, fh)
    zz = _longconv(hy, 0, hy, 1, conv_w, conv_b, f, gm, kre, kim, 0, dbias[0:1], conv_u=True, fh=fh)
    return _longconv(zz, 0, hy, 2, conv_w, conv_b, f, gm, kre, kim, 1, dbias[1:2], conv_u=False, fh=fh)


FFN_CHUNK = 256


def _mix_ffn_kernel(x_ref, a_ref, b_ref, c_ref, wo_ref, g1_ref, gn_ref, sh_ref, sc_ref, g2_ref,
                    wg_ref, wu_ref, wd_ref, o_ref, *, fc):
    wa, wb = a_ref.shape[1], b_ref.shape[1]
    mix = (_dot(a_ref[...], wo_ref[:wa, :]) + _dot(b_ref[...], wo_ref[wa:wa + wb, :])
           + _dot(c_ref[...], wo_ref[wa + wb:, :]))
    x1 = x_ref[...] + g1_ref[...] * mix
    xb = _rms_mod(x1, gn_ref[...], sh_ref[...], sc_ref[...]).astype(BF16)
    acc = None
    for c0 in range(0, wg_ref.shape[1], fc):
        hg = _dot(xb, wg_ref[:, c0:c0 + fc])
        hu = _dot(xb, wu_ref[:, c0:c0 + fc])
        act = ((hg * jax.nn.sigmoid(hg)) * hu).astype(BF16)
        part = _dot(act, wd_ref[c0:c0 + fc, :])
        acc = part if acc is None else acc + part
    o_ref[...] = x1 + g2_ref[...] * acc


def _mix_ffn(x, a, b, c, wo, g1, gn, sh2, sc2, g2, wg, wu, wd, layer):
    bn, l, d = x.shape
    tm = min(TOKEN_TILE, l)
    row = pl.BlockSpec((None, 1, d), lambda bb, t: (bb, 0, 0))
    tok = lambda width: pl.BlockSpec((None, tm, width), lambda bb, t: (bb, t, 0))
    return pl.pallas_call(
        functools.partial(_mix_ffn_kernel, fc=FFN_CHUNK),
        out_shape=jax.ShapeDtypeStruct((bn, l, d), F32),
        grid=(bn, l // tm),
        in_specs=[tok(d), tok(a.shape[2]), tok(b.shape[2]), tok(c.shape[2]), _layer_of(wo, layer),
                  row, pl.BlockSpec((1, d), lambda bb, t: (0, 0)), row, row, row,
                  _layer_of(wg, layer), _layer_of(wu, layer), _layer_of(wd, layer)],
        out_specs=tok(d),
        compiler_params=_params("parallel", "parallel"),
        name="mix_ffn",
    )(x, a, b, c, wo, g1, gn, sh2, sc2, g2, wg, wu, wd)


def _rope_tables(l):
    half = HEAD_DIM // 2
    nfreq = half // 2
    inv = ROPE_BASE ** (-jnp.arange(nfreq, dtype=F32) / nfreq)
    pos = jnp.arange(l)
    rows, cols = pos // GRID_W, pos % GRID_W
    ang = jnp.concatenate([rows.astype(F32)[:, None] * inv[None, :]] * 2
                          + [cols.astype(F32)[:, None] * inv[None, :]] * 2, axis=-1)
    first = (np.arange(HEAD_DIM) % half) < nfreq
    cos, sin = jnp.cos(ang), jnp.sin(ang)
    sin_a = jnp.where(first[None, :], -sin, 0.0)
    sin_b = jnp.where(first[None, :], 0.0, sin)
    reps = LANES // HEAD_DIM
    return tuple(jnp.tile(tb, (1, reps)) for tb in (cos, sin_a, sin_b))


def kernel(x, c, ctx, c_ctx, ada_w, ada_b, norm1_g, norm2_g, w_in, qnorm_a, knorm_a, sink_a, qnorm_b, knorm_b,
           rpb_b, conv_w, conv_b, filt_w1, filt_b1, filt_freq, filt_w2, filt_b2, filt_w3, hyena_bias, w_out,
           ffn_w_gate, ffn_w_up, ffn_w_down):
    bn, l, d = x.shape
    lc = ctx.shape[1]
    depth = ada_w.shape[0]
    cw = hyena_bias.shape[2]
    assert l % (A_STEP_BLOCKS * A_BLOCK) == 0 and l >= 3 * A_SPAN and lc % 256 == 0
    assert l % (B_STEP_GROUPS * B_GROUP * GRID_W) == 0 and l // GRID_W >= 3 * B_GROUP

    mod_rows = 8 * (-(-(bn + 1) // 8))
    c_all = jnp.zeros((mod_rows, d), F32).at[:bn].set(c).at[bn].set(c_ctx)
    mod = _modulation(c_all, ada_w, ada_b)

    a_order = (0, 2, 1, 3)
    a_kv = tuple(hh // (A_HEADS // A_KV_HEADS) for hh in a_order)
    heads_a = lambda t, axis: [lax.slice_in_dim(t, hh * HEAD_DIM, (hh + 1) * HEAD_DIM, axis=axis) for hh in a_order]
    qa_w = A_HEADS * HEAD_DIM
    w_in_b = w_in.astype(BF16)
    w_out_b = jnp.concatenate(heads_a(w_out, 1) + [w_out[:, qa_w:]], axis=1).astype(BF16)
    sink_p = LOG2E * jnp.stack([sink_a[:, hh] for hh in a_order], axis=1)
    wg_b, wu_b, wd_b = ffn_w_gate.astype(BF16), ffn_w_up.astype(BF16), ffn_w_down.astype(BF16)

    lane = np.arange(256)
    e_heads = jnp.asarray((lane[:, None] // HEAD_DIM) == (lane[None, :] // HEAD_DIM), BF16)
    tile4 = lambda g: jnp.tile(g.reshape(1, HEAD_DIM), (1, 256 // HEAD_DIM))
    rope_lat = _rope_tables(l)

    fh_lat, fh_ctx = min(FREQ_TILE, l // 2), min(FREQ_TILE, lc // 2)
    f_lat, g_lat = _dft_matrices(l // 2, fh_lat)
    f_ctx, g_ctx = _dft_matrices(lc // 2, fh_ctx)
    hc_lat = _hyena_consts(l, cw)
    hc_ctx = _hyena_consts(lc, cw)

    xc = ctx
    for i in range(depth):
        last = i == depth - 1
        m = mod[i]
        part = lambda rows, k: rows[:, None, k * d:(k + 1) * d]
        lat = [part(m[:bn], k) for k in range(6)]
        cx = [part(m[bn:bn + 1], k) for k in range(6)]
        g1n = norm1_g[i].reshape(1, d)
        g2n = norm2_g[i].reshape(1, d)
        gqa, gka, gqb, gkb = tile4(qnorm_a[i]), tile4(knorm_a[i]), tile4(qnorm_b[i]), tile4(knorm_b[i])

        qa, ka, va, qb, kb, vb, hy = _proj_in(x, lat[0], lat[1], g1n, w_in_b, i, gqa, gka, gqb, gkb, e_heads,
                                              rope_lat)
        ctx_out = _proj_in(xc.reshape(1, bn * lc, d), cx[0], cx[1], g1n, w_in_b, i, gqa, gka, gqb, gkb, e_heads,
                           None)
        qa_c, ka_c, va_c, qb_c, kb_c, vb_c, hy_c = [t.reshape(bn, lc, -1) for t in ctx_out]
        hy_args = (conv_w[i], conv_b[i].reshape(1, -1),
                   *_pack_filter_mlp(filt_w1[i], filt_b1[i], filt_freq[i], filt_w2[i], filt_b2[i]),
                   filt_w3[i], hyena_bias[i])

        sink_row = jnp.repeat(sink_p[i], A_BLOCK).reshape(1, A_HEADS * A_BLOCK)
        out_a = _attn_a(qa, ka, va, ka_c, va_c, sink_row)
        out_b = _attn_b(qb, kb, vb, kb_c, vb_c, _rpb_tables(rpb_b[i]))
        out_c = _hyena(hy, hc_lat, f_lat, g_lat, fh_lat, *hy_args)
        x = _mix_ffn(x, out_a, out_b, out_c, w_out_b, lat[2], g2n, lat[3], lat[4], lat[5], wg_b, wu_b, wd_b, i)

        if not last:
            oa_c = _dense_attn(qa_c, ka_c, va_c, sink_p[i], kv_of=a_kv, use_sink=True)
            ob_c = _dense_attn(qb_c, kb_c, vb_c, sink_p[i], kv_of=tuple(range(B_HEADS)), use_sink=False)
            oc_c = _hyena(hy_c, hc_ctx, f_ctx, g_ctx, fh_ctx, *hy_args)
            flat = lambda t: t.reshape(1, bn * lc, -1)
            xc = _mix_ffn(flat(xc), flat(oa_c), flat(ob_c), flat(oc_c), w_out_b, cx[2], g2n, cx[3], cx[4], cx[5],
                          wg_b, wu_b, wd_b, i).reshape(bn, lc, d)
    return x
```

```python
import functools
import math

import jax
import jax.numpy as jnp
import numpy as np
from jax import lax
from jax.experimental import pallas as pl
from jax.experimental.pallas import tpu as pltpu

F32 = jnp.float32
BF16 = jnp.bfloat16

GRID_W = 64
HEAD_DIM = 64
A_HEADS = 4
A_KV_HEADS = 2
A_BLOCK = 128
B_HEADS = 4
NA_ROWS = 8
NA_COLS = 16
C_ORDER = 2
C_DIRS = 2
C_FILTER_WIDTH = 64
C_BANDS = 16
ROPE_BASE = 10000.0
EPS = 1e-6
LOG2E = math.log2(math.e)
HYENA_MIN_DECAY = math.log(1e-2) / 1.5
HYENA_MAX_DECAY = math.log(1e-2) / 0.3

V7X_VMEM_BYTES = 64 * 1024 * 1024
VMEM_LIMIT = V7X_VMEM_BYTES - 8 * 1024 * 1024
LANES = 128
TOKEN_TILE = 512
FREQ_TILE = 512


def _params(*sem):
    return pltpu.CompilerParams(dimension_semantics=sem, vmem_limit_bytes=VMEM_LIMIT)


def _dot(a, b):
    return jnp.dot(a, b, preferred_element_type=F32)


def _dot_nt(a, b):
    return lax.dot_general(a, b, (((1,), (1,)), ((), ())), preferred_element_type=F32)


def _dot_hp(a, b):
    ah, al = _split_bf16(a)
    bh, bl = _split_bf16(b)
    return _dot(ah, bh) + (_dot(ah, bl) + _dot(al, bh))


def _split_bf16(v):
    hi = v.astype(BF16)
    lo = (v - hi.astype(F32)).astype(BF16)
    return hi, lo


def _layer_of(stack, layer):
    return pl.BlockSpec((None,) + stack.shape[1:], lambda *_: (layer, 0, 0), pipeline_mode=pl.Buffered(1))


def _resident(shape):
    nd = len(shape)
    return pl.BlockSpec(shape, lambda *_: (0,) * nd, pipeline_mode=pl.Buffered(1))


def _mod_kernel(c_ref, w_ref, b_ref, o_ref):
    cv = c_ref[...]
    sc = (cv * jax.nn.sigmoid(cv)).astype(BF16)
    o_ref[...] = _dot(sc, w_ref[...].astype(BF16)) + b_ref[...]


def _modulation(c_all, ada_w, ada_b):
    depth, d, n = ada_w.shape
    rows = c_all.shape[0]
    tn = 1536
    return pl.pallas_call(
        _mod_kernel,
        out_shape=jax.ShapeDtypeStruct((depth, rows, n), F32),
        grid=(depth, n // tn),
        in_specs=[
            pl.BlockSpec((rows, d), lambda i, j: (0, 0)),
            pl.BlockSpec((None, d, tn), lambda i, j: (i, 0, j)),
            pl.BlockSpec((None, 1, tn), lambda i, j: (i, 0, j)),
        ],
        out_specs=pl.BlockSpec((None, rows, tn), lambda i, j: (i, 0, j)),
        compiler_params=_params("arbitrary", "arbitrary"),
        name="modulation",
    )(c_all, ada_w, ada_b.reshape(depth, 1, n))


def _rms_mod(x, g, shift, scale):
    y = x * lax.rsqrt(jnp.mean(x * x, axis=-1, keepdims=True) + EPS)
    return (y * g) * (1 + scale) + shift


def _head_norm(h, g, e):
    ss = _dot((h * h).astype(BF16), e)
    return (h * lax.rsqrt(ss * (1.0 / HEAD_DIM) + EPS)) * g


def _rope(t, cos, sin_a, sin_b):
    outs = []
    for c in range(t.shape[1] // LANES):
        tc = t[:, c * LANES:(c + 1) * LANES]
        outs.append(tc * cos + pltpu.roll(tc, LANES - 16, 1) * sin_a + pltpu.roll(tc, 16, 1) * sin_b)
    return outs[0] if len(outs) == 1 else jnp.concatenate(outs, axis=1)


PROJ_ROWS = 1024
PROJ_SPLIT = 2


def _proj_in_kernel(x_ref, shift_ref, scale_ref, g_ref, w_ref, gqa_ref, gka_ref, gqb_ref, gkb_ref, e_ref, *rest,
                    rope):
    if rope:
        cos_ref, sa_ref, sb_ref = rest[:3]
        rest = rest[3:]
    qa_ref, ka_ref, va_ref, qb_ref, kb_ref, vb_ref, hy_ref = rest
    qscale = LOG2E * HEAD_DIM ** -0.5
    hm = x_ref.shape[0] // PROJ_SPLIT

    def project(part):
        rows = slice(part * hm, (part + 1) * hm)
        xb = _rms_mod(x_ref[rows, :], g_ref[...], shift_ref[...], scale_ref[...]).astype(BF16)
        return _dot(xb, w_ref[...])

    def finish(part, h_all):
        rows = slice(part * hm, (part + 1) * hm)

        def proj(lo, hi):
            return h_all[:, lo:hi]

        def maybe_rope(t):
            return _rope(t, cos_ref[rows, :], sa_ref[rows, :], sb_ref[rows, :]) if rope else t

        def store_v(ref, v):
            if rope:
                for i in range(hm // LANES):
                    ref[part * (hm // LANES) + i] = v[i * LANES:(i + 1) * LANES, :].T.astype(BF16)
            else:
                ref[rows, :] = v.astype(BF16)

        qa = maybe_rope(_head_norm(proj(0, 256), gqa_ref[...], e_ref[...])) * qscale
        t0, t1 = qa[:, :LANES], qa[:, LANES:]
        lo = lax.broadcasted_iota(jnp.int32, t0.shape, 1) < HEAD_DIM
        qa_ref[rows, :LANES] = jnp.where(lo, t0, pltpu.roll(t1, HEAD_DIM, 1)).astype(BF16)
        qa_ref[rows, LANES:] = jnp.where(lo, pltpu.roll(t0, HEAD_DIM, 1), t1).astype(BF16)
        kva = proj(256, 512)
        ka = maybe_rope(_head_norm(kva[:, :LANES], gka_ref[:, :128], e_ref[:128, :128]))
        ka_ref[rows, :] = ka.astype(BF16)
        store_v(va_ref, kva[:, LANES:])
        qb = _head_norm(proj(512, 768), gqb_ref[...], e_ref[...])
        qb_ref[rows, :] = (qb * qscale).astype(BF16)
        kb_ref[rows, :] = _head_norm(proj(768, 1024), gkb_ref[...], e_ref[...]).astype(BF16)
        store_v(vb_ref, proj(1024, 1280))
        hy_ref[rows, :] = proj(1280, 2816).astype(BF16)

    for part in range(PROJ_SPLIT):
        finish(part, project(part))


def _proj_in(x, shift, scale, g, w, layer, gqa, gka, gqb, gkb, e, rope_tabs):
    bn, l, d = x.shape
    n = w.shape[2]
    rope = rope_tabs is not None
    tm = min(PROJ_ROWS, l)
    row = lambda width: pl.BlockSpec((None, 1, width), lambda b, t: (b, 0, 0))
    const = lambda shape: pl.BlockSpec(shape, lambda b, t: (0,) * len(shape))
    tok = lambda width: pl.BlockSpec((None, tm, width), lambda b, t: (b, t, 0))
    tab = pl.BlockSpec((tm, LANES), lambda b, t: (t, 0))
    widths = (256, 128, 128, 256, 256, 256, n - 1280)
    shapes = [(bn, l, wd) for wd in widths]
    specs = [tok(wd) for wd in widths]
    if rope:
        for i in (2, 5):
            shapes[i] = (bn, l // LANES, widths[i], LANES)
            specs[i] = pl.BlockSpec((None, tm // LANES, widths[i], LANES), lambda b, t: (b, t, 0, 0))
    return pl.pallas_call(
        functools.partial(_proj_in_kernel, rope=rope),
        out_shape=[jax.ShapeDtypeStruct(sh, BF16) for sh in shapes],
        grid=(bn, l // tm),
        in_specs=[tok(d), row(d), row(d), const((1, d)), _layer_of(w, layer),
                  const((1, 256)), const((1, 256)), const((1, 256)), const((1, 256)), const((256, 256))]
                 + ([tab, tab, tab] if rope else []),
        out_specs=specs,
        compiler_params=_params("parallel", "parallel"),
        name="proj_in_rope" if rope else "proj_in",
    )(x, shift, scale, g, w, gqa, gka, gqb, gkb, e, *(rope_tabs or ()))


def _softmax_pv(s_list, v_list, sink):
    m = s_list[0].max(axis=-1, keepdims=True)
    for s in s_list[1:]:
        m = jnp.maximum(m, s.max(axis=-1, keepdims=True))
    if sink is not None:
        m = jnp.maximum(m, sink)
    den = None
    out = None
    for s, v in zip(s_list, v_list):
        p = jnp.exp2(s - m)
        ps = p.sum(axis=-1, keepdims=True)
        den = ps if den is None else den + ps
        o = _dot(p.astype(BF16), v)
        out = o if out is None else out + o
    if sink is not None:
        den = den + jnp.exp2(sink - m)
    return out * (1.0 / den)


def _stack_heads(q):
    lo = lax.broadcasted_iota(jnp.int32, q.shape, 1) < HEAD_DIM
    zero = jnp.zeros_like(q)
    return jnp.concatenate([jnp.where(lo, q, zero), jnp.where(lo, zero, q)], axis=0)


SOFTMAX_CHUNK = 32


def _softmax_keys(s_ref, p_ref, nloc, add_loc, sink):
    nk = s_ref.shape[0]
    ch = SOFTMAX_CHUNK
    macc = None
    for r0 in range(0, nk, ch):
        s = s_ref[r0:r0 + ch, :]
        if r0 < nloc:
            s = s + add_loc(r0)
            s_ref[r0:r0 + ch, :] = s
        macc = s if macc is None else jnp.maximum(macc, s)
    m = macc.max(axis=0, keepdims=True)
    if sink is not None:
        m = jnp.maximum(m, sink)
    sacc = None
    for r0 in range(0, nk, ch):
        p = jnp.exp2(s_ref[r0:r0 + ch, :] - m)
        sacc = p if sacc is None else sacc + p
        p_ref[r0:r0 + ch, :] = p.astype(BF16)
    den = sacc.sum(axis=0, keepdims=True)
    if sink is not None:
        den = den + jnp.exp2(sink - m)
    return 1.0 / den


A_STEP_BLOCKS = 16
A_SPAN = 3 * A_BLOCK
A_SCORE_BUFS = 2


def _attn_a_kernel(q_ref, k_ref, vt_ref, kx_ref, vxt_ref, mask_ref, sink_ref, o_ref, s_ref, p_ref):
    l = k_ref.shape[0]
    nb = l // A_BLOCK
    hd = HEAD_DIM
    def window(u):
        n = pl.program_id(1) * A_STEP_BLOCKS + u
        tile0 = jnp.clip(n - 1, 0, nb - 3)
        pat = jnp.where(n == 0, 0, jnp.where(n == nb - 1, 2, 1))
        return tile0, pat

    def scores(u):
        tile0, _ = window(u)
        start = pl.multiple_of(tile0 * A_BLOCK, A_BLOCK)
        q = q_ref[u * A_BLOCK:(u + 1) * A_BLOCK, :]
        qs = jnp.concatenate([_stack_heads(q[:, :LANES]), _stack_heads(q[:, LANES:])], axis=0)
        s_ref[u % A_SCORE_BUFS] = _dot_nt(jnp.concatenate([k_ref[pl.ds(start, A_SPAN), :], kx_ref[...]], axis=0), qs)

    def output(u, r):
        tile0, _ = window(u)
        vt = jnp.concatenate([vt_ref[tile0 + i] for i in range(3)] + [vxt_ref[...]], axis=1)
        ot = _dot(vt, p_ref[u % 2]) * r
        ot = jnp.concatenate([ot[(i % 2) * hd:(i % 2 + 1) * hd, i * A_BLOCK:(i + 1) * A_BLOCK]
                              for i in range(A_HEADS)], axis=0)
        o_ref[u * A_BLOCK:(u + 1) * A_BLOCK, :] = ot.T.astype(o_ref.dtype)

    ahead = A_SCORE_BUFS - 1
    for u in range(ahead):
        scores(u)
    pending = None
    for u in range(A_STEP_BLOCKS):
        if u + ahead < A_STEP_BLOCKS:
            scores(u + ahead)
        _, pat = window(u)
        r = _softmax_keys(s_ref.at[u % A_SCORE_BUFS], p_ref.at[u % 2], A_SPAN,
                          lambda r0: mask_ref[pat, r0:r0 + SOFTMAX_CHUNK, :], sink_ref[...])
        if pending is not None:
            output(*pending)
        pending = (u, r)
    output(*pending)


def _attn_a_mask():
    i = np.arange(A_BLOCK)[None, :]
    j = np.arange(A_SPAN)[:, None]
    offs = (0, A_BLOCK, 2 * A_BLOCK)
    m = np.stack([np.where(np.abs(j - i - o) <= A_BLOCK, 0.0, -np.inf) for o in offs])
    return jnp.asarray(np.tile(m, (1, 1, A_HEADS)), F32)


def _attn_a(q, k, vt, kx, vx, sink_row):
    bn, l, qw = q.shape
    lc = kx.shape[1]
    kvw = k.shape[2]
    nb = l // A_BLOCK
    qs = A_STEP_BLOCKS * A_BLOCK
    mask = _attn_a_mask()
    vxt = jnp.swapaxes(vx, 1, 2)
    seq = pl.BlockSpec((None, l, kvw), lambda b, s: (b, 0, 0))
    qblk = pl.BlockSpec((None, qs, qw), lambda b, s: (b, s, 0))
    nq = A_HEADS * A_BLOCK
    return pl.pallas_call(
        _attn_a_kernel,
        out_shape=jax.ShapeDtypeStruct((bn, l, qw), BF16),
        grid=(bn, l // qs),
        in_specs=[qblk, seq, pl.BlockSpec((None, nb, kvw, A_BLOCK), lambda b, s: (b, 0, 0, 0)),
                  pl.BlockSpec((None, lc, kvw), lambda b, s: (b, 0, 0)),
                  pl.BlockSpec((None, kvw, lc), lambda b, s: (b, 0, 0)),
                  _resident(mask.shape), _resident(sink_row.shape)],
        out_specs=qblk,
        scratch_shapes=[pltpu.VMEM((A_SCORE_BUFS, A_SPAN + lc, nq), F32), pltpu.VMEM((2, A_SPAN + lc, nq), BF16)],
        compiler_params=_params("parallel", "arbitrary"),
        name="window_attn",
    )(q, k, vt, kx, vxt, mask, sink_row)


def _dense_attn_kernel(sink_ref, q_ref, k_ref, v_ref, o_ref, *, kv_of, use_sink):
    for h, kv in enumerate(kv_of):
        hs = slice(h * HEAD_DIM, (h + 1) * HEAD_DIM)
        ks = slice(kv * HEAD_DIM, (kv + 1) * HEAD_DIM)
        s = _dot_nt(q_ref[:, hs], k_ref[:, ks])
        o = _softmax_pv([s], [v_ref[:, ks]], sink_ref[h] if use_sink else None)
        o_ref[:, hs] = o.astype(o_ref.dtype)


def _dense_attn(q, k, v, sink, *, kv_of, use_sink):
    bn, l, qw = q.shape
    kvw = k.shape[2]
    full = lambda wd: pl.BlockSpec((None, l, wd), lambda b: (b, 0, 0))
    return pl.pallas_call(
        functools.partial(_dense_attn_kernel, kv_of=kv_of, use_sink=use_sink),
        out_shape=jax.ShapeDtypeStruct((bn, l, qw), BF16),
        grid=(bn,),
        in_specs=[pl.BlockSpec(memory_space=pltpu.SMEM), full(qw), full(kvw), full(kvw)],
        out_specs=full(qw),
        compiler_params=_params("parallel"),
        name="ctx_attn_sink" if use_sink else "ctx_attn",
    )(sink, q, k, v)


def _rpb_kernel(r_ref, oh_ref, ok_ref, o_ref):
    r = r_ref[...]
    b1 = r.astype(BF16)
    r2 = r - b1.astype(F32)
    b2 = r2.astype(BF16)
    b3 = (r2 - b2.astype(F32)).astype(BF16)
    oh = oh_ref[...]
    bias = (_dot(b1, oh) + _dot(b2, oh)) + _dot(b3, oh)
    o_ref[...] = jnp.where(ok_ref[...] > 0.5, bias * LOG2E, -jnp.inf)


def _rpb_slots():
    a = np.arange(B_GROUP)[:, None]
    kr = np.arange(B_SLAB)[None, :]
    dr = np.stack([kr - a + NA_ROWS - 1, kr - a + NA_ROWS // 2 - 1, kr - a + (B_GROUP + NA_ROWS - 1 - B_SLAB)])
    lo = np.stack([0 * a + 0 * kr, a + 0 * kr, 0 * a + (B_SLAB - NA_ROWS) + 0 * kr])
    valid = (kr[None] >= lo) & (kr[None] < lo + NA_ROWS)
    return np.where(valid, dr, -1)


def _rpb_assemble_kernel(t_ref, o_ref, *, n_dr):
    slots = _rpb_slots()
    w = GRID_W
    blank = jnp.full((w, w), -jnp.inf, F32)
    for pat in range(3):
        for t in range(o_ref.shape[1]):
            for kr in range(B_SLAB):
                for hh in range(2):
                    for a in range(B_GROUP):
                        dr = int(slots[pat, a, kr])
                        tile = t_ref[(2 * t + hh) * n_dr + dr] if dr >= 0 else blank
                        c0 = (hh * B_GROUP + a) * w
                        o_ref[pat, t, kr * w:(kr + 1) * w, c0:c0 + w] = tile


def _rpb_tables(rpb):
    h, nr, nc = rpb.shape
    col = np.arange(GRID_W)
    dc = np.clip(col[:, None] - col[None, :], 1 - NA_COLS, NA_COLS - 1) + NA_COLS - 1
    onehot = (dc.reshape(1, -1) == np.arange(nc)[:, None]).astype(np.float32)
    onehot = np.concatenate([onehot, np.zeros((32 - nc, GRID_W * GRID_W), np.float32)], axis=0)
    col_start = np.clip(col - NA_COLS // 2, 0, GRID_W - NA_COLS)
    col_ok = (col[:, None] >= col_start[None, :]) & (col[:, None] < col_start[None, :] + NA_COLS)
    rows = 64
    r2 = jnp.zeros((rows, 32), F32).at[:h * nr, :nc].set(rpb.reshape(h * nr, nc))
    tiles = pl.pallas_call(
        _rpb_kernel,
        out_shape=jax.ShapeDtypeStruct((rows, GRID_W * GRID_W), F32),
        name="rpb_table",
    )(r2, jnp.asarray(onehot, BF16), jnp.asarray(col_ok.reshape(1, -1), F32))
    tiles = tiles.reshape(rows, GRID_W, GRID_W)
    return pl.pallas_call(
        functools.partial(_rpb_assemble_kernel, n_dr=nr),
        out_shape=jax.ShapeDtypeStruct((3, h // 2, B_SLAB * GRID_W, 2 * B_GROUP * GRID_W), F32),
        compiler_params=_params(),
        name="rpb_assemble",
    )(tiles)


B_GROUP = 4
B_SCORE_AHEAD = 2
B_STEP_GROUPS = 4
B_SLAB = 12


def _attn_b_kernel(q_ref, k_ref, vt_ref, kx_ref, vxt_ref, tbl_ref, o_ref, s_ref, p_ref):
    ng = pl.num_programs(1) * B_STEP_GROUPS
    rows = k_ref.shape[0] // GRID_W
    nloc = B_SLAB * GRID_W
    gq = B_GROUP * GRID_W
    units = [(u, t) for u in range(B_STEP_GROUPS) for t in range(B_HEADS // 2)]

    def slab(u):
        g = pl.program_id(1) * B_STEP_GROUPS + u
        base = jnp.clip(g * B_GROUP - NA_ROWS // 2, 0, rows - B_SLAB)
        pat = jnp.where(g == 0, 0, jnp.where(g == ng - 1, 2, 1))
        return base, pat

    def scores(u, t):
        base, _ = slab(u)
        start = pl.multiple_of(base * GRID_W, LANES)
        ts = slice(t * LANES, (t + 1) * LANES)
        qs = _stack_heads(q_ref[u * gq:(u + 1) * gq, ts])
        s_ref[u, t] = _dot_nt(jnp.concatenate([k_ref[pl.ds(start, nloc), ts], kx_ref[:, ts]], axis=0), qs)

    def output(u, t, r):
        base, _ = slab(u)
        tile0 = base // (LANES // GRID_W)
        ts = slice(t * LANES, (t + 1) * LANES)
        vt = jnp.concatenate([vt_ref[tile0 + j, ts, :] for j in range(nloc // LANES)] + [vxt_ref[ts, :]],
                             axis=1)
        ot = _dot(vt, p_ref[u, t]) * r
        ot = jnp.concatenate([ot[:HEAD_DIM, :gq], ot[HEAD_DIM:, gq:]], axis=0)
        o_ref[u * gq:(u + 1) * gq, ts] = ot.T.astype(o_ref.dtype)

    ahead = B_SCORE_AHEAD
    for unit in units[:ahead]:
        scores(*unit)
    pending = None
    for i, (u, t) in enumerate(units):
        if i + ahead < len(units):
            scores(*units[i + ahead])
        _, pat = slab(u)
        r = _softmax_keys(s_ref.at[u, t], p_ref.at[u, t], nloc,
                          lambda r0: tbl_ref[pat, t, r0:r0 + SOFTMAX_CHUNK, :], None)
        if pending is not None:
            output(*pending)
        pending = (u, t, r)
    output(*pending)


def _attn_b(q, k, vt, kx, vx, tbl):
    bn, l, w = q.shape
    lc = kx.shape[1]
    gq = B_GROUP * GRID_W
    sq = B_STEP_GROUPS * gq
    nk = B_SLAB * GRID_W + lc
    vxt = jnp.swapaxes(vx, 1, 2)
    seq = pl.BlockSpec((None, l, w), lambda b, g: (b, 0, 0))
    qblk = pl.BlockSpec((None, sq, w), lambda b, g: (b, g, 0))
    return pl.pallas_call(
        _attn_b_kernel,
        out_shape=jax.ShapeDtypeStruct((bn, l, w), BF16),
        grid=(bn, l // sq),
        in_specs=[qblk, seq, pl.BlockSpec((None, l // LANES, w, LANES), lambda b, g: (b, 0, 0, 0)),
                  pl.BlockSpec((None, lc, w), lambda b, g: (b, 0, 0)),
                  pl.BlockSpec((None, w, lc), lambda b, g: (b, 0, 0)),
                  _resident(tbl.shape)],
        out_specs=qblk,
        scratch_shapes=[pltpu.VMEM((B_STEP_GROUPS, B_HEADS // 2, nk, 2 * gq), F32),
                        pltpu.VMEM((B_STEP_GROUPS, B_HEADS // 2, nk, 2 * gq), BF16)],
        compiler_params=_params("parallel", "arbitrary"),
        name="nbr_attn",
    )(q, k, vt, kx, vxt, tbl)


def _dft_matrices(l, fh):
    k = jnp.arange(l, dtype=jnp.int32)[:, None]
    n = jnp.arange(l, dtype=jnp.int32)[None, :]
    ang = (((2 * k + 1) * n) % (4 * l)).astype(F32) * (math.pi / (2 * l))
    fre = jnp.cos(ang).reshape(l // fh, fh, l)
    fim = (-jnp.sin(ang)).reshape(l // fh, fh, l)
    f = jnp.concatenate([fre, fim], axis=1).reshape(2 * l, l)
    g = f.T * (1.0 / l)
    return f.astype(BF16), g.astype(BF16)


def _filter_kernel(z_ref, t_ref, w1_ref, b1_ref, fr_ref, w2_ref, b2_ref, w3_ref, dl_ref,
                   ah_ref, al_ref, sh_ref, sl_ref, hid_ref):
    @pl.when(pl.program_id(0) == 0)
    def _():
        fr = fr_ref[...]
        hid = jnp.sin(fr * (_dot_hp(z_ref[...], w1_ref[...]) + b1_ref[...]))
        hid_ref[...] = jnp.sin(fr * (_dot_hp(hid, w2_ref[...]) + b2_ref[...]))

    cw = dl_ref.shape[1]
    p = t_ref.shape[0] // 3
    hid = hid_ref[...]
    w3 = w3_ref[...]
    none = jnp.zeros_like(w3)
    taps = jnp.concatenate([_dot_hp(hid, jnp.concatenate([w3, none], axis=0)),
                            _dot_hp(hid, jnp.concatenate([none, w3], axis=0))], axis=0)
    decay = jnp.exp(-t_ref[...] * dl_ref[...])
    kf = taps[:, :cw] * decay
    kb = taps[:, cw:] * decay
    kf0, kf1, kfr = kf[:p], kf[p:2 * p], kf[2 * p:]
    kb0, kb1, kbr = kb[:p], kb[p:2 * p], kb[2 * p:]
    first = lax.broadcasted_iota(jnp.int32, (p, cw), 0) == 0
    drop0 = lambda v: jnp.where(first, 0.0, v)
    kb0 = drop0(kb0)
    colsum = lambda v: jnp.sum(jnp.abs(v), axis=0, keepdims=True)
    inv = 1.0 / (colsum(kf0) + colsum(kf1) + colsum(kb0) + colsum(kb1))
    pairs = ((kf0, kb0), (kf1, drop0(kfr)), (kbr, drop0(kb1)))
    for d, (cp, cm) in enumerate(pairs):
        cols = slice(d * cw, (d + 1) * cw)
        ah_ref[:, cols], al_ref[:, cols] = _split_bf16((cp + cm) * inv)
        sh_ref[:, cols], sl_ref[:, cols] = _split_bf16((cp - cm) * inv)


def _filters(z, t, w1, b1, freq, w2, b2, w3, deltas):
    p = t.shape[0] // 3
    cw = deltas.shape[1]
    fw = w2.shape[0]
    const = lambda shape: pl.BlockSpec(shape, lambda o: (0,) * len(shape))
    out = jax.ShapeDtypeStruct((p, C_ORDER * 3 * cw), BF16)
    oblk = pl.BlockSpec((p, 3 * cw), lambda o: (0, o))
    return pl.pallas_call(
        _filter_kernel,
        out_shape=[out] * 4,
        grid=(C_ORDER,),
        in_specs=[const(z.shape), const(t.shape), const(w1.shape), const((1, fw)), const((1, fw)),
                  const((fw, fw)), const((1, fw)), pl.BlockSpec((w3.shape[0], C_DIRS * cw), lambda o: (0, o)),
                  const((1, cw))],
        out_specs=[oblk] * 4,
        scratch_shapes=[pltpu.VMEM(z.shape, F32)],
        compiler_params=_params("arbitrary"),
        name="hyena_filter",
    )(z, t, w1, b1, freq, w2, b2, w3, deltas)


def _spectrum_kernel(f_ref, ah_ref, al_ref, sh_ref, sl_ref, kre_ref, kim_ref):
    fh = kre_ref.shape[0]
    fre = f_ref[:fh, :]
    fim = f_ref[fh:, :]
    kre_ref[...] = _dot(fre, ah_ref[...]) + _dot(fre, al_ref[...])
    kim_ref[...] = _dot(fim, sh_ref[...]) + _dot(fim, sl_ref[...])


def _spectrum(f, ah, al, sh, sl, fh):
    p, n = ah.shape
    tn = n // C_ORDER
    taps = pl.BlockSpec((p, tn), lambda o, j: (0, o))
    out = jax.ShapeDtypeStruct((p, n), F32)
    oblk = pl.BlockSpec((fh, tn), lambda o, j: (j, o))
    return pl.pallas_call(
        _spectrum_kernel,
        out_shape=[out, out],
        grid=(C_ORDER, p // fh),
        in_specs=[pl.BlockSpec((2 * fh, p), lambda o, j: (j, 0)), taps, taps, taps, taps],
        out_specs=[oblk, oblk],
        compiler_params=_params("arbitrary", "arbitrary"),
        name="hyena_spectrum",
    )(f, ah, al, sh, sl)


def _short_conv(u, w_ref, b_ref):
    n = u.shape[0]
    row = lax.broadcasted_iota(jnp.int32, u.shape, 0)
    prev = jnp.where(row == 0, 0.0, pltpu.roll(u, 1, 0))
    nxt = jnp.where(row == n - 1, 0.0, pltpu.roll(u, n - 1, 0))
    return prev * w_ref[0:1, :] + u * w_ref[1:2, :] + nxt * w_ref[2:3, :] + b_ref[...]


def _short_conv_wrap(u, w_ref, b_ref):
    n = u.shape[0]
    return (pltpu.roll(u, 1, 0) * w_ref[0:1, :] + u * w_ref[1:2, :] + pltpu.roll(u, n - 1, 0) * w_ref[2:3, :]
            + b_ref[...])


LONGCONV_ROWS = 1024
CONV_CHUNKS = 2


def _longconv_kernel(u_ref, g_ref, cwu_ref, cbu_ref, cwg_ref, cbg_ref, f_ref, gm_ref, kre_ref, kim_ref, d_ref,
                     o_ref, ub_ref, acc_ref, *, conv_u, n_steps):
    j = pl.program_id(1)
    fh = kre_ref.shape[0]
    nb, p, cw = u_ref.shape[0], ub_ref.shape[2], d_ref.shape[1]
    cc = cw // CONV_CHUNKS
    l = 2 * p
    edge = 16

    def edge_conv(ref, w_ref, b_ref, cs):
        head = _short_conv(ref[0:2 * edge, cs].astype(F32), w_ref.at[:, cs], b_ref.at[:, cs])[:edge]
        tail = _short_conv(ref[l - 2 * edge:l, cs].astype(F32), w_ref.at[:, cs], b_ref.at[:, cs])[edge:]
        return head, tail

    def load_u(ib, c, cs):
        ub = ub_ref.at[ib, c]
        if conv_u:
            u = _short_conv_wrap(u_ref[ib, :, cs].astype(F32), cwu_ref.at[:, cs], cbu_ref.at[:, cs]).astype(BF16)
        else:
            u = u_ref[ib, :, cs]
        ub[:, :cc] = u[:p]
        ub[:, cc:] = u[p:]
        if conv_u:
            head, tail = edge_conv(u_ref.at[ib], cwu_ref, cbu_ref, cs)
            ub[0:edge, :cc] = head.astype(BF16)
            ub[p - edge:p, cc:] = tail.astype(BF16)

    def forward(ib, c):
        return _dot(f_ref[...], ub_ref[ib, c])

    def inverse(c, spec):
        u0r, u1r, u0i, u1i = spec[:fh, :cc], spec[:fh, cc:], spec[fh:, :cc], spec[fh:, cc:]
        tap = lambda ref, d: ref[:, d * cw + c * cc:d * cw + (c + 1) * cc]
        c0r, c1r, cmr = (tap(kre_ref, d) for d in range(3))
        c0i, c1i, cmi = (tap(kim_ref, d) for d in range(3))
        y0r = (c0r * u0r - c0i * u0i) + (cmr * u1r - cmi * u1i)
        y0i = (c0r * u0i + c0i * u0r) + (cmr * u1i + cmi * u1r)
        y1r = (c1r * u0r - c1i * u0i) + (c0r * u1r - c0i * u1i)
        y1i = (c1r * u0i + c1i * u0r) + (c0r * u1i + c0i * u1r)
        y = jnp.concatenate([jnp.concatenate([y0r, y1r], axis=1), jnp.concatenate([y0i, y1i], axis=1)], axis=0)
        return _dot(gm_ref[...], y.astype(BF16))

    def gated_out(ib, c, cs, conv):
        gate = _short_conv_wrap(g_ref[ib, :, cs].astype(F32), cwg_ref.at[:, cs], cbg_ref.at[:, cs])
        ub = ub_ref.at[ib, c]
        y = jnp.concatenate([conv[:, :cc], conv[:, cc:]], axis=0)
        y = y + jnp.concatenate([ub[:, :cc], ub[:, cc:]], axis=0).astype(F32) * d_ref[:, cs]
        o_ref[ib, :, cs] = (gate * y).astype(o_ref.dtype)
        head, tail = edge_conv(g_ref.at[ib], cwg_ref, cbg_ref, cs)
        o_ref[ib, 0:edge, cs] = (head * y[:edge]).astype(o_ref.dtype)
        o_ref[ib, l - edge:l, cs] = (tail * y[l - edge:]).astype(o_ref.dtype)

    def step(first, last):
        chunks = [(ib, c, slice(c * cc, (c + 1) * cc)) for ib in range(nb) for c in range(CONV_CHUNKS)]

        def start(ib, c, cs):
            if first:
                load_u(ib, c, cs)
            return forward(ib, c)

        spec = start(*chunks[0])
        pending = None
        for i, (ib, c, cs) in enumerate(chunks):
            nxt = start(*chunks[i + 1]) if i + 1 < len(chunks) else None
            conv = inverse(c, spec)
            spec = nxt
            if not first:
                conv = acc_ref[ib, c] + conv
            if last:
                if pending is not None:
                    gated_out(*pending)
                pending = (ib, c, cs, conv)
            else:
                acc_ref[ib, c] = conv
        if last:
            gated_out(*pending)

    if n_steps == 1:
        step(True, True)
    else:
        pl.when(j == 0)(lambda: step(True, False))
        pl.when(j == n_steps - 1)(lambda: step(False, True))
        if n_steps > 2:
            pl.when(jnp.logical_and(j > 0, j < n_steps - 1))(lambda: step(False, False))


def _longconv(u_arr, u_blk, g_arr, g_blk, conv_w, conv_b, f, gm, kre, kim, order, d, *, conv_u, fh):
    bn, l, _ = u_arr.shape
    p = l // 2
    cw = d.shape[1]
    ub = u_blk if conv_u else 0
    nb = math.gcd(bn, max(1, LONGCONV_ROWS // l))
    tok = lambda blk: pl.BlockSpec((nb, l, cw), lambda b, j: (b, 0, blk))
    cpar = lambda rows, blk: pl.BlockSpec((rows, cw), lambda b, j: (0, blk))
    ktab = pl.BlockSpec((fh, 3 * cw), lambda b, j: (j, order))
    return pl.pallas_call(
        functools.partial(_longconv_kernel, conv_u=conv_u, n_steps=p // fh),
        out_shape=jax.ShapeDtypeStruct((bn, l, cw), BF16),
        grid=(bn // nb, p // fh),
        in_specs=[tok(u_blk), tok(g_blk), cpar(3, ub), cpar(1, ub), cpar(3, g_blk), cpar(1, g_blk),
                  pl.BlockSpec((2 * fh, p), lambda b, j: (j, 0)),
                  pl.BlockSpec((p, 2 * fh), lambda b, j: (0, j)),
                  ktab, ktab, pl.BlockSpec((1, cw), lambda b, j: (0, 0))],
        out_specs=tok(0),
        scratch_shapes=[pltpu.VMEM((nb, CONV_CHUNKS, p, 2 * cw // CONV_CHUNKS), BF16),
                        pltpu.VMEM((nb, CONV_CHUNKS, p, 2 * cw // CONV_CHUNKS), F32)],
        compiler_params=_params("parallel", "arbitrary"),
        name="hyena_longconv",
    )(u_arr, g_arr, conv_w, conv_b, conv_w, conv_b, f, gm, kre, kim, d)


def _hyena_consts(l, cw):
    p = l // 2
    e = np.arange(p)
    pos = np.concatenate([e, p + e, p - e])
    t = np.linspace(0.0, 1.0, l, dtype=np.float32)[pos][:, None]
    w = (2.0 * math.pi * np.arange(l, dtype=np.float32) / l).astype(np.float32)[pos][:, None]
    fq = np.linspace(1e-4, C_BANDS - 1, C_BANDS, dtype=np.float32)[None, :]
    wf = jnp.asarray(w) * jnp.asarray(fq)
    z = jnp.concatenate([jnp.asarray(t), jnp.cos(wf), -jnp.sin(wf)], axis=-1)
    z = jnp.pad(z, ((0, 0), (0, C_FILTER_WIDTH - z.shape[1])))
    z = jnp.concatenate([z[:3 * p // 2], z[3 * p // 2:]], axis=1)
    deltas = np.abs(np.linspace(HYENA_MIN_DECAY, HYENA_MAX_DECAY, cw, dtype=np.float32))[None, :]
    return z, jnp.asarray(t), jnp.asarray(deltas)


def _pack_filter_mlp(w1, b1, freq, w2, b2):
    fw = w2.shape[0]
    w1p = jnp.pad(w1, ((0, fw - w1.shape[0]), (0, 0)))
    zero = jnp.zeros((fw, fw), F32)
    diag = lambda w: jnp.concatenate([jnp.concatenate([w, zero], axis=1), jnp.concatenate([zero, w], axis=1)], axis=0)
    twice = lambda v: jnp.tile(v.reshape(1, fw), (1, 2))
    return diag(w1p), twice(b1), twice(freq), diag(w2), twice(b2)


def _hyena(hy, hconst, f, gm, fh, conv_w, conv_b, w1p, b1, freq, w2, b2, w3, dbias):
    z, t, deltas = hconst
    ah, al, sh, sl = _filters(z, t, w1p, b1, freq, w2, b2, w3, deltas)
    kre, kim = _spectrum(f, ah, al, sh, sl, fh)
    zz = _longconv(hy, 0, hy, 1, conv_w, conv_b, f, gm, kre, kim, 0, dbias[0:1], conv_u=True, fh=fh)
    return _longconv(zz, 0, hy, 2, conv_w, conv_b, f, gm, kre, kim, 1, dbias[1:2], conv_u=False, fh=fh)


FFN_CHUNK = 256


def _mix_ffn_kernel(x_ref, a_ref, b_ref, c_ref, wo_ref, g1_ref, gn_ref, sh_ref, sc_ref, g2_ref,
                    wg_ref, wu_ref, wd_ref, o_ref, *, fc):
    wa, wb = a_ref.shape[1], b_ref.shape[1]
    mix = (_dot(a_ref[...], wo_ref[:wa, :]) + _dot(b_ref[...], wo_ref[wa:wa + wb, :])
           + _dot(c_ref[...], wo_ref[wa + wb:, :]))
    x1 = x_ref[...] + g1_ref[...] * mix
    xb = _rms_mod(x1, gn_ref[...], sh_ref[...], sc_ref[...]).astype(BF16)
    acc = None
    for c0 in range(0, wg_ref.shape[1], fc):
        hg = _dot(xb, wg_ref[:, c0:c0 + fc])
        hu = _dot(xb, wu_ref[:, c0:c0 + fc])
        act = ((hg * jax.nn.sigmoid(hg)) * hu).astype(BF16)
        part = _dot(act, wd_ref[c0:c0 + fc, :])
        acc = part if acc is None else acc + part
    o_ref[...] = x1 + g2_ref[...] * acc


def _mix_ffn(x, a, b, c, wo, g1, gn, sh2, sc2, g2, wg, wu, wd, layer):
    bn, l, d = x.shape
    tm = min(TOKEN_TILE, l)
    row = pl.BlockSpec((None, 1, d), lambda bb, t: (bb, 0, 0))
    tok = lambda width: pl.BlockSpec((None, tm, width), lambda bb, t: (bb, t, 0))
    return pl.pallas_call(
        functools.partial(_mix_ffn_kernel, fc=FFN_CHUNK),
        out_shape=jax.ShapeDtypeStruct((bn, l, d), F32),
        grid=(bn, l // tm),
        in_specs=[tok(d), tok(a.shape[2]), tok(b.shape[2]), tok(c.shape[2]), _layer_of(wo, layer),
                  row, pl.BlockSpec((1, d), lambda bb, t: (0, 0)), row, row, row,
                  _layer_of(wg, layer), _layer_of(wu, layer), _layer_of(wd, layer)],
        out_specs=tok(d),
        compiler_params=_params("parallel", "parallel"),
        name="mix_ffn",
    )(x, a, b, c, wo, g1, gn, sh2, sc2, g2, wg, wu, wd)


def _rope_tables(l):
    half = HEAD_DIM // 2
    nfreq = half // 2
    inv = ROPE_BASE ** (-jnp.arange(nfreq, dtype=F32) / nfreq)
    pos = jnp.arange(l)
    rows, cols = pos // GRID_W, pos % GRID_W
    ang = jnp.concatenate([rows.astype(F32)[:, None] * inv[None, :]] * 2
                          + [cols.astype(F32)[:, None] * inv[None, :]] * 2, axis=-1)
    first = (np.arange(HEAD_DIM) % half) < nfreq
    cos, sin = jnp.cos(ang), jnp.sin(ang)
    sin_a = jnp.where(first[None, :], -sin, 0.0)
    sin_b = jnp.where(first[None, :], 0.0, sin)
    reps = LANES // HEAD_DIM
    return tuple(jnp.tile(tb, (1, reps)) for tb in (cos, sin_a, sin_b))


def kernel(x, c, ctx, c_ctx, ada_w, ada_b, norm1_g, norm2_g, w_in, qnorm_a, knorm_a, sink_a, qnorm_b, knorm_b,
           rpb_b, conv_w, conv_b, filt_w1, filt_b1, filt_freq, filt_w2, filt_b2, filt_w3, hyena_bias, w_out,
           ffn_w_gate, ffn_w_up, ffn_w_down):
    bn, l, d = x.shape
    lc = ctx.shape[1]
    depth = ada_w.shape[0]
    cw = hyena_bias.shape[2]
    assert l % (A_STEP_BLOCKS * A_BLOCK) == 0 and l >= 3 * A_SPAN and lc % 256 == 0
    assert l % (B_STEP_GROUPS * B_GROUP * GRID_W) == 0 and l // GRID_W >= 3 * B_GROUP

    mod_rows = 8 * (-(-(bn + 1) // 8))
    c_all = jnp.zeros((mod_rows, d), F32).at[:bn].set(c).at[bn].set(c_ctx)
    mod = _modulation(c_all, ada_w, ada_b)

    a_order = (0, 2, 1, 3)
    a_kv = tuple(hh // (A_HEADS // A_KV_HEADS) for hh in a_order)
    heads_a = lambda t, axis: [lax.slice_in_dim(t, hh * HEAD_DIM, (hh + 1) * HEAD_DIM, axis=axis) for hh in a_order]
    qa_w = A_HEADS * HEAD_DIM
    w_in_b = w_in.astype(BF16)
    w_out_b = jnp.concatenate(heads_a(w_out, 1) + [w_out[:, qa_w:]], axis=1).astype(BF16)
    sink_p = LOG2E * jnp.stack([sink_a[:, hh] for hh in a_order], axis=1)
    wg_b, wu_b, wd_b = ffn_w_gate.astype(BF16), ffn_w_up.astype(BF16), ffn_w_down.astype(BF16)

    lane = np.arange(256)
    e_heads = jnp.asarray((lane[:, None] // HEAD_DIM) == (lane[None, :] // HEAD_DIM), BF16)
    tile4 = lambda g: jnp.tile(g.reshape(1, HEAD_DIM), (1, 256 // HEAD_DIM))
    rope_lat = _rope_tables(l)

    fh_lat, fh_ctx = min(FREQ_TILE, l // 2), min(FREQ_TILE, lc // 2)
    f_lat, g_lat = _dft_matrices(l // 2, fh_lat)
    f_ctx, g_ctx = _dft_matrices(lc // 2, fh_ctx)
    hc_lat = _hyena_consts(l, cw)
    hc_ctx = _hyena_consts(lc, cw)

    xc = ctx
    for i in range(depth):
        last = i == depth - 1
        m = mod[i]
        part = lambda rows, k: rows[:, None, k * d:(k + 1) * d]
        lat = [part(m[:bn], k) for k in range(6)]
        cx = [part(m[bn:bn + 1], k) for k in range(6)]
        g1n = norm1_g[i].reshape(1, d)
        g2n = norm2_g[i].reshape(1, d)
        gqa, gka, gqb, gkb = tile4(qnorm_a[i]), tile4(knorm_a[i]), tile4(qnorm_b[i]), tile4(knorm_b[i])

        qa, ka, va, qb, kb, vb, hy = _proj_in(x, lat[0], lat[1], g1n, w_in_b, i, gqa, gka, gqb, gkb, e_heads,
                                              rope_lat)
        ctx_out = _proj_in(xc.reshape(1, bn * lc, d), cx[0], cx[1], g1n, w_in_b, i, gqa, gka, gqb, gkb, e_heads,
                           None)
        qa_c, ka_c, va_c, qb_c, kb_c, vb_c, hy_c = [t.reshape(bn, lc, -1) for t in ctx_out]
        hy_args = (conv_w[i], conv_b[i].reshape(1, -1),
                   *_pack_filter_mlp(filt_w1[i], filt_b1[i], filt_freq[i], filt_w2[i], filt_b2[i]),
                   filt_w3[i], hyena_bias[i])

        sink_row = jnp.repeat(sink_p[i], A_BLOCK).reshape(1, A_HEADS * A_BLOCK)
        out_a = _attn_a(qa, ka, va, ka_c, va_c, sink_row)
        out_b = _attn_b(qb, kb, vb, kb_c, vb_c, _rpb_tables(rpb_b[i]))
        out_c = _hyena(hy, hc_lat, f_lat, g_lat, fh_lat, *hy_args)
        x = _mix_ffn(x, out_a, out_b, out_c, w_out_b, lat[2], g2n, lat[3], lat[4], lat[5], wg_b, wu_b, wd_b, i)

        if not last:
            oa_c = _dense_attn(qa_c, ka_c, va_c, sink_p[i], kv_of=a_kv, use_sink=True)
            ob_c = _dense_attn(qb_c, kb_c, vb_c, sink_p[i], kv_of=tuple(range(B_HEADS)), use_sink=False)
            oc_c = _hyena(hy_c, hc_ctx, f_ctx, g_ctx, fh_ctx, *hy_args)
            flat = lambda t: t.reshape(1, bn * lc, -1)
            xc = _mix_ffn(flat(xc), flat(oa_c), flat(ob_c), flat(oc_c), w_out_b, cx[2], g2n, cx[3], cx[4], cx[5],
                          wg_b, wu_b, wd_b, i).reshape(bn, lc, d)
    return x
```

```python
import functools
import math

import jax
import jax.numpy as jnp
import numpy as np
from jax import lax
from jax.experimental import pallas as pl
from jax.experimental.pallas import tpu as pltpu

F32 = jnp.float32
BF16 = jnp.bfloat16

GRID_W = 64
HEAD_DIM = 64
A_HEADS = 4
A_KV_HEADS = 2
A_BLOCK = 128
B_HEADS = 4
NA_ROWS = 8
NA_COLS = 16
C_ORDER = 2
C_DIRS = 2
C_FILTER_WIDTH = 64
C_BANDS = 16
ROPE_BASE = 10000.0
EPS = 1e-6
LOG2E = math.log2(math.e)
HYENA_MIN_DECAY = math.log(1e-2) / 1.5
HYENA_MAX_DECAY = math.log(1e-2) / 0.3

V7X_VMEM_BYTES = 64 * 1024 * 1024
VMEM_LIMIT = V7X_VMEM_BYTES - 8 * 1024 * 1024
LANES = 128
TOKEN_TILE = 1024
FREQ_TILE = 512


def _params(*sem):
    return pltpu.CompilerParams(dimension_semantics=sem, vmem_limit_bytes=VMEM_LIMIT)


def _dot(a, b):
    return jnp.dot(a, b, preferred_element_type=F32)


def _dot_nt(a, b):
    return lax.dot_general(a, b, (((1,), (1,)), ((), ())), preferred_element_type=F32)


def _dot_hp(a, b):
    ah, al = _split_bf16(a)
    bh, bl = _split_bf16(b)
    return _dot(ah, bh) + (_dot(ah, bl) + _dot(al, bh))


def _split_bf16(v):
    hi = v.astype(BF16)
    lo = (v - hi.astype(F32)).astype(BF16)
    return hi, lo


def _layer_of(stack, layer):
    return pl.BlockSpec((None,) + stack.shape[1:], lambda *_: (layer, 0, 0), pipeline_mode=pl.Buffered(1))


def _resident(shape):
    nd = len(shape)
    return pl.BlockSpec(shape, lambda *_: (0,) * nd, pipeline_mode=pl.Buffered(1))


def _mod_kernel(c_ref, w_ref, b_ref, o_ref):
    cv = c_ref[...]
    sc = (cv * jax.nn.sigmoid(cv)).astype(BF16)
    o_ref[...] = _dot(sc, w_ref[...].astype(BF16)) + b_ref[...]


def _modulation(c_all, ada_w, ada_b):
    depth, d, n = ada_w.shape
    rows = c_all.shape[0]
    tn = 1536
    return pl.pallas_call(
        _mod_kernel,
        out_shape=jax.ShapeDtypeStruct((depth, rows, n), F32),
        grid=(depth, n // tn),
        in_specs=[
            pl.BlockSpec((rows, d), lambda i, j: (0, 0)),
            pl.BlockSpec((None, d, tn), lambda i, j: (i, 0, j)),
            pl.BlockSpec((None, 1, tn), lambda i, j: (i, 0, j)),
        ],
        out_specs=pl.BlockSpec((None, rows, tn), lambda i, j: (i, 0, j)),
        compiler_params=_params("arbitrary", "arbitrary"),
        name="modulation",
    )(c_all, ada_w, ada_b.reshape(depth, 1, n))


def _rms_mod(x, g, shift, scale):
    y = x * lax.rsqrt(jnp.mean(x * x, axis=-1, keepdims=True) + EPS)
    return (y * g) * (1 + scale) + shift


def _head_norm(h, g, e):
    ss = _dot((h * h).astype(BF16), e)
    return (h * lax.rsqrt(ss * (1.0 / HEAD_DIM) + EPS)) * g


def _rope(t, cos, sin_a, sin_b):
    outs = []
    for c in range(t.shape[1] // LANES):
        tc = t[:, c * LANES:(c + 1) * LANES]
        outs.append(tc * cos + pltpu.roll(tc, LANES - 16, 1) * sin_a + pltpu.roll(tc, 16, 1) * sin_b)
    return outs[0] if len(outs) == 1 else jnp.concatenate(outs, axis=1)


PROJ_ROWS = 1024
PROJ_SPLIT = 2


def _proj_in_kernel(x_ref, shift_ref, scale_ref, g_ref, w_ref, gqa_ref, gka_ref, gqb_ref, gkb_ref, e_ref, *rest,
                    rope):
    if rope:
        cos_ref, sa_ref, sb_ref = rest[:3]
        rest = rest[3:]
    qa_ref, ka_ref, va_ref, qb_ref, kb_ref, vb_ref, hy_ref = rest
    qscale = LOG2E * HEAD_DIM ** -0.5
    hm = x_ref.shape[0] // PROJ_SPLIT

    def project(part):
        rows = slice(part * hm, (part + 1) * hm)
        xb = _rms_mod(x_ref[rows, :], g_ref[...], shift_ref[...], scale_ref[...]).astype(BF16)
        return _dot(xb, w_ref[...])

    def finish(part, h_all):
        rows = slice(part * hm, (part + 1) * hm)

        def proj(lo, hi):
            return h_all[:, lo:hi]

        def maybe_rope(t):
            return _rope(t, cos_ref[rows, :], sa_ref[rows, :], sb_ref[rows, :]) if rope else t

        def store_v(ref, v):
            if rope:
                for i in range(hm // LANES):
                    ref[part * (hm // LANES) + i] = v[i * LANES:(i + 1) * LANES, :].T.astype(BF16)
            else:
                ref[rows, :] = v.astype(BF16)

        qa = maybe_rope(_head_norm(proj(0, 256), gqa_ref[...], e_ref[...])) * qscale
        t0, t1 = qa[:, :LANES], qa[:, LANES:]
        lo = lax.broadcasted_iota(jnp.int32, t0.shape, 1) < HEAD_DIM
        qa_ref[rows, :LANES] = jnp.where(lo, t0, pltpu.roll(t1, HEAD_DIM, 1)).astype(BF16)
        qa_ref[rows, LANES:] = jnp.where(lo, pltpu.roll(t0, HEAD_DIM, 1), t1).astype(BF16)
        kva = proj(256, 512)
        ka = maybe_rope(_head_norm(kva[:, :LANES], gka_ref[:, :128], e_ref[:128, :128]))
        ka_ref[rows, :] = ka.astype(BF16)
        store_v(va_ref, kva[:, LANES:])
        qb = _head_norm(proj(512, 768), gqb_ref[...], e_ref[...])
        qb_ref[rows, :] = (qb * qscale).astype(BF16)
        kb_ref[rows, :] = _head_norm(proj(768, 1024), gkb_ref[...], e_ref[...]).astype(BF16)
        store_v(vb_ref, proj(1024, 1280))
        hy_ref[rows, :] = proj(1280, 2816).astype(BF16)

    for part in range(PROJ_SPLIT):
        finish(part, project(part))


def _proj_in(x, shift, scale, g, w, layer, gqa, gka, gqb, gkb, e, rope_tabs):
    bn, l, d = x.shape
    n = w.shape[2]
    rope = rope_tabs is not None
    tm = min(PROJ_ROWS, l)
    row = lambda width: pl.BlockSpec((None, 1, width), lambda b, t: (b, 0, 0))
    const = lambda shape: pl.BlockSpec(shape, lambda b, t: (0,) * len(shape))
    tok = lambda width: pl.BlockSpec((None, tm, width), lambda b, t: (b, t, 0))
    tab = pl.BlockSpec((tm, LANES), lambda b, t: (t, 0))
    widths = (256, 128, 128, 256, 256, 256, n - 1280)
    shapes = [(bn, l, wd) for wd in widths]
    specs = [tok(wd) for wd in widths]
    if rope:
        for i in (2, 5):
            shapes[i] = (bn, l // LANES, widths[i], LANES)
            specs[i] = pl.BlockSpec((None, tm // LANES, widths[i], LANES), lambda b, t: (b, t, 0, 0))
    return pl.pallas_call(
        functools.partial(_proj_in_kernel, rope=rope),
        out_shape=[jax.ShapeDtypeStruct(sh, BF16) for sh in shapes],
        grid=(bn, l // tm),
        in_specs=[tok(d), row(d), row(d), const((1, d)), _layer_of(w, layer),
                  const((1, 256)), const((1, 256)), const((1, 256)), const((1, 256)), const((256, 256))]
                 + ([tab, tab, tab] if rope else []),
        out_specs=specs,
        compiler_params=_params("parallel", "parallel"),
        name="proj_in_rope" if rope else "proj_in",
    )(x, shift, scale, g, w, gqa, gka, gqb, gkb, e, *(rope_tabs or ()))


def _softmax_pv(s_list, v_list, sink):
    m = s_list[0].max(axis=-1, keepdims=True)
    for s in s_list[1:]:
        m = jnp.maximum(m, s.max(axis=-1, keepdims=True))
    if sink is not None:
        m = jnp.maximum(m, sink)
    den = None
    out = None
    for s, v in zip(s_list, v_list):
        p = jnp.exp2(s - m)
        ps = p.sum(axis=-1, keepdims=True)
        den = ps if den is None else den + ps
        o = _dot(p.astype(BF16), v)
        out = o if out is None else out + o
    if sink is not None:
        den = den + jnp.exp2(sink - m)
    return out * (1.0 / den)


def _stack_heads(q):
    lo = lax.broadcasted_iota(jnp.int32, q.shape, 1) < HEAD_DIM
    zero = jnp.zeros_like(q)
    return jnp.concatenate([jnp.where(lo, q, zero), jnp.where(lo, zero, q)], axis=0)


SOFTMAX_CHUNK = 32


def _softmax_keys(s_ref, p_ref, nloc, add_loc, sink):
    nk = s_ref.shape[0]
    ch = SOFTMAX_CHUNK
    macc = None
    for r0 in range(0, nk, ch):
        s = s_ref[r0:r0 + ch, :]
        if r0 < nloc:
            s = s + add_loc(r0)
            s_ref[r0:r0 + ch, :] = s
        macc = s if macc is None else jnp.maximum(macc, s)
    m = macc.max(axis=0, keepdims=True)
    if sink is not None:
        m = jnp.maximum(m, sink)
    sacc = None
    for r0 in range(0, nk, ch):
        p = jnp.exp2(s_ref[r0:r0 + ch, :] - m)
        sacc = p if sacc is None else sacc + p
        p_ref[r0:r0 + ch, :] = p.astype(BF16)
    den = sacc.sum(axis=0, keepdims=True)
    if sink is not None:
        den = den + jnp.exp2(sink - m)
    return 1.0 / den


A_STEP_BLOCKS = 16
A_SPAN = 3 * A_BLOCK
A_SCORE_BUFS = 2


def _attn_a_kernel(q_ref, k_ref, vt_ref, kx_ref, vxt_ref, mask_ref, sink_ref, o_ref, s_ref, p_ref):
    l = k_ref.shape[0]
    nb = l // A_BLOCK
    hd = HEAD_DIM
    def window(u):
        n = pl.program_id(1) * A_STEP_BLOCKS + u
        tile0 = jnp.clip(n - 1, 0, nb - 3)
        pat = jnp.where(n == 0, 0, jnp.where(n == nb - 1, 2, 1))
        return tile0, pat

    def scores(u):
        tile0, _ = window(u)
        start = pl.multiple_of(tile0 * A_BLOCK, A_BLOCK)
        q = q_ref[u * A_BLOCK:(u + 1) * A_BLOCK, :]
        qs = jnp.concatenate([_stack_heads(q[:, :LANES]), _stack_heads(q[:, LANES:])], axis=0)
        s_ref[u % A_SCORE_BUFS] = _dot_nt(jnp.concatenate([k_ref[pl.ds(start, A_SPAN), :], kx_ref[...]], axis=0), qs)

    def output(u, r):
        tile0, _ = window(u)
        vt = jnp.concatenate([vt_ref[tile0 + i] for i in range(3)] + [vxt_ref[...]], axis=1)
        ot = _dot(vt, p_ref[u % 2]) * r
        ot = jnp.concatenate([ot[(i % 2) * hd:(i % 2 + 1) * hd, i * A_BLOCK:(i + 1) * A_BLOCK]
                              for i in range(A_HEADS)], axis=0)
        o_ref[u * A_BLOCK:(u + 1) * A_BLOCK, :] = ot.T.astype(o_ref.dtype)

    ahead = A_SCORE_BUFS - 1
    for u in range(ahead):
        scores(u)
    pending = None
    for u in range(A_STEP_BLOCKS):
        if u + ahead < A_STEP_BLOCKS:
            scores(u + ahead)
        _, pat = window(u)
        r = _softmax_keys(s_ref.at[u % A_SCORE_BUFS], p_ref.at[u % 2], A_SPAN,
                          lambda r0: mask_ref[pat, r0:r0 + SOFTMAX_CHUNK, :], sink_ref[...])
        if pending is not None:
            output(*pending)
        pending = (u, r)
    output(*pending)


def _attn_a_mask():
    i = np.arange(A_BLOCK)[None, :]
    j = np.arange(A_SPAN)[:, None]
    offs = (0, A_BLOCK, 2 * A_BLOCK)
    m = np.stack([np.where(np.abs(j - i - o) <= A_BLOCK, 0.0, -np.inf) for o in offs])
    return jnp.asarray(np.tile(m, (1, 1, A_HEADS)), F32)


def _attn_a(q, k, vt, kx, vx, sink_row):
    bn, l, qw = q.shape
    lc = kx.shape[1]
    kvw = k.shape[2]
    nb = l // A_BLOCK
    qs = A_STEP_BLOCKS * A_BLOCK
    mask = _attn_a_mask()
    vxt = jnp.swapaxes(vx, 1, 2)
    seq = pl.BlockSpec((None, l, kvw), lambda b, s: (b, 0, 0))
    qblk = pl.BlockSpec((None, qs, qw), lambda b, s: (b, s, 0))
    nq = A_HEADS * A_BLOCK
    return pl.pallas_call(
        _attn_a_kernel,
        out_shape=jax.ShapeDtypeStruct((bn, l, qw), BF16),
        grid=(bn, l // qs),
        in_specs=[qblk, seq, pl.BlockSpec((None, nb, kvw, A_BLOCK), lambda b, s: (b, 0, 0, 0)),
                  pl.BlockSpec((None, lc, kvw), lambda b, s: (b, 0, 0)),
                  pl.BlockSpec((None, kvw, lc), lambda b, s: (b, 0, 0)),
                  _resident(mask.shape), _resident(sink_row.shape)],
        out_specs=qblk,
        scratch_shapes=[pltpu.VMEM((A_SCORE_BUFS, A_SPAN + lc, nq), F32), pltpu.VMEM((2, A_SPAN + lc, nq), BF16)],
        compiler_params=_params("parallel", "arbitrary"),
        name="window_attn",
    )(q, k, vt, kx, vxt, mask, sink_row)


def _dense_attn_kernel(sink_ref, q_ref, k_ref, v_ref, o_ref, *, kv_of, use_sink):
    for h, kv in enumerate(kv_of):
        hs = slice(h * HEAD_DIM, (h + 1) * HEAD_DIM)
        ks = slice(kv * HEAD_DIM, (kv + 1) * HEAD_DIM)
        s = _dot_nt(q_ref[:, hs], k_ref[:, ks])
        o = _softmax_pv([s], [v_ref[:, ks]], sink_ref[h] if use_sink else None)
        o_ref[:, hs] = o.astype(o_ref.dtype)


def _dense_attn(q, k, v, sink, *, kv_of, use_sink):
    bn, l, qw = q.shape
    kvw = k.shape[2]
    full = lambda wd: pl.BlockSpec((None, l, wd), lambda b: (b, 0, 0))
    return pl.pallas_call(
        functools.partial(_dense_attn_kernel, kv_of=kv_of, use_sink=use_sink),
        out_shape=jax.ShapeDtypeStruct((bn, l, qw), BF16),
        grid=(bn,),
        in_specs=[pl.BlockSpec(memory_space=pltpu.SMEM), full(qw), full(kvw), full(kvw)],
        out_specs=full(qw),
        compiler_params=_params("parallel"),
        name="ctx_attn_sink" if use_sink else "ctx_attn",
    )(sink, q, k, v)


def _rpb_kernel(r_ref, oh_ref, ok_ref, o_ref):
    r = r_ref[...]
    b1 = r.astype(BF16)
    r2 = r - b1.astype(F32)
    b2 = r2.astype(BF16)
    b3 = (r2 - b2.astype(F32)).astype(BF16)
    oh = oh_ref[...]
    bias = (_dot(b1, oh) + _dot(b2, oh)) + _dot(b3, oh)
    o_ref[...] = jnp.where(ok_ref[...] > 0.5, bias * LOG2E, -jnp.inf)


def _rpb_slots():
    a = np.arange(B_GROUP)[:, None]
    kr = np.arange(B_SLAB)[None, :]
    dr = np.stack([kr - a + NA_ROWS - 1, kr - a + NA_ROWS // 2 - 1, kr - a + (B_GROUP + NA_ROWS - 1 - B_SLAB)])
    lo = np.stack([0 * a + 0 * kr, a + 0 * kr, 0 * a + (B_SLAB - NA_ROWS) + 0 * kr])
    valid = (kr[None] >= lo) & (kr[None] < lo + NA_ROWS)
    return np.where(valid, dr, -1)


def _rpb_assemble_kernel(t_ref, o_ref, *, n_dr):
    slots = _rpb_slots()
    w = GRID_W
    blank = jnp.full((w, w), -jnp.inf, F32)
    for pat in range(3):
        for t in range(o_ref.shape[1]):
            for kr in range(B_SLAB):
                for hh in range(2):
                    for a in range(B_GROUP):
                        dr = int(slots[pat, a, kr])
                        tile = t_ref[(2 * t + hh) * n_dr + dr] if dr >= 0 else blank
                        c0 = (hh * B_GROUP + a) * w
                        o_ref[pat, t, kr * w:(kr + 1) * w, c0:c0 + w] = tile


def _rpb_tables(rpb):
    h, nr, nc = rpb.shape
    col = np.arange(GRID_W)
    dc = np.clip(col[:, None] - col[None, :], 1 - NA_COLS, NA_COLS - 1) + NA_COLS - 1
    onehot = (dc.reshape(1, -1) == np.arange(nc)[:, None]).astype(np.float32)
    onehot = np.concatenate([onehot, np.zeros((32 - nc, GRID_W * GRID_W), np.float32)], axis=0)
    col_start = np.clip(col - NA_COLS // 2, 0, GRID_W - NA_COLS)
    col_ok = (col[:, None] >= col_start[None, :]) & (col[:, None] < col_start[None, :] + NA_COLS)
    rows = 64
    r2 = jnp.zeros((rows, 32), F32).at[:h * nr, :nc].set(rpb.reshape(h * nr, nc))
    tiles = pl.pallas_call(
        _rpb_kernel,
        out_shape=jax.ShapeDtypeStruct((rows, GRID_W * GRID_W), F32),
        name="rpb_table",
    )(r2, jnp.asarray(onehot, BF16), jnp.asarray(col_ok.reshape(1, -1), F32))
    tiles = tiles.reshape(rows, GRID_W, GRID_W)
    return pl.pallas_call(
        functools.partial(_rpb_assemble_kernel, n_dr=nr),
        out_shape=jax.ShapeDtypeStruct((3, h // 2, B_SLAB * GRID_W, 2 * B_GROUP * GRID_W), F32),
        compiler_params=_params(),
        name="rpb_assemble",
    )(tiles)


B_GROUP = 4
B_SCORE_AHEAD = 2
B_STEP_GROUPS = 4
B_SLAB = 12


def _attn_b_kernel(q_ref, k_ref, vt_ref, kx_ref, vxt_ref, tbl_ref, o_ref, s_ref, p_ref):
    ng = pl.num_programs(1) * B_STEP_GROUPS
    rows = k_ref.shape[0] // GRID_W
    nloc = B_SLAB * GRID_W
    gq = B_GROUP * GRID_W
    units = [(u, t) for u in range(B_STEP_GROUPS) for t in range(B_HEADS // 2)]

    def slab(u):
        g = pl.program_id(1) * B_STEP_GROUPS + u
        base = jnp.clip(g * B_GROUP - NA_ROWS // 2, 0, rows - B_SLAB)
        pat = jnp.where(g == 0, 0, jnp.where(g == ng - 1, 2, 1))
        return base, pat

    def scores(u, t):
        base, _ = slab(u)
        start = pl.multiple_of(base * GRID_W, LANES)
        ts = slice(t * LANES, (t + 1) * LANES)
        qs = _stack_heads(q_ref[u * gq:(u + 1) * gq, ts])
        s_ref[u, t] = _dot_nt(jnp.concatenate([k_ref[pl.ds(start, nloc), ts], kx_ref[:, ts]], axis=0), qs)

    def output(u, t, r):
        base, _ = slab(u)
        tile0 = base // (LANES // GRID_W)
        ts = slice(t * LANES, (t + 1) * LANES)
        vt = jnp.concatenate([vt_ref[tile0 + j, ts, :] for j in range(nloc // LANES)] + [vxt_ref[ts, :]],
                             axis=1)
        ot = _dot(vt, p_ref[u, t]) * r
        ot = jnp.concatenate([ot[:HEAD_DIM, :gq], ot[HEAD_DIM:, gq:]], axis=0)
        o_ref[u * gq:(u + 1) * gq, ts] = ot.T.astype(o_ref.dtype)

    ahead = B_SCORE_AHEAD
    for unit in units[:ahead]:
        scores(*unit)
    pending = None
    for i, (u, t) in enumerate(units):
        if i + ahead < len(units):
            scores(*units[i + ahead])
        _, pat = slab(u)
        r = _softmax_keys(s_ref.at[u, t], p_ref.at[u, t], nloc,
                          lambda r0: tbl_ref[pat, t, r0:r0 + SOFTMAX_CHUNK, :], None)
        if pending is not None:
            output(*pending)
        pending = (u, t, r)
    output(*pending)


def _attn_b(q, k, vt, kx, vx, tbl):
    bn, l, w = q.shape
    lc = kx.shape[1]
    gq = B_GROUP * GRID_W
    sq = B_STEP_GROUPS * gq
    nk = B_SLAB * GRID_W + lc
    vxt = jnp.swapaxes(vx, 1, 2)
    seq = pl.BlockSpec((None, l, w), lambda b, g: (b, 0, 0))
    qblk = pl.BlockSpec((None, sq, w), lambda b, g: (b, g, 0))
    return pl.pallas_call(
        _attn_b_kernel,
        out_shape=jax.ShapeDtypeStruct((bn, l, w), BF16),
        grid=(bn, l // sq),
        in_specs=[qblk, seq, pl.BlockSpec((None, l // LANES, w, LANES), lambda b, g: (b, 0, 0, 0)),
                  pl.BlockSpec((None, lc, w), lambda b, g: (b, 0, 0)),
                  pl.BlockSpec((None, w, lc), lambda b, g: (b, 0, 0)),
                  _resident(tbl.shape)],
        out_specs=qblk,
        scratch_shapes=[pltpu.VMEM((B_STEP_GROUPS, B_HEADS // 2, nk, 2 * gq), F32),
                        pltpu.VMEM((B_STEP_GROUPS, B_HEADS // 2, nk, 2 * gq), BF16)],
        compiler_params=_params("parallel", "arbitrary"),
        name="nbr_attn",
    )(q, k, vt, kx, vxt, tbl)


def _dft_matrices(l, fh):
    k = jnp.arange(l, dtype=jnp.int32)[:, None]
    n = jnp.arange(l, dtype=jnp.int32)[None, :]
    ang = (((2 * k + 1) * n) % (4 * l)).astype(F32) * (math.pi / (2 * l))
    fre = jnp.cos(ang).reshape(l // fh, fh, l)
    fim = (-jnp.sin(ang)).reshape(l // fh, fh, l)
    f = jnp.concatenate([fre, fim], axis=1).reshape(2 * l, l)
    g = f.T * (1.0 / l)
    return f.astype(BF16), g.astype(BF16)


def _filter_kernel(z_ref, t_ref, w1_ref, b1_ref, fr_ref, w2_ref, b2_ref, w3_ref, dl_ref,
                   ah_ref, al_ref, sh_ref, sl_ref, hid_ref):
    @pl.when(pl.program_id(0) == 0)
    def _():
        fr = fr_ref[...]
        hid = jnp.sin(fr * (_dot_hp(z_ref[...], w1_ref[...]) + b1_ref[...]))
        hid_ref[...] = jnp.sin(fr * (_dot_hp(hid, w2_ref[...]) + b2_ref[...]))

    cw = dl_ref.shape[1]
    p = t_ref.shape[0] // 3
    hid = hid_ref[...]
    w3 = w3_ref[...]
    none = jnp.zeros_like(w3)
    taps = jnp.concatenate([_dot_hp(hid, jnp.concatenate([w3, none], axis=0)),
                            _dot_hp(hid, jnp.concatenate([none, w3], axis=0))], axis=0)
    decay = jnp.exp(-t_ref[...] * dl_ref[...])
    kf = taps[:, :cw] * decay
    kb = taps[:, cw:] * decay
    kf0, kf1, kfr = kf[:p], kf[p:2 * p], kf[2 * p:]
    kb0, kb1, kbr = kb[:p], kb[p:2 * p], kb[2 * p:]
    first = lax.broadcasted_iota(jnp.int32, (p, cw), 0) == 0
    drop0 = lambda v: jnp.where(first, 0.0, v)
    kb0 = drop0(kb0)
    colsum = lambda v: jnp.sum(jnp.abs(v), axis=0, keepdims=True)
    inv = 1.0 / (colsum(kf0) + colsum(kf1) + colsum(kb0) + colsum(kb1))
    pairs = ((kf0, kb0), (kf1, drop0(kfr)), (kbr, drop0(kb1)))
    for d, (cp, cm) in enumerate(pairs):
        cols = slice(d * cw, (d + 1) * cw)
        ah_ref[:, cols], al_ref[:, cols] = _split_bf16((cp + cm) * inv)
        sh_ref[:, cols], sl_ref[:, cols] = _split_bf16((cp - cm) * inv)


def _filters(z, t, w1, b1, freq, w2, b2, w3, deltas):
    p = t.shape[0] // 3
    cw = deltas.shape[1]
    fw = w2.shape[0]
    const = lambda shape: pl.BlockSpec(shape, lambda o: (0,) * len(shape))
    out = jax.ShapeDtypeStruct((p, C_ORDER * 3 * cw), BF16)
    oblk = pl.BlockSpec((p, 3 * cw), lambda o: (0, o))
    return pl.pallas_call(
        _filter_kernel,
        out_shape=[out] * 4,
        grid=(C_ORDER,),
        in_specs=[const(z.shape), const(t.shape), const(w1.shape), const((1, fw)), const((1, fw)),
                  const((fw, fw)), const((1, fw)), pl.BlockSpec((w3.shape[0], C_DIRS * cw), lambda o: (0, o)),
                  const((1, cw))],
        out_specs=[oblk] * 4,
        scratch_shapes=[pltpu.VMEM(z.shape, F32)],
        compiler_params=_params("arbitrary"),
        name="hyena_filter",
    )(z, t, w1, b1, freq, w2, b2, w3, deltas)


def _spectrum_kernel(f_ref, ah_ref, al_ref, sh_ref, sl_ref, kre_ref, kim_ref):
    fh = kre_ref.shape[0]
    fre = f_ref[:fh, :]
    fim = f_ref[fh:, :]
    kre_ref[...] = _dot(fre, ah_ref[...]) + _dot(fre, al_ref[...])
    kim_ref[...] = _dot(fim, sh_ref[...]) + _dot(fim, sl_ref[...])


def _spectrum(f, ah, al, sh, sl, fh):
    p, n = ah.shape
    tn = n // C_ORDER
    taps = pl.BlockSpec((p, tn), lambda o, j: (0, o))
    out = jax.ShapeDtypeStruct((p, n), F32)
    oblk = pl.BlockSpec((fh, tn), lambda o, j: (j, o))
    return pl.pallas_call(
        _spectrum_kernel,
        out_shape=[out, out],
        grid=(C_ORDER, p // fh),
        in_specs=[pl.BlockSpec((2 * fh, p), lambda o, j: (j, 0)), taps, taps, taps, taps],
        out_specs=[oblk, oblk],
        compiler_params=_params("arbitrary", "arbitrary"),
        name="hyena_spectrum",
    )(f, ah, al, sh, sl)


def _short_conv(u, w_ref, b_ref):
    n = u.shape[0]
    row = lax.broadcasted_iota(jnp.int32, u.shape, 0)
    prev = jnp.where(row == 0, 0.0, pltpu.roll(u, 1, 0))
    nxt = jnp.where(row == n - 1, 0.0, pltpu.roll(u, n - 1, 0))
    return prev * w_ref[0:1, :] + u * w_ref[1:2, :] + nxt * w_ref[2:3, :] + b_ref[...]


def _short_conv_wrap(u, w_ref, b_ref):
    n = u.shape[0]
    return (pltpu.roll(u, 1, 0) * w_ref[0:1, :] + u * w_ref[1:2, :] + pltpu.roll(u, n - 1, 0) * w_ref[2:3, :]
            + b_ref[...])


LONGCONV_ROWS = 1024
CONV_CHUNKS = 2


def _longconv_kernel(u_ref, g_ref, cwu_ref, cbu_ref, cwg_ref, cbg_ref, f_ref, gm_ref, kre_ref, kim_ref, d_ref,
                     o_ref, ub_ref, acc_ref, *, conv_u, n_steps):
    j = pl.program_id(1)
    fh = kre_ref.shape[0]
    nb, p, cw = u_ref.shape[0], ub_ref.shape[2], d_ref.shape[1]
    cc = cw // CONV_CHUNKS
    l = 2 * p
    edge = 16

    def edge_conv(ref, w_ref, b_ref, cs):
        head = _short_conv(ref[0:2 * edge, cs].astype(F32), w_ref.at[:, cs], b_ref.at[:, cs])[:edge]
        tail = _short_conv(ref[l - 2 * edge:l, cs].astype(F32), w_ref.at[:, cs], b_ref.at[:, cs])[edge:]
        return head, tail

    def load_u(ib, c, cs):
        ub = ub_ref.at[ib, c]
        if conv_u:
            u = _short_conv_wrap(u_ref[ib, :, cs].astype(F32), cwu_ref.at[:, cs], cbu_ref.at[:, cs]).astype(BF16)
        else:
            u = u_ref[ib, :, cs]
        ub[:, :cc] = u[:p]
        ub[:, cc:] = u[p:]
        if conv_u:
            head, tail = edge_conv(u_ref.at[ib], cwu_ref, cbu_ref, cs)
            ub[0:edge, :cc] = head.astype(BF16)
            ub[p - edge:p, cc:] = tail.astype(BF16)

    def forward(ib, c):
        return _dot(f_ref[...], ub_ref[ib, c])

    def inverse(c, spec):
        u0r, u1r, u0i, u1i = spec[:fh, :cc], spec[:fh, cc:], spec[fh:, :cc], spec[fh:, cc:]
        tap = lambda ref, d: ref[:, d * cw + c * cc:d * cw + (c + 1) * cc]
        c0r, c1r, cmr = (tap(kre_ref, d) for d in range(3))
        c0i, c1i, cmi = (tap(kim_ref, d) for d in range(3))
        y0r = (c0r * u0r - c0i * u0i) + (cmr * u1r - cmi * u1i)
        y0i = (c0r * u0i + c0i * u0r) + (cmr * u1i + cmi * u1r)
        y1r = (c1r * u0r - c1i * u0i) + (c0r * u1r - c0i * u1i)
        y1i = (c1r * u0i + c1i * u0r) + (c0r * u1i + c0i * u1r)
        y = jnp.concatenate([jnp.concatenate([y0r, y1r], axis=1), jnp.concatenate([y0i, y1i], axis=1)], axis=0)
        return _dot(gm_ref[...], y.astype(BF16))

    def gated_out(ib, c, cs, conv):
        gate = _short_conv_wrap(g_ref[ib, :, cs].astype(F32), cwg_ref.at[:, cs], cbg_ref.at[:, cs])
        ub = ub_ref.at[ib, c]
        y = jnp.concatenate([conv[:, :cc], conv[:, cc:]], axis=0)
        y = y + jnp.concatenate([ub[:, :cc], ub[:, cc:]], axis=0).astype(F32) * d_ref[:, cs]
        o_ref[ib, :, cs] = (gate * y).astype(o_ref.dtype)
        head, tail = edge_conv(g_ref.at[ib], cwg_ref, cbg_ref, cs)
        o_ref[ib, 0:edge, cs] = (head * y[:edge]).astype(o_ref.dtype)
        o_ref[ib, l - edge:l, cs] = (tail * y[l - edge:]).astype(o_ref.dtype)

    def step(first, last):
        chunks = [(ib, c, slice(c * cc, (c + 1) * cc)) for ib in range(nb) for c in range(CONV_CHUNKS)]

        def start(ib, c, cs):
            if first:
                load_u(ib, c, cs)
            return forward(ib, c)

        spec = start(*chunks[0])
        pending = None
        for i, (ib, c, cs) in enumerate(chunks):
            nxt = start(*chunks[i + 1]) if i + 1 < len(chunks) else None
            conv = inverse(c, spec)
            spec = nxt
            if not first:
                conv = acc_ref[ib, c] + conv
            if last:
                if pending is not None:
                    gated_out(*pending)
                pending = (ib, c, cs, conv)
            else:
                acc_ref[ib, c] = conv
        if last:
            gated_out(*pending)

    if n_steps == 1:
        step(True, True)
    else:
        pl.when(j == 0)(lambda: step(True, False))
        pl.when(j == n_steps - 1)(lambda: step(False, True))
        if n_steps > 2:
            pl.when(jnp.logical_and(j > 0, j < n_steps - 1))(lambda: step(False, False))


def _longconv(u_arr, u_blk, g_arr, g_blk, conv_w, conv_b, f, gm, kre, kim, order, d, *, conv_u, fh):
    bn, l, _ = u_arr.shape
    p = l // 2
    cw = d.shape[1]
    ub = u_blk if conv_u else 0
    nb = math.gcd(bn, max(1, LONGCONV_ROWS // l))
    tok = lambda blk: pl.BlockSpec((nb, l, cw), lambda b, j: (b, 0, blk))
    cpar = lambda rows, blk: pl.BlockSpec((rows, cw), lambda b, j: (0, blk))
    ktab = pl.BlockSpec((fh, 3 * cw), lambda b, j: (j, order))
    return pl.pallas_call(
        functools.partial(_longconv_kernel, conv_u=conv_u, n_steps=p // fh),
        out_shape=jax.ShapeDtypeStruct((bn, l, cw), BF16),
        grid=(bn // nb, p // fh),
        in_specs=[tok(u_blk), tok(g_blk), cpar(3, ub), cpar(1, ub), cpar(3, g_blk), cpar(1, g_blk),
                  pl.BlockSpec((2 * fh, p), lambda b, j: (j, 0)),
                  pl.BlockSpec((p, 2 * fh), lambda b, j: (0, j)),
                  ktab, ktab, pl.BlockSpec((1, cw), lambda b, j: (0, 0))],
        out_specs=tok(0),
        scratch_shapes=[pltpu.VMEM((nb, CONV_CHUNKS, p, 2 * cw // CONV_CHUNKS), BF16),
                        pltpu.VMEM((nb, CONV_CHUNKS, p, 2 * cw // CONV_CHUNKS), F32)],
        compiler_params=_params("parallel", "arbitrary"),
        name="hyena_longconv",
    )(u_arr, g_arr, conv_w, conv_b, conv_w, conv_b, f, gm, kre, kim, d)


def _hyena_consts(l, cw):
    p = l // 2
    e = np.arange(p)
    pos = np.concatenate([e, p + e, p - e])
    t = np.linspace(0.0, 1.0, l, dtype=np.float32)[pos][:, None]
    w = (2.0 * math.pi * np.arange(l, dtype=np.float32) / l).astype(np.float32)[pos][:, None]
    fq = np.linspace(1e-4, C_BANDS - 1, C_BANDS, dtype=np.float32)[None, :]
    wf = jnp.asarray(w) * jnp.asarray(fq)
    z = jnp.concatenate([jnp.asarray(t), jnp.cos(wf), -jnp.sin(wf)], axis=-1)
    z = jnp.pad(z, ((0, 0), (0, C_FILTER_WIDTH - z.shape[1])))
    z = jnp.concatenate([z[:3 * p // 2], z[3 * p // 2:]], axis=1)
    deltas = np.abs(np.linspace(HYENA_MIN_DECAY, HYENA_MAX_DECAY, cw, dtype=np.float32))[None, :]
    return z, jnp.asarray(t), jnp.asarray(deltas)


def _pack_filter_mlp(w1, b1, freq, w2, b2):
    fw = w2.shape[0]
    w1p = jnp.pad(w1, ((0, fw - w1.shape[0]), (0, 0)))
    zero = jnp.zeros((fw, fw), F32)
    diag = lambda w: jnp.concatenate([jnp.concatenate([w, zero], axis=1), jnp.concatenate([zero, w], axis=1)], axis=0)
    twice = lambda v: jnp.tile(v.reshape(1, fw), (1, 2))
    return diag(w1p), twice(b1), twice(freq), diag(w2), twice(b2)


def _hyena(hy, hconst, f, gm, fh, conv_w, conv_b, w1p, b1, freq, w2, b2, w3, dbias):
    z, t, deltas = hconst
    ah, al, sh, sl = _filters(z, t, w1p, b1, freq, w2, b2, w3, deltas)
    kre, kim = _spectrum(f, ah, al, sh, sl, fh)
    zz = _longconv(hy, 0, hy, 1, conv_w, conv_b, f, gm, kre, kim, 0, dbias[0:1], conv_u=True, fh=fh)
    return _longconv(zz, 0, hy, 2, conv_w, conv_b, f, gm, kre, kim, 1, dbias[1:2], conv_u=False, fh=fh)


FFN_CHUNK = 256


def _mix_ffn_kernel(x_ref, a_ref, b_ref, c_ref, wo_ref, g1_ref, gn_ref, sh_ref, sc_ref, g2_ref,
                    wg_ref, wu_ref, wd_ref, o_ref, *, fc):
    wa, wb = a_ref.shape[1], b_ref.shape[1]
    mix = (_dot(a_ref[...], wo_ref[:wa, :]) + _dot(b_ref[...], wo_ref[wa:wa + wb, :])
           + _dot(c_ref[...], wo_ref[wa + wb:, :]))
    x1 = x_ref[...] + g1_ref[...] * mix
    xb = _rms_mod(x1, gn_ref[...], sh_ref[...], sc_ref[...]).astype(BF16)
    acc = None
    for c0 in range(0, wg_ref.shape[1], fc):
        hg = _dot(xb, wg_ref[:, c0:c0 + fc])
        hu = _dot(xb, wu_ref[:, c0:c0 + fc])
        act = ((hg * jax.nn.sigmoid(hg)) * hu).astype(BF16)
        part = _dot(act, wd_ref[c0:c0 + fc, :])
        acc = part if acc is None else acc + part
    o_ref[...] = x1 + g2_ref[...] * acc


def _mix_ffn(x, a, b, c, wo, g1, gn, sh2, sc2, g2, wg, wu, wd, layer):
    bn, l, d = x.shape
    tm = min(TOKEN_TILE, l)
    row = pl.BlockSpec((None, 1, d), lambda bb, t: (bb, 0, 0))
    tok = lambda width: pl.BlockSpec((None, tm, width), lambda bb, t: (bb, t, 0))
    return pl.pallas_call(
        functools.partial(_mix_ffn_kernel, fc=FFN_CHUNK),
        out_shape=jax.ShapeDtypeStruct((bn, l, d), F32),
        grid=(bn, l // tm),
        in_specs=[tok(d), tok(a.shape[2]), tok(b.shape[2]), tok(c.shape[2]), _layer_of(wo, layer),
                  row, pl.BlockSpec((1, d), lambda bb, t: (0, 0)), row, row, row,
                  _layer_of(wg, layer), _layer_of(wu, layer), _layer_of(wd, layer)],
        out_specs=tok(d),
        compiler_params=_params("parallel", "parallel"),
        name="mix_ffn",
    )(x, a, b, c, wo, g1, gn, sh2, sc2, g2, wg, wu, wd)


def _rope_tables(l):
    half = HEAD_DIM // 2
    nfreq = half // 2
    inv = ROPE_BASE ** (-jnp.arange(nfreq, dtype=F32) / nfreq)
    pos = jnp.arange(l)
    rows, cols = pos // GRID_W, pos % GRID_W
    ang = jnp.concatenate([rows.astype(F32)[:, None] * inv[None, :]] * 2
                          + [cols.astype(F32)[:, None] * inv[None, :]] * 2, axis=-1)
    first = (np.arange(HEAD_DIM) % half) < nfreq
    cos, sin = jnp.cos(ang), jnp.sin(ang)
    sin_a = jnp.where(first[None, :], -sin, 0.0)
    sin_b = jnp.where(first[None, :], 0.0, sin)
    reps = LANES // HEAD_DIM
    return tuple(jnp.tile(tb, (1, reps)) for tb in (cos, sin_a, sin_b))


def kernel(x, c, ctx, c_ctx, ada_w, ada_b, norm1_g, norm2_g, w_in, qnorm_a, knorm_a, sink_a, qnorm_b, knorm_b,
           rpb_b, conv_w, conv_b, filt_w1, filt_b1, filt_freq, filt_w2, filt_b2, filt_w3, hyena_bias, w_out,
           ffn_w_gate, ffn_w_up, ffn_w_down):
    bn, l, d = x.shape
    lc = ctx.shape[1]
    depth = ada_w.shape[0]
    cw = hyena_bias.shape[2]
    assert l % (A_STEP_BLOCKS * A_BLOCK) == 0 and l >= 3 * A_SPAN and lc % 256 == 0
    assert l % (B_STEP_GROUPS * B_GROUP * GRID_W) == 0 and l // GRID_W >= 3 * B_GROUP

    mod_rows = 8 * (-(-(bn + 1) // 8))
    c_all = jnp.zeros((mod_rows, d), F32).at[:bn].set(c).at[bn].set(c_ctx)
    mod = _modulation(c_all, ada_w, ada_b)

    a_order = (0, 2, 1, 3)
    a_kv = tuple(hh // (A_HEADS // A_KV_HEADS) for hh in a_order)
    heads_a = lambda t, axis: [lax.slice_in_dim(t, hh * HEAD_DIM, (hh + 1) * HEAD_DIM, axis=axis) for hh in a_order]
    qa_w = A_HEADS * HEAD_DIM
    w_in_b = w_in.astype(BF16)
    w_out_b = jnp.concatenate(heads_a(w_out, 1) + [w_out[:, qa_w:]], axis=1).astype(BF16)
    sink_p = LOG2E * jnp.stack([sink_a[:, hh] for hh in a_order], axis=1)
    wg_b, wu_b, wd_b = ffn_w_gate.astype(BF16), ffn_w_up.astype(BF16), ffn_w_down.astype(BF16)

    lane = np.arange(256)
    e_heads = jnp.asarray((lane[:, None] // HEAD_DIM) == (lane[None, :] // HEAD_DIM), BF16)
    tile4 = lambda g: jnp.tile(g.reshape(1, HEAD_DIM), (1, 256 // HEAD_DIM))
    rope_lat = _rope_tables(l)

    fh_lat, fh_ctx = min(FREQ_TILE, l // 2), min(FREQ_TILE, lc // 2)
    f_lat, g_lat = _dft_matrices(l // 2, fh_lat)
    f_ctx, g_ctx = _dft_matrices(lc // 2, fh_ctx)
    hc_lat = _hyena_consts(l, cw)
    hc_ctx = _hyena_consts(lc, cw)

    xc = ctx
    for i in range(depth):
        last = i == depth - 1
        m = mod[i]
        part = lambda rows, k: rows[:, None, k * d:(k + 1) * d]
        lat = [part(m[:bn], k) for k in range(6)]
        cx = [part(m[bn:bn + 1], k) for k in range(6)]
        g1n = norm1_g[i].reshape(1, d)
        g2n = norm2_g[i].reshape(1, d)
        gqa, gka, gqb, gkb = tile4(qnorm_a[i]), tile4(knorm_a[i]), tile4(qnorm_b[i]), tile4(knorm_b[i])

        qa, ka, va, qb, kb, vb, hy = _proj_in(x, lat[0], lat[1], g1n, w_in_b, i, gqa, gka, gqb, gkb, e_heads,
                                              rope_lat)
        ctx_out = _proj_in(xc.reshape(1, bn * lc, d), cx[0], cx[1], g1n, w_in_b, i, gqa, gka, gqb, gkb, e_heads,
                           None)
        qa_c, ka_c, va_c, qb_c, kb_c, vb_c, hy_c = [t.reshape(bn, lc, -1) for t in ctx_out]
        hy_args = (conv_w[i], conv_b[i].reshape(1, -1),
                   *_pack_filter_mlp(filt_w1[i], filt_b1[i], filt_freq[i], filt_w2[i], filt_b2[i]),
                   filt_w3[i], hyena_bias[i])

        sink_row = jnp.repeat(sink_p[i], A_BLOCK).reshape(1, A_HEADS * A_BLOCK)
        out_a = _attn_a(qa, ka, va, ka_c, va_c, sink_row)
        out_b = _attn_b(qb, kb, vb, kb_c, vb_c, _rpb_tables(rpb_b[i]))
        out_c = _hyena(hy, hc_lat, f_lat, g_lat, fh_lat, *hy_args)
        x = _mix_ffn(x, out_a, out_b, out_c, w_out_b, lat[2], g2n, lat[3], lat[4], lat[5], wg_b, wu_b, wd_b, i)

        if not last:
            oa_c = _dense_attn(qa_c, ka_c, va_c, sink_p[i], kv_of=a_kv, use_sink=True)
            ob_c = _dense_attn(qb_c, kb_c, vb_c, sink_p[i], kv_of=tuple(range(B_HEADS)), use_sink=False)
            oc_c = _hyena(hy_c, hc_ctx, f_ctx, g_ctx, fh_ctx, *hy_args)
            flat = lambda t: t.reshape(1, bn * lc, -1)
            xc = _mix_ffn(flat(xc), flat(oa_c), flat(ob_c), flat(oc_c), w_out_b, cx[2], g2n, cx[3], cx[4], cx[5],
                          wg_b, wu_b, wd_b, i).reshape(bn, lc, d)
    return x
```

```python
import functools
import math

import jax
import jax.numpy as jnp
import numpy as np
from jax import lax
from jax.experimental import pallas as pl
from jax.experimental.pallas import tpu as pltpu

F32 = jnp.float32
BF16 = jnp.bfloat16

GRID_W = 64
HEAD_DIM = 64
A_HEADS = 4
A_KV_HEADS = 2
A_BLOCK = 128
B_HEADS = 4
NA_ROWS = 8
NA_COLS = 16
C_ORDER = 2
C_DIRS = 2
C_FILTER_WIDTH = 64
C_BANDS = 16
ROPE_BASE = 10000.0
EPS = 1e-6
LOG2E = math.log2(math.e)
HYENA_MIN_DECAY = math.log(1e-2) / 1.5
HYENA_MAX_DECAY = math.log(1e-2) / 0.3

V7X_VMEM_BYTES = 64 * 1024 * 1024
VMEM_LIMIT = V7X_VMEM_BYTES - 8 * 1024 * 1024
LANES = 128
TOKEN_TILE = 512
FREQ_TILE = 512


def _params(*sem):
    return pltpu.CompilerParams(dimension_semantics=sem, vmem_limit_bytes=VMEM_LIMIT)


def _dot(a, b):
    return jnp.dot(a, b, preferred_element_type=F32)


def _dot_nt(a, b):
    return lax.dot_general(a, b, (((1,), (1,)), ((), ())), preferred_element_type=F32)


def _dot_hp(a, b):
    ah, al = _split_bf16(a)
    bh, bl = _split_bf16(b)
    return _dot(ah, bh) + (_dot(ah, bl) + _dot(al, bh))


def _split_bf16(v):
    hi = v.astype(BF16)
    lo = (v - hi.astype(F32)).astype(BF16)
    return hi, lo


def _layer_of(stack, layer):
    return pl.BlockSpec((None,) + stack.shape[1:], lambda *_: (layer, 0, 0), pipeline_mode=pl.Buffered(1))


def _resident(shape):
    nd = len(shape)
    return pl.BlockSpec(shape, lambda *_: (0,) * nd, pipeline_mode=pl.Buffered(1))


def _mod_kernel(c_ref, w_ref, b_ref, o_ref):
    cv = c_ref[...]
    sc = (cv * jax.nn.sigmoid(cv)).astype(BF16)
    o_ref[...] = _dot(sc, w_ref[...].astype(BF16)) + b_ref[...]


def _modulation(c_all, ada_w, ada_b):
    depth, d, n = ada_w.shape
    rows = c_all.shape[0]
    tn = 1536
    return pl.pallas_call(
        _mod_kernel,
        out_shape=jax.ShapeDtypeStruct((depth, rows, n), F32),
        grid=(depth, n // tn),
        in_specs=[
            pl.BlockSpec((rows, d), lambda i, j: (0, 0)),
            pl.BlockSpec((None, d, tn), lambda i, j: (i, 0, j)),
            pl.BlockSpec((None, 1, tn), lambda i, j: (i, 0, j)),
        ],
        out_specs=pl.BlockSpec((None, rows, tn), lambda i, j: (i, 0, j)),
        compiler_params=_params("arbitrary", "arbitrary"),
        name="modulation",
    )(c_all, ada_w, ada_b.reshape(depth, 1, n))


def _rms_mod(x, g, shift, scale):
    y = x * lax.rsqrt(jnp.mean(x * x, axis=-1, keepdims=True) + EPS)
    return (y * g) * (1 + scale) + shift


def _head_norm(h, g, e):
    ss = _dot((h * h).astype(BF16), e)
    return (h * lax.rsqrt(ss * (1.0 / HEAD_DIM) + EPS)) * g


def _rope(t, cos, sin_a, sin_b):
    outs = []
    for c in range(t.shape[1] // LANES):
        tc = t[:, c * LANES:(c + 1) * LANES]
        outs.append(tc * cos + pltpu.roll(tc, LANES - 16, 1) * sin_a + pltpu.roll(tc, 16, 1) * sin_b)
    return outs[0] if len(outs) == 1 else jnp.concatenate(outs, axis=1)


PROJ_ROWS = 1024
PROJ_SPLIT = 2


def _proj_in_kernel(x_ref, shift_ref, scale_ref, g_ref, w_ref, gqa_ref, gka_ref, gqb_ref, gkb_ref, e_ref, *rest,
                    rope):
    if rope:
        cos_ref, sa_ref, sb_ref = rest[:3]
        rest = rest[3:]
    qa_ref, ka_ref, va_ref, qb_ref, kb_ref, vb_ref, hy_ref = rest
    qscale = LOG2E * HEAD_DIM ** -0.5
    hm = x_ref.shape[0] // PROJ_SPLIT

    def project(part):
        rows = slice(part * hm, (part + 1) * hm)
        xb = _rms_mod(x_ref[rows, :], g_ref[...], shift_ref[...], scale_ref[...]).astype(BF16)
        return _dot(xb, w_ref[...])

    def finish(part, h_all):
        rows = slice(part * hm, (part + 1) * hm)

        def proj(lo, hi):
            return h_all[:, lo:hi]

        def maybe_rope(t):
            return _rope(t, cos_ref[rows, :], sa_ref[rows, :], sb_ref[rows, :]) if rope else t

        def store_v(ref, v):
            if rope:
                for i in range(hm // LANES):
                    ref[part * (hm // LANES) + i] = v[i * LANES:(i + 1) * LANES, :].T.astype(BF16)
            else:
                ref[rows, :] = v.astype(BF16)

        qa = maybe_rope(_head_norm(proj(0, 256), gqa_ref[...], e_ref[...])) * qscale
        t0, t1 = qa[:, :LANES], qa[:, LANES:]
        lo = lax.broadcasted_iota(jnp.int32, t0.shape, 1) < HEAD_DIM
        qa_ref[rows, :LANES] = jnp.where(lo, t0, pltpu.roll(t1, HEAD_DIM, 1)).astype(BF16)
        qa_ref[rows, LANES:] = jnp.where(lo, pltpu.roll(t0, HEAD_DIM, 1), t1).astype(BF16)
        kva = proj(256, 512)
        ka = maybe_rope(_head_norm(kva[:, :LANES], gka_ref[:, :128], e_ref[:128, :128]))
        ka_ref[rows, :] = ka.astype(BF16)
        store_v(va_ref, kva[:, LANES:])
        qb = _head_norm(proj(512, 768), gqb_ref[...], e_ref[...])
        qb_ref[rows, :] = (qb * qscale).astype(BF16)
        kb_ref[rows, :] = _head_norm(proj(768, 1024), gkb_ref[...], e_ref[...]).astype(BF16)
        store_v(vb_ref, proj(1024, 1280))
        hy_ref[rows, :] = proj(1280, 2816).astype(BF16)

    for part in range(PROJ_SPLIT):
        finish(part, project(part))


def _proj_in(x, shift, scale, g, w, layer, gqa, gka, gqb, gkb, e, rope_tabs):
    bn, l, d = x.shape
    n = w.shape[2]
    rope = rope_tabs is not None
    tm = min(PROJ_ROWS, l)
    row = lambda width: pl.BlockSpec((None, 1, width), lambda b, t: (b, 0, 0))
    const = lambda shape: pl.BlockSpec(shape, lambda b, t: (0,) * len(shape))
    tok = lambda width: pl.BlockSpec((None, tm, width), lambda b, t: (b, t, 0))
    tab = pl.BlockSpec((tm, LANES), lambda b, t: (t, 0))
    widths = (256, 128, 128, 256, 256, 256, n - 1280)
    shapes = [(bn, l, wd) for wd in widths]
    specs = [tok(wd) for wd in widths]
    if rope:
        for i in (2, 5):
            shapes[i] = (bn, l // LANES, widths[i], LANES)
            specs[i] = pl.BlockSpec((None, tm // LANES, widths[i], LANES), lambda b, t: (b, t, 0, 0))
    return pl.pallas_call(
        functools.partial(_proj_in_kernel, rope=rope),
        out_shape=[jax.ShapeDtypeStruct(sh, BF16) for sh in shapes],
        grid=(bn, l // tm),
        in_specs=[tok(d), row(d), row(d), const((1, d)), _layer_of(w, layer),
                  const((1, 256)), const((1, 256)), const((1, 256)), const((1, 256)), const((256, 256))]
                 + ([tab, tab, tab] if rope else []),
        out_specs=specs,
        compiler_params=_params("parallel", "parallel"),
        name="proj_in_rope" if rope else "proj_in",
    )(x, shift, scale, g, w, gqa, gka, gqb, gkb, e, *(rope_tabs or ()))


def _softmax_pv(s_list, v_list, sink):
    m = s_list[0].max(axis=-1, keepdims=True)
    for s in s_list[1:]:
        m = jnp.maximum(m, s.max(axis=-1, keepdims=True))
    if sink is not None:
        m = jnp.maximum(m, sink)
    den = None
    out = None
    for s, v in zip(s_list, v_list):
        p = jnp.exp2(s - m)
        ps = p.sum(axis=-1, keepdims=True)
        den = ps if den is None else den + ps
        o = _dot(p.astype(BF16), v)
        out = o if out is None else out + o
    if sink is not None:
        den = den + jnp.exp2(sink - m)
    return out * (1.0 / den)


def _stack_heads(q):
    lo = lax.broadcasted_iota(jnp.int32, q.shape, 1) < HEAD_DIM
    zero = jnp.zeros_like(q)
    return jnp.concatenate([jnp.where(lo, q, zero), jnp.where(lo, zero, q)], axis=0)


SOFTMAX_CHUNK = 32


def _softmax_keys(s_ref, p_ref, nloc, add_loc, sink):
    nk = s_ref.shape[0]
    ch = SOFTMAX_CHUNK
    macc = None
    for r0 in range(0, nk, ch):
        s = s_ref[r0:r0 + ch, :]
        if r0 < nloc:
            s = s + add_loc(r0)
            s_ref[r0:r0 + ch, :] = s
        macc = s if macc is None else jnp.maximum(macc, s)
    m = macc.max(axis=0, keepdims=True)
    if sink is not None:
        m = jnp.maximum(m, sink)
    sacc = None
    for r0 in range(0, nk, ch):
        p = jnp.exp2(s_ref[r0:r0 + ch, :] - m)
        sacc = p if sacc is None else sacc + p
        p_ref[r0:r0 + ch, :] = p.astype(BF16)
    den = sacc.sum(axis=0, keepdims=True)
    if sink is not None:
        den = den + jnp.exp2(sink - m)
    return 1.0 / den


A_STEP_BLOCKS = 16
A_SPAN = 3 * A_BLOCK
A_SCORE_BUFS = 2


def _attn_a_kernel(q_ref, k_ref, vt_ref, kx_ref, vxt_ref, mask_ref, sink_ref, o_ref, s_ref, p_ref):
    l = k_ref.shape[0]
    nb = l // A_BLOCK
    hd = HEAD_DIM
    def window(u):
        n = pl.program_id(1) * A_STEP_BLOCKS + u
        tile0 = jnp.clip(n - 1, 0, nb - 3)
        pat = jnp.where(n == 0, 0, jnp.where(n == nb - 1, 2, 1))
        return tile0, pat

    def scores(u):
        tile0, _ = window(u)
        start = pl.multiple_of(tile0 * A_BLOCK, A_BLOCK)
        q = q_ref[u * A_BLOCK:(u + 1) * A_BLOCK, :]
        qs = jnp.concatenate([_stack_heads(q[:, :LANES]), _stack_heads(q[:, LANES:])], axis=0)
        s_ref[u % A_SCORE_BUFS] = _dot_nt(jnp.concatenate([k_ref[pl.ds(start, A_SPAN), :], kx_ref[...]], axis=0), qs)

    def output(u, r):
        tile0, _ = window(u)
        vt = jnp.concatenate([vt_ref[tile0 + i] for i in range(3)] + [vxt_ref[...]], axis=1)
        ot = _dot(vt, p_ref[u % 2]) * r
        ot = jnp.concatenate([ot[(i % 2) * hd:(i % 2 + 1) * hd, i * A_BLOCK:(i + 1) * A_BLOCK]
                              for i in range(A_HEADS)], axis=0)
        o_ref[u * A_BLOCK:(u + 1) * A_BLOCK, :] = ot.T.astype(o_ref.dtype)

    ahead = A_SCORE_BUFS - 1
    for u in range(ahead):
        scores(u)
    pending = None
    for u in range(A_STEP_BLOCKS):
        if u + ahead < A_STEP_BLOCKS:
            scores(u + ahead)
        _, pat = window(u)
        r = _softmax_keys(s_ref.at[u % A_SCORE_BUFS], p_ref.at[u % 2], A_SPAN,
                          lambda r0: mask_ref[pat, r0:r0 + SOFTMAX_CHUNK, :], sink_ref[...])
        if pending is not None:
            output(*pending)
        pending = (u, r)
    output(*pending)


def _attn_a_mask():
    i = np.arange(A_BLOCK)[None, :]
    j = np.arange(A_SPAN)[:, None]
    offs = (0, A_BLOCK, 2 * A_BLOCK)
    m = np.stack([np.where(np.abs(j - i - o) <= A_BLOCK, 0.0, -np.inf) for o in offs])
    return jnp.asarray(np.tile(m, (1, 1, A_HEADS)), F32)


def _attn_a(q, k, vt, kx, vx, sink_row):
    bn, l, qw = q.shape
    lc = kx.shape[1]
    kvw = k.shape[2]
    nb = l // A_BLOCK
    qs = A_STEP_BLOCKS * A_BLOCK
    mask = _attn_a_mask()
    vxt = jnp.swapaxes(vx, 1, 2)
    seq = pl.BlockSpec((None, l, kvw), lambda b, s: (b, 0, 0))
    qblk = pl.BlockSpec((None, qs, qw), lambda b, s: (b, s, 0))
    nq = A_HEADS * A_BLOCK
    return pl.pallas_call(
        _attn_a_kernel,
        out_shape=jax.ShapeDtypeStruct((bn, l, qw), BF16),
        grid=(bn, l // qs),
        in_specs=[qblk, seq, pl.BlockSpec((None, nb, kvw, A_BLOCK), lambda b, s: (b, 0, 0, 0)),
                  pl.BlockSpec((None, lc, kvw), lambda b, s: (b, 0, 0)),
                  pl.BlockSpec((None, kvw, lc), lambda b, s: (b, 0, 0)),
                  _resident(mask.shape), _resident(sink_row.shape)],
        out_specs=qblk,
        scratch_shapes=[pltpu.VMEM((A_SCORE_BUFS, A_SPAN + lc, nq), F32), pltpu.VMEM((2, A_SPAN + lc, nq), BF16)],
        compiler_params=_params("parallel", "arbitrary"),
        name="window_attn",
    )(q, k, vt, kx, vxt, mask, sink_row)


def _ctx_attn_kernel(sink_ref, qa_ref, ka_ref, va_ref, qb_ref, kb_ref, vb_ref, oa_ref, ob_ref, *, a_kv):
    def heads(q_ref, k_ref, v_ref, o_ref, kv_of, use_sink):
        for h, kv in enumerate(kv_of):
            hs = slice(h * HEAD_DIM, (h + 1) * HEAD_DIM)
            ks = slice(kv * HEAD_DIM, (kv + 1) * HEAD_DIM)
            s = _dot_nt(q_ref[:, hs], k_ref[:, ks])
            o = _softmax_pv([s], [v_ref[:, ks]], sink_ref[h] if use_sink else None)
            o_ref[:, hs] = o.astype(o_ref.dtype)

    heads(qa_ref, ka_ref, va_ref, oa_ref, a_kv, True)
    heads(qb_ref, kb_ref, vb_ref, ob_ref, tuple(range(B_HEADS)), False)


def _ctx_attn(qa, ka, va, qb, kb, vb, sink, *, a_kv):
    bn, l, _ = qa.shape
    full = lambda t: pl.BlockSpec((None, l, t.shape[2]), lambda b: (b, 0, 0))
    out = lambda t: jax.ShapeDtypeStruct(t.shape, BF16)
    return pl.pallas_call(
        functools.partial(_ctx_attn_kernel, a_kv=a_kv),
        out_shape=[out(qa), out(qb)],
        grid=(bn,),
        in_specs=[pl.BlockSpec(memory_space=pltpu.SMEM)] + [full(t) for t in (qa, ka, va, qb, kb, vb)],
        out_specs=[full(qa), full(qb)],
        compiler_params=_params("parallel"),
        name="ctx_attn",
    )(sink, qa, ka, va, qb, kb, vb)


def _rpb_kernel(r_ref, oh_ref, ok_ref, o_ref):
    r = r_ref[...]
    b1 = r.astype(BF16)
    r2 = r - b1.astype(F32)
    b2 = r2.astype(BF16)
    b3 = (r2 - b2.astype(F32)).astype(BF16)
    oh = oh_ref[...]
    bias = (_dot(b1, oh) + _dot(b2, oh)) + _dot(b3, oh)
    o_ref[...] = jnp.where(ok_ref[...] > 0.5, bias * LOG2E, -jnp.inf)


def _rpb_slots():
    a = np.arange(B_GROUP)[:, None]
    kr = np.arange(B_SLAB)[None, :]
    dr = np.stack([kr - a + NA_ROWS - 1, kr - a + NA_ROWS // 2 - 1, kr - a + (B_GROUP + NA_ROWS - 1 - B_SLAB)])
    lo = np.stack([0 * a + 0 * kr, a + 0 * kr, 0 * a + (B_SLAB - NA_ROWS) + 0 * kr])
    valid = (kr[None] >= lo) & (kr[None] < lo + NA_ROWS)
    return np.where(valid, dr, -1)


def _rpb_assemble_kernel(t_ref, o_ref, *, n_dr):
    slots = _rpb_slots()
    w = GRID_W
    blank = jnp.full((w, w), -jnp.inf, F32)
    for pat in range(3):
        for t in range(o_ref.shape[1]):
            for kr in range(B_SLAB):
                for hh in range(2):
                    for a in range(B_GROUP):
                        dr = int(slots[pat, a, kr])
                        tile = t_ref[(2 * t + hh) * n_dr + dr] if dr >= 0 else blank
                        c0 = (hh * B_GROUP + a) * w
                        o_ref[pat, t, kr * w:(kr + 1) * w, c0:c0 + w] = tile


def _rpb_tables(rpb):
    h, nr, nc = rpb.shape
    col = np.arange(GRID_W)
    dc = np.clip(col[:, None] - col[None, :], 1 - NA_COLS, NA_COLS - 1) + NA_COLS - 1
    onehot = (dc.reshape(1, -1) == np.arange(nc)[:, None]).astype(np.float32)
    onehot = np.concatenate([onehot, np.zeros((32 - nc, GRID_W * GRID_W), np.float32)], axis=0)
    col_start = np.clip(col - NA_COLS // 2, 0, GRID_W - NA_COLS)
    col_ok = (col[:, None] >= col_start[None, :]) & (col[:, None] < col_start[None, :] + NA_COLS)
    rows = 64
    r2 = jnp.zeros((rows, 32), F32).at[:h * nr, :nc].set(rpb.reshape(h * nr, nc))
    tiles = pl.pallas_call(
        _rpb_kernel,
        out_shape=jax.ShapeDtypeStruct((rows, GRID_W * GRID_W), F32),
        name="rpb_table",
    )(r2, jnp.asarray(onehot, BF16), jnp.asarray(col_ok.reshape(1, -1), F32))
    tiles = tiles.reshape(rows, GRID_W, GRID_W)
    return pl.pallas_call(
        functools.partial(_rpb_assemble_kernel, n_dr=nr),
        out_shape=jax.ShapeDtypeStruct((3, h // 2, B_SLAB * GRID_W, 2 * B_GROUP * GRID_W), F32),
        compiler_params=_params(),
        name="rpb_assemble",
    )(tiles)


B_GROUP = 4
B_SCORE_AHEAD = 2
B_STEP_GROUPS = 4
B_SLAB = 12


def _attn_b_kernel(q_ref, k_ref, vt_ref, kx_ref, vxt_ref, tbl_ref, o_ref, s_ref, p_ref):
    ng = pl.num_programs(1) * B_STEP_GROUPS
    rows = k_ref.shape[0] // GRID_W
    nloc = B_SLAB * GRID_W
    gq = B_GROUP * GRID_W
    units = [(u, t) for u in range(B_STEP_GROUPS) for t in range(B_HEADS // 2)]

    def slab(u):
        g = pl.program_id(1) * B_STEP_GROUPS + u
        base = jnp.clip(g * B_GROUP - NA_ROWS // 2, 0, rows - B_SLAB)
        pat = jnp.where(g == 0, 0, jnp.where(g == ng - 1, 2, 1))
        return base, pat

    def scores(u, t):
        base, _ = slab(u)
        start = pl.multiple_of(base * GRID_W, LANES)
        ts = slice(t * LANES, (t + 1) * LANES)
        qs = _stack_heads(q_ref[u * gq:(u + 1) * gq, ts])
        s_ref[u, t] = _dot_nt(jnp.concatenate([k_ref[pl.ds(start, nloc), ts], kx_ref[:, ts]], axis=0), qs)

    def output(u, t, r):
        base, _ = slab(u)
        tile0 = base // (LANES // GRID_W)
        ts = slice(t * LANES, (t + 1) * LANES)
        vt = jnp.concatenate([vt_ref[tile0 + j, ts, :] for j in range(nloc // LANES)] + [vxt_ref[ts, :]],
                             axis=1)
        ot = _dot(vt, p_ref[u, t]) * r
        ot = jnp.concatenate([ot[:HEAD_DIM, :gq], ot[HEAD_DIM:, gq:]], axis=0)
        o_ref[u * gq:(u + 1) * gq, ts] = ot.T.astype(o_ref.dtype)

    ahead = B_SCORE_AHEAD
    for unit in units[:ahead]:
        scores(*unit)
    pending = None
    for i, (u, t) in enumerate(units):
        if i + ahead < len(units):
            scores(*units[i + ahead])
        _, pat = slab(u)
        r = _softmax_keys(s_ref.at[u, t], p_ref.at[u, t], nloc,
                          lambda r0: tbl_ref[pat, t, r0:r0 + SOFTMAX_CHUNK, :], None)
        if pending is not None:
            output(*pending)
        pending = (u, t, r)
    output(*pending)


def _attn_b(q, k, vt, kx, vx, tbl):
    bn, l, w = q.shape
    lc = kx.shape[1]
    gq = B_GROUP * GRID_W
    sq = B_STEP_GROUPS * gq
    nk = B_SLAB * GRID_W + lc
    vxt = jnp.swapaxes(vx, 1, 2)
    seq = pl.BlockSpec((None, l, w), lambda b, g: (b, 0, 0))
    qblk = pl.BlockSpec((None, sq, w), lambda b, g: (b, g, 0))
    return pl.pallas_call(
        _attn_b_kernel,
        out_shape=jax.ShapeDtypeStruct((bn, l, w), BF16),
        grid=(bn, l // sq),
        in_specs=[qblk, seq, pl.BlockSpec((None, l // LANES, w, LANES), lambda b, g: (b, 0, 0, 0)),
                  pl.BlockSpec((None, lc, w), lambda b, g: (b, 0, 0)),
                  pl.BlockSpec((None, w, lc), lambda b, g: (b, 0, 0)),
                  _resident(tbl.shape)],
        out_specs=qblk,
        scratch_shapes=[pltpu.VMEM((B_STEP_GROUPS, B_HEADS // 2, nk, 2 * gq), F32),
                        pltpu.VMEM((B_STEP_GROUPS, B_HEADS // 2, nk, 2 * gq), BF16)],
        compiler_params=_params("parallel", "arbitrary"),
        name="nbr_attn",
    )(q, k, vt, kx, vxt, tbl)


def _dft_matrices(l, fh):
    k = jnp.arange(l, dtype=jnp.int32)[:, None]
    n = jnp.arange(l, dtype=jnp.int32)[None, :]
    ang = (((2 * k + 1) * n) % (4 * l)).astype(F32) * (math.pi / (2 * l))
    fre = jnp.cos(ang).reshape(l // fh, fh, l)
    fim = (-jnp.sin(ang)).reshape(l // fh, fh, l)
    f = jnp.concatenate([fre, fim], axis=1).reshape(2 * l, l)
    g = f.T * (1.0 / l)
    return f.astype(BF16), g.astype(BF16)


def _filter_kernel(z_ref, t_ref, w1_ref, b1_ref, fr_ref, w2_ref, b2_ref, w3_ref, dl_ref,
                   ah_ref, al_ref, sh_ref, sl_ref, hid_ref):
    @pl.when(pl.program_id(0) == 0)
    def _():
        fr = fr_ref[...]
        hid = jnp.sin(fr * (_dot_hp(z_ref[...], w1_ref[...]) + b1_ref[...]))
        hid_ref[...] = jnp.sin(fr * (_dot_hp(hid, w2_ref[...]) + b2_ref[...]))

    cw = dl_ref.shape[1]
    p = t_ref.shape[0] // 3
    hid = hid_ref[...]
    w3 = w3_ref[...]
    none = jnp.zeros_like(w3)
    taps = jnp.concatenate([_dot_hp(hid, jnp.concatenate([w3, none], axis=0)),
                            _dot_hp(hid, jnp.concatenate([none, w3], axis=0))], axis=0)
    decay = jnp.exp(-t_ref[...] * dl_ref[...])
    kf = taps[:, :cw] * decay
    kb = taps[:, cw:] * decay
    kf0, kf1, kfr = kf[:p], kf[p:2 * p], kf[2 * p:]
    kb0, kb1, kbr = kb[:p], kb[p:2 * p], kb[2 * p:]
    first = lax.broadcasted_iota(jnp.int32, (p, cw), 0) == 0
    drop0 = lambda v: jnp.where(first, 0.0, v)
    kb0 = drop0(kb0)
    colsum = lambda v: jnp.sum(jnp.abs(v), axis=0, keepdims=True)
    inv = 1.0 / (colsum(kf0) + colsum(kf1) + colsum(kb0) + colsum(kb1))
    pairs = ((kf0, kb0), (kf1, drop0(kfr)), (kbr, drop0(kb1)))
    for d, (cp, cm) in enumerate(pairs):
        cols = slice(d * cw, (d + 1) * cw)
        ah_ref[:, cols], al_ref[:, cols] = _split_bf16((cp + cm) * inv)
        sh_ref[:, cols], sl_ref[:, cols] = _split_bf16((cp - cm) * inv)


def _filters(z, t, w1, b1, freq, w2, b2, w3, deltas):
    p = t.shape[0] // 3
    cw = deltas.shape[1]
    fw = w2.shape[0]
    const = lambda shape: pl.BlockSpec(shape, lambda o: (0,) * len(shape))
    out = jax.ShapeDtypeStruct((p, C_ORDER * 3 * cw), BF16)
    oblk = pl.BlockSpec((p, 3 * cw), lambda o: (0, o))
    return pl.pallas_call(
        _filter_kernel,
        out_shape=[out] * 4,
        grid=(C_ORDER,),
        in_specs=[const(z.shape), const(t.shape), const(w1.shape), const((1, fw)), const((1, fw)),
                  const((fw, fw)), const((1, fw)), pl.BlockSpec((w3.shape[0], C_DIRS * cw), lambda o: (0, o)),
                  const((1, cw))],
        out_specs=[oblk] * 4,
        scratch_shapes=[pltpu.VMEM(z.shape, F32)],
        compiler_params=_params("arbitrary"),
        name="hyena_filter",
    )(z, t, w1, b1, freq, w2, b2, w3, deltas)


def _spectrum_kernel(f_ref, ah_ref, al_ref, sh_ref, sl_ref, kre_ref, kim_ref):
    fh = kre_ref.shape[0]
    fre = f_ref[:fh, :]
    fim = f_ref[fh:, :]
    kre_ref[...] = _dot(fre, ah_ref[...]) + _dot(fre, al_ref[...])
    kim_ref[...] = _dot(fim, sh_ref[...]) + _dot(fim, sl_ref[...])


def _spectrum(f, ah, al, sh, sl, fh):
    p, n = ah.shape
    tn = n // C_ORDER
    taps = pl.BlockSpec((p, tn), lambda o, j: (0, o))
    out = jax.ShapeDtypeStruct((p, n), F32)
    oblk = pl.BlockSpec((fh, tn), lambda o, j: (j, o))
    return pl.pallas_call(
        _spectrum_kernel,
        out_shape=[out, out],
        grid=(C_ORDER, p // fh),
        in_specs=[pl.BlockSpec((2 * fh, p), lambda o, j: (j, 0)), taps, taps, taps, taps],
        out_specs=[oblk, oblk],
        compiler_params=_params("arbitrary", "arbitrary"),
        name="hyena_spectrum",
    )(f, ah, al, sh, sl)


def _short_conv(u, w_ref, b_ref):
    n = u.shape[0]
    row = lax.broadcasted_iota(jnp.int32, u.shape, 0)
    prev = jnp.where(row == 0, 0.0, pltpu.roll(u, 1, 0))
    nxt = jnp.where(row == n - 1, 0.0, pltpu.roll(u, n - 1, 0))
    return prev * w_ref[0:1, :] + u * w_ref[1:2, :] + nxt * w_ref[2:3, :] + b_ref[...]


def _short_conv_wrap(u, w_ref, b_ref):
    n = u.shape[0]
    return (pltpu.roll(u, 1, 0) * w_ref[0:1, :] + u * w_ref[1:2, :] + pltpu.roll(u, n - 1, 0) * w_ref[2:3, :]
            + b_ref[...])


LONGCONV_ROWS = 1024
CONV_CHUNKS = 2


def _longconv_kernel(u_ref, g_ref, cwu_ref, cbu_ref, cwg_ref, cbg_ref, f_ref, gm_ref, kre_ref, kim_ref, d_ref,
                     o_ref, ub_ref, acc_ref, *, conv_u, n_steps):
    j = pl.program_id(1)
    fh = kre_ref.shape[0]
    nb, p, cw = u_ref.shape[0], ub_ref.shape[2], d_ref.shape[1]
    cc = cw // CONV_CHUNKS
    l = 2 * p
    edge = 16

    def edge_conv(ref, w_ref, b_ref, cs):
        head = _short_conv(ref[0:2 * edge, cs].astype(F32), w_ref.at[:, cs], b_ref.at[:, cs])[:edge]
        tail = _short_conv(ref[l - 2 * edge:l, cs].astype(F32), w_ref.at[:, cs], b_ref.at[:, cs])[edge:]
        return head, tail

    def load_u(ib, c, cs):
        ub = ub_ref.at[ib, c]
        if conv_u:
            u = _short_conv_wrap(u_ref[ib, :, cs].astype(F32), cwu_ref.at[:, cs], cbu_ref.at[:, cs]).astype(BF16)
        else:
            u = u_ref[ib, :, cs]
        ub[:, :cc] = u[:p]
        ub[:, cc:] = u[p:]
        if conv_u:
            head, tail = edge_conv(u_ref.at[ib], cwu_ref, cbu_ref, cs)
            ub[0:edge, :cc] = head.astype(BF16)
            ub[p - edge:p, cc:] = tail.astype(BF16)

    def forward(ib, c):
        return _dot(f_ref[...], ub_ref[ib, c])

    def inverse(c, spec):
        u0r, u1r, u0i, u1i = spec[:fh, :cc], spec[:fh, cc:], spec[fh:, :cc], spec[fh:, cc:]
        tap = lambda ref, d: ref[:, d * cw + c * cc:d * cw + (c + 1) * cc]
        c0r, c1r, cmr = (tap(kre_ref, d) for d in range(3))
        c0i, c1i, cmi = (tap(kim_ref, d) for d in range(3))
        y0r = (c0r * u0r - c0i * u0i) + (cmr * u1r - cmi * u1i)
        y0i = (c0r * u0i + c0i * u0r) + (cmr * u1i + cmi * u1r)
        y1r = (c1r * u0r - c1i * u0i) + (c0r * u1r - c0i * u1i)
        y1i = (c1r * u0i + c1i * u0r) + (c0r * u1i + c0i * u1r)
        y = jnp.concatenate([jnp.concatenate([y0r, y1r], axis=1), jnp.concatenate([y0i, y1i], axis=1)], axis=0)
        return _dot(gm_ref[...], y.astype(BF16))

    def gated_out(ib, c, cs, conv):
        gate = _short_conv_wrap(g_ref[ib, :, cs].astype(F32), cwg_ref.at[:, cs], cbg_ref.at[:, cs])
        ub = ub_ref.at[ib, c]
        y = jnp.concatenate([conv[:, :cc], conv[:, cc:]], axis=0)
        y = y + jnp.concatenate([ub[:, :cc], ub[:, cc:]], axis=0).astype(F32) * d_ref[:, cs]
        o_ref[ib, :, cs] = (gate * y).astype(o_ref.dtype)
        head, tail = edge_conv(g_ref.at[ib], cwg_ref, cbg_ref, cs)
        o_ref[ib, 0:edge, cs] = (head * y[:edge]).astype(o_ref.dtype)
        o_ref[ib, l - edge:l, cs] = (tail * y[l - edge:]).astype(o_ref.dtype)

    def step(first, last):
        chunks = [(ib, c, slice(c * cc, (c + 1) * cc)) for ib in range(nb) for c in range(CONV_CHUNKS)]

        def start(ib, c, cs):
            if first:
                load_u(ib, c, cs)
            return forward(ib, c)

        spec = start(*chunks[0])
        pending = None
        for i, (ib, c, cs) in enumerate(chunks):
            nxt = start(*chunks[i + 1]) if i + 1 < len(chunks) else None
            conv = inverse(c, spec)
            spec = nxt
            if not first:
                conv = acc_ref[ib, c] + conv
            if last:
                if pending is not None:
                    gated_out(*pending)
                pending = (ib, c, cs, conv)
            else:
                acc_ref[ib, c] = conv
        if last:
            gated_out(*pending)

    if n_steps == 1:
        step(True, True)
    else:
        pl.when(j == 0)(lambda: step(True, False))
        pl.when(j == n_steps - 1)(lambda: step(False, True))
        if n_steps > 2:
            pl.when(jnp.logical_and(j > 0, j < n_steps - 1))(lambda: step(False, False))


def _longconv(u_arr, u_blk, g_arr, g_blk, conv_w, conv_b, f, gm, kre, kim, order, d, *, conv_u, fh):
    bn, l, _ = u_arr.shape
    p = l // 2
    cw = d.shape[1]
    ub = u_blk if conv_u else 0
    nb = math.gcd(bn, max(1, LONGCONV_ROWS // l))
    tok = lambda blk: pl.BlockSpec((nb, l, cw), lambda b, j: (b, 0, blk))
    cpar = lambda rows, blk: pl.BlockSpec((rows, cw), lambda b, j: (0, blk))
    ktab = pl.BlockSpec((fh, 3 * cw), lambda b, j: (j, order))
    return pl.pallas_call(
        functools.partial(_longconv_kernel, conv_u=conv_u, n_steps=p // fh),
        out_shape=jax.ShapeDtypeStruct((bn, l, cw), BF16),
        grid=(bn // nb, p // fh),
        in_specs=[tok(u_blk), tok(g_blk), cpar(3, ub), cpar(1, ub), cpar(3, g_blk), cpar(1, g_blk),
                  pl.BlockSpec((2 * fh, p), lambda b, j: (j, 0)),
                  pl.BlockSpec((p, 2 * fh), lambda b, j: (0, j)),
                  ktab, ktab, pl.BlockSpec((1, cw), lambda b, j: (0, 0))],
        out_specs=tok(0),
        scratch_shapes=[pltpu.VMEM((nb, CONV_CHUNKS, p, 2 * cw // CONV_CHUNKS), BF16),
                        pltpu.VMEM((nb, CONV_CHUNKS, p, 2 * cw // CONV_CHUNKS), F32)],
        compiler_params=_params("parallel", "arbitrary"),
        name="hyena_longconv",
    )(u_arr, g_arr, conv_w, conv_b, conv_w, conv_b, f, gm, kre, kim, d)


def _hyena_consts(l, cw):
    p = l // 2
    e = np.arange(p)
    pos = np.concatenate([e, p + e, p - e])
    t = np.linspace(0.0, 1.0, l, dtype=np.float32)[pos][:, None]
    w = (2.0 * math.pi * np.arange(l, dtype=np.float32) / l).astype(np.float32)[pos][:, None]
    fq = np.linspace(1e-4, C_BANDS - 1, C_BANDS, dtype=np.float32)[None, :]
    wf = jnp.asarray(w) * jnp.asarray(fq)
    z = jnp.concatenate([jnp.asarray(t), jnp.cos(wf), -jnp.sin(wf)], axis=-1)
    z = jnp.pad(z, ((0, 0), (0, C_FILTER_WIDTH - z.shape[1])))
    z = jnp.concatenate([z[:3 * p // 2], z[3 * p // 2:]], axis=1)
    deltas = np.abs(np.linspace(HYENA_MIN_DECAY, HYENA_MAX_DECAY, cw, dtype=np.float32))[None, :]
    return z, jnp.asarray(t), jnp.asarray(deltas)


def _pack_filter_mlp(w1, b1, freq, w2, b2):
    fw = w2.shape[0]
    w1p = jnp.pad(w1, ((0, fw - w1.shape[0]), (0, 0)))
    zero = jnp.zeros((fw, fw), F32)
    diag = lambda w: jnp.concatenate([jnp.concatenate([w, zero], axis=1), jnp.concatenate([zero, w], axis=1)], axis=0)
    twice = lambda v: jnp.tile(v.reshape(1, fw), (1, 2))
    return diag(w1p), twice(b1), twice(freq), diag(w2), twice(b2)


def _hyena(hy, hconst, f, gm, fh, conv_w, conv_b, w1p, b1, freq, w2, b2, w3, dbias):
    z, t, deltas = hconst
    ah, al, sh, sl = _filters(z, t, w1p, b1, freq, w2, b2, w3, deltas)
    kre, kim = _spectrum(f, ah, al, sh, sl, fh)
    zz = _longconv(hy, 0, hy, 1, conv_w, conv_b, f, gm, kre, kim, 0, dbias[0:1], conv_u=True, fh=fh)
    return _longconv(zz, 0, hy, 2, conv_w, conv_b, f, gm, kre, kim, 1, dbias[1:2], conv_u=False, fh=fh)


FFN_CHUNK = 256


def _mix_ffn_kernel(x_ref, a_ref, b_ref, c_ref, wo_ref, g1_ref, gn_ref, sh_ref, sc_ref, g2_ref,
                    wg_ref, wu_ref, wd_ref, o_ref, *, fc):
    wa, wb = a_ref.shape[1], b_ref.shape[1]
    mix = (_dot(a_ref[...], wo_ref[:wa, :]) + _dot(b_ref[...], wo_ref[wa:wa + wb, :])
           + _dot(c_ref[...], wo_ref[wa + wb:, :]))
    x1 = x_ref[...] + g1_ref[...] * mix
    xb = _rms_mod(x1, gn_ref[...], sh_ref[...], sc_ref[...]).astype(BF16)
    acc = None
    for c0 in range(0, wg_ref.shape[1], fc):
        hg = _dot(xb, wg_ref[:, c0:c0 + fc])
        hu = _dot(xb, wu_ref[:, c0:c0 + fc])
        act = ((hg * jax.nn.sigmoid(hg)) * hu).astype(BF16)
        part = _dot(act, wd_ref[c0:c0 + fc, :])
        acc = part if acc is None else acc + part
    o_ref[...] = x1 + g2_ref[...] * acc


def _mix_ffn(x, a, b, c, wo, g1, gn, sh2, sc2, g2, wg, wu, wd, layer):
    bn, l, d = x.shape
    tm = min(TOKEN_TILE, l)
    row = pl.BlockSpec((None, 1, d), lambda bb, t: (bb, 0, 0))
    tok = lambda width: pl.BlockSpec((None, tm, width), lambda bb, t: (bb, t, 0))
    return pl.pallas_call(
        functools.partial(_mix_ffn_kernel, fc=FFN_CHUNK),
        out_shape=jax.ShapeDtypeStruct((bn, l, d), F32),
        grid=(bn, l // tm),
        in_specs=[tok(d), tok(a.shape[2]), tok(b.shape[2]), tok(c.shape[2]), _layer_of(wo, layer),
                  row, pl.BlockSpec((1, d), lambda bb, t: (0, 0)), row, row, row,
                  _layer_of(wg, layer), _layer_of(wu, layer), _layer_of(wd, layer)],
        out_specs=tok(d),
        compiler_params=_params("parallel", "parallel"),
        name="mix_ffn",
    )(x, a, b, c, wo, g1, gn, sh2, sc2, g2, wg, wu, wd)


def _rope_tables(l):
    half = HEAD_DIM // 2
    nfreq = half // 2
    inv = ROPE_BASE ** (-jnp.arange(nfreq, dtype=F32) / nfreq)
    pos = jnp.arange(l)
    rows, cols = pos // GRID_W, pos % GRID_W
    ang = jnp.concatenate([rows.astype(F32)[:, None] * inv[None, :]] * 2
                          + [cols.astype(F32)[:, None] * inv[None, :]] * 2, axis=-1)
    first = (np.arange(HEAD_DIM) % half) < nfreq
    cos, sin = jnp.cos(ang), jnp.sin(ang)
    sin_a = jnp.where(first[None, :], -sin, 0.0)
    sin_b = jnp.where(first[None, :], 0.0, sin)
    reps = LANES // HEAD_DIM
    return tuple(jnp.tile(tb, (1, reps)) for tb in (cos, sin_a, sin_b))


def kernel(x, c, ctx, c_ctx, ada_w, ada_b, norm1_g, norm2_g, w_in, qnorm_a, knorm_a, sink_a, qnorm_b, knorm_b,
           rpb_b, conv_w, conv_b, filt_w1, filt_b1, filt_freq, filt_w2, filt_b2, filt_w3, hyena_bias, w_out,
           ffn_w_gate, ffn_w_up, ffn_w_down):
    bn, l, d = x.shape
    lc = ctx.shape[1]
    depth = ada_w.shape[0]
    cw = hyena_bias.shape[2]
    assert l % (A_STEP_BLOCKS * A_BLOCK) == 0 and l >= 3 * A_SPAN and lc % 256 == 0
    assert l % (B_STEP_GROUPS * B_GROUP * GRID_W) == 0 and l // GRID_W >= 3 * B_GROUP

    mod_rows = 8 * (-(-(bn + 1) // 8))
    c_all = jnp.zeros((mod_rows, d), F32).at[:bn].set(c).at[bn].set(c_ctx)
    mod = _modulation(c_all, ada_w, ada_b)

    a_order = (0, 2, 1, 3)
    a_kv = tuple(hh // (A_HEADS // A_KV_HEADS) for hh in a_order)
    heads_a = lambda t, axis: [lax.slice_in_dim(t, hh * HEAD_DIM, (hh + 1) * HEAD_DIM, axis=axis) for hh in a_order]
    qa_w = A_HEADS * HEAD_DIM
    w_in_b = w_in.astype(BF16)
    w_out_b = jnp.concatenate(heads_a(w_out, 1) + [w_out[:, qa_w:]], axis=1).astype(BF16)
    sink_p = LOG2E * jnp.stack([sink_a[:, hh] for hh in a_order], axis=1)
    wg_b, wu_b, wd_b = ffn_w_gate.astype(BF16), ffn_w_up.astype(BF16), ffn_w_down.astype(BF16)

    lane = np.arange(256)
    e_heads = jnp.asarray((lane[:, None] // HEAD_DIM) == (lane[None, :] // HEAD_DIM), BF16)
    tile4 = lambda g: jnp.tile(g.reshape(1, HEAD_DIM), (1, 256 // HEAD_DIM))
    rope_lat = _rope_tables(l)

    fh_lat, fh_ctx = min(FREQ_TILE, l // 2), min(FREQ_TILE, lc // 2)
    f_lat, g_lat = _dft_matrices(l // 2, fh_lat)
    f_ctx, g_ctx = _dft_matrices(lc // 2, fh_ctx)
    hc_lat = _hyena_consts(l, cw)
    hc_ctx = _hyena_consts(lc, cw)

    xc = ctx
    for i in range(depth):
        last = i == depth - 1
        m = mod[i]
        part = lambda rows, k: rows[:, None, k * d:(k + 1) * d]
        lat = [part(m[:bn], k) for k in range(6)]
        cx = [part(m[bn:bn + 1], k) for k in range(6)]
        g1n = norm1_g[i].reshape(1, d)
        g2n = norm2_g[i].reshape(1, d)
        gqa, gka, gqb, gkb = tile4(qnorm_a[i]), tile4(knorm_a[i]), tile4(qnorm_b[i]), tile4(knorm_b[i])

        qa, ka, va, qb, kb, vb, hy = _proj_in(x, lat[0], lat[1], g1n, w_in_b, i, gqa, gka, gqb, gkb, e_heads,
                                              rope_lat)
        ctx_out = _proj_in(xc.reshape(1, bn * lc, d), cx[0], cx[1], g1n, w_in_b, i, gqa, gka, gqb, gkb, e_heads,
                           None)
        qa_c, ka_c, va_c, qb_c, kb_c, vb_c, hy_c = [t.reshape(bn, lc, -1) for t in ctx_out]
        hy_args = (conv_w[i], conv_b[i].reshape(1, -1),
                   *_pack_filter_mlp(filt_w1[i], filt_b1[i], filt_freq[i], filt_w2[i], filt_b2[i]),
                   filt_w3[i], hyena_bias[i])

        sink_row = jnp.repeat(sink_p[i], A_BLOCK).reshape(1, A_HEADS * A_BLOCK)
        out_a = _attn_a(qa, ka, va, ka_c, va_c, sink_row)
        out_b = _attn_b(qb, kb, vb, kb_c, vb_c, _rpb_tables(rpb_b[i]))
        out_c = _hyena(hy, hc_lat, f_lat, g_lat, fh_lat, *hy_args)
        x = _mix_ffn(x, out_a, out_b, out_c, w_out_b, lat[2], g2n, lat[3], lat[4], lat[5], wg_b, wu_b, wd_b, i)

        if not last:
            oa_c, ob_c = _ctx_attn(qa_c, ka_c, va_c, qb_c, kb_c, vb_c, sink_p[i], a_kv=a_kv)
            oc_c = _hyena(hy_c, hc_ctx, f_ctx, g_ctx, fh_ctx, *hy_args)
            flat = lambda t: t.reshape(1, bn * lc, -1)
            xc = _mix_ffn(flat(xc), flat(oa_c), flat(ob_c), flat(oc_c), w_out_b, cx[2], g2n, cx[3], cx[4], cx[5],
                          wg_b, wu_b, wd_b, i).reshape(bn, lc, d)
    return x
```
